```python
import jax
import jax.numpy as jnp
from jax import lax
import numpy as np

D_MODEL = 1024
BATCH = 16
SEQ = 2048
DEPTH = 2

GRID_W = 64
CTX_LEN = 256
HEAD_DIM = 64
ROPE_THETA = 10000.0
Q_BLOCK = 128
ATTN_SCALE = HEAD_DIM ** -0.5
RMS_EPS = 1e-6
NEG_INF = -1e30

FNET_GROUPS = 4
FNET_GROUP_DIM = 64
A_W = FNET_GROUPS * FNET_GROUP_DIM
GA_Q_HEADS = 4
GA_KV_HEADS = 2
GA_Q_W = GA_Q_HEADS * HEAD_DIM
GA_KV_W = GA_KV_HEADS * HEAD_DIM
WA_Q_HEADS = 4
WA_KV_HEADS = 2
WA_Q_W = WA_Q_HEADS * HEAD_DIM
WA_KV_W = WA_KV_HEADS * HEAD_DIM
WINDOW = 128
GM_GROUPS = 4
GM_GROUP_DIM = 64
GM_W = GM_GROUPS * GM_GROUP_DIM
GM_CHUNK = 128
N_BRANCHES = 4
BRANCH_W = 256
OFF_GA_K = 0
OFF_GA_V = OFF_GA_K + GA_KV_W
OFF_WA_K = OFF_GA_V + GA_KV_W
OFF_WA_V = OFF_WA_K + WA_KV_W
KV_COLS = OFF_WA_V + WA_KV_W
OFF_GA_Q = KV_COLS
OFF_WA_Q = OFF_GA_Q + GA_Q_W
OFF_FN = OFF_WA_Q + WA_Q_W
OFF_GM_U = OFF_FN + A_W
OFF_GM_V = OFF_GM_U + GM_W
OFF_GATE = OFF_GM_V + GM_W
IN_W = OFF_GATE + N_BRANCHES * D_MODEL
N_EXPERTS = 64
TOP_K = 6
N_GROUPS = 8
TOPK_GROUPS = 4
D_EXPERT = 256
D_SHARED = 256
ROUTED_SCALE = 2.5

kernel_name = "hybrid_parallel_dit_moe_block"


def rmsnorm(x, g):
    xf = x.astype(jnp.float32)
    y = xf * lax.rsqrt(jnp.mean(xf * xf, axis=-1, keepdims=True) + RMS_EPS)
    return (y * g.astype(jnp.float32)).astype(x.dtype)


def modulate(h, shift, scale):
    return h * (1 + scale) + shift


def axial_rope_tables(seq, dtype):
    rows = seq // GRID_W
    row = jnp.repeat(jnp.arange(rows), GRID_W).astype(jnp.float32)
    col = jnp.tile(jnp.arange(GRID_W), rows).astype(jnp.float32)
    axis_dim = HEAD_DIM // 2
    inv_freq = 1.0 / (ROPE_THETA ** (jnp.arange(0, axis_dim, 2, dtype=jnp.float32) / axis_dim))
    ang_r = row[:, None] * inv_freq
    ang_c = col[:, None] * inv_freq
    ang = jnp.concatenate([ang_r, ang_r, ang_c, ang_c], axis=-1)
    return jnp.cos(ang).astype(dtype), jnp.sin(ang).astype(dtype)


def apply_axial_rope(t, cos, sin):
    t1, t2, t3, t4 = jnp.split(t, 4, axis=-1)
    rot = jnp.concatenate([-t2, t1, -t4, t3], axis=-1)
    return t * cos[:, None, :] + rot * sin[:, None, :]


def heads(z, n):
    b, l, _ = z.shape
    return z.reshape(b, l, n, HEAD_DIM)


def group_q(q, n_kv):
    b, l, h, d = q.shape
    return q.reshape(b, l, n_kv, h // n_kv, d)


def attend(q, k, v, sink=None):
    s = jnp.einsum('bqkgd,bnkd->bkgqn', q, k).astype(jnp.float32) * ATTN_SCALE
    if sink is not None:
        sk = jnp.broadcast_to(sink.astype(jnp.float32).reshape(1, k.shape[2], -1, 1, 1), s.shape[:-1] + (1,))
        p = jax.nn.softmax(jnp.concatenate([s, sk], axis=-1), axis=-1)[..., :-1]
    else:
        p = jax.nn.softmax(s, axis=-1)
    return jnp.einsum('bkgqn,bnkd->bqkgd', p.astype(v.dtype), v)


def global_attention(q, k, v, kc, vc):
    b, s = q.shape[:2]
    nb = s // Q_BLOCK
    keys = jnp.concatenate([kc, k], axis=1)
    vals = jnp.concatenate([vc, v], axis=1)
    qb = jnp.moveaxis(q.reshape(b, nb, Q_BLOCK, *q.shape[2:]), 1, 0)
    ob = lax.map(lambda t: attend(t, keys, vals), qb)
    return jnp.moveaxis(ob, 0, 1).reshape(b, s, -1)


def window_attention(q, k, v, kc, vc, sink):
    b, s, hkv, g, hd = q.shape
    nb = s // Q_BLOCK
    qb = q.reshape(b, nb, Q_BLOCK, hkv, g, hd)

    def band(t):
        tb = jnp.pad(t.reshape(b, nb, Q_BLOCK, hkv, hd), ((0, 0), (1, 1), (0, 0), (0, 0), (0, 0)))
        return jnp.concatenate([tb[:, :-2], tb[:, 1:-1], tb[:, 2:]], axis=2)

    kw, vw = band(k), band(v)
    a = jnp.arange(Q_BLOCK)[:, None]
    m = jnp.arange(3 * Q_BLOCK)[None, :]
    in_band = jnp.abs(m - Q_BLOCK - a) <= WINDOW
    j = jnp.arange(nb)[:, None] * Q_BLOCK - Q_BLOCK + jnp.arange(3 * Q_BLOCK)[None, :]
    valid = (j >= 0) & (j < s)
    mask = in_band[None] & valid[:, None, :]
    s_win = jnp.einsum('bnqkgd,bnmkd->bnkgqm', qb, kw).astype(jnp.float32) * ATTN_SCALE
    s_win = jnp.where(mask[None, :, None, None], s_win, NEG_INF)
    s_ctx = jnp.einsum('bnqkgd,bckd->bnkgqc', qb, kc).astype(jnp.float32) * ATTN_SCALE
    sk = jnp.broadcast_to(sink.astype(jnp.float32).reshape(1, 1, hkv, g, 1, 1), s_ctx.shape[:-1] + (1,))
    p = jax.nn.softmax(jnp.concatenate([s_ctx, s_win, sk], axis=-1), axis=-1)
    n_ctx = kc.shape[1]
    p_ctx = p[..., :n_ctx].astype(v.dtype)
    p_win = p[..., n_ctx:n_ctx + 3 * Q_BLOCK].astype(v.dtype)
    o = jnp.einsum('bnkgqc,bckd->bnqkgd', p_ctx, vc) + jnp.einsum('bnkgqm,bnmkd->bnqkgd', p_win, vw)
    return o.reshape(b, s, hkv * g * hd)


def fourier_mix(za):
    b, l, _ = za.shape
    zg = za.reshape(b, l, FNET_GROUPS, FNET_GROUP_DIM).astype(jnp.float32)
    return jnp.fft.fft2(zg, axes=(1, 3), norm='ortho').real.astype(za.dtype).reshape(b, l, A_W)


def spatial_gating(zu, zv, norm_g, ws, bs):
    u = jax.nn.gelu(zu)
    v = rmsnorm(jax.nn.gelu(zv), norm_g)
    b, l, _ = zv.shape
    vg = v.reshape(b, l // GM_CHUNK, GM_CHUNK, GM_GROUPS, GM_GROUP_DIM)
    sv = jnp.einsum('gpq,bnqgc->bnpgc', ws, vg) + bs.T[None, None, :, :, None]
    return u * sv.reshape(b, l, GM_W)


def merge_branches(h, w_in, branches, w_br, w_o):
    y = None
    for i, o in enumerate(branches):
        gate = jax.nn.sigmoid(h @ w_in[:, OFF_GATE + i * D_MODEL:OFF_GATE + (i + 1) * D_MODEL])
        term = gate * (o @ w_br[i])
        y = term if y is None else y + term
    return y @ w_o


def token_mixer(h, hc, cos, sin, w_in, q_norm_g, k_norm_g, sink, gm_norm_g, gm_ws, gm_b, w_br, w_o, ctx_out):
    z = h @ w_in[:, :OFF_GATE]
    zc = hc @ w_in[:, :(OFF_GATE if ctx_out else KV_COLS)]
    kc_g = rmsnorm(heads(zc[..., OFF_GA_K:OFF_GA_K + GA_KV_W], GA_KV_HEADS), k_norm_g)
    vc_g = heads(zc[..., OFF_GA_V:OFF_GA_V + GA_KV_W], GA_KV_HEADS)
    kc_w = heads(zc[..., OFF_WA_K:OFF_WA_K + WA_KV_W], WA_KV_HEADS)
    vc_w = heads(zc[..., OFF_WA_V:OFF_WA_V + WA_KV_W], WA_KV_HEADS)
    o_a = fourier_mix(z[..., OFF_FN:OFF_FN + A_W])
    q_g = apply_axial_rope(rmsnorm(heads(z[..., OFF_GA_Q:OFF_GA_Q + GA_Q_W], GA_Q_HEADS), q_norm_g), cos, sin)
    k_g = apply_axial_rope(rmsnorm(heads(z[..., OFF_GA_K:OFF_GA_K + GA_KV_W], GA_KV_HEADS), k_norm_g), cos, sin)
    v_g = heads(z[..., OFF_GA_V:OFF_GA_V + GA_KV_W], GA_KV_HEADS)
    o_b = global_attention(group_q(q_g, GA_KV_HEADS), k_g, v_g, kc_g, vc_g)
    q_w = apply_axial_rope(heads(z[..., OFF_WA_Q:OFF_WA_Q + WA_Q_W], WA_Q_HEADS), cos, sin)
    k_w = apply_axial_rope(heads(z[..., OFF_WA_K:OFF_WA_K + WA_KV_W], WA_KV_HEADS), cos, sin)
    v_w = heads(z[..., OFF_WA_V:OFF_WA_V + WA_KV_W], WA_KV_HEADS)
    o_c = window_attention(group_q(q_w, WA_KV_HEADS), k_w, v_w, kc_w, vc_w, sink)
    o_d = spatial_gating(z[..., OFF_GM_U:OFF_GM_U + GM_W], z[..., OFF_GM_V:OFF_GM_V + GM_W], gm_norm_g, gm_ws, gm_b)
    y = merge_branches(h, w_in, [o_a, o_b, o_c, o_d], w_br, w_o)
    if not ctx_out:
        return y, None
    bsz, n_ctx, _ = hc.shape
    oc_a = fourier_mix(zc[..., OFF_FN:OFF_FN + A_W])
    qc_g = rmsnorm(heads(zc[..., OFF_GA_Q:OFF_GA_Q + GA_Q_W], GA_Q_HEADS), q_norm_g)
    oc_b = attend(group_q(qc_g, GA_KV_HEADS), kc_g, vc_g).reshape(bsz, n_ctx, GA_Q_W)
    qc_w = heads(zc[..., OFF_WA_Q:OFF_WA_Q + WA_Q_W], WA_Q_HEADS)
    oc_c = attend(group_q(qc_w, WA_KV_HEADS), kc_w, vc_w, sink).reshape(bsz, n_ctx, WA_Q_W)
    oc_d = spatial_gating(zc[..., OFF_GM_U:OFF_GM_U + GM_W], zc[..., OFF_GM_V:OFF_GM_V + GM_W], gm_norm_g, gm_ws, gm_b)
    yc = merge_branches(hc, w_in, [oc_a, oc_b, oc_c, oc_d], w_br, w_o)
    return y, yc


def moe_ffn(t, w_router, router_bias, wg, wu, wd, wsg, wsu, wsd):
    n = t.shape[0]
    scores = jax.nn.sigmoid((t @ w_router).astype(jnp.float32))
    choice = scores + router_bias.astype(jnp.float32)
    grp = choice.reshape(n, N_GROUPS, N_EXPERTS // N_GROUPS)
    grp_score = jnp.sum(lax.top_k(grp, 2)[0], axis=-1)
    _, gidx = lax.top_k(grp_score, TOPK_GROUPS)
    gmask = jnp.sum(jax.nn.one_hot(gidx, N_GROUPS, dtype=jnp.float32), axis=1) > 0
    emask = jnp.repeat(gmask, N_EXPERTS // N_GROUPS, axis=1)
    _, eidx = lax.top_k(jnp.where(emask, choice, NEG_INF), TOP_K)
    w = jnp.take_along_axis(scores, eidx, axis=1)
    w = w / jnp.sum(w, axis=-1, keepdims=True) * ROUTED_SCALE
    gates = jnp.sum(jax.nn.one_hot(eidx, N_EXPERTS, dtype=jnp.float32) * w[..., None], axis=1)
    shared = (jax.nn.silu(t @ wsg) * (t @ wsu)) @ wsd

    def expert_step(acc, xs):
        wg_e, wu_e, wd_e, g_e = xs
        hid = jax.nn.silu(t @ wg_e) * (t @ wu_e)
        return acc + (hid * g_e[:, None]) @ wd_e, None

    routed, _ = lax.scan(expert_step, jnp.zeros_like(t), (wg, wu, wd, gates.T.astype(t.dtype)))
    return shared + routed


def setup_inputs(seed: int = 0) -> dict:
    key = jax.random.key(seed)
    ks = jax.random.split(key, 32)
    f32 = jnp.float32
    L = DEPTH

    def nrm(k, shape, scale):
        return jax.random.normal(k, shape, f32) * scale

    return {
        'x': nrm(ks[0], (BATCH, SEQ, D_MODEL), 1.0),
        'c': nrm(ks[1], (BATCH, D_MODEL), 1.0),
        'ctx': nrm(ks[2], (BATCH, CTX_LEN, D_MODEL), 1.0),
        'c_ctx': nrm(ks[3], (D_MODEL,), 1.0),
        'w_mod': nrm(ks[4], (L, D_MODEL, 6 * D_MODEL), 0.5 * D_MODEL ** -0.5),
        'b_mod': nrm(ks[5], (L, 6 * D_MODEL), 0.02),
        'norm1_g': 1.0 + nrm(ks[6], (L, D_MODEL), 0.02),
        'norm2_g': 1.0 + nrm(ks[7], (L, D_MODEL), 0.02),
        'w_in': nrm(ks[8], (L, D_MODEL, IN_W), D_MODEL ** -0.5),
        'q_norm_g': 1.0 + nrm(ks[9], (L, HEAD_DIM), 0.02),
        'k_norm_g': 1.0 + nrm(ks[10], (L, HEAD_DIM), 0.02),
        'sink': nrm(ks[11], (L, WA_Q_HEADS), 0.5),
        'gm_norm_g': 1.0 + nrm(ks[12], (L, GM_W), 0.02),
        'gm_ws': nrm(ks[13], (L, GM_GROUPS, GM_CHUNK, GM_CHUNK), GM_CHUNK ** -0.5),
        'gm_b': 1.0 + nrm(ks[14], (L, GM_GROUPS, GM_CHUNK), 0.02),
        'w_br': nrm(ks[15], (L, N_BRANCHES, BRANCH_W, D_MODEL), BRANCH_W ** -0.5),
        'w_o': nrm(ks[16], (L, D_MODEL, D_MODEL), D_MODEL ** -0.5),
        'w_router': nrm(ks[17], (L, D_MODEL, N_EXPERTS), D_MODEL ** -0.5),
        'router_bias': nrm(ks[18], (L, N_EXPERTS), 0.01),
        'w_exp_gate': nrm(ks[19], (L, N_EXPERTS, D_MODEL, D_EXPERT), D_MODEL ** -0.5),
        'w_exp_up': nrm(ks[20], (L, N_EXPERTS, D_MODEL, D_EXPERT), D_MODEL ** -0.5),
        'w_exp_down': nrm(ks[21], (L, N_EXPERTS, D_EXPERT, D_MODEL), D_EXPERT ** -0.5),
        'w_sh_gate': nrm(ks[22], (L, D_MODEL, D_SHARED), D_MODEL ** -0.5),
        'w_sh_up': nrm(ks[23], (L, D_MODEL, D_SHARED), D_MODEL ** -0.5),
        'w_sh_down': nrm(ks[24], (L, D_SHARED, D_MODEL), D_SHARED ** -0.5),
        'final_norm_g': 1.0 + nrm(ks[25], (D_MODEL,), 0.02),
    }


def reference(x, c, ctx, c_ctx, w_mod, b_mod, norm1_g, norm2_g, w_in, q_norm_g, k_norm_g, sink, gm_norm_g, gm_ws, gm_b,
              w_br, w_o, w_router, router_bias, w_exp_gate, w_exp_up, w_exp_down, w_sh_gate, w_sh_up, w_sh_down,
              final_norm_g):
    bsz, seq, d = x.shape
    n_ctx = ctx.shape[1]
    cos, sin = axial_rope_tables(seq, x.dtype)
    xc = ctx
    for l in range(DEPTH):
        ctx_out = l < DEPTH - 1
        mod = jax.nn.silu(c) @ w_mod[l] + b_mod[l]
        modc = jax.nn.silu(c_ctx) @ w_mod[l] + b_mod[l]
        sh1, sc1, g1, sh2, sc2, g2 = [m[:, None, :] for m in jnp.split(mod, 6, axis=-1)]
        sh1c, sc1c, g1c, sh2c, sc2c, g2c = jnp.split(modc, 6, axis=-1)
        h = modulate(rmsnorm(x, norm1_g[l]), sh1, sc1)
        hc = modulate(rmsnorm(xc, norm1_g[l]), sh1c, sc1c)
        y, yc = token_mixer(h, hc, cos, sin, w_in[l], q_norm_g[l], k_norm_g[l], sink[l], gm_norm_g[l], gm_ws[l],
                            gm_b[l], w_br[l], w_o[l], ctx_out)
        x = x + g1 * y
        h2 = modulate(rmsnorm(x, norm2_g[l]), sh2, sc2).reshape(bsz * seq, d)
        moe_args = (w_router[l], router_bias[l], w_exp_gate[l], w_exp_up[l], w_exp_down[l], w_sh_gate[l],
                    w_sh_up[l], w_sh_down[l])
        if ctx_out:
            xc = xc + g1c * yc
            h2c = modulate(rmsnorm(xc, norm2_g[l]), sh2c, sc2c).reshape(bsz * n_ctx, d)
            f = moe_ffn(jnp.concatenate([h2, h2c], axis=0), *moe_args)
            x = x + g2 * f[:bsz * seq].reshape(bsz, seq, d)
            xc = xc + g2c * f[bsz * seq:].reshape(bsz, n_ctx, d)
        else:
            x = x + g2 * moe_ffn(h2, *moe_args).reshape(bsz, seq, d)
    return rmsnorm(x, final_norm_g)
```

```python
import functools
import math

import jax
import jax.numpy as jnp
import numpy as np
from jax import lax
from jax.experimental import pallas as pl
from jax.experimental.pallas import tpu as pltpu
from jax.experimental.pallas import tpu_sc as plsc

f32 = jnp.float32
bf16 = jnp.bfloat16
i32 = jnp.int32

D_MODEL = 1024
HEAD_DIM = 64
GRID_W = 64
ROPE_THETA = 10000.0
ATTN_SCALE = HEAD_DIM ** -0.5
RMS_EPS = 1e-6
NEG_INF = -1e30
Q_BLOCK = 128
WINDOW = 128
GM_CHUNK = 128
N_BRANCHES = 4
BRANCH_W = 256
KV_W = 128
OFF_KV = 0
OFF_Q = 512
OFF_FN = 1024
OFF_GM = 1280
OFF_GATE = 1792
N_EXPERTS = 64
TOP_K = 6
N_GROUPS = 8
GROUP_SIZE = N_EXPERTS // N_GROUPS
TOPK_GROUPS = 4
D_EXPERT = 256
ROUTED_SCALE = 2.5

SC_CORES = 2
SC_SUBCORES = 16
SC_WORKERS = SC_CORES * SC_SUBCORES
SC_IDX_CHUNK = 128
SC_ROWS = 32

ROW_TILE = 512
EXPERT_TILE = 256
MOD_ROWS = 24
VMEM_LIMIT = 56 * 1024 * 1024


def _params(sem, vmem=VMEM_LIMIT):
    return pltpu.CompilerParams(dimension_semantics=sem, vmem_limit_bytes=vmem)


def _const_spec(shape):
    nd = len(shape)
    return pl.BlockSpec(shape, lambda *_: (0,) * nd, pipeline_mode=pl.Buffered(1))


def _rms_mod(x, g, sc, sh):
    ms = jnp.mean(x * x, axis=-1, keepdims=True)
    return (x * lax.rsqrt(ms + RMS_EPS) * g) * (1.0 + sc) + sh


def _gelu(x):
    return 0.5 * x * (1.0 + jnp.tanh(math.sqrt(2.0 / math.pi) * (x + 0.044715 * (x * x * x))))


def _silu(x):
    return x * jax.nn.sigmoid(x)


def _mod_kernel(a_ref, w_ref, b_ref, o_ref):
    a = _silu(a_ref[...]).astype(bf16)
    o_ref[0] = jnp.dot(a, w_ref[0].astype(bf16), preferred_element_type=f32) + b_ref[0]


def compute_mod(cc, w_mod, b_mod):
    depth, d, n = w_mod.shape
    tn = 1536
    return pl.pallas_call(
        _mod_kernel,
        out_shape=jax.ShapeDtypeStruct((depth, MOD_ROWS, n), f32),
        grid=(depth, n // tn),
        in_specs=[
            pl.BlockSpec((MOD_ROWS, d), lambda l, j: (0, 0)),
            pl.BlockSpec((1, d, tn), lambda l, j: (l, 0, j)),
            pl.BlockSpec((1, 1, tn), lambda l, j: (l, 0, j)),
        ],
        out_specs=pl.BlockSpec((1, MOD_ROWS, tn), lambda l, j: (l, 0, j)),
        compiler_params=_params(("parallel", "parallel")),
        name="mod_proj",
    )(cc, w_mod, b_mod.reshape(depth, 1, n))


def _inproj_kernel(x_ref, sh_ref, sc_ref, g_ref, w_ref, qn_ref, kn_ref, bd_ref, gmg_ref, ws_ref, gb_ref, cs_ref,
                   cos_ref, sa_ref, sb_ref,
                   qg_ref, kg_ref, vg_ref, qw_ref, kw_ref, vw_ref, xc_ref, xs_ref, od_ref):
    tile = x_ref.shape[0]
    hb = _rms_mod(x_ref[...], g_ref[...], sc_ref[0], sh_ref[0]).astype(bf16)

    def proj(a, b):
        return jnp.dot(hb, w_ref[:, a:b], preferred_element_type=f32)

    def headnorm(t, gain):
        w = t.shape[1]
        sq = t * t
        hi = sq.astype(bf16)
        lo = (sq - hi.astype(f32)).astype(bf16)
        b = bd_ref[:w, :w]
        ms = jnp.dot(hi, b, preferred_element_type=f32) + jnp.dot(lo, b, preferred_element_type=f32)
        return t * lax.rsqrt(ms + RMS_EPS) * gain

    def rope(t):
        w = t.shape[1]
        return (t * cos_ref[:, :w] + pltpu.roll(t, w - 16, 1) * sa_ref[:, :w]
                + pltpu.roll(t, 16, 1) * sb_ref[:, :w])

    def expand_heads(q):
        lane = lax.broadcasted_iota(i32, (1, KV_W), 1)
        low = lane < HEAD_DIM
        blocks = []
        for kv in range(2):
            pair = q[:, KV_W * kv:KV_W * (kv + 1)]
            swapped = pltpu.roll(pair, HEAD_DIM, 1)
            keep = low if kv == 0 else jnp.logical_not(low)
            g0, g1 = (pair, swapped) if kv == 0 else (swapped, pair)
            blocks.append(jnp.where(keep, g0, 0.0))
            blocks.append(jnp.where(keep, g1, 0.0))
        return jnp.concatenate(blocks, axis=1)

    kv = proj(OFF_KV, OFF_Q)
    kg_ref[...] = rope(headnorm(kv[:, 0:128], kn_ref[...])).astype(bf16)
    vg_ref[...] = kv[:, 128:256].astype(bf16)
    kw_ref[...] = rope(kv[:, 256:384]).astype(bf16)
    vw_ref[...] = kv[:, 384:512].astype(bf16)

    qq = proj(OFF_Q, OFF_FN)
    qg = rope(headnorm(qq[:, :256], qn_ref[...])) * ATTN_SCALE
    qg_ref[...] = expand_heads(qg).astype(bf16)
    qw = rope(qq[:, 256:]) * ATTN_SCALE
    qw_ref[...] = expand_heads(qw).astype(bf16)

    fn = proj(OFF_FN, OFF_GM).astype(bf16)
    xcs = jnp.dot(fn, cs_ref[...], preferred_element_type=f32)
    xc_ref[...] = xcs[:, :256].astype(bf16)
    xs_ref[...] = xcs[:, 256:].astype(bf16)

    uv = proj(OFF_GM, OFF_GATE)
    u = _gelu(uv[:, :256])
    v = _gelu(uv[:, 256:])
    vms = jnp.mean(v * v, axis=-1, keepdims=True)
    vn = (v * lax.rsqrt(vms + RMS_EPS) * gmg_ref[...]).astype(bf16)
    lane_grp = lax.broadcasted_iota(i32, (1, 256), 1) // 64
    for c in range(tile // GM_CHUNK):
        rows = slice(c * GM_CHUNK, (c + 1) * GM_CHUNK)
        vch = vn[rows]
        sv = gb_ref[...]
        for g in range(4):
            r = jnp.dot(ws_ref[g], vch, preferred_element_type=f32)
            sv = sv + jnp.where(lane_grp == g, r, 0.0)
        od_ref[rows, :] = (u[rows] * sv).astype(bf16)


def in_projection(x2, sh, sc, rows_per_mod, tables, rows_per_seq, tile, lw):
    m, d = x2.shape
    cos_t, sa_t, sb_t = tables
    seq_blocks = rows_per_seq // tile
    row = lambda w: pl.BlockSpec((tile, w), lambda i: (i, 0))
    modspec = pl.BlockSpec((1, 1, d), lambda i: ((i * tile) // rows_per_mod, 0, 0))
    tabspec = pl.BlockSpec((tile, 256), lambda i: (i % seq_blocks, 0))
    out_w = [512, 128, 128, 512, 128, 128, 256, 256, 256]
    return pl.pallas_call(
        _inproj_kernel,
        out_shape=[jax.ShapeDtypeStruct((m, w), bf16) for w in out_w],
        grid=(m // tile,),
        in_specs=[
            row(d), modspec, modspec, _const_spec((1, d)), _const_spec((d, OFF_GATE)),
            _const_spec((1, 256)), _const_spec((1, 128)), _const_spec((256, 256)), _const_spec((1, 256)),
            _const_spec((4, GM_CHUNK, GM_CHUNK)), _const_spec((GM_CHUNK, 256)), _const_spec((256, 512)),
            tabspec, tabspec, tabspec,
        ],
        out_specs=[row(w) for w in out_w],
        compiler_params=_params(("parallel",)),
        name="in_projection",
    )(x2, sh, sc, lw["norm1_g"], lw["w_z"], lw["qn"], lw["kn"], lw["bd"], lw["gm_norm_g"], lw["gm_ws"], lw["gm_bias"],
      lw["cs64"], cos_t, sa_t, sb_t)


def _dft_kernel(wc_ref, ws_ref, xc_ref, xs_ref, o_ref):
    acc = jnp.dot(wc_ref[...], xc_ref[0], preferred_element_type=f32)
    acc = acc + jnp.dot(ws_ref[...], xs_ref[0], preferred_element_type=f32)
    o_ref[0] = acc.astype(bf16)


def dft_mix(wc, ws, xc, xs, tile):
    nb, length, w = xc.shape
    return pl.pallas_call(
        _dft_kernel,
        out_shape=jax.ShapeDtypeStruct((nb, length, w), bf16),
        grid=(length // tile, nb),
        in_specs=[
            pl.BlockSpec((tile, length), lambda i, b: (i, 0)),
            pl.BlockSpec((tile, length), lambda i, b: (i, 0)),
            pl.BlockSpec((1, length, w), lambda i, b: (b, 0, 0)),
            pl.BlockSpec((1, length, w), lambda i, b: (b, 0, 0)),
        ],
        out_specs=pl.BlockSpec((1, tile, w), lambda i, b: (b, i, 0)),
        compiler_params=_params(("parallel", "parallel")),
        name="dft_mix",
    )(wc, ws, xc, xs)


def dft_tables(length):
    jk = (np.arange(length)[:, None] * np.arange(length)[None, :]) % length
    ang = 2.0 * np.pi * jk / length
    s = 1.0 / math.sqrt(length)
    return jnp.asarray(np.cos(ang) * s, dtype=bf16), jnp.asarray(-np.sin(ang) * s, dtype=bf16)


def channel_dft_table():
    jk = (np.arange(64)[:, None] * np.arange(64)[None, :]) % 64
    ang = 2.0 * np.pi * jk / 64
    eye = np.eye(4)
    c = np.kron(eye, np.cos(ang) / 8.0)
    s = np.kron(eye, np.sin(ang) / 8.0)
    return jnp.asarray(np.concatenate([c, s], axis=1), dtype=bf16)


def _attend_pieces(q, pieces, sink_col):
    scores = []
    for k, _, mask in pieces:
        s = lax.dot_general(q, k, (((1,), (1,)), ((), ())), preferred_element_type=f32)
        if mask is not None:
            s = jnp.where(mask, s, NEG_INF)
        scores.append(s)
    m = scores[0].max(axis=-1, keepdims=True)
    for s in scores[1:]:
        m = jnp.maximum(m, s.max(axis=-1, keepdims=True))
    if sink_col is not None:
        m = jnp.maximum(m, sink_col)
        denom = jnp.exp(sink_col - m)
    else:
        denom = jnp.zeros_like(m)
    acc = None
    for s, (_, v, _) in zip(scores, pieces):
        p = jnp.exp(s - m)
        denom = denom + p.sum(axis=-1, keepdims=True)
        pv = jnp.dot(p.astype(bf16), v, preferred_element_type=f32)
        acc = pv if acc is None else acc + pv
    return acc * (1.0 / denom)


def _attend_heads(q_all, piece_fn, sink_ref, qb):
    lane = lax.broadcasted_iota(i32, (1, KV_W), 1)
    low = lane < HEAD_DIM
    halves = []
    for kv in range(2):
        q = jnp.concatenate([q_all[:, KV_W * (2 * kv):KV_W * (2 * kv + 1)],
                             q_all[:, KV_W * (2 * kv + 1):KV_W * (2 * kv + 2)]], axis=0)
        if sink_ref is not None:
            s0 = jnp.full((qb, 1), sink_ref[2 * kv], f32)
            s1 = jnp.full((qb, 1), sink_ref[2 * kv + 1], f32)
            sink_col = jnp.concatenate([s0, s1], axis=0)
        else:
            sink_col = None
        res = _attend_pieces(q, piece_fn(), sink_col)
        r0, r1 = res[:qb], res[qb:]
        if kv == 0:
            halves.append(jnp.where(low, r0, pltpu.roll(r1, HEAD_DIM, 1)))
        else:
            halves.append(jnp.where(low, pltpu.roll(r0, HEAD_DIM, 1), r1))
    return jnp.concatenate(halves, axis=1)


def _full_attn_kernel(*refs, n_pieces, has_sink, qb):
    pos = 0
    sink_ref = None
    if has_sink:
        sink_ref = refs[0]
        pos = 1
    q_ref = refs[pos]
    kv_refs = refs[pos + 1:pos + 1 + 2 * n_pieces]
    o_ref = refs[pos + 1 + 2 * n_pieces]

    def piece_fn():
        return [(kv_refs[2 * i][0], kv_refs[2 * i + 1][0], None) for i in range(n_pieces)]

    o_ref[0] = _attend_heads(q_ref[0], piece_fn, sink_ref, qb).astype(bf16)


def full_attention(q, pieces, sink, qb):
    nb, lq, _ = q.shape
    in_specs = []
    args = []
    if sink is not None:
        in_specs.append(pl.BlockSpec(memory_space=pltpu.SMEM))
        args.append(sink)
    in_specs.append(pl.BlockSpec((1, qb, 512), lambda b, i: (b, i, 0)))
    args.append(q)
    for k, v in pieces:
        spec = pl.BlockSpec((1, k.shape[1], KV_W), lambda b, i: (b, 0, 0))
        in_specs += [spec, spec]
        args += [k, v]
    return pl.pallas_call(
        functools.partial(_full_attn_kernel, n_pieces=len(pieces), has_sink=sink is not None, qb=qb),
        out_shape=jax.ShapeDtypeStruct((nb, lq, 256), bf16),
        grid=(nb, lq // qb),
        in_specs=in_specs,
        out_specs=pl.BlockSpec((1, qb, 256), lambda b, i: (b, i, 0)),
        compiler_params=_params(("parallel", "parallel")),
        name="full_attention",
    )(*args)


def _window_attn_kernel(sink_ref, q_ref, k_ref, v_ref, kc_ref, vc_ref, o_ref, *, seq):
    n = pl.program_id(1)
    span = 3 * Q_BLOCK
    start = pl.multiple_of(jnp.clip((n - 1) * Q_BLOCK, 0, seq - span), Q_BLOCK)
    kwin = k_ref[0, pl.ds(start, span), :]
    vwin = v_ref[0, pl.ds(start, span), :]
    row = lax.broadcasted_iota(i32, (2 * Q_BLOCK, span), 0) % Q_BLOCK + n * Q_BLOCK
    col = lax.broadcasted_iota(i32, (2 * Q_BLOCK, span), 1) + start
    mask = jnp.abs(row - col) <= WINDOW

    def piece_fn():
        return [(kc_ref[0], vc_ref[0], None), (kwin, vwin, mask)]

    o_ref[0] = _attend_heads(q_ref[0], piece_fn, sink_ref, Q_BLOCK).astype(bf16)


def window_attention(q, k, v, kc, vc, sink):
    nb, seq, _ = q.shape
    n_ctx = kc.shape[1]
    full = lambda l: pl.BlockSpec((1, l, KV_W), lambda b, i: (b, 0, 0))
    return pl.pallas_call(
        functools.partial(_window_attn_kernel, seq=seq),
        out_shape=jax.ShapeDtypeStruct((nb, seq, 256), bf16),
        grid=(nb, seq // Q_BLOCK),
        in_specs=[pl.BlockSpec(memory_space=pltpu.SMEM),
                  pl.BlockSpec((1, Q_BLOCK, 512), lambda b, i: (b, i, 0)),
                  full(seq), full(seq), full(n_ctx), full(n_ctx)],
        out_specs=pl.BlockSpec((1, Q_BLOCK, 256), lambda b, i: (b, i, 0)),
        compiler_params=_params(("parallel", "parallel")),
        name="window_attention",
    )(sink, q, k, v, kc, vc)


def _route(logits_t, bias_col):
    t = logits_t.shape[1]
    scores = jax.nn.sigmoid(logits_t)
    choice = scores + bias_col
    sub = lax.broadcasted_iota(i32, (GROUP_SIZE, t), 0)
    grp_score = []
    for g in range(N_GROUPS):
        cg = choice[g * GROUP_SIZE:(g + 1) * GROUP_SIZE]
        m1 = cg.max(axis=0, keepdims=True)
        first = jnp.min(jnp.where(cg == m1, sub, GROUP_SIZE), axis=0, keepdims=True)
        m2 = jnp.where(sub == first, -jnp.inf, cg).max(axis=0, keepdims=True)
        grp_score.append(m1 + m2)
    keep = []
    for g in range(N_GROUPS):
        beaten = jnp.zeros((1, t), i32)
        for o in range(N_GROUPS):
            if o == g:
                continue
            wins = (grp_score[o] > grp_score[g]) | ((grp_score[o] == grp_score[g]) & (o < g))
            beaten = beaten + wins.astype(i32)
        keep.append(jnp.broadcast_to(beaten < TOPK_GROUPS, (GROUP_SIZE, t)))
    masked = jnp.where(jnp.concatenate(keep, axis=0), choice, NEG_INF)
    eid = lax.broadcasted_iota(i32, (N_EXPERTS, t), 0)
    ids, wts = [], []
    for _ in range(TOP_K):
        m = masked.max(axis=0, keepdims=True)
        pick = jnp.min(jnp.where(masked == m, eid, N_EXPERTS), axis=0, keepdims=True)
        sel = eid == pick
        ids.append(pick)
        wts.append(jnp.sum(jnp.where(sel, scores, 0.0), axis=0, keepdims=True))
        masked = jnp.where(sel, -jnp.inf, masked)
    total = wts[0]
    for w in wts[1:]:
        total = total + w
    norm = ROUTED_SCALE / total
    pad_i = [jnp.zeros((1, t), i32)] * (8 - TOP_K)
    pad_w = [jnp.zeros((1, t), f32)] * (8 - TOP_K)
    return jnp.concatenate(ids + pad_i, axis=0), jnp.concatenate([w * norm for w in wts] + pad_w, axis=0)


def _merge_kernel(x_ref, sh_ref, sc_ref, g1_ref, sh2_ref, sc2_ref, n1_ref, n2_ref, oa_ref, ob_ref, oc_ref, od_ref,
                  wg_ref, wbr_ref, wo_ref, wr_ref, rb_ref, xo_ref, h2_ref, eid_ref, wt_ref):
    x = x_ref[...]
    hb = _rms_mod(x, n1_ref[...], sc_ref[0], sh_ref[0]).astype(bf16)
    y = None
    for i, o_ref in enumerate((oa_ref, ob_ref, oc_ref, od_ref)):
        gate = jax.nn.sigmoid(jnp.dot(hb, wg_ref[:, i * D_MODEL:(i + 1) * D_MODEL], preferred_element_type=f32))
        term = gate * jnp.dot(o_ref[...], wbr_ref[i], preferred_element_type=f32)
        y = term if y is None else y + term
    xn = x + g1_ref[0] * jnp.dot(y.astype(bf16), wo_ref[...], preferred_element_type=f32)
    xo_ref[...] = xn
    h2 = _rms_mod(xn, n2_ref[...], sc2_ref[0], sh2_ref[0])
    h2_ref[...] = h2
    logits_t = lax.dot_general(wr_ref[...], h2, (((1,), (1,)), ((), ())), preferred_element_type=f32,
                               precision=lax.Precision.HIGHEST)
    eid, wt = _route(logits_t, rb_ref[...])
    eid_ref[...] = eid
    wt_ref[...] = wt


def merge_and_route(x2, mods, rows_per_mod, branches, tile, lw):
    m, d = x2.shape
    row = lambda w: pl.BlockSpec((tile, w), lambda i: (i, 0))
    modspec = pl.BlockSpec((1, 1, d), lambda i: ((i * tile) // rows_per_mod, 0, 0))
    col = pl.BlockSpec((8, tile), lambda i: (0, i))
    return pl.pallas_call(
        _merge_kernel,
        out_shape=[jax.ShapeDtypeStruct((m, d), f32), jax.ShapeDtypeStruct((m, d), f32),
                   jax.ShapeDtypeStruct((8, m), i32), jax.ShapeDtypeStruct((8, m), f32)],
        grid=(m // tile,),
        in_specs=[row(d)] + [modspec] * 5 + [_const_spec((1, d)), _const_spec((1, d))] + [row(BRANCH_W)] * 4 + [
            _const_spec((d, N_BRANCHES * d)), _const_spec((N_BRANCHES, BRANCH_W, d)), _const_spec((d, d)),
            _const_spec((N_EXPERTS, d)), _const_spec((N_EXPERTS, 1))],
        out_specs=[row(d), row(d), col, col],
        compiler_params=_params(("parallel",)),
        name="merge_and_route",
    )(x2, mods["sh1"], mods["sc1"], mods["g1"], mods["sh2"], mods["sc2"], lw["norm1_g"], lw["norm2_g"], *branches,
      lw["w_gate"], lw["w_br"], lw["w_o"], lw["w_router_t"], lw["router_bias"])


def routing_plan(eid, p_max):
    m = eid.shape[1]
    n_ent = m * TOP_K
    e_flat = eid[:TOP_K].T.reshape(n_ent)
    counts = jnp.sum((e_flat[:, None] == jnp.arange(N_EXPERTS, dtype=i32)[None, :]).astype(i32), axis=0)
    padded = ((counts + EXPERT_TILE - 1) // EXPERT_TILE) * EXPERT_TILE
    ends = jnp.cumsum(padded)
    starts = ends - padded
    excl = jnp.cumsum(counts) - counts
    order = jnp.argsort(e_flat, stable=True).astype(i32)
    e_sorted = e_flat[order]
    n_tiles = p_max // EXPERT_TILE
    tile_start = jnp.arange(n_tiles, dtype=i32) * EXPERT_TILE
    tile_valid = tile_start < ends[-1]
    tile_exp = jnp.minimum(jnp.searchsorted(ends, tile_start, side="right").astype(i32), N_EXPERTS - 1)
    slot = jnp.arange(p_max, dtype=i32)
    slot_exp = jnp.repeat(tile_exp, EXPERT_TILE)
    rank = slot - starts[slot_exp]
    live = (rank < counts[slot_exp]) & jnp.repeat(tile_valid, EXPERT_TILE)
    src_entry = order[jnp.clip(excl[slot_exp] + rank, 0, n_ent - 1)]
    src_tok = jnp.where(live, src_entry // TOP_K, 0)
    pos_sorted = starts[e_sorted] + (jnp.arange(n_ent, dtype=i32) - excl[e_sorted])
    pos = jnp.zeros((n_ent,), i32).at[order].set(pos_sorted)
    return src_tok, pos, tile_exp, tile_valid.astype(i32)


def sc_gather_rows(table, idx):
    n_idx = idx.shape[0]
    d = table.shape[1]
    per_w = n_idx // SC_WORKERS
    n_chunks = per_w // SC_IDX_CHUNK
    n_sub = SC_IDX_CHUNK // SC_ROWS
    mesh = plsc.VectorSubcoreMesh(core_axis_name="core", subcore_axis_name="subcore")

    @functools.partial(
        pl.kernel,
        out_type=jax.ShapeDtypeStruct((n_idx, d), table.dtype),
        mesh=mesh,
        scratch_types=[
            pltpu.VMEM((SC_IDX_CHUNK,), i32),
            pltpu.VMEM((SC_ROWS, d), table.dtype),
            pltpu.SemaphoreType.DMA,
        ],
    )
    def gather(x_hbm, i_hbm, o_hbm, idx_v, rows_v, sem):
        wid = lax.axis_index("subcore") * SC_CORES + lax.axis_index("core")
        base = wid * per_w

        @pl.loop(0, n_chunks)
        def _(ci):
            cbase = base + ci * SC_IDX_CHUNK
            pltpu.sync_copy(i_hbm.at[pl.ds(cbase, SC_IDX_CHUNK)], idx_v)
            for j in range(n_sub):
                pltpu.async_copy(x_hbm.at[idx_v.at[pl.ds(j * SC_ROWS, SC_ROWS)]], rows_v, sem).wait()
                pltpu.sync_copy(rows_v, o_hbm.at[pl.ds(cbase + j * SC_ROWS, SC_ROWS)])

    return gather(table, idx)


def _expert_kernel(te_ref, tv_ref, x_ref, wg_ref, wu_ref, wd_ref, o_ref):
    i = pl.program_id(0)

    @pl.when(tv_ref[i] != 0)
    def _():
        xb = x_ref[...].astype(bf16)
        a = jnp.dot(xb, wg_ref[0], preferred_element_type=f32)
        b = jnp.dot(xb, wu_ref[0], preferred_element_type=f32)
        hid = (_silu(a) * b).astype(bf16)
        o_ref[...] = jnp.dot(hid, wd_ref[0], preferred_element_type=f32)

    @pl.when(tv_ref[i] == 0)
    def _():
        o_ref[...] = jnp.zeros_like(o_ref)


def grouped_experts(xs, tile_exp, tile_valid, wg, wu, wd):
    p, d = xs.shape
    grid_spec = pltpu.PrefetchScalarGridSpec(
        num_scalar_prefetch=2,
        grid=(p // EXPERT_TILE,),
        in_specs=[
            pl.BlockSpec((EXPERT_TILE, d), lambda i, te, tv: (i, 0)),
            pl.BlockSpec((1, d, D_EXPERT), lambda i, te, tv: (te[i], 0, 0)),
            pl.BlockSpec((1, d, D_EXPERT), lambda i, te, tv: (te[i], 0, 0)),
            pl.BlockSpec((1, D_EXPERT, d), lambda i, te, tv: (te[i], 0, 0)),
        ],
        out_specs=pl.BlockSpec((EXPERT_TILE, d), lambda i, te, tv: (i, 0)),
    )
    return pl.pallas_call(
        _expert_kernel,
        out_shape=jax.ShapeDtypeStruct((p, d), f32),
        grid_spec=grid_spec,
        compiler_params=_params(("arbitrary",)),
        name="grouped_experts",
    )(tile_exp, tile_valid, xs, wg, wu, wd)


def _combine_kernel(x_ref, yg_ref, wt_ref, g2_ref, sh2_ref, sc2_ref, n2_ref, wsg_ref, wsu_ref, wsd_ref, fg_ref, o_ref,
                    *, final):
    x = x_ref[...]
    hb = _rms_mod(x, n2_ref[...], sc2_ref[0], sh2_ref[0]).astype(bf16)
    a = jnp.dot(hb, wsg_ref[...], preferred_element_type=f32)
    b = jnp.dot(hb, wsu_ref[...], preferred_element_type=f32)
    f = jnp.dot((_silu(a) * b).astype(bf16), wsd_ref[...], preferred_element_type=f32)
    wt = wt_ref[...]
    for k in range(TOP_K):
        f = f + wt[:, k:k + 1] * yg_ref[:, k * D_MODEL:(k + 1) * D_MODEL]
    xo = x + g2_ref[0] * f
    if final:
        ms = jnp.mean(xo * xo, axis=-1, keepdims=True)
        xo = xo * lax.rsqrt(ms + RMS_EPS) * fg_ref[...]
    o_ref[...] = xo


def combine(x2, yg, wt_rows, mods, rows_per_mod, tile, lw, final_g, final):
    m, d = x2.shape
    row = lambda w: pl.BlockSpec((tile, w), lambda i: (i, 0))
    modspec = pl.BlockSpec((1, 1, d), lambda i: ((i * tile) // rows_per_mod, 0, 0))
    return pl.pallas_call(
        functools.partial(_combine_kernel, final=final),
        out_shape=jax.ShapeDtypeStruct((m, d), f32),
        grid=(m // tile,),
        in_specs=[row(d), row(TOP_K * d), row(8), modspec, modspec, modspec, _const_spec((1, d)),
                  _const_spec((d, D_EXPERT)), _const_spec((d, D_EXPERT)), _const_spec((D_EXPERT, d)),
                  _const_spec((1, d))],
        out_specs=row(d),
        compiler_params=_params(("parallel",)),
        name="combine",
    )(x2, yg, wt_rows, mods["g2"], mods["sh2"], mods["sc2"], lw["norm2_g"], lw["w_sh_gate"], lw["w_sh_up"],
      lw["w_sh_down"], final_g)


def moe_block(x2, h2, eid, wt, mods, rows_per_mod, tile, lw, final_g, final):
    m = x2.shape[0]
    p_max = m * TOP_K + N_EXPERTS * EXPERT_TILE
    src_tok, pos, tile_exp, tile_valid = routing_plan(eid, p_max)
    xs = sc_gather_rows(h2, src_tok)
    ys = grouped_experts(xs, tile_exp, tile_valid, lw["w_exp_gate"], lw["w_exp_up"], lw["w_exp_down"])
    yg = sc_gather_rows(ys, pos).reshape(m, TOP_K * D_MODEL)
    return combine(x2, yg, wt.T, mods, rows_per_mod, tile, lw, final_g, final)


def rope_tables(seq):
    rows = seq // GRID_W
    row = jnp.repeat(jnp.arange(rows), GRID_W).astype(f32)
    col = jnp.tile(jnp.arange(GRID_W), rows).astype(f32)
    axis_dim = HEAD_DIM // 2
    inv_freq = 1.0 / (ROPE_THETA ** (jnp.arange(0, axis_dim, 2, dtype=f32) / axis_dim))
    ang_r = row[:, None] * inv_freq
    ang_c = col[:, None] * inv_freq
    ang = jnp.concatenate([ang_r, ang_r, ang_c, ang_c], axis=-1)
    cos, sin = jnp.cos(ang), jnp.sin(ang)
    seg = (jnp.arange(HEAD_DIM) // 16) % 2
    sa = jnp.where(seg == 0, -sin, 0.0)
    sb = jnp.where(seg == 1, sin, 0.0)
    rep = lambda t: jnp.tile(t, (1, 4))
    return rep(cos), rep(sa), rep(sb)


def identity_rope_tables(rows):
    return jnp.ones((rows, 256), f32), jnp.zeros((rows, 256), f32), jnp.zeros((rows, 256), f32)


def kernel(x, c, ctx, c_ctx, w_mod, b_mod, norm1_g, norm2_g, w_in, q_norm_g, k_norm_g, sink, gm_norm_g, gm_ws, gm_b, w_br, w_o, w_router, router_bias, w_exp_gate, w_exp_up, w_exp_down, w_sh_gate, w_sh_up, w_sh_down, final_norm_g):
    bsz, seq, d = x.shape
    n_ctx = ctx.shape[1]
    depth = w_mod.shape[0]
    n_lat = bsz * seq
    n_cx = bsz * n_ctx

    cc = jnp.concatenate([c, c_ctx[None, :], jnp.zeros((MOD_ROWS - bsz - 1, d), f32)], axis=0)
    mod_all = compute_mod(cc, w_mod, b_mod)

    lat_tables = rope_tables(seq)
    ctx_tables = identity_rope_tables(n_ctx)
    wc_lat, ws_lat = dft_tables(seq)
    wc_ctx, ws_ctx = dft_tables(n_ctx)
    cs64 = channel_dft_table()
    bd = jnp.asarray(np.kron(np.eye(4), np.full((HEAD_DIM, HEAD_DIM), 1.0 / HEAD_DIM)), dtype=bf16)
    final_g = final_norm_g.reshape(1, d)

    xl = x.reshape(n_lat, d)
    xc = ctx.reshape(n_cx, d)
    for l in range(depth):
        ctx_out = l < depth - 1
        names = ("sh1", "sc1", "g1", "sh2", "sc2", "g2")
        mods_lat = {n: mod_all[l, :bsz, i * d:(i + 1) * d].reshape(bsz, 1, d) for i, n in enumerate(names)}
        mods_ctx = {n: mod_all[l, bsz:bsz + 1, i * d:(i + 1) * d].reshape(1, 1, d) for i, n in enumerate(names)}
        lw = {
            "norm1_g": norm1_g[l].reshape(1, d),
            "norm2_g": norm2_g[l].reshape(1, d),
            "w_z": w_in[l, :, :OFF_GATE].astype(bf16),
            "w_gate": w_in[l, :, OFF_GATE:].astype(bf16),
            "qn": jnp.tile(q_norm_g[l], 4).reshape(1, 256),
            "kn": jnp.tile(k_norm_g[l], 2).reshape(1, 128),
            "bd": bd,
            "gm_norm_g": gm_norm_g[l].reshape(1, 256),
            "gm_ws": gm_ws[l].astype(bf16),
            "gm_bias": jnp.repeat(gm_b[l].T, 64, axis=1),
            "cs64": cs64,
            "w_br": w_br[l].astype(bf16),
            "w_o": w_o[l].astype(bf16),
            "w_router_t": w_router[l].T,
            "router_bias": router_bias[l].reshape(N_EXPERTS, 1),
            "w_exp_gate": w_exp_gate[l].astype(bf16),
            "w_exp_up": w_exp_up[l].astype(bf16),
            "w_exp_down": w_exp_down[l].astype(bf16),
            "w_sh_gate": w_sh_gate[l].astype(bf16),
            "w_sh_up": w_sh_up[l].astype(bf16),
            "w_sh_down": w_sh_down[l].astype(bf16),
        }
        sink_l = sink[l]

        qg, kg, vg, qw, kw, vw, fxc, fxs, o_d = in_projection(
            xl, mods_lat["sh1"], mods_lat["sc1"], seq, lat_tables, seq, ROW_TILE, lw)
        cqg, ckg, cvg, cqw, ckw, cvw, cfxc, cfxs, co_d = in_projection(
            xc, mods_ctx["sh1"], mods_ctx["sc1"], n_cx, ctx_tables, n_ctx, n_ctx, lw)
        b3 = lambda t, rows: t.reshape(bsz, rows, t.shape[-1])
        ckg3, cvg3, ckw3, cvw3 = b3(ckg, n_ctx), b3(cvg, n_ctx), b3(ckw, n_ctx), b3(cvw, n_ctx)

        o_a = dft_mix(wc_lat, ws_lat, b3(fxc, seq), b3(fxs, seq), ROW_TILE).reshape(n_lat, BRANCH_W)
        o_b = full_attention(b3(qg, seq), [(ckg3, cvg3), (b3(kg, seq), b3(vg, seq))], None, 256).reshape(n_lat, BRANCH_W)
        o_c = window_attention(b3(qw, seq), b3(kw, seq), b3(vw, seq), ckw3, cvw3, sink_l).reshape(n_lat, BRANCH_W)
        xl, h2, eid, wt = merge_and_route(xl, mods_lat, seq, (o_a, o_b, o_c, o_d), ROW_TILE, lw)
        xl_next = moe_block(xl, h2, eid, wt, mods_lat, seq, ROW_TILE, lw, final_g, not ctx_out)

        if ctx_out:
            co_a = dft_mix(wc_ctx, ws_ctx, b3(cfxc, n_ctx), b3(cfxs, n_ctx), n_ctx).reshape(n_cx, BRANCH_W)
            co_b = full_attention(b3(cqg, n_ctx), [(ckg3, cvg3)], None, n_ctx).reshape(n_cx, BRANCH_W)
            co_c = full_attention(b3(cqw, n_ctx), [(ckw3, cvw3)], sink_l, n_ctx).reshape(n_cx, BRANCH_W)
            xc, ch2, ceid, cwt = merge_and_route(xc, mods_ctx, n_cx, (co_a, co_b, co_c, co_d), ROW_TILE, lw)
            xc = moe_block(xc, ch2, ceid, cwt, mods_ctx, n_cx, ROW_TILE, lw, final_g, False)
        xl = xl_next
    return xl.reshape(bsz, seq, d)
```

```python
import functools
import math

import jax
import jax.numpy as jnp
import numpy as np
from jax import lax
from jax.experimental import pallas as pl
from jax.experimental.pallas import tpu as pltpu
from jax.experimental.pallas import tpu_sc as plsc

f32 = jnp.float32
bf16 = jnp.bfloat16
i32 = jnp.int32

D_MODEL = 1024
HEAD_DIM = 64
GRID_W = 64
ROPE_THETA = 10000.0
ATTN_SCALE = HEAD_DIM ** -0.5
RMS_EPS = 1e-6
NEG_INF = -1e30
Q_BLOCK = 128
WINDOW = 128
GM_CHUNK = 128
N_BRANCHES = 4
BRANCH_W = 256
KV_W = 128
OFF_KV = 0
OFF_Q = 512
OFF_FN = 1024
OFF_GM = 1280
OFF_GATE = 1792
N_EXPERTS = 64
TOP_K = 6
N_GROUPS = 8
GROUP_SIZE = N_EXPERTS // N_GROUPS
TOPK_GROUPS = 4
D_EXPERT = 256
ROUTED_SCALE = 2.5

SC_CORES = 2
SC_SUBCORES = 16
SC_WORKERS = SC_CORES * SC_SUBCORES
SC_IDX_CHUNK = 128

ROW_TILE = 512
EXPERT_TILE = 512
MOD_ROWS = 24
VMEM_LIMIT = 56 * 1024 * 1024


def _params(sem, vmem=VMEM_LIMIT):
    return pltpu.CompilerParams(dimension_semantics=sem, vmem_limit_bytes=vmem)


def _const_spec(shape):
    nd = len(shape)
    return pl.BlockSpec(shape, lambda *_: (0,) * nd, pipeline_mode=pl.Buffered(1))


def _rms_mod(x, g, sc, sh):
    ms = jnp.mean(x * x, axis=-1, keepdims=True)
    return (x * lax.rsqrt(ms + RMS_EPS) * g) * (1.0 + sc) + sh


def _gelu(x):
    return 0.5 * x * (1.0 + jnp.tanh(math.sqrt(2.0 / math.pi) * (x + 0.044715 * (x * x * x))))


def _silu(x):
    return x * jax.nn.sigmoid(x)


def _mod_kernel(a_ref, w_ref, b_ref, o_ref):
    a = _silu(a_ref[...]).astype(bf16)
    o_ref[0] = jnp.dot(a, w_ref[0].astype(bf16), preferred_element_type=f32) + b_ref[0]


def compute_mod(cc, w_mod, b_mod):
    depth, d, n = w_mod.shape
    tn = 1536
    return pl.pallas_call(
        _mod_kernel,
        out_shape=jax.ShapeDtypeStruct((depth, MOD_ROWS, n), f32),
        grid=(depth, n // tn),
        in_specs=[
            pl.BlockSpec((MOD_ROWS, d), lambda l, j: (0, 0)),
            pl.BlockSpec((1, d, tn), lambda l, j: (l, 0, j)),
            pl.BlockSpec((1, 1, tn), lambda l, j: (l, 0, j)),
        ],
        out_specs=pl.BlockSpec((1, MOD_ROWS, tn), lambda l, j: (l, 0, j)),
        compiler_params=_params(("parallel", "parallel")),
        name="mod_proj",
    )(cc, w_mod, b_mod.reshape(depth, 1, n))


def _inproj_kernel(x_ref, sh_ref, sc_ref, g_ref, w_ref, qn_ref, kn_ref, bd_ref, gmg_ref, ws_ref, gb_ref, cs_ref,
                   cos_ref, sa_ref, sb_ref,
                   qg_ref, kg_ref, vg_ref, qw_ref, kw_ref, vw_ref, xc_ref, xs_ref, od_ref):
    tile = x_ref.shape[0]
    hb = _rms_mod(x_ref[...], g_ref[...], sc_ref[0], sh_ref[0]).astype(bf16)

    def proj(a, b):
        return jnp.dot(hb, w_ref[:, a:b], preferred_element_type=f32)

    def headnorm(t, gain):
        w = t.shape[1]
        sq = t * t
        hi = sq.astype(bf16)
        lo = (sq - hi.astype(f32)).astype(bf16)
        b = bd_ref[:w, :w]
        ms = jnp.dot(hi, b, preferred_element_type=f32) + jnp.dot(lo, b, preferred_element_type=f32)
        return t * lax.rsqrt(ms + RMS_EPS) * gain

    def rope(t):
        w = t.shape[1]
        return (t * cos_ref[:, :w] + pltpu.roll(t, w - 16, 1) * sa_ref[:, :w]
                + pltpu.roll(t, 16, 1) * sb_ref[:, :w])

    def expand_heads(q):
        lane = lax.broadcasted_iota(i32, (1, KV_W), 1)
        low = lane < HEAD_DIM
        blocks = []
        for kv in range(2):
            pair = q[:, KV_W * kv:KV_W * (kv + 1)]
            swapped = pltpu.roll(pair, HEAD_DIM, 1)
            keep = low if kv == 0 else jnp.logical_not(low)
            g0, g1 = (pair, swapped) if kv == 0 else (swapped, pair)
            blocks.append(jnp.where(keep, g0, 0.0))
            blocks.append(jnp.where(keep, g1, 0.0))
        return jnp.concatenate(blocks, axis=1)

    kv = proj(OFF_KV, OFF_Q)
    kg_ref[...] = rope(headnorm(kv[:, 0:128], kn_ref[...])).astype(bf16)
    vg_ref[...] = kv[:, 128:256].astype(bf16)
    kw_ref[...] = rope(kv[:, 256:384]).astype(bf16)
    vw_ref[...] = kv[:, 384:512].astype(bf16)

    qq = proj(OFF_Q, OFF_FN)
    qg = rope(headnorm(qq[:, :256], qn_ref[...])) * ATTN_SCALE
    qg_ref[...] = expand_heads(qg).astype(bf16)
    qw = rope(qq[:, 256:]) * ATTN_SCALE
    qw_ref[...] = expand_heads(qw).astype(bf16)

    fn = proj(OFF_FN, OFF_GM).astype(bf16)
    xcs = jnp.dot(fn, cs_ref[...], preferred_element_type=f32)
    xc_ref[...] = xcs[:, :256].astype(bf16)
    xs_ref[...] = xcs[:, 256:].astype(bf16)

    uv = proj(OFF_GM, OFF_GATE)
    u = _gelu(uv[:, :256])
    v = _gelu(uv[:, 256:])
    vms = jnp.mean(v * v, axis=-1, keepdims=True)
    vn = (v * lax.rsqrt(vms + RMS_EPS) * gmg_ref[...]).astype(bf16)
    lane_grp = lax.broadcasted_iota(i32, (1, 256), 1) // 64
    for c in range(tile // GM_CHUNK):
        rows = slice(c * GM_CHUNK, (c + 1) * GM_CHUNK)
        vch = vn[rows]
        sv = gb_ref[...]
        for g in range(4):
            r = jnp.dot(ws_ref[g], vch, preferred_element_type=f32)
            sv = sv + jnp.where(lane_grp == g, r, 0.0)
        od_ref[rows, :] = (u[rows] * sv).astype(bf16)


def in_projection(x2, sh, sc, rows_per_mod, tables, rows_per_seq, tile, lw):
    m, d = x2.shape
    cos_t, sa_t, sb_t = tables
    seq_blocks = rows_per_seq // tile
    row = lambda w: pl.BlockSpec((tile, w), lambda i: (i, 0))
    modspec = pl.BlockSpec((1, 1, d), lambda i: ((i * tile) // rows_per_mod, 0, 0))
    tabspec = pl.BlockSpec((tile, 256), lambda i: (i % seq_blocks, 0))
    out_w = [512, 128, 128, 512, 128, 128, 256, 256, 256]
    return pl.pallas_call(
        _inproj_kernel,
        out_shape=[jax.ShapeDtypeStruct((m, w), bf16) for w in out_w],
        grid=(m // tile,),
        in_specs=[
            row(d), modspec, modspec, _const_spec((1, d)), _const_spec((d, OFF_GATE)),
            _const_spec((1, 256)), _const_spec((1, 128)), _const_spec((256, 256)), _const_spec((1, 256)),
            _const_spec((4, GM_CHUNK, GM_CHUNK)), _const_spec((GM_CHUNK, 256)), _const_spec((256, 512)),
            tabspec, tabspec, tabspec,
        ],
        out_specs=[row(w) for w in out_w],
        compiler_params=_params(("parallel",)),
        name="in_projection",
    )(x2, sh, sc, lw["norm1_g"], lw["w_z"], lw["qn"], lw["kn"], lw["bd"], lw["gm_norm_g"], lw["gm_ws"], lw["gm_bias"],
      lw["cs64"], cos_t, sa_t, sb_t)


def _dft_kernel(wc_ref, ws_ref, xc_ref, xs_ref, o_ref):
    acc = jnp.dot(wc_ref[...], xc_ref[0], preferred_element_type=f32)
    acc = acc + jnp.dot(ws_ref[...], xs_ref[0], preferred_element_type=f32)
    o_ref[0] = acc.astype(bf16)


def dft_mix(wc, ws, xc, xs, tile):
    nb, length, w = xc.shape
    return pl.pallas_call(
        _dft_kernel,
        out_shape=jax.ShapeDtypeStruct((nb, length, w), bf16),
        grid=(length // tile, nb),
        in_specs=[
            pl.BlockSpec((tile, length), lambda i, b: (i, 0)),
            pl.BlockSpec((tile, length), lambda i, b: (i, 0)),
            pl.BlockSpec((1, length, w), lambda i, b: (b, 0, 0)),
            pl.BlockSpec((1, length, w), lambda i, b: (b, 0, 0)),
        ],
        out_specs=pl.BlockSpec((1, tile, w), lambda i, b: (b, i, 0)),
        compiler_params=_params(("parallel", "parallel")),
        name="dft_mix",
    )(wc, ws, xc, xs)


def dft_tables(length):
    jk = (np.arange(length)[:, None] * np.arange(length)[None, :]) % length
    ang = 2.0 * np.pi * jk / length
    s = 1.0 / math.sqrt(length)
    return jnp.asarray(np.cos(ang) * s, dtype=bf16), jnp.asarray(-np.sin(ang) * s, dtype=bf16)


def channel_dft_table():
    jk = (np.arange(64)[:, None] * np.arange(64)[None, :]) % 64
    ang = 2.0 * np.pi * jk / 64
    eye = np.eye(4)
    c = np.kron(eye, np.cos(ang) / 8.0)
    s = np.kron(eye, np.sin(ang) / 8.0)
    return jnp.asarray(np.concatenate([c, s], axis=1), dtype=bf16)


def _attend_pieces(q, pieces, sink_col):
    scores = []
    for k, _, mask in pieces:
        s = lax.dot_general(q, k, (((1,), (1,)), ((), ())), preferred_element_type=f32)
        if mask is not None:
            s = jnp.where(mask, s, NEG_INF)
        scores.append(s)
    m = scores[0].max(axis=-1, keepdims=True)
    for s in scores[1:]:
        m = jnp.maximum(m, s.max(axis=-1, keepdims=True))
    if sink_col is not None:
        m = jnp.maximum(m, sink_col)
        denom = jnp.exp(sink_col - m)
    else:
        denom = jnp.zeros_like(m)
    acc = None
    for s, (_, v, _) in zip(scores, pieces):
        p = jnp.exp(s - m)
        denom = denom + p.sum(axis=-1, keepdims=True)
        pv = jnp.dot(p.astype(bf16), v, preferred_element_type=f32)
        acc = pv if acc is None else acc + pv
    return acc * (1.0 / denom)


def _attend_heads(q_all, piece_fn, sink_ref, qb):
    lane = lax.broadcasted_iota(i32, (1, KV_W), 1)
    low = lane < HEAD_DIM
    halves = []
    for kv in range(2):
        q = jnp.concatenate([q_all[:, KV_W * (2 * kv):KV_W * (2 * kv + 1)],
                             q_all[:, KV_W * (2 * kv + 1):KV_W * (2 * kv + 2)]], axis=0)
        if sink_ref is not None:
            s0 = jnp.full((qb, 1), sink_ref[2 * kv], f32)
            s1 = jnp.full((qb, 1), sink_ref[2 * kv + 1], f32)
            sink_col = jnp.concatenate([s0, s1], axis=0)
        else:
            sink_col = None
        res = _attend_pieces(q, piece_fn(), sink_col)
        r0, r1 = res[:qb], res[qb:]
        if kv == 0:
            halves.append(jnp.where(low, r0, pltpu.roll(r1, HEAD_DIM, 1)))
        else:
            halves.append(jnp.where(low, pltpu.roll(r0, HEAD_DIM, 1), r1))
    return jnp.concatenate(halves, axis=1)


def _full_attn_kernel(*refs, n_pieces, has_sink, qb):
    pos = 0
    sink_ref = None
    if has_sink:
        sink_ref = refs[0]
        pos = 1
    q_ref = refs[pos]
    kv_refs = refs[pos + 1:pos + 1 + 2 * n_pieces]
    o_ref = refs[pos + 1 + 2 * n_pieces]

    def piece_fn():
        return [(kv_refs[2 * i][0], kv_refs[2 * i + 1][0], None) for i in range(n_pieces)]

    o_ref[0] = _attend_heads(q_ref[0], piece_fn, sink_ref, qb).astype(bf16)


def full_attention(q, pieces, sink, qb):
    nb, lq, _ = q.shape
    in_specs = []
    args = []
    if sink is not None:
        in_specs.append(pl.BlockSpec(memory_space=pltpu.SMEM))
        args.append(sink)
    in_specs.append(pl.BlockSpec((1, qb, 512), lambda b, i: (b, i, 0)))
    args.append(q)
    for k, v in pieces:
        spec = pl.BlockSpec((1, k.shape[1], KV_W), lambda b, i: (b, 0, 0))
        in_specs += [spec, spec]
        args += [k, v]
    return pl.pallas_call(
        functools.partial(_full_attn_kernel, n_pieces=len(pieces), has_sink=sink is not None, qb=qb),
        out_shape=jax.ShapeDtypeStruct((nb, lq, 256), bf16),
        grid=(nb, lq // qb),
        in_specs=in_specs,
        out_specs=pl.BlockSpec((1, qb, 256), lambda b, i: (b, i, 0)),
        compiler_params=_params(("parallel", "parallel")),
        name="full_attention",
    )(*args)


def _window_attn_kernel(sink_ref, q_ref, k_ref, v_ref, kc_ref, vc_ref, o_ref, *, seq):
    n = pl.program_id(1)
    span = 3 * Q_BLOCK
    start = pl.multiple_of(jnp.clip((n - 1) * Q_BLOCK, 0, seq - span), Q_BLOCK)
    kwin = k_ref[0, pl.ds(start, span), :]
    vwin = v_ref[0, pl.ds(start, span), :]
    row = lax.broadcasted_iota(i32, (2 * Q_BLOCK, span), 0) % Q_BLOCK + n * Q_BLOCK
    col = lax.broadcasted_iota(i32, (2 * Q_BLOCK, span), 1) + start
    mask = jnp.abs(row - col) <= WINDOW

    def piece_fn():
        return [(kc_ref[0], vc_ref[0], None), (kwin, vwin, mask)]

    o_ref[0] = _attend_heads(q_ref[0], piece_fn, sink_ref, Q_BLOCK).astype(bf16)


def window_attention(q, k, v, kc, vc, sink):
    nb, seq, _ = q.shape
    n_ctx = kc.shape[1]
    full = lambda l: pl.BlockSpec((1, l, KV_W), lambda b, i: (b, 0, 0))
    return pl.pallas_call(
        functools.partial(_window_attn_kernel, seq=seq),
        out_shape=jax.ShapeDtypeStruct((nb, seq, 256), bf16),
        grid=(nb, seq // Q_BLOCK),
        in_specs=[pl.BlockSpec(memory_space=pltpu.SMEM),
                  pl.BlockSpec((1, Q_BLOCK, 512), lambda b, i: (b, i, 0)),
                  full(seq), full(seq), full(n_ctx), full(n_ctx)],
        out_specs=pl.BlockSpec((1, Q_BLOCK, 256), lambda b, i: (b, i, 0)),
        compiler_params=_params(("parallel", "parallel")),
        name="window_attention",
    )(sink, q, k, v, kc, vc)


def _route(logits_t, bias_col):
    t = logits_t.shape[1]
    scores = jax.nn.sigmoid(logits_t)
    choice = scores + bias_col
    sub = lax.broadcasted_iota(i32, (GROUP_SIZE, t), 0)
    grp_score = []
    for g in range(N_GROUPS):
        cg = choice[g * GROUP_SIZE:(g + 1) * GROUP_SIZE]
        m1 = cg.max(axis=0, keepdims=True)
        first = jnp.min(jnp.where(cg == m1, sub, GROUP_SIZE), axis=0, keepdims=True)
        m2 = jnp.where(sub == first, -jnp.inf, cg).max(axis=0, keepdims=True)
        grp_score.append(m1 + m2)
    keep = []
    for g in range(N_GROUPS):
        beaten = jnp.zeros((1, t), i32)
        for o in range(N_GROUPS):
            if o == g:
                continue
            wins = (grp_score[o] > grp_score[g]) | ((grp_score[o] == grp_score[g]) & (o < g))
            beaten = beaten + wins.astype(i32)
        keep.append(jnp.broadcast_to(beaten < TOPK_GROUPS, (GROUP_SIZE, t)))
    masked = jnp.where(jnp.concatenate(keep, axis=0), choice, NEG_INF)
    eid = lax.broadcasted_iota(i32, (N_EXPERTS, t), 0)
    ids, wts = [], []
    for _ in range(TOP_K):
        m = masked.max(axis=0, keepdims=True)
        pick = jnp.min(jnp.where(masked == m, eid, N_EXPERTS), axis=0, keepdims=True)
        sel = eid == pick
        ids.append(pick)
        wts.append(jnp.sum(jnp.where(sel, scores, 0.0), axis=0, keepdims=True))
        masked = jnp.where(sel, -jnp.inf, masked)
    total = wts[0]
    for w in wts[1:]:
        total = total + w
    norm = ROUTED_SCALE / total
    return ids, [w * norm for w in wts]


def _pack_bf16_pairs(x):
    w = x.shape[1] // 2
    lo = lax.bitcast_convert_type(x[:, :w].astype(bf16).astype(f32), i32)
    hi = lax.bitcast_convert_type(x[:, w:].astype(bf16).astype(f32), i32)
    return lax.shift_right_logical(lo, 16) | (hi & jnp.int32(-65536))


def _unpack_bf16_pairs(p):
    lo = lax.bitcast_convert_type(lax.shift_left(p, 16), f32)
    hi = lax.bitcast_convert_type(p & jnp.int32(-65536), f32)
    return lo.astype(bf16), hi.astype(bf16)


def _merge_kernel(x_ref, sh_ref, sc_ref, g1_ref, sh2_ref, sc2_ref, n1_ref, n2_ref, oa_ref, ob_ref, oc_ref, od_ref,
                  wg_ref, wbr_ref, wo_ref, wr_ref, rb_ref, xo_ref, h2_ref, eid_ref, wt_ref, rank_ref, cnt_ref,
                  run_ref):
    @pl.when(pl.program_id(0) == 0)
    def _():
        run_ref[...] = jnp.zeros_like(run_ref)

    x = x_ref[...]
    hb = _rms_mod(x, n1_ref[...], sc_ref[0], sh_ref[0]).astype(bf16)
    y = None
    for i, o_ref in enumerate((oa_ref, ob_ref, oc_ref, od_ref)):
        gate = jax.nn.sigmoid(jnp.dot(hb, wg_ref[:, i * D_MODEL:(i + 1) * D_MODEL], preferred_element_type=f32))
        term = gate * jnp.dot(o_ref[...], wbr_ref[i], preferred_element_type=f32)
        y = term if y is None else y + term
    xn = x + g1_ref[0] * jnp.dot(y.astype(bf16), wo_ref[...], preferred_element_type=f32)
    xo_ref[...] = xn
    h2 = _rms_mod(xn, n2_ref[...], sc2_ref[0], sh2_ref[0])
    h2_ref[...] = _pack_bf16_pairs(h2)
    logits_t = lax.dot_general(wr_ref[...], h2, (((1,), (1,)), ((), ())), preferred_element_type=f32,
                               precision=lax.Precision.HIGHEST)
    ids, wts = _route(logits_t, rb_ref[...])

    t = x.shape[0]
    eid = lax.broadcasted_iota(i32, (N_EXPERTS, t), 0)
    hits = [eid == pick for pick in ids]
    chosen = hits[0]
    for h in hits[1:]:
        chosen = chosen | h
    chosen = jnp.where(chosen, 1.0, 0.0)
    before = lax.broadcasted_iota(i32, (t, t), 0) < lax.broadcasted_iota(i32, (t, t), 1)
    prefix = jnp.dot(chosen.astype(bf16), jnp.where(before, 1.0, 0.0).astype(bf16), preferred_element_type=f32)
    offset = run_ref[...] + prefix
    ranks = [jnp.sum(jnp.where(h, offset, 0.0), axis=0, keepdims=True).astype(i32) for h in hits]
    run_ref[...] += jnp.sum(chosen, axis=1, keepdims=True)
    cnt_ref[...] = run_ref[...].astype(i32)

    pad_i = [jnp.zeros((1, t), i32)] * (8 - TOP_K)
    eid_ref[...] = jnp.concatenate(ids + pad_i, axis=0)
    rank_ref[...] = jnp.concatenate(ranks + pad_i, axis=0)
    wt_ref[...] = jnp.concatenate(wts + [jnp.zeros((1, t), f32)] * (8 - TOP_K), axis=0)


def merge_and_route(x2, mods, rows_per_mod, branches, tile, lw):
    m, d = x2.shape
    row = lambda w: pl.BlockSpec((tile, w), lambda i: (i, 0))
    modspec = pl.BlockSpec((1, 1, d), lambda i: ((i * tile) // rows_per_mod, 0, 0))
    col = pl.BlockSpec((8, tile), lambda i: (0, i))
    return pl.pallas_call(
        _merge_kernel,
        out_shape=[jax.ShapeDtypeStruct((m, d), f32), jax.ShapeDtypeStruct((m, d // 2), i32),
                   jax.ShapeDtypeStruct((8, m), i32), jax.ShapeDtypeStruct((8, m), f32),
                   jax.ShapeDtypeStruct((8, m), i32), jax.ShapeDtypeStruct((N_EXPERTS, 1), i32)],
        grid=(m // tile,),
        in_specs=[row(d)] + [modspec] * 5 + [_const_spec((1, d)), _const_spec((1, d))] + [row(BRANCH_W)] * 4 + [
            _const_spec((d, N_BRANCHES * d)), _const_spec((N_BRANCHES, BRANCH_W, d)), _const_spec((d, d)),
            _const_spec((N_EXPERTS, d)), _const_spec((N_EXPERTS, 1))],
        out_specs=[row(d), row(d // 2), col, col, col, pl.BlockSpec((N_EXPERTS, 1), lambda i: (0, 0))],
        scratch_shapes=[pltpu.VMEM((N_EXPERTS, 1), f32)],
        compiler_params=_params(("arbitrary",)),
        name="merge_and_route",
    )(x2, mods["sh1"], mods["sc1"], mods["g1"], mods["sh2"], mods["sc2"], lw["norm1_g"], lw["norm2_g"], *branches,
      lw["w_gate"], lw["w_br"], lw["w_o"], lw["w_router_t"], lw["router_bias"])


def routing_plan(eid, rank, counts, p_max):
    counts = counts.reshape(N_EXPERTS)
    padded = ((counts + EXPERT_TILE - 1) // EXPERT_TILE) * EXPERT_TILE
    ends = jnp.cumsum(padded)
    starts = ends - padded
    onehot = eid[:, :, None] == jnp.arange(N_EXPERTS, dtype=i32)[None, None, :]
    pos = rank + jnp.sum(jnp.where(onehot, starts[None, None, :], 0), axis=-1)
    n_tiles = p_max // EXPERT_TILE
    tile_start = jnp.arange(n_tiles, dtype=i32) * EXPERT_TILE
    tile_valid = tile_start < ends[-1]
    tile_exp = jnp.sum((ends[None, :] <= tile_start[:, None]).astype(i32), axis=1)
    return pos.astype(i32), jnp.minimum(tile_exp, N_EXPERTS - 1), tile_valid.astype(i32)


def _sc_worker_id():
    return lax.axis_index("subcore") * SC_CORES + lax.axis_index("core")


def sc_scatter_rows(table, pos, p_rows):
    m, w = table.shape
    per_w = m // SC_WORKERS
    n_chunks = per_w // SC_IDX_CHUNK
    pos3 = pos.reshape(8, m // SC_IDX_CHUNK, SC_IDX_CHUNK).transpose(1, 0, 2)
    mesh = plsc.VectorSubcoreMesh(core_axis_name="core", subcore_axis_name="subcore")

    @functools.partial(
        pl.kernel,
        out_type=jax.ShapeDtypeStruct((p_rows, w), table.dtype),
        mesh=mesh,
        scratch_types=[
            pltpu.VMEM((8, SC_IDX_CHUNK), i32),
            pltpu.VMEM((SC_IDX_CHUNK, w), table.dtype),
            pltpu.SemaphoreType.DMA,
        ],
    )
    def scatter(x_hbm, p_hbm, o_hbm, idx_v, rows_v, sem):
        wid = _sc_worker_id()

        @pl.loop(0, n_chunks)
        def _(ci):
            chunk = wid * n_chunks + ci
            pltpu.sync_copy(p_hbm.at[chunk], idx_v)
            pltpu.sync_copy(x_hbm.at[pl.ds(chunk * SC_IDX_CHUNK, SC_IDX_CHUNK)], rows_v)
            copies = [pltpu.async_copy(rows_v, o_hbm.at[idx_v.at[k]], sem) for k in range(TOP_K)]
            for cp in copies:
                cp.wait()

    return scatter(table, pos3)


def sc_gather_rows(table, idx):
    n_idx = idx.shape[0]
    w = table.shape[1]
    per_w = n_idx // SC_WORKERS
    n_chunks = per_w // SC_IDX_CHUNK
    half = SC_IDX_CHUNK // 2
    mesh = plsc.VectorSubcoreMesh(core_axis_name="core", subcore_axis_name="subcore")

    @functools.partial(
        pl.kernel,
        out_type=jax.ShapeDtypeStruct((n_idx, w), table.dtype),
        mesh=mesh,
        scratch_types=[
            pltpu.VMEM((SC_IDX_CHUNK,), i32),
            pltpu.VMEM((half, w), table.dtype),
            pltpu.VMEM((half, w), table.dtype),
            pltpu.SemaphoreType.DMA,
            pltpu.SemaphoreType.DMA,
            pltpu.SemaphoreType.DMA,
            pltpu.SemaphoreType.DMA,
        ],
    )
    def gather(x_hbm, i_hbm, o_hbm, idx_v, buf0, buf1, g0_sem, g1_sem, w0_sem, w1_sem):
        base = _sc_worker_id() * per_w

        @pl.loop(0, n_chunks)
        def _(ci):
            cbase = base + ci * SC_IDX_CHUNK
            pltpu.sync_copy(i_hbm.at[pl.ds(cbase, SC_IDX_CHUNK)], idx_v)
            g0 = pltpu.async_copy(x_hbm.at[idx_v.at[pl.ds(0, half)]], buf0, g0_sem)
            g1 = pltpu.async_copy(x_hbm.at[idx_v.at[pl.ds(half, half)]], buf1, g1_sem)
            g0.wait()
            w0 = pltpu.async_copy(buf0, o_hbm.at[pl.ds(cbase, half)], w0_sem)
            g1.wait()
            w1 = pltpu.async_copy(buf1, o_hbm.at[pl.ds(cbase + half, half)], w1_sem)
            w0.wait()
            w1.wait()

    return gather(table, idx)


def _expert_kernel(te_ref, tv_ref, x_ref, wg_ref, wu_ref, wd_ref, o_ref, wg_b, wu_b, wd_b):
    i = pl.program_id(0)

    @pl.when((i == 0) | (te_ref[i] != te_ref[jnp.maximum(i - 1, 0)]))
    def _():
        wg_b[...] = wg_ref[0, 0].astype(bf16)
        wu_b[...] = wu_ref[0, 0].astype(bf16)
        wd_b[...] = wd_ref[0, 0].astype(bf16)

    @pl.when(tv_ref[i] != 0)
    def _():
        lo, hi = _unpack_bf16_pairs(x_ref[...])
        half = lo.shape[1]
        a = (jnp.dot(lo, wg_b[:half], preferred_element_type=f32)
             + jnp.dot(hi, wg_b[half:], preferred_element_type=f32))
        b = (jnp.dot(lo, wu_b[:half], preferred_element_type=f32)
             + jnp.dot(hi, wu_b[half:], preferred_element_type=f32))
        hid = (_silu(a) * b).astype(bf16)
        o_ref[...] = _pack_bf16_pairs(jnp.dot(hid, wd_b[...], preferred_element_type=f32))

    @pl.when(tv_ref[i] == 0)
    def _():
        o_ref[...] = jnp.zeros_like(o_ref)


def grouped_experts(xs, tile_exp, tile_valid, wg, wu, wd, layer):
    p, half = xs.shape
    d = 2 * half
    wspec = lambda a, b: pl.BlockSpec((1, 1, a, b), lambda i, te, tv: (layer, te[i], 0, 0))
    grid_spec = pltpu.PrefetchScalarGridSpec(
        num_scalar_prefetch=2,
        grid=(p // EXPERT_TILE,),
        in_specs=[
            pl.BlockSpec((EXPERT_TILE, half), lambda i, te, tv: (i, 0)),
            wspec(d, D_EXPERT), wspec(d, D_EXPERT), wspec(D_EXPERT, d),
        ],
        out_specs=pl.BlockSpec((EXPERT_TILE, half), lambda i, te, tv: (i, 0)),
        scratch_shapes=[pltpu.VMEM((d, D_EXPERT), bf16), pltpu.VMEM((d, D_EXPERT), bf16),
                        pltpu.VMEM((D_EXPERT, d), bf16)],
    )
    return pl.pallas_call(
        _expert_kernel,
        out_shape=jax.ShapeDtypeStruct((p, half), i32),
        grid_spec=grid_spec,
        compiler_params=_params(("arbitrary",)),
        name="grouped_experts",
    )(tile_exp, tile_valid, xs, wg, wu, wd)


def _combine_kernel(x_ref, yg_ref, wt_ref, g2_ref, sh2_ref, sc2_ref, n2_ref, wsg_ref, wsu_ref, wsd_ref, fg_ref, o_ref,
                    *, final):
    x = x_ref[...]
    hb = _rms_mod(x, n2_ref[...], sc2_ref[0], sh2_ref[0]).astype(bf16)
    a = jnp.dot(hb, wsg_ref[...], preferred_element_type=f32)
    b = jnp.dot(hb, wsu_ref[...], preferred_element_type=f32)
    f = jnp.dot((_silu(a) * b).astype(bf16), wsd_ref[...], preferred_element_type=f32)
    wt = wt_ref[...]
    half = x.shape[1] // 2
    f_lo, f_hi = f[:, :half], f[:, half:]
    for k in range(TOP_K):
        lo, hi = _unpack_bf16_pairs(yg_ref[k])
        w = wt[:, k:k + 1]
        f_lo = f_lo + w * lo.astype(f32)
        f_hi = f_hi + w * hi.astype(f32)
    xo = x + g2_ref[0] * jnp.concatenate([f_lo, f_hi], axis=1)
    if final:
        ms = jnp.mean(xo * xo, axis=-1, keepdims=True)
        xo = xo * lax.rsqrt(ms + RMS_EPS) * fg_ref[...]
    o_ref[...] = xo


def combine(x2, yg, wt_rows, mods, rows_per_mod, tile, lw, final_g, final):
    m, d = x2.shape
    row = lambda w: pl.BlockSpec((tile, w), lambda i: (i, 0))
    modspec = pl.BlockSpec((1, 1, d), lambda i: ((i * tile) // rows_per_mod, 0, 0))
    return pl.pallas_call(
        functools.partial(_combine_kernel, final=final),
        out_shape=jax.ShapeDtypeStruct((m, d), f32),
        grid=(m // tile,),
        in_specs=[row(d), pl.BlockSpec((TOP_K, tile, d // 2), lambda i: (0, i, 0)), row(8), modspec, modspec, modspec,
                  _const_spec((1, d)), _const_spec((d, D_EXPERT)), _const_spec((d, D_EXPERT)),
                  _const_spec((D_EXPERT, d)), _const_spec((1, d))],
        out_specs=row(d),
        compiler_params=_params(("parallel",)),
        name="combine",
    )(x2, yg, wt_rows, mods["g2"], mods["sh2"], mods["sc2"], lw["norm2_g"], lw["w_sh_gate"], lw["w_sh_up"],
      lw["w_sh_down"], final_g)


def moe_block(x2, h2p, eid, wt, rank, counts, mods, rows_per_mod, tile, lw, layer, final_g, final):
    m = x2.shape[0]
    p_max = m * TOP_K + N_EXPERTS * EXPERT_TILE
    pos, tile_exp, tile_valid = routing_plan(eid, rank, counts, p_max)
    xs = sc_scatter_rows(h2p, pos, p_max)
    ys = grouped_experts(xs, tile_exp, tile_valid, lw["w_exp_gate"], lw["w_exp_up"], lw["w_exp_down"], layer)
    yg = sc_gather_rows(ys, pos[:TOP_K].reshape(TOP_K * m)).reshape(TOP_K, m, D_MODEL // 2)
    return combine(x2, yg, wt.T, mods, rows_per_mod, tile, lw, final_g, final)


def rope_tables(seq):
    rows = seq // GRID_W
    row = jnp.repeat(jnp.arange(rows), GRID_W).astype(f32)
    col = jnp.tile(jnp.arange(GRID_W), rows).astype(f32)
    axis_dim = HEAD_DIM // 2
    inv_freq = 1.0 / (ROPE_THETA ** (jnp.arange(0, axis_dim, 2, dtype=f32) / axis_dim))
    ang_r = row[:, None] * inv_freq
    ang_c = col[:, None] * inv_freq
    ang = jnp.concatenate([ang_r, ang_r, ang_c, ang_c], axis=-1)
    cos, sin = jnp.cos(ang), jnp.sin(ang)
    seg = (jnp.arange(HEAD_DIM) // 16) % 2
    sa = jnp.where(seg == 0, -sin, 0.0)
    sb = jnp.where(seg == 1, sin, 0.0)
    rep = lambda t: jnp.tile(t, (1, 4))
    return rep(cos), rep(sa), rep(sb)


def identity_rope_tables(rows):
    return jnp.ones((rows, 256), f32), jnp.zeros((rows, 256), f32), jnp.zeros((rows, 256), f32)


def kernel(x, c, ctx, c_ctx, w_mod, b_mod, norm1_g, norm2_g, w_in, q_norm_g, k_norm_g, sink, gm_norm_g, gm_ws, gm_b, w_br, w_o, w_router, router_bias, w_exp_gate, w_exp_up, w_exp_down, w_sh_gate, w_sh_up, w_sh_down, final_norm_g):
    bsz, seq, d = x.shape
    n_ctx = ctx.shape[1]
    depth = w_mod.shape[0]
    n_lat = bsz * seq
    n_cx = bsz * n_ctx

    cc = jnp.concatenate([c, c_ctx[None, :], jnp.zeros((MOD_ROWS - bsz - 1, d), f32)], axis=0)
    mod_all = compute_mod(cc, w_mod, b_mod)

    lat_tables = rope_tables(seq)
    ctx_tables = identity_rope_tables(n_ctx)
    wc_lat, ws_lat = dft_tables(seq)
    wc_ctx, ws_ctx = dft_tables(n_ctx)
    cs64 = channel_dft_table()
    bd = jnp.asarray(np.kron(np.eye(4), np.full((HEAD_DIM, HEAD_DIM), 1.0 / HEAD_DIM)), dtype=bf16)
    final_g = final_norm_g.reshape(1, d)

    xl = x.reshape(n_lat, d)
    xc = ctx.reshape(n_cx, d)
    for l in range(depth):
        ctx_out = l < depth - 1
        names = ("sh1", "sc1", "g1", "sh2", "sc2", "g2")
        mods_lat = {n: mod_all[l, :bsz, i * d:(i + 1) * d].reshape(bsz, 1, d) for i, n in enumerate(names)}
        mods_ctx = {n: mod_all[l, bsz:bsz + 1, i * d:(i + 1) * d].reshape(1, 1, d) for i, n in enumerate(names)}
        lw = {
            "norm1_g": norm1_g[l].reshape(1, d),
            "norm2_g": norm2_g[l].reshape(1, d),
            "w_z": w_in[l, :, :OFF_GATE].astype(bf16),
            "w_gate": w_in[l, :, OFF_GATE:].astype(bf16),
            "qn": jnp.tile(q_norm_g[l], 4).reshape(1, 256),
            "kn": jnp.tile(k_norm_g[l], 2).reshape(1, 128),
            "bd": bd,
            "gm_norm_g": gm_norm_g[l].reshape(1, 256),
            "gm_ws": gm_ws[l].astype(bf16),
            "gm_bias": jnp.repeat(gm_b[l].T, 64, axis=1),
            "cs64": cs64,
            "w_br": w_br[l].astype(bf16),
            "w_o": w_o[l].astype(bf16),
            "w_router_t": w_router[l].T,
            "router_bias": router_bias[l].reshape(N_EXPERTS, 1),
            "w_exp_gate": w_exp_gate,
            "w_exp_up": w_exp_up,
            "w_exp_down": w_exp_down,
            "w_sh_gate": w_sh_gate[l].astype(bf16),
            "w_sh_up": w_sh_up[l].astype(bf16),
            "w_sh_down": w_sh_down[l].astype(bf16),
        }
        sink_l = sink[l]

        qg, kg, vg, qw, kw, vw, fxc, fxs, o_d = in_projection(
            xl, mods_lat["sh1"], mods_lat["sc1"], seq, lat_tables, seq, ROW_TILE, lw)
        cqg, ckg, cvg, cqw, ckw, cvw, cfxc, cfxs, co_d = in_projection(
            xc, mods_ctx["sh1"], mods_ctx["sc1"], n_cx, ctx_tables, n_ctx, n_ctx, lw)
        b3 = lambda t, rows: t.reshape(bsz, rows, t.shape[-1])
        ckg3, cvg3, ckw3, cvw3 = b3(ckg, n_ctx), b3(cvg, n_ctx), b3(ckw, n_ctx), b3(cvw, n_ctx)

        o_a = dft_mix(wc_lat, ws_lat, b3(fxc, seq), b3(fxs, seq), ROW_TILE).reshape(n_lat, BRANCH_W)
        o_b = full_attention(b3(qg, seq), [(ckg3, cvg3), (b3(kg, seq), b3(vg, seq))], None, 256).reshape(n_lat, BRANCH_W)
        o_c = window_attention(b3(qw, seq), b3(kw, seq), b3(vw, seq), ckw3, cvw3, sink_l).reshape(n_lat, BRANCH_W)
        xl, *route = merge_and_route(xl, mods_lat, seq, (o_a, o_b, o_c, o_d), ROW_TILE, lw)
        xl_next = moe_block(xl, *route, mods_lat, seq, ROW_TILE, lw, l, final_g, not ctx_out)

        if ctx_out:
            co_a = dft_mix(wc_ctx, ws_ctx, b3(cfxc, n_ctx), b3(cfxs, n_ctx), n_ctx).reshape(n_cx, BRANCH_W)
            co_b = full_attention(b3(cqg, n_ctx), [(ckg3, cvg3)], None, n_ctx).reshape(n_cx, BRANCH_W)
            co_c = full_attention(b3(cqw, n_ctx), [(ckw3, cvw3)], sink_l, n_ctx).reshape(n_cx, BRANCH_W)
            xc, *croute = merge_and_route(xc, mods_ctx, n_cx, (co_a, co_b, co_c, co_d), ROW_TILE, lw)
            xc = moe_block(xc, *croute, mods_ctx, n_cx, ROW_TILE, lw, l, final_g, False)
        xl = xl_next
    return xl.reshape(bsz, seq, d)
```

```python
import functools
import math

import jax
import jax.numpy as jnp
import numpy as np
from jax import lax
from jax.experimental import pallas as pl
from jax.experimental.pallas import tpu as pltpu
from jax.experimental.pallas import tpu_sc as plsc

f32 = jnp.float32
bf16 = jnp.bfloat16
i32 = jnp.int32

D_MODEL = 1024
HEAD_DIM = 64
GRID_W = 64
ROPE_THETA = 10000.0
ATTN_SCALE = HEAD_DIM ** -0.5
RMS_EPS = 1e-6
NEG_INF = -1e30
Q_BLOCK = 128
WINDOW = 128
GM_CHUNK = 128
N_BRANCHES = 4
BRANCH_W = 256
KV_W = 128
OFF_KV = 0
OFF_Q = 512
OFF_FN = 1024
OFF_GM = 1280
OFF_GATE = 1792
N_EXPERTS = 64
TOP_K = 6
N_GROUPS = 8
GROUP_SIZE = N_EXPERTS // N_GROUPS
TOPK_GROUPS = 4
D_EXPERT = 256
ROUTED_SCALE = 2.5

SC_CORES = 2
SC_SUBCORES = 16
SC_WORKERS = SC_CORES * SC_SUBCORES
SC_IDX_CHUNK = 128

ROW_TILE = 512
EXPERT_TILE = 512
WINDOW_BLOCKS_PER_STEP = 4
MOD_ROWS = 24
VMEM_LIMIT = 56 * 1024 * 1024


def _params(sem, vmem=VMEM_LIMIT):
    return pltpu.CompilerParams(dimension_semantics=sem, vmem_limit_bytes=vmem)


def _const_spec(shape):
    nd = len(shape)
    return pl.BlockSpec(shape, lambda *_: (0,) * nd, pipeline_mode=pl.Buffered(1))


def _rms_mod(x, g, sc, sh):
    ms = jnp.mean(x * x, axis=-1, keepdims=True)
    return (x * lax.rsqrt(ms + RMS_EPS) * g) * (1.0 + sc) + sh


def _gelu(x):
    return 0.5 * x * (1.0 + jnp.tanh(math.sqrt(2.0 / math.pi) * (x + 0.044715 * (x * x * x))))


def _silu(x):
    return x * jax.nn.sigmoid(x)


def _mod_kernel(a_ref, w_ref, b_ref, o_ref):
    a = _silu(a_ref[...]).astype(bf16)
    o_ref[0] = jnp.dot(a, w_ref[0].astype(bf16), preferred_element_type=f32) + b_ref[0]


def compute_mod(cc, w_mod, b_mod):
    depth, d, n = w_mod.shape
    tn = 1536
    return pl.pallas_call(
        _mod_kernel,
        out_shape=jax.ShapeDtypeStruct((depth, MOD_ROWS, n), f32),
        grid=(depth, n // tn),
        in_specs=[
            pl.BlockSpec((MOD_ROWS, d), lambda l, j: (0, 0)),
            pl.BlockSpec((1, d, tn), lambda l, j: (l, 0, j)),
            pl.BlockSpec((1, 1, tn), lambda l, j: (l, 0, j)),
        ],
        out_specs=pl.BlockSpec((1, MOD_ROWS, tn), lambda l, j: (l, 0, j)),
        compiler_params=_params(("parallel", "parallel")),
        name="mod_proj",
    )(cc, w_mod, b_mod.reshape(depth, 1, n))


def _inproj_kernel(x_ref, sh_ref, sc_ref, g_ref, w_ref, qn_ref, kn_ref, bd_ref, gmg_ref, ws_ref, gb_ref, cs_ref,
                   cos_ref, sa_ref, sb_ref,
                   qg_ref, kg_ref, vg_ref, qw_ref, kw_ref, vw_ref, xc_ref, xs_ref, od_ref):
    tile = x_ref.shape[0]
    hb = _rms_mod(x_ref[...], g_ref[...], sc_ref[0], sh_ref[0]).astype(bf16)

    def proj(a, b):
        return jnp.dot(hb, w_ref[:, a:b], preferred_element_type=f32)

    def headnorm(t, gain):
        w = t.shape[1]
        sq = t * t
        hi = sq.astype(bf16)
        lo = (sq - hi.astype(f32)).astype(bf16)
        b = bd_ref[:w, :w]
        ms = jnp.dot(hi, b, preferred_element_type=f32) + jnp.dot(lo, b, preferred_element_type=f32)
        return t * lax.rsqrt(ms + RMS_EPS) * gain

    def rope(t):
        w = t.shape[1]
        return (t * cos_ref[:, :w] + pltpu.roll(t, w - 16, 1) * sa_ref[:, :w]
                + pltpu.roll(t, 16, 1) * sb_ref[:, :w])

    def expand_heads(q):
        lane = lax.broadcasted_iota(i32, (1, KV_W), 1)
        low = lane < HEAD_DIM
        blocks = []
        for kv in range(2):
            pair = q[:, KV_W * kv:KV_W * (kv + 1)]
            swapped = pltpu.roll(pair, HEAD_DIM, 1)
            keep = low if kv == 0 else jnp.logical_not(low)
            g0, g1 = (pair, swapped) if kv == 0 else (swapped, pair)
            blocks.append(jnp.where(keep, g0, 0.0))
            blocks.append(jnp.where(keep, g1, 0.0))
        return jnp.concatenate(blocks, axis=1)

    kv = proj(OFF_KV, OFF_Q)
    kg_ref[...] = rope(headnorm(kv[:, 0:128], kn_ref[...])).astype(bf16)
    vg_ref[...] = kv[:, 128:256].astype(bf16)
    kw_ref[...] = rope(kv[:, 256:384]).astype(bf16)
    vw_ref[...] = kv[:, 384:512].astype(bf16)

    qq = proj(OFF_Q, OFF_FN)
    qg = rope(headnorm(qq[:, :256], qn_ref[...])) * ATTN_SCALE
    qg_ref[...] = expand_heads(qg).astype(bf16)
    qw = rope(qq[:, 256:]) * ATTN_SCALE
    qw_ref[...] = expand_heads(qw).astype(bf16)

    fn = proj(OFF_FN, OFF_GM).astype(bf16)
    xcs = jnp.dot(fn, cs_ref[...], preferred_element_type=f32)
    xc_ref[...] = xcs[:, :256].astype(bf16)
    xs_ref[...] = xcs[:, 256:].astype(bf16)

    uv = proj(OFF_GM, OFF_GATE)
    u = _gelu(uv[:, :256])
    v = _gelu(uv[:, 256:])
    vms = jnp.mean(v * v, axis=-1, keepdims=True)
    vn = (v * lax.rsqrt(vms + RMS_EPS) * gmg_ref[...]).astype(bf16)
    lane_grp = lax.broadcasted_iota(i32, (1, 256), 1) // 64
    for c in range(tile // GM_CHUNK):
        rows = slice(c * GM_CHUNK, (c + 1) * GM_CHUNK)
        vch = vn[rows]
        sv = gb_ref[...]
        for g in range(4):
            r = jnp.dot(ws_ref[g], vch, preferred_element_type=f32)
            sv = sv + jnp.where(lane_grp == g, r, 0.0)
        od_ref[rows, :] = (u[rows] * sv).astype(bf16)


def in_projection(x2, sh, sc, rows_per_mod, tables, rows_per_seq, tile, lw):
    m, d = x2.shape
    cos_t, sa_t, sb_t = tables
    seq_blocks = rows_per_seq // tile
    row = lambda w: pl.BlockSpec((tile, w), lambda i: (i, 0))
    modspec = pl.BlockSpec((1, 1, d), lambda i: ((i * tile) // rows_per_mod, 0, 0))
    tabspec = pl.BlockSpec((tile, 256), lambda i: (i % seq_blocks, 0))
    out_w = [512, 128, 128, 512, 128, 128, 256, 256, 256]
    return pl.pallas_call(
        _inproj_kernel,
        out_shape=[jax.ShapeDtypeStruct((m, w), bf16) for w in out_w],
        grid=(m // tile,),
        in_specs=[
            row(d), modspec, modspec, _const_spec((1, d)), _const_spec((d, OFF_GATE)),
            _const_spec((1, 256)), _const_spec((1, 128)), _const_spec((256, 256)), _const_spec((1, 256)),
            _const_spec((4, GM_CHUNK, GM_CHUNK)), _const_spec((GM_CHUNK, 256)), _const_spec((256, 512)),
            tabspec, tabspec, tabspec,
        ],
        out_specs=[row(w) for w in out_w],
        compiler_params=_params(("parallel",)),
        name="in_projection",
    )(x2, sh, sc, lw["norm1_g"], lw["w_z"], lw["qn"], lw["kn"], lw["bd"], lw["gm_norm_g"], lw["gm_ws"], lw["gm_bias"],
      lw["cs64"], cos_t, sa_t, sb_t)


def _dft_kernel(wc_ref, ws_ref, xc_ref, xs_ref, o_ref):
    acc = jnp.dot(wc_ref[...], xc_ref[0], preferred_element_type=f32)
    acc = acc + jnp.dot(ws_ref[...], xs_ref[0], preferred_element_type=f32)
    o_ref[0] = acc.astype(bf16)


def dft_mix(wc, ws, xc, xs, tile):
    nb, length, w = xc.shape
    return pl.pallas_call(
        _dft_kernel,
        out_shape=jax.ShapeDtypeStruct((nb, length, w), bf16),
        grid=(length // tile, nb),
        in_specs=[
            pl.BlockSpec((tile, length), lambda i, b: (i, 0)),
            pl.BlockSpec((tile, length), lambda i, b: (i, 0)),
            pl.BlockSpec((1, length, w), lambda i, b: (b, 0, 0)),
            pl.BlockSpec((1, length, w), lambda i, b: (b, 0, 0)),
        ],
        out_specs=pl.BlockSpec((1, tile, w), lambda i, b: (b, i, 0)),
        compiler_params=_params(("parallel", "parallel")),
        name="dft_mix",
    )(wc, ws, xc, xs)


def dft_tables(length):
    jk = (np.arange(length)[:, None] * np.arange(length)[None, :]) % length
    ang = 2.0 * np.pi * jk / length
    s = 1.0 / math.sqrt(length)
    return jnp.asarray(np.cos(ang) * s, dtype=bf16), jnp.asarray(-np.sin(ang) * s, dtype=bf16)


def channel_dft_table():
    jk = (np.arange(64)[:, None] * np.arange(64)[None, :]) % 64
    ang = 2.0 * np.pi * jk / 64
    eye = np.eye(4)
    c = np.kron(eye, np.cos(ang) / 8.0)
    s = np.kron(eye, np.sin(ang) / 8.0)
    return jnp.asarray(np.concatenate([c, s], axis=1), dtype=bf16)


def _attend_pieces(q, pieces, sink_col):
    scores = []
    for k, _, mask in pieces:
        s = lax.dot_general(q, k, (((1,), (1,)), ((), ())), preferred_element_type=f32)
        if mask is not None:
            s = jnp.where(mask, s, NEG_INF)
        scores.append(s)
    m = scores[0].max(axis=-1, keepdims=True)
    for s in scores[1:]:
        m = jnp.maximum(m, s.max(axis=-1, keepdims=True))
    if sink_col is not None:
        m = jnp.maximum(m, sink_col)
        denom = jnp.exp(sink_col - m)
    else:
        denom = jnp.zeros_like(m)
    acc = None
    for s, (_, v, _) in zip(scores, pieces):
        p = jnp.exp(s - m)
        denom = denom + p.sum(axis=-1, keepdims=True)
        pv = jnp.dot(p.astype(bf16), v, preferred_element_type=f32)
        acc = pv if acc is None else acc + pv
    return acc * (1.0 / denom)


def _attend_heads(q_all, piece_fn, sink_ref, qb):
    lane = lax.broadcasted_iota(i32, (1, KV_W), 1)
    low = lane < HEAD_DIM
    halves = []
    for kv in range(2):
        q = jnp.concatenate([q_all[:, KV_W * (2 * kv):KV_W * (2 * kv + 1)],
                             q_all[:, KV_W * (2 * kv + 1):KV_W * (2 * kv + 2)]], axis=0)
        if sink_ref is not None:
            s0 = jnp.full((qb, 1), sink_ref[2 * kv], f32)
            s1 = jnp.full((qb, 1), sink_ref[2 * kv + 1], f32)
            sink_col = jnp.concatenate([s0, s1], axis=0)
        else:
            sink_col = None
        res = _attend_pieces(q, piece_fn(), sink_col)
        r0, r1 = res[:qb], res[qb:]
        if kv == 0:
            halves.append(jnp.where(low, r0, pltpu.roll(r1, HEAD_DIM, 1)))
        else:
            halves.append(jnp.where(low, pltpu.roll(r0, HEAD_DIM, 1), r1))
    return jnp.concatenate(halves, axis=1)


def _full_attn_kernel(*refs, n_pieces, has_sink, qb):
    pos = 0
    sink_ref = None
    if has_sink:
        sink_ref = refs[0]
        pos = 1
    q_ref = refs[pos]
    kv_refs = refs[pos + 1:pos + 1 + 2 * n_pieces]
    o_ref = refs[pos + 1 + 2 * n_pieces]

    def piece_fn():
        return [(kv_refs[2 * i][0], kv_refs[2 * i + 1][0], None) for i in range(n_pieces)]

    o_ref[0] = _attend_heads(q_ref[0], piece_fn, sink_ref, qb).astype(bf16)


def full_attention(q, pieces, sink, qb):
    nb, lq, _ = q.shape
    in_specs = []
    args = []
    if sink is not None:
        in_specs.append(pl.BlockSpec(memory_space=pltpu.SMEM))
        args.append(sink)
    in_specs.append(pl.BlockSpec((1, qb, 512), lambda b, i: (b, i, 0)))
    args.append(q)
    for k, v in pieces:
        spec = pl.BlockSpec((1, k.shape[1], KV_W), lambda b, i: (b, 0, 0))
        in_specs += [spec, spec]
        args += [k, v]
    return pl.pallas_call(
        functools.partial(_full_attn_kernel, n_pieces=len(pieces), has_sink=sink is not None, qb=qb),
        out_shape=jax.ShapeDtypeStruct((nb, lq, 256), bf16),
        grid=(nb, lq // qb),
        in_specs=in_specs,
        out_specs=pl.BlockSpec((1, qb, 256), lambda b, i: (b, i, 0)),
        compiler_params=_params(("parallel", "parallel")),
        name="full_attention",
    )(*args)


def _window_attn_kernel(sink_ref, q_ref, k_ref, v_ref, kc_ref, vc_ref, o_ref, *, seq, blocks):
    span = 3 * Q_BLOCK
    for j in range(blocks):
        n = pl.program_id(1) * blocks + j
        rows = slice(j * Q_BLOCK, (j + 1) * Q_BLOCK)
        start = pl.multiple_of(jnp.clip((n - 1) * Q_BLOCK, 0, seq - span), Q_BLOCK)
        kwin = k_ref[0, pl.ds(start, span), :]
        vwin = v_ref[0, pl.ds(start, span), :]
        row = lax.broadcasted_iota(i32, (2 * Q_BLOCK, span), 0) % Q_BLOCK + n * Q_BLOCK
        col = lax.broadcasted_iota(i32, (2 * Q_BLOCK, span), 1) + start
        mask = jnp.abs(row - col) <= WINDOW

        def piece_fn():
            return [(kc_ref[0], vc_ref[0], None), (kwin, vwin, mask)]

        o_ref[0, rows, :] = _attend_heads(q_ref[0, rows, :], piece_fn, sink_ref, Q_BLOCK).astype(bf16)


def window_attention(q, k, v, kc, vc, sink):
    nb, seq, _ = q.shape
    n_ctx = kc.shape[1]
    blocks = WINDOW_BLOCKS_PER_STEP
    full = lambda l: pl.BlockSpec((1, l, KV_W), lambda b, i: (b, 0, 0))
    return pl.pallas_call(
        functools.partial(_window_attn_kernel, seq=seq, blocks=blocks),
        out_shape=jax.ShapeDtypeStruct((nb, seq, 256), bf16),
        grid=(nb, seq // (Q_BLOCK * blocks)),
        in_specs=[pl.BlockSpec(memory_space=pltpu.SMEM),
                  pl.BlockSpec((1, Q_BLOCK * blocks, 512), lambda b, i: (b, i, 0)),
                  full(seq), full(seq), full(n_ctx), full(n_ctx)],
        out_specs=pl.BlockSpec((1, Q_BLOCK * blocks, 256), lambda b, i: (b, i, 0)),
        compiler_params=_params(("parallel", "parallel")),
        name="window_attention",
    )(sink, q, k, v, kc, vc)


def _route(logits_t, bias_col):
    t = logits_t.shape[1]
    scores = jax.nn.sigmoid(logits_t)
    choice = scores + bias_col
    sub = lax.broadcasted_iota(i32, (GROUP_SIZE, t), 0)
    grp_score = []
    for g in range(N_GROUPS):
        cg = choice[g * GROUP_SIZE:(g + 1) * GROUP_SIZE]
        m1 = cg.max(axis=0, keepdims=True)
        first = jnp.min(jnp.where(cg == m1, sub, GROUP_SIZE), axis=0, keepdims=True)
        m2 = jnp.where(sub == first, -jnp.inf, cg).max(axis=0, keepdims=True)
        grp_score.append(m1 + m2)
    keep = []
    for g in range(N_GROUPS):
        beaten = jnp.zeros((1, t), i32)
        for o in range(N_GROUPS):
            if o == g:
                continue
            wins = (grp_score[o] > grp_score[g]) | ((grp_score[o] == grp_score[g]) & (o < g))
            beaten = beaten + wins.astype(i32)
        keep.append(jnp.broadcast_to(beaten < TOPK_GROUPS, (GROUP_SIZE, t)))
    masked = jnp.where(jnp.concatenate(keep, axis=0), choice, NEG_INF)
    eid = lax.broadcasted_iota(i32, (N_EXPERTS, t), 0)
    ids, wts = [], []
    for _ in range(TOP_K):
        m = masked.max(axis=0, keepdims=True)
        pick = jnp.min(jnp.where(masked == m, eid, N_EXPERTS), axis=0, keepdims=True)
        sel = eid == pick
        ids.append(pick)
        wts.append(jnp.sum(jnp.where(sel, scores, 0.0), axis=0, keepdims=True))
        masked = jnp.where(sel, -jnp.inf, masked)
    total = wts[0]
    for w in wts[1:]:
        total = total + w
    norm = ROUTED_SCALE / total
    return ids, [w * norm for w in wts]


def _pack_bf16_pairs(x):
    w = x.shape[1] // 2
    lo = lax.bitcast_convert_type(x[:, :w].astype(bf16).astype(f32), i32)
    hi = lax.bitcast_convert_type(x[:, w:].astype(bf16).astype(f32), i32)
    return lax.shift_right_logical(lo, 16) | (hi & jnp.int32(-65536))


def _unpack_bf16_pairs(p):
    lo = lax.bitcast_convert_type(lax.shift_left(p, 16), f32)
    hi = lax.bitcast_convert_type(p & jnp.int32(-65536), f32)
    return lo.astype(bf16), hi.astype(bf16)


def _merge_kernel(x_ref, sh_ref, sc_ref, g1_ref, sh2_ref, sc2_ref, n1_ref, n2_ref, oa_ref, ob_ref, oc_ref, od_ref,
                  wg_ref, wbr_ref, wo_ref, wr_ref, rb_ref, *rest, extends):
    if extends:
        cnt_in_ref = rest[0]
        rest = rest[5:]
    xo_ref, h2_ref, eid_ref, wt_ref, rank_ref, cnt_ref, run_ref = rest

    @pl.when(pl.program_id(0) == 0)
    def _():
        run_ref[...] = cnt_in_ref[...].astype(f32) if extends else jnp.zeros_like(run_ref)

    x = x_ref[...]
    hb = _rms_mod(x, n1_ref[...], sc_ref[0], sh_ref[0]).astype(bf16)
    y = None
    for i, o_ref in enumerate((oa_ref, ob_ref, oc_ref, od_ref)):
        gate = jax.nn.sigmoid(jnp.dot(hb, wg_ref[:, i * D_MODEL:(i + 1) * D_MODEL], preferred_element_type=f32))
        term = gate * jnp.dot(o_ref[...], wbr_ref[i], preferred_element_type=f32)
        y = term if y is None else y + term
    xn = x + g1_ref[0] * jnp.dot(y.astype(bf16), wo_ref[...], preferred_element_type=f32)
    xo_ref[...] = xn
    h2 = _rms_mod(xn, n2_ref[...], sc2_ref[0], sh2_ref[0])
    h2_ref[...] = _pack_bf16_pairs(h2)
    logits_t = lax.dot_general(wr_ref[...], h2, (((1,), (1,)), ((), ())), preferred_element_type=f32,
                               precision=lax.Precision.HIGHEST)
    ids, wts = _route(logits_t, rb_ref[...])

    t = x.shape[0]
    eid = lax.broadcasted_iota(i32, (N_EXPERTS, t), 0)
    hits = [eid == pick for pick in ids]
    chosen = hits[0]
    for h in hits[1:]:
        chosen = chosen | h
    chosen = jnp.where(chosen, 1.0, 0.0)
    before = lax.broadcasted_iota(i32, (t, t), 0) < lax.broadcasted_iota(i32, (t, t), 1)
    prefix = jnp.dot(chosen.astype(bf16), jnp.where(before, 1.0, 0.0).astype(bf16), preferred_element_type=f32)
    offset = run_ref[...] + prefix
    ranks = [jnp.sum(jnp.where(h, offset, 0.0), axis=0, keepdims=True).astype(i32) for h in hits]
    run_ref[...] += jnp.sum(chosen, axis=1, keepdims=True)
    cnt_ref[...] = run_ref[...].astype(i32)

    pad_i = [jnp.zeros((1, t), i32)] * (8 - TOP_K)
    eid_ref[...] = jnp.concatenate(ids + pad_i, axis=0)
    rank_ref[...] = jnp.concatenate(ranks + pad_i, axis=0)
    wt_ref[...] = jnp.concatenate(wts + [jnp.zeros((1, t), f32)] * (8 - TOP_K), axis=0)


def merge_and_route(x2, mods, rows_per_mod, branches, tile, lw, m_total, row_offset, prior):
    m, d = x2.shape
    off = row_offset // tile
    row = lambda w: pl.BlockSpec((tile, w), lambda i: (i, 0))
    row_off = lambda w: pl.BlockSpec((tile, w), lambda i: (i + off, 0))
    modspec = pl.BlockSpec((1, 1, d), lambda i: ((i * tile) // rows_per_mod, 0, 0))
    col = pl.BlockSpec((8, tile), lambda i: (0, i + off))
    in_specs = [row(d)] + [modspec] * 5 + [_const_spec((1, d)), _const_spec((1, d))] + [row(BRANCH_W)] * 4 + [
        _const_spec((d, N_BRANCHES * d)), _const_spec((N_BRANCHES, BRANCH_W, d)), _const_spec((d, d)),
        _const_spec((N_EXPERTS, d)), _const_spec((N_EXPERTS, 1))]
    args = [x2, mods["sh1"], mods["sc1"], mods["g1"], mods["sh2"], mods["sc2"], lw["norm1_g"], lw["norm2_g"],
            *branches, lw["w_gate"], lw["w_br"], lw["w_o"], lw["w_router_t"], lw["router_bias"]]
    aliases = {}
    if prior is not None:
        h2p, eid, wt, rank, counts = prior
        n_in = len(args)
        in_specs += [_const_spec((N_EXPERTS, 1))] + [pl.BlockSpec(memory_space=pl.ANY)] * 4
        args += [counts, h2p, eid, wt, rank]
        aliases = {n_in + 1 + j: 1 + j for j in range(4)}
    return pl.pallas_call(
        functools.partial(_merge_kernel, extends=prior is not None),
        out_shape=[jax.ShapeDtypeStruct((m, d), f32), jax.ShapeDtypeStruct((m_total, d // 2), i32),
                   jax.ShapeDtypeStruct((8, m_total), i32), jax.ShapeDtypeStruct((8, m_total), f32),
                   jax.ShapeDtypeStruct((8, m_total), i32), jax.ShapeDtypeStruct((N_EXPERTS, 1), i32)],
        grid=(m // tile,),
        in_specs=in_specs,
        out_specs=[row(d), row_off(d // 2), col, col, col, pl.BlockSpec((N_EXPERTS, 1), lambda i: (0, 0))],
        scratch_shapes=[pltpu.VMEM((N_EXPERTS, 1), f32)],
        input_output_aliases=aliases,
        compiler_params=_params(("arbitrary",)),
        name="merge_and_route",
    )(*args)


def routing_plan(eid, rank, counts, p_max):
    counts = counts.reshape(N_EXPERTS)
    padded = ((counts + EXPERT_TILE - 1) // EXPERT_TILE) * EXPERT_TILE
    ends = jnp.cumsum(padded)
    starts = ends - padded
    onehot = eid[:, :, None] == jnp.arange(N_EXPERTS, dtype=i32)[None, None, :]
    pos = rank + jnp.sum(jnp.where(onehot, starts[None, None, :], 0), axis=-1)
    n_tiles = p_max // EXPERT_TILE
    tile_start = jnp.arange(n_tiles, dtype=i32) * EXPERT_TILE
    tile_valid = tile_start < ends[-1]
    tile_exp = jnp.sum((ends[None, :] <= tile_start[:, None]).astype(i32), axis=1)
    return pos.astype(i32), jnp.minimum(tile_exp, N_EXPERTS - 1), tile_valid.astype(i32)


def _sc_worker_id():
    return lax.axis_index("subcore") * SC_CORES + lax.axis_index("core")


def sc_scatter_rows(table, pos, p_rows):
    m, w = table.shape
    per_w = m // SC_WORKERS
    n_chunks = per_w // SC_IDX_CHUNK
    pos3 = pos.reshape(8, m // SC_IDX_CHUNK, SC_IDX_CHUNK).transpose(1, 0, 2)
    mesh = plsc.VectorSubcoreMesh(core_axis_name="core", subcore_axis_name="subcore")

    @functools.partial(
        pl.kernel,
        out_type=jax.ShapeDtypeStruct((p_rows, w), table.dtype),
        mesh=mesh,
        scratch_types=[
            pltpu.VMEM((8, SC_IDX_CHUNK), i32),
            pltpu.VMEM((SC_IDX_CHUNK, w), table.dtype),
            pltpu.SemaphoreType.DMA,
        ],
    )
    def scatter(x_hbm, p_hbm, o_hbm, idx_v, rows_v, sem):
        wid = _sc_worker_id()

        @pl.loop(0, n_chunks)
        def _(ci):
            chunk = wid * n_chunks + ci
            pltpu.sync_copy(p_hbm.at[chunk], idx_v)
            pltpu.sync_copy(x_hbm.at[pl.ds(chunk * SC_IDX_CHUNK, SC_IDX_CHUNK)], rows_v)
            copies = [pltpu.async_copy(rows_v, o_hbm.at[idx_v.at[k]], sem) for k in range(TOP_K)]
            for cp in copies:
                cp.wait()

    return scatter(table, pos3)


def sc_gather_rows(table, idx):
    n_idx = idx.shape[0]
    w = table.shape[1]
    per_w = n_idx // SC_WORKERS
    n_chunks = per_w // SC_IDX_CHUNK
    half = SC_IDX_CHUNK // 2
    mesh = plsc.VectorSubcoreMesh(core_axis_name="core", subcore_axis_name="subcore")

    @functools.partial(
        pl.kernel,
        out_type=jax.ShapeDtypeStruct((n_idx, w), table.dtype),
        mesh=mesh,
        scratch_types=[
            pltpu.VMEM((SC_IDX_CHUNK,), i32),
            pltpu.VMEM((half, w), table.dtype),
            pltpu.VMEM((half, w), table.dtype),
            pltpu.SemaphoreType.DMA,
            pltpu.SemaphoreType.DMA,
            pltpu.SemaphoreType.DMA,
            pltpu.SemaphoreType.DMA,
        ],
    )
    def gather(x_hbm, i_hbm, o_hbm, idx_v, buf0, buf1, g0_sem, g1_sem, w0_sem, w1_sem):
        base = _sc_worker_id() * per_w

        @pl.loop(0, n_chunks)
        def _(ci):
            cbase = base + ci * SC_IDX_CHUNK
            pltpu.sync_copy(i_hbm.at[pl.ds(cbase, SC_IDX_CHUNK)], idx_v)
            g0 = pltpu.async_copy(x_hbm.at[idx_v.at[pl.ds(0, half)]], buf0, g0_sem)
            g1 = pltpu.async_copy(x_hbm.at[idx_v.at[pl.ds(half, half)]], buf1, g1_sem)
            g0.wait()
            w0 = pltpu.async_copy(buf0, o_hbm.at[pl.ds(cbase, half)], w0_sem)
            g1.wait()
            w1 = pltpu.async_copy(buf1, o_hbm.at[pl.ds(cbase + half, half)], w1_sem)
            w0.wait()
            w1.wait()

    return gather(table, idx)


def _expert_kernel(te_ref, tv_ref, nx_ref, sl_ref, x_ref, wg_hbm, wu_hbm, wd_hbm, o_ref,
                   wg_f, wu_f, wd_f, wg_b, wu_b, wd_b, sems, *, layer):
    i = pl.program_id(0)

    def weight_copies(expert, slot):
        return [pltpu.make_async_copy(hbm.at[layer, expert], buf.at[slot], sems.at[slot, j])
                for j, (hbm, buf) in enumerate(((wg_hbm, wg_f), (wu_hbm, wu_f), (wd_hbm, wd_f)))]

    @pl.when(i == 0)
    def _():
        for cp in weight_copies(te_ref[0], 0):
            cp.start()

    @pl.when((i == 0) | (te_ref[i] != te_ref[jnp.maximum(i - 1, 0)]))
    def _():
        slot = sl_ref[i]
        for cp in weight_copies(te_ref[i], slot):
            cp.wait()
        wg_b[...] = wg_f[slot].astype(bf16)
        wu_b[...] = wu_f[slot].astype(bf16)
        wd_b[...] = wd_f[slot].astype(bf16)

        @pl.when(nx_ref[i] >= 0)
        def _():
            for cp in weight_copies(nx_ref[i], 1 - slot):
                cp.start()

    @pl.when(tv_ref[i] != 0)
    def _():
        lo, hi = _unpack_bf16_pairs(x_ref[...])
        half = lo.shape[1]
        a = (jnp.dot(lo, wg_b[:half], preferred_element_type=f32)
             + jnp.dot(hi, wg_b[half:], preferred_element_type=f32))
        b = (jnp.dot(lo, wu_b[:half], preferred_element_type=f32)
             + jnp.dot(hi, wu_b[half:], preferred_element_type=f32))
        hid = (_silu(a) * b).astype(bf16)
        o_ref[...] = _pack_bf16_pairs(jnp.dot(hid, wd_b[...], preferred_element_type=f32))

    @pl.when(tv_ref[i] == 0)
    def _():
        o_ref[...] = jnp.zeros_like(o_ref)


def grouped_experts(xs, tile_exp, tile_valid, wg, wu, wd, layer):
    p, half = xs.shape
    d = 2 * half
    n_tiles = p // EXPERT_TILE
    first = jnp.concatenate([jnp.ones((1,), bool), tile_exp[1:] != tile_exp[:-1]])
    slot = (jnp.cumsum(first.astype(i32)) - 1) % 2
    nxt_at = jnp.sum((tile_exp[None, :] <= tile_exp[:, None]).astype(i32), axis=1)
    nxt = jnp.where(nxt_at < n_tiles, tile_exp[jnp.minimum(nxt_at, n_tiles - 1)], -1)
    tile = pl.BlockSpec((EXPERT_TILE, half), lambda i, *_: (i, 0))
    hbm = pl.BlockSpec(memory_space=pl.ANY)
    grid_spec = pltpu.PrefetchScalarGridSpec(
        num_scalar_prefetch=4,
        grid=(n_tiles,),
        in_specs=[tile, hbm, hbm, hbm],
        out_specs=tile,
        scratch_shapes=[pltpu.VMEM((2, d, D_EXPERT), f32), pltpu.VMEM((2, d, D_EXPERT), f32),
                        pltpu.VMEM((2, D_EXPERT, d), f32),
                        pltpu.VMEM((d, D_EXPERT), bf16), pltpu.VMEM((d, D_EXPERT), bf16),
                        pltpu.VMEM((D_EXPERT, d), bf16), pltpu.SemaphoreType.DMA((2, 3))],
    )
    return pl.pallas_call(
        functools.partial(_expert_kernel, layer=layer),
        out_shape=jax.ShapeDtypeStruct((p, half), i32),
        grid_spec=grid_spec,
        compiler_params=_params(("arbitrary",)),
        name="grouped_experts",
    )(tile_exp, tile_valid, nxt.astype(i32), slot.astype(i32), xs, wg, wu, wd)


def _combine_kernel(x_ref, yg_ref, wt_ref, g2_ref, sh2_ref, sc2_ref, n2_ref, wsg_ref, wsu_ref, wsd_ref, fg_ref, o_ref,
                    *, final):
    x = x_ref[...]
    hb = _rms_mod(x, n2_ref[...], sc2_ref[0], sh2_ref[0]).astype(bf16)
    a = jnp.dot(hb, wsg_ref[...], preferred_element_type=f32)
    b = jnp.dot(hb, wsu_ref[...], preferred_element_type=f32)
    f = jnp.dot((_silu(a) * b).astype(bf16), wsd_ref[...], preferred_element_type=f32)
    wt = wt_ref[...]
    half = x.shape[1] // 2
    f_lo, f_hi = f[:, :half], f[:, half:]
    for k in range(TOP_K):
        lo, hi = _unpack_bf16_pairs(yg_ref[k])
        w = wt[:, k:k + 1]
        f_lo = f_lo + w * lo.astype(f32)
        f_hi = f_hi + w * hi.astype(f32)
    xo = x + g2_ref[0] * jnp.concatenate([f_lo, f_hi], axis=1)
    if final:
        ms = jnp.mean(xo * xo, axis=-1, keepdims=True)
        xo = xo * lax.rsqrt(ms + RMS_EPS) * fg_ref[...]
    o_ref[...] = xo


def combine(x2, yg, wt_rows, row_offset, mods, rows_per_mod, tile, lw, final_g, final):
    m, d = x2.shape
    off = row_offset // tile
    row = lambda w: pl.BlockSpec((tile, w), lambda i: (i, 0))
    modspec = pl.BlockSpec((1, 1, d), lambda i: ((i * tile) // rows_per_mod, 0, 0))
    return pl.pallas_call(
        functools.partial(_combine_kernel, final=final),
        out_shape=jax.ShapeDtypeStruct((m, d), f32),
        grid=(m // tile,),
        in_specs=[row(d), pl.BlockSpec((TOP_K, tile, d // 2), lambda i: (0, i + off, 0)),
                  pl.BlockSpec((tile, 8), lambda i: (i + off, 0)), modspec, modspec, modspec,
                  _const_spec((1, d)), _const_spec((d, D_EXPERT)), _const_spec((d, D_EXPERT)),
                  _const_spec((D_EXPERT, d)), _const_spec((1, d))],
        out_specs=row(d),
        compiler_params=_params(("parallel",)),
        name="combine",
    )(x2, yg, wt_rows, mods["g2"], mods["sh2"], mods["sc2"], lw["norm2_g"], lw["w_sh_gate"], lw["w_sh_up"],
      lw["w_sh_down"], final_g)


def routed_experts(h2p, eid, rank, counts, lw, layer):
    m = h2p.shape[0]
    p_max = m * TOP_K + N_EXPERTS * EXPERT_TILE
    pos, tile_exp, tile_valid = routing_plan(eid, rank, counts, p_max)
    xs = sc_scatter_rows(h2p, pos, p_max)
    ys = grouped_experts(xs, tile_exp, tile_valid, lw["w_exp_gate"], lw["w_exp_up"], lw["w_exp_down"], layer)
    return sc_gather_rows(ys, pos[:TOP_K].reshape(TOP_K * m)).reshape(TOP_K, m, D_MODEL // 2)


def rope_tables(seq):
    rows = seq // GRID_W
    row = jnp.repeat(jnp.arange(rows), GRID_W).astype(f32)
    col = jnp.tile(jnp.arange(GRID_W), rows).astype(f32)
    axis_dim = HEAD_DIM // 2
    inv_freq = 1.0 / (ROPE_THETA ** (jnp.arange(0, axis_dim, 2, dtype=f32) / axis_dim))
    ang_r = row[:, None] * inv_freq
    ang_c = col[:, None] * inv_freq
    ang = jnp.concatenate([ang_r, ang_r, ang_c, ang_c], axis=-1)
    cos, sin = jnp.cos(ang), jnp.sin(ang)
    seg = (jnp.arange(HEAD_DIM) // 16) % 2
    sa = jnp.where(seg == 0, -sin, 0.0)
    sb = jnp.where(seg == 1, sin, 0.0)
    rep = lambda t: jnp.tile(t, (1, 4))
    return rep(cos), rep(sa), rep(sb)


def identity_rope_tables(rows):
    return jnp.ones((rows, 256), f32), jnp.zeros((rows, 256), f32), jnp.zeros((rows, 256), f32)


def kernel(x, c, ctx, c_ctx, w_mod, b_mod, norm1_g, norm2_g, w_in, q_norm_g, k_norm_g, sink, gm_norm_g, gm_ws, gm_b, w_br, w_o, w_router, router_bias, w_exp_gate, w_exp_up, w_exp_down, w_sh_gate, w_sh_up, w_sh_down, final_norm_g):
    bsz, seq, d = x.shape
    n_ctx = ctx.shape[1]
    depth = w_mod.shape[0]
    n_lat = bsz * seq
    n_cx = bsz * n_ctx

    cc = jnp.concatenate([c, c_ctx[None, :], jnp.zeros((MOD_ROWS - bsz - 1, d), f32)], axis=0)
    mod_all = compute_mod(cc, w_mod, b_mod)

    lat_tables = rope_tables(seq)
    ctx_tables = identity_rope_tables(n_ctx)
    wc_lat, ws_lat = dft_tables(seq)
    wc_ctx, ws_ctx = dft_tables(n_ctx)
    cs64 = channel_dft_table()
    bd = jnp.asarray(np.kron(np.eye(4), np.full((HEAD_DIM, HEAD_DIM), 1.0 / HEAD_DIM)), dtype=bf16)
    final_g = final_norm_g.reshape(1, d)

    xl = x.reshape(n_lat, d)
    xc = ctx.reshape(n_cx, d)
    for l in range(depth):
        ctx_out = l < depth - 1
        names = ("sh1", "sc1", "g1", "sh2", "sc2", "g2")
        mods_lat = {n: mod_all[l, :bsz, i * d:(i + 1) * d].reshape(bsz, 1, d) for i, n in enumerate(names)}
        mods_ctx = {n: mod_all[l, bsz:bsz + 1, i * d:(i + 1) * d].reshape(1, 1, d) for i, n in enumerate(names)}
        lw = {
            "norm1_g": norm1_g[l].reshape(1, d),
            "norm2_g": norm2_g[l].reshape(1, d),
            "w_z": w_in[l, :, :OFF_GATE].astype(bf16),
            "w_gate": w_in[l, :, OFF_GATE:].astype(bf16),
            "qn": jnp.tile(q_norm_g[l], 4).reshape(1, 256),
            "kn": jnp.tile(k_norm_g[l], 2).reshape(1, 128),
            "bd": bd,
            "gm_norm_g": gm_norm_g[l].reshape(1, 256),
            "gm_ws": gm_ws[l].astype(bf16),
            "gm_bias": jnp.repeat(gm_b[l].T, 64, axis=1),
            "cs64": cs64,
            "w_br": w_br[l].astype(bf16),
            "w_o": w_o[l].astype(bf16),
            "w_router_t": w_router[l].T,
            "router_bias": router_bias[l].reshape(N_EXPERTS, 1),
            "w_exp_gate": w_exp_gate,
            "w_exp_up": w_exp_up,
            "w_exp_down": w_exp_down,
            "w_sh_gate": w_sh_gate[l].astype(bf16),
            "w_sh_up": w_sh_up[l].astype(bf16),
            "w_sh_down": w_sh_down[l].astype(bf16),
        }
        sink_l = sink[l]

        qg, kg, vg, qw, kw, vw, fxc, fxs, o_d = in_projection(
            xl, mods_lat["sh1"], mods_lat["sc1"], seq, lat_tables, seq, ROW_TILE, lw)
        cqg, ckg, cvg, cqw, ckw, cvw, cfxc, cfxs, co_d = in_projection(
            xc, mods_ctx["sh1"], mods_ctx["sc1"], n_cx, ctx_tables, n_ctx, n_ctx, lw)
        b3 = lambda t, rows: t.reshape(bsz, rows, t.shape[-1])
        ckg3, cvg3, ckw3, cvw3 = b3(ckg, n_ctx), b3(cvg, n_ctx), b3(ckw, n_ctx), b3(cvw, n_ctx)

        o_a = dft_mix(wc_lat, ws_lat, b3(fxc, seq), b3(fxs, seq), ROW_TILE).reshape(n_lat, BRANCH_W)
        o_b = full_attention(b3(qg, seq), [(ckg3, cvg3), (b3(kg, seq), b3(vg, seq))], None, 256).reshape(n_lat, BRANCH_W)
        o_c = window_attention(b3(qw, seq), b3(kw, seq), b3(vw, seq), ckw3, cvw3, sink_l).reshape(n_lat, BRANCH_W)
        m_total = n_lat + (n_cx if ctx_out else 0)
        route = None
        if ctx_out:
            route = [jnp.zeros((m_total, d // 2), i32), jnp.zeros((8, m_total), i32), jnp.zeros((8, m_total), f32),
                     jnp.zeros((8, m_total), i32), jnp.zeros((N_EXPERTS, 1), i32)]
        xl, *route = merge_and_route(xl, mods_lat, seq, (o_a, o_b, o_c, o_d), ROW_TILE, lw, m_total, 0, route)
        if ctx_out:
            co_a = dft_mix(wc_ctx, ws_ctx, b3(cfxc, n_ctx), b3(cfxs, n_ctx), n_ctx).reshape(n_cx, BRANCH_W)
            co_b = full_attention(b3(cqg, n_ctx), [(ckg3, cvg3)], None, n_ctx).reshape(n_cx, BRANCH_W)
            co_c = full_attention(b3(cqw, n_ctx), [(ckw3, cvw3)], sink_l, n_ctx).reshape(n_cx, BRANCH_W)
            xc, *route = merge_and_route(xc, mods_ctx, n_cx, (co_a, co_b, co_c, co_d), ROW_TILE, lw, m_total, n_lat,
                                         route)
        h2p, eid, wt, rank, counts = route
        yg = routed_experts(h2p, eid, rank, counts, lw, l)
        wt_rows = wt.T
        xl = combine(xl, yg, wt_rows, 0, mods_lat, seq, ROW_TILE, lw, final_g, not ctx_out)
        if ctx_out:
            xc = combine(xc, yg, wt_rows, n_lat, mods_ctx, n_cx, ROW_TILE, lw, final_g, False)
    return xl.reshape(bsz, seq, d)
```

```python
import functools
import math

import jax
import jax.numpy as jnp
import numpy as np
from jax import lax
from jax.experimental import pallas as pl
from jax.experimental.pallas import tpu as pltpu
from jax.experimental.pallas import tpu_sc as plsc

f32 = jnp.float32
bf16 = jnp.bfloat16
i32 = jnp.int32

D_MODEL = 1024
HEAD_DIM = 64
GRID_W = 64
ROPE_THETA = 10000.0
ATTN_SCALE = HEAD_DIM ** -0.5
RMS_EPS = 1e-6
NEG_INF = -1e30
Q_BLOCK = 128
WINDOW = 128
GM_CHUNK = 128
N_BRANCHES = 4
BRANCH_W = 256
KV_W = 128
OFF_KV = 0
OFF_Q = 512
OFF_FN = 1024
OFF_GM = 1280
OFF_GATE = 1792
N_EXPERTS = 64
TOP_K = 6
N_GROUPS = 8
GROUP_SIZE = N_EXPERTS // N_GROUPS
TOPK_GROUPS = 4
D_EXPERT = 256
ROUTED_SCALE = 2.5

SC_CORES = 2
SC_SUBCORES = 16
SC_WORKERS = SC_CORES * SC_SUBCORES
SC_IDX_CHUNK = 128

ROW_TILE = 512
EXPERT_TILE = 512
BATCH_CHUNKS = 2
WINDOW_BLOCKS_PER_STEP = 4
MOD_ROWS = 24
VMEM_LIMIT = 56 * 1024 * 1024


def _params(sem, vmem=VMEM_LIMIT):
    return pltpu.CompilerParams(dimension_semantics=sem, vmem_limit_bytes=vmem)


def _const_spec(shape):
    nd = len(shape)
    return pl.BlockSpec(shape, lambda *_: (0,) * nd, pipeline_mode=pl.Buffered(1))


def _rms_mod(x, g, sc, sh):
    ms = jnp.mean(x * x, axis=-1, keepdims=True)
    return (x * lax.rsqrt(ms + RMS_EPS) * g) * (1.0 + sc) + sh


def _gelu(x):
    return 0.5 * x * (1.0 + jnp.tanh(math.sqrt(2.0 / math.pi) * (x + 0.044715 * (x * x * x))))


def _silu(x):
    return x * jax.nn.sigmoid(x)


def _mod_kernel(a_ref, w_ref, b_ref, o_ref):
    a = _silu(a_ref[...]).astype(bf16)
    o_ref[0] = jnp.dot(a, w_ref[0].astype(bf16), preferred_element_type=f32) + b_ref[0]


def compute_mod(cc, w_mod, b_mod):
    depth, d, n = w_mod.shape
    tn = 1536
    return pl.pallas_call(
        _mod_kernel,
        out_shape=jax.ShapeDtypeStruct((depth, MOD_ROWS, n), f32),
        grid=(depth, n // tn),
        in_specs=[
            pl.BlockSpec((MOD_ROWS, d), lambda l, j: (0, 0)),
            pl.BlockSpec((1, d, tn), lambda l, j: (l, 0, j)),
            pl.BlockSpec((1, 1, tn), lambda l, j: (l, 0, j)),
        ],
        out_specs=pl.BlockSpec((1, MOD_ROWS, tn), lambda l, j: (l, 0, j)),
        compiler_params=_params(("parallel", "parallel")),
        name="mod_proj",
    )(cc, w_mod, b_mod.reshape(depth, 1, n))


def _inproj_kernel(x_ref, sh_ref, sc_ref, g_ref, w_ref, qn_ref, kn_ref, bd_ref, gmg_ref, ws_ref, gb_ref, cs_ref,
                   cos_ref, sa_ref, sb_ref,
                   qg_ref, kg_ref, vg_ref, qw_ref, kw_ref, vw_ref, xc_ref, xs_ref, od_ref):
    tile = x_ref.shape[0]
    hb = _rms_mod(x_ref[...], g_ref[...], sc_ref[0], sh_ref[0]).astype(bf16)

    def proj(a, b):
        return jnp.dot(hb, w_ref[:, a:b], preferred_element_type=f32)

    def headnorm(t, gain):
        w = t.shape[1]
        sq = t * t
        hi = sq.astype(bf16)
        lo = (sq - hi.astype(f32)).astype(bf16)
        b = bd_ref[:w, :w]
        ms = jnp.dot(hi, b, preferred_element_type=f32) + jnp.dot(lo, b, preferred_element_type=f32)
        return t * lax.rsqrt(ms + RMS_EPS) * gain

    def rope(t):
        w = t.shape[1]
        return (t * cos_ref[:, :w] + pltpu.roll(t, w - 16, 1) * sa_ref[:, :w]
                + pltpu.roll(t, 16, 1) * sb_ref[:, :w])

    def expand_heads(q):
        lane = lax.broadcasted_iota(i32, (1, KV_W), 1)
        low = lane < HEAD_DIM
        blocks = []
        for kv in range(2):
            pair = q[:, KV_W * kv:KV_W * (kv + 1)]
            swapped = pltpu.roll(pair, HEAD_DIM, 1)
            keep = low if kv == 0 else jnp.logical_not(low)
            g0, g1 = (pair, swapped) if kv == 0 else (swapped, pair)
            blocks.append(jnp.where(keep, g0, 0.0))
            blocks.append(jnp.where(keep, g1, 0.0))
        return jnp.concatenate(blocks, axis=1)

    kv = proj(OFF_KV, OFF_Q)
    kg_ref[...] = rope(headnorm(kv[:, 0:128], kn_ref[...])).astype(bf16)
    vg_ref[...] = kv[:, 128:256].astype(bf16)
    kw_ref[...] = rope(kv[:, 256:384]).astype(bf16)
    vw_ref[...] = kv[:, 384:512].astype(bf16)

    qq = proj(OFF_Q, OFF_FN)
    qg = rope(headnorm(qq[:, :256], qn_ref[...])) * ATTN_SCALE
    qg_ref[...] = expand_heads(qg).astype(bf16)
    qw = rope(qq[:, 256:]) * ATTN_SCALE
    qw_ref[...] = expand_heads(qw).astype(bf16)

    fn = proj(OFF_FN, OFF_GM).astype(bf16)
    xcs = jnp.dot(fn, cs_ref[...], preferred_element_type=f32)
    xc_ref[...] = xcs[:, :256].astype(bf16)
    xs_ref[...] = xcs[:, 256:].astype(bf16)

    uv = proj(OFF_GM, OFF_GATE)
    u = _gelu(uv[:, :256])
    v = _gelu(uv[:, 256:])
    vms = jnp.mean(v * v, axis=-1, keepdims=True)
    vn = (v * lax.rsqrt(vms + RMS_EPS) * gmg_ref[...]).astype(bf16)
    lane_grp = lax.broadcasted_iota(i32, (1, 256), 1) // 64
    for c in range(tile // GM_CHUNK):
        rows = slice(c * GM_CHUNK, (c + 1) * GM_CHUNK)
        vch = vn[rows]
        sv = gb_ref[...]
        for g in range(4):
            r = jnp.dot(ws_ref[g], vch, preferred_element_type=f32)
            sv = sv + jnp.where(lane_grp == g, r, 0.0)
        od_ref[rows, :] = (u[rows] * sv).astype(bf16)


def in_projection(x2, sh, sc, rows_per_mod, tables, rows_per_seq, tile, lw):
    m, d = x2.shape
    cos_t, sa_t, sb_t = tables
    seq_blocks = rows_per_seq // tile
    row = lambda w: pl.BlockSpec((tile, w), lambda i: (i, 0))
    modspec = pl.BlockSpec((1, 1, d), lambda i: ((i * tile) // rows_per_mod, 0, 0))
    tabspec = pl.BlockSpec((tile, 256), lambda i: (i % seq_blocks, 0))
    out_w = [512, 128, 128, 512, 128, 128, 256, 256, 256]
    return pl.pallas_call(
        _inproj_kernel,
        out_shape=[jax.ShapeDtypeStruct((m, w), bf16) for w in out_w],
        grid=(m // tile,),
        in_specs=[
            row(d), modspec, modspec, _const_spec((1, d)), _const_spec((d, OFF_GATE)),
            _const_spec((1, 256)), _const_spec((1, 128)), _const_spec((256, 256)), _const_spec((1, 256)),
            _const_spec((4, GM_CHUNK, GM_CHUNK)), _const_spec((GM_CHUNK, 256)), _const_spec((256, 512)),
            tabspec, tabspec, tabspec,
        ],
        out_specs=[row(w) for w in out_w],
        compiler_params=_params(("parallel",)),
        name="in_projection",
    )(x2, sh, sc, lw["norm1_g"], lw["w_z"], lw["qn"], lw["kn"], lw["bd"], lw["gm_norm_g"], lw["gm_ws"], lw["gm_bias"],
      lw["cs64"], cos_t, sa_t, sb_t)


def _dft_kernel(wc_ref, ws_ref, xc_ref, xs_ref, o_ref):
    acc = jnp.dot(wc_ref[...], xc_ref[0], preferred_element_type=f32)
    acc = acc + jnp.dot(ws_ref[...], xs_ref[0], preferred_element_type=f32)
    o_ref[0] = acc.astype(bf16)


def dft_mix(wc, ws, xc, xs, tile):
    nb, length, w = xc.shape
    return pl.pallas_call(
        _dft_kernel,
        out_shape=jax.ShapeDtypeStruct((nb, length, w), bf16),
        grid=(length // tile, nb),
        in_specs=[
            pl.BlockSpec((tile, length), lambda i, b: (i, 0)),
            pl.BlockSpec((tile, length), lambda i, b: (i, 0)),
            pl.BlockSpec((1, length, w), lambda i, b: (b, 0, 0)),
            pl.BlockSpec((1, length, w), lambda i, b: (b, 0, 0)),
        ],
        out_specs=pl.BlockSpec((1, tile, w), lambda i, b: (b, i, 0)),
        compiler_params=_params(("parallel", "parallel")),
        name="dft_mix",
    )(wc, ws, xc, xs)


def dft_tables(length):
    jk = (np.arange(length)[:, None] * np.arange(length)[None, :]) % length
    ang = 2.0 * np.pi * jk / length
    s = 1.0 / math.sqrt(length)
    return jnp.asarray(np.cos(ang) * s, dtype=bf16), jnp.asarray(-np.sin(ang) * s, dtype=bf16)


def channel_dft_table():
    jk = (np.arange(64)[:, None] * np.arange(64)[None, :]) % 64
    ang = 2.0 * np.pi * jk / 64
    eye = np.eye(4)
    c = np.kron(eye, np.cos(ang) / 8.0)
    s = np.kron(eye, np.sin(ang) / 8.0)
    return jnp.asarray(np.concatenate([c, s], axis=1), dtype=bf16)


def _attend_pieces(q, pieces, sink_col):
    scores = []
    for k, _, mask in pieces:
        s = lax.dot_general(q, k, (((1,), (1,)), ((), ())), preferred_element_type=f32)
        if mask is not None:
            s = jnp.where(mask, s, NEG_INF)
        scores.append(s)
    m = scores[0].max(axis=-1, keepdims=True)
    for s in scores[1:]:
        m = jnp.maximum(m, s.max(axis=-1, keepdims=True))
    if sink_col is not None:
        m = jnp.maximum(m, sink_col)
        denom = jnp.exp(sink_col - m)
    else:
        denom = jnp.zeros_like(m)
    acc = None
    for s, (_, v, _) in zip(scores, pieces):
        p = jnp.exp(s - m)
        denom = denom + p.sum(axis=-1, keepdims=True)
        pv = jnp.dot(p.astype(bf16), v, preferred_element_type=f32)
        acc = pv if acc is None else acc + pv
    return acc * (1.0 / denom)


def _attend_heads(q_all, piece_fn, sink_ref, qb):
    lane = lax.broadcasted_iota(i32, (1, KV_W), 1)
    low = lane < HEAD_DIM
    halves = []
    for kv in range(2):
        q = jnp.concatenate([q_all[:, KV_W * (2 * kv):KV_W * (2 * kv + 1)],
                             q_all[:, KV_W * (2 * kv + 1):KV_W * (2 * kv + 2)]], axis=0)
        if sink_ref is not None:
            s0 = jnp.full((qb, 1), sink_ref[2 * kv], f32)
            s1 = jnp.full((qb, 1), sink_ref[2 * kv + 1], f32)
            sink_col = jnp.concatenate([s0, s1], axis=0)
        else:
            sink_col = None
        res = _attend_pieces(q, piece_fn(), sink_col)
        r0, r1 = res[:qb], res[qb:]
        if kv == 0:
            halves.append(jnp.where(low, r0, pltpu.roll(r1, HEAD_DIM, 1)))
        else:
            halves.append(jnp.where(low, pltpu.roll(r0, HEAD_DIM, 1), r1))
    return jnp.concatenate(halves, axis=1)


def _full_attn_kernel(*refs, n_pieces, has_sink, qb):
    pos = 0
    sink_ref = None
    if has_sink:
        sink_ref = refs[0]
        pos = 1
    q_ref = refs[pos]
    kv_refs = refs[pos + 1:pos + 1 + 2 * n_pieces]
    o_ref = refs[pos + 1 + 2 * n_pieces]

    def piece_fn():
        return [(kv_refs[2 * i][0], kv_refs[2 * i + 1][0], None) for i in range(n_pieces)]

    o_ref[0] = _attend_heads(q_ref[0], piece_fn, sink_ref, qb).astype(bf16)


def full_attention(q, pieces, sink, qb):
    nb, lq, _ = q.shape
    in_specs = []
    args = []
    if sink is not None:
        in_specs.append(pl.BlockSpec(memory_space=pltpu.SMEM))
        args.append(sink)
    in_specs.append(pl.BlockSpec((1, qb, 512), lambda b, i: (b, i, 0)))
    args.append(q)
    for k, v in pieces:
        spec = pl.BlockSpec((1, k.shape[1], KV_W), lambda b, i: (b, 0, 0))
        in_specs += [spec, spec]
        args += [k, v]
    return pl.pallas_call(
        functools.partial(_full_attn_kernel, n_pieces=len(pieces), has_sink=sink is not None, qb=qb),
        out_shape=jax.ShapeDtypeStruct((nb, lq, 256), bf16),
        grid=(nb, lq // qb),
        in_specs=in_specs,
        out_specs=pl.BlockSpec((1, qb, 256), lambda b, i: (b, i, 0)),
        compiler_params=_params(("parallel", "parallel")),
        name="full_attention",
    )(*args)


def _window_attn_kernel(sink_ref, q_ref, k_ref, v_ref, kc_ref, vc_ref, o_ref, *, seq, blocks):
    span = 3 * Q_BLOCK
    for j in range(blocks):
        n = pl.program_id(1) * blocks + j
        rows = slice(j * Q_BLOCK, (j + 1) * Q_BLOCK)
        start = pl.multiple_of(jnp.clip((n - 1) * Q_BLOCK, 0, seq - span), Q_BLOCK)
        kwin = k_ref[0, pl.ds(start, span), :]
        vwin = v_ref[0, pl.ds(start, span), :]
        row = lax.broadcasted_iota(i32, (2 * Q_BLOCK, span), 0) % Q_BLOCK + n * Q_BLOCK
        col = lax.broadcasted_iota(i32, (2 * Q_BLOCK, span), 1) + start
        mask = jnp.abs(row - col) <= WINDOW

        def piece_fn():
            return [(kc_ref[0], vc_ref[0], None), (kwin, vwin, mask)]

        o_ref[0, rows, :] = _attend_heads(q_ref[0, rows, :], piece_fn, sink_ref, Q_BLOCK).astype(bf16)


def window_attention(q, k, v, kc, vc, sink):
    nb, seq, _ = q.shape
    n_ctx = kc.shape[1]
    blocks = WINDOW_BLOCKS_PER_STEP
    full = lambda l: pl.BlockSpec((1, l, KV_W), lambda b, i: (b, 0, 0))
    return pl.pallas_call(
        functools.partial(_window_attn_kernel, seq=seq, blocks=blocks),
        out_shape=jax.ShapeDtypeStruct((nb, seq, 256), bf16),
        grid=(nb, seq // (Q_BLOCK * blocks)),
        in_specs=[pl.BlockSpec(memory_space=pltpu.SMEM),
                  pl.BlockSpec((1, Q_BLOCK * blocks, 512), lambda b, i: (b, i, 0)),
                  full(seq), full(seq), full(n_ctx), full(n_ctx)],
        out_specs=pl.BlockSpec((1, Q_BLOCK * blocks, 256), lambda b, i: (b, i, 0)),
        compiler_params=_params(("parallel", "parallel")),
        name="window_attention",
    )(sink, q, k, v, kc, vc)


def _route(logits_t, bias_col):
    t = logits_t.shape[1]
    scores = jax.nn.sigmoid(logits_t)
    choice = scores + bias_col
    sub = lax.broadcasted_iota(i32, (GROUP_SIZE, t), 0)
    grp_score = []
    for g in range(N_GROUPS):
        cg = choice[g * GROUP_SIZE:(g + 1) * GROUP_SIZE]
        m1 = cg.max(axis=0, keepdims=True)
        first = jnp.min(jnp.where(cg == m1, sub, GROUP_SIZE), axis=0, keepdims=True)
        m2 = jnp.where(sub == first, -jnp.inf, cg).max(axis=0, keepdims=True)
        grp_score.append(m1 + m2)
    keep = []
    for g in range(N_GROUPS):
        beaten = jnp.zeros((1, t), i32)
        for o in range(N_GROUPS):
            if o == g:
                continue
            wins = (grp_score[o] > grp_score[g]) | ((grp_score[o] == grp_score[g]) & (o < g))
            beaten = beaten + wins.astype(i32)
        keep.append(jnp.broadcast_to(beaten < TOPK_GROUPS, (GROUP_SIZE, t)))
    masked = jnp.where(jnp.concatenate(keep, axis=0), choice, NEG_INF)
    eid = lax.broadcasted_iota(i32, (N_EXPERTS, t), 0)
    ids, wts = [], []
    for _ in range(TOP_K):
        m = masked.max(axis=0, keepdims=True)
        pick = jnp.min(jnp.where(masked == m, eid, N_EXPERTS), axis=0, keepdims=True)
        sel = eid == pick
        ids.append(pick)
        wts.append(jnp.sum(jnp.where(sel, scores, 0.0), axis=0, keepdims=True))
        masked = jnp.where(sel, -jnp.inf, masked)
    total = wts[0]
    for w in wts[1:]:
        total = total + w
    norm = ROUTED_SCALE / total
    return ids, [w * norm for w in wts]


def _pack_bf16_pairs(x):
    w = x.shape[1] // 2
    lo = lax.bitcast_convert_type(x[:, :w].astype(bf16).astype(f32), i32)
    hi = lax.bitcast_convert_type(x[:, w:].astype(bf16).astype(f32), i32)
    return lax.shift_right_logical(lo, 16) | (hi & jnp.int32(-65536))


def _unpack_bf16_pairs(p):
    lo = lax.bitcast_convert_type(lax.shift_left(p, 16), f32)
    hi = lax.bitcast_convert_type(p & jnp.int32(-65536), f32)
    return lo.astype(bf16), hi.astype(bf16)


def _merge_kernel(x_ref, sh_ref, sc_ref, g1_ref, sh2_ref, sc2_ref, n1_ref, n2_ref, oa_ref, ob_ref, oc_ref, od_ref,
                  wg_ref, wbr_ref, wo_ref, wr_ref, rb_ref, tri_ref, *rest, extends):
    if extends:
        cnt_in_ref = rest[0]
        rest = rest[5:]
    xo_ref, h2_ref, eid_ref, wt_ref, rank_ref, cnt_ref, run_ref = rest

    @pl.when(pl.program_id(0) == 0)
    def _():
        run_ref[...] = cnt_in_ref[...].astype(f32) if extends else jnp.zeros_like(run_ref)

    x = x_ref[...]
    hb = _rms_mod(x, n1_ref[...], sc_ref[0], sh_ref[0]).astype(bf16)
    y = None
    for i, o_ref in enumerate((oa_ref, ob_ref, oc_ref, od_ref)):
        gate = jax.nn.sigmoid(jnp.dot(hb, wg_ref[:, i * D_MODEL:(i + 1) * D_MODEL], preferred_element_type=f32))
        term = gate * jnp.dot(o_ref[...], wbr_ref[i], preferred_element_type=f32)
        y = term if y is None else y + term
    xn = x + g1_ref[0] * jnp.dot(y.astype(bf16), wo_ref[...], preferred_element_type=f32)
    xo_ref[...] = xn
    h2 = _rms_mod(xn, n2_ref[...], sc2_ref[0], sh2_ref[0])
    h2_ref[...] = _pack_bf16_pairs(h2)
    logits_t = lax.dot_general(wr_ref[...], h2, (((1,), (1,)), ((), ())), preferred_element_type=f32,
                               precision=lax.Precision.HIGHEST)
    ids, wts = _route(logits_t, rb_ref[...])

    t = x.shape[0]
    eid = lax.broadcasted_iota(i32, (N_EXPERTS, t), 0)
    hits = [eid == pick for pick in ids]
    chosen = hits[0]
    for h in hits[1:]:
        chosen = chosen | h
    chosen = jnp.where(chosen, 1.0, 0.0)
    prefix = jnp.dot(chosen.astype(bf16), tri_ref[...], preferred_element_type=f32)
    offset = run_ref[...] + prefix
    ranks = [jnp.sum(jnp.where(h, offset, 0.0), axis=0, keepdims=True).astype(i32) for h in hits]
    run_ref[...] += jnp.sum(chosen, axis=1, keepdims=True)
    cnt_ref[...] = run_ref[...].astype(i32)

    pad_i = [jnp.zeros((1, t), i32)] * (8 - TOP_K)
    eid_ref[...] = jnp.concatenate(ids + pad_i, axis=0)
    rank_ref[...] = jnp.concatenate(ranks + pad_i, axis=0)
    wt_ref[...] = jnp.concatenate(wts + [jnp.zeros((1, t), f32)] * (8 - TOP_K), axis=0)


def merge_and_route(x2, mods, rows_per_mod, branches, tile, lw, m_total, row_offset, prior):
    m, d = x2.shape
    off = row_offset // tile
    row = lambda w: pl.BlockSpec((tile, w), lambda i: (i, 0))
    row_off = lambda w: pl.BlockSpec((tile, w), lambda i: (i + off, 0))
    modspec = pl.BlockSpec((1, 1, d), lambda i: ((i * tile) // rows_per_mod, 0, 0))
    col = pl.BlockSpec((8, tile), lambda i: (0, i + off))
    in_specs = [row(d)] + [modspec] * 5 + [_const_spec((1, d)), _const_spec((1, d))] + [row(BRANCH_W)] * 4 + [
        _const_spec((d, N_BRANCHES * d)), _const_spec((N_BRANCHES, BRANCH_W, d)), _const_spec((d, d)),
        _const_spec((N_EXPERTS, d)), _const_spec((N_EXPERTS, 1)), _const_spec((tile, tile))]
    args = [x2, mods["sh1"], mods["sc1"], mods["g1"], mods["sh2"], mods["sc2"], lw["norm1_g"], lw["norm2_g"],
            *branches, lw["w_gate"], lw["w_br"], lw["w_o"], lw["w_router_t"], lw["router_bias"], lw["tri"]]
    aliases = {}
    if prior is not None:
        h2p, eid, wt, rank, counts = prior
        n_in = len(args)
        in_specs += [_const_spec((N_EXPERTS, 1))] + [pl.BlockSpec(memory_space=pl.ANY)] * 4
        args += [counts, h2p, eid, wt, rank]
        aliases = {n_in + 1 + j: 1 + j for j in range(4)}
    return pl.pallas_call(
        functools.partial(_merge_kernel, extends=prior is not None),
        out_shape=[jax.ShapeDtypeStruct((m, d), f32), jax.ShapeDtypeStruct((m_total, d // 2), i32),
                   jax.ShapeDtypeStruct((8, m_total), i32), jax.ShapeDtypeStruct((8, m_total), f32),
                   jax.ShapeDtypeStruct((8, m_total), i32), jax.ShapeDtypeStruct((N_EXPERTS, 1), i32)],
        grid=(m // tile,),
        in_specs=in_specs,
        out_specs=[row(d), row_off(d // 2), col, col, col, pl.BlockSpec((N_EXPERTS, 1), lambda i: (0, 0))],
        scratch_shapes=[pltpu.VMEM((N_EXPERTS, 1), f32)],
        input_output_aliases=aliases,
        compiler_params=_params(("arbitrary",)),
        name="merge_and_route",
    )(*args)


def routing_plan(eid, rank, counts, p_max):
    counts = counts.reshape(N_EXPERTS)
    padded = ((counts + EXPERT_TILE - 1) // EXPERT_TILE) * EXPERT_TILE
    ends = jnp.cumsum(padded)
    starts = ends - padded
    onehot = eid[:, :, None] == jnp.arange(N_EXPERTS, dtype=i32)[None, None, :]
    pos = rank + jnp.sum(jnp.where(onehot, starts[None, None, :], 0), axis=-1)
    n_tiles = p_max // EXPERT_TILE
    tile_start = jnp.arange(n_tiles, dtype=i32) * EXPERT_TILE
    tile_valid = tile_start < ends[-1]
    tile_exp = jnp.sum((ends[None, :] <= tile_start[:, None]).astype(i32), axis=1)
    return pos.astype(i32), jnp.minimum(tile_exp, N_EXPERTS - 1), tile_valid.astype(i32)


def _sc_worker_id():
    return lax.axis_index("subcore") * SC_CORES + lax.axis_index("core")


def sc_scatter_rows(table, pos, p_rows):
    m, w = table.shape
    n_chunks = m // SC_IDX_CHUNK
    steps = -(-n_chunks // SC_WORKERS)
    pos3 = pos.reshape(8, n_chunks, SC_IDX_CHUNK).transpose(1, 0, 2)
    mesh = plsc.VectorSubcoreMesh(core_axis_name="core", subcore_axis_name="subcore")

    @functools.partial(
        pl.kernel,
        out_type=jax.ShapeDtypeStruct((p_rows, w), table.dtype),
        mesh=mesh,
        scratch_types=[
            pltpu.VMEM((8, SC_IDX_CHUNK), i32),
            pltpu.VMEM((SC_IDX_CHUNK, w), table.dtype),
            pltpu.SemaphoreType.DMA,
        ],
    )
    def scatter(x_hbm, p_hbm, o_hbm, idx_v, rows_v, sem):
        wid = _sc_worker_id()

        @pl.loop(0, steps)
        def _(si):
            chunk = si * SC_WORKERS + wid

            @pl.when(chunk < n_chunks)
            def _():
                pltpu.sync_copy(p_hbm.at[chunk], idx_v)
                pltpu.sync_copy(x_hbm.at[pl.ds(chunk * SC_IDX_CHUNK, SC_IDX_CHUNK)], rows_v)
                copies = [pltpu.async_copy(rows_v, o_hbm.at[idx_v.at[k]], sem) for k in range(TOP_K)]
                for cp in copies:
                    cp.wait()

    return scatter(table, pos3)


def sc_gather_rows(table, idx):
    n_idx = idx.shape[0]
    w = table.shape[1]
    n_chunks = n_idx // SC_IDX_CHUNK
    steps = -(-n_chunks // SC_WORKERS)
    half = SC_IDX_CHUNK // 2
    mesh = plsc.VectorSubcoreMesh(core_axis_name="core", subcore_axis_name="subcore")

    @functools.partial(
        pl.kernel,
        out_type=jax.ShapeDtypeStruct((n_idx, w), table.dtype),
        mesh=mesh,
        scratch_types=[
            pltpu.VMEM((SC_IDX_CHUNK,), i32),
            pltpu.VMEM((half, w), table.dtype),
            pltpu.VMEM((half, w), table.dtype),
            pltpu.SemaphoreType.DMA,
            pltpu.SemaphoreType.DMA,
            pltpu.SemaphoreType.DMA,
            pltpu.SemaphoreType.DMA,
        ],
    )
    def gather(x_hbm, i_hbm, o_hbm, idx_v, buf0, buf1, g0_sem, g1_sem, w0_sem, w1_sem):
        wid = _sc_worker_id()

        @pl.loop(0, steps)
        def _(si):
            chunk = si * SC_WORKERS + wid

            @pl.when(chunk < n_chunks)
            def _():
                cbase = chunk * SC_IDX_CHUNK
                pltpu.sync_copy(i_hbm.at[pl.ds(cbase, SC_IDX_CHUNK)], idx_v)
                g0 = pltpu.async_copy(x_hbm.at[idx_v.at[pl.ds(0, half)]], buf0, g0_sem)
                g1 = pltpu.async_copy(x_hbm.at[idx_v.at[pl.ds(half, half)]], buf1, g1_sem)
                g0.wait()
                w0 = pltpu.async_copy(buf0, o_hbm.at[pl.ds(cbase, half)], w0_sem)
                g1.wait()
                w1 = pltpu.async_copy(buf1, o_hbm.at[pl.ds(cbase + half, half)], w1_sem)
                w0.wait()
                w1.wait()

    return gather(table, idx)


def _expert_kernel(te_ref, tv_ref, nx_ref, sl_ref, x_ref, wg_hbm, wu_hbm, wd_hbm, o_ref,
                   wg_f, wu_f, wd_f, wg_b, wu_b, wd_b, sems, *, layer):
    i = pl.program_id(0)

    def weight_copies(expert, slot):
        return [pltpu.make_async_copy(hbm.at[layer, expert], buf.at[slot], sems.at[slot, j])
                for j, (hbm, buf) in enumerate(((wg_hbm, wg_f), (wu_hbm, wu_f), (wd_hbm, wd_f)))]

    @pl.when(i == 0)
    def _():
        for cp in weight_copies(te_ref[0], 0):
            cp.start()

    @pl.when((i == 0) | (te_ref[i] != te_ref[jnp.maximum(i - 1, 0)]))
    def _():
        slot = sl_ref[i]
        for cp in weight_copies(te_ref[i], slot):
            cp.wait()
        wg_b[...] = wg_f[slot].astype(bf16)
        wu_b[...] = wu_f[slot].astype(bf16)
        wd_b[...] = wd_f[slot].astype(bf16)

        @pl.when(nx_ref[i] >= 0)
        def _():
            for cp in weight_copies(nx_ref[i], 1 - slot):
                cp.start()

    @pl.when(tv_ref[i] != 0)
    def _():
        lo, hi = _unpack_bf16_pairs(x_ref[...])
        half = lo.shape[1]
        a = (jnp.dot(lo, wg_b[:half], preferred_element_type=f32)
             + jnp.dot(hi, wg_b[half:], preferred_element_type=f32))
        b = (jnp.dot(lo, wu_b[:half], preferred_element_type=f32)
             + jnp.dot(hi, wu_b[half:], preferred_element_type=f32))
        hid = (_silu(a) * b).astype(bf16)
        o_ref[...] = _pack_bf16_pairs(jnp.dot(hid, wd_b[...], preferred_element_type=f32))

    @pl.when(tv_ref[i] == 0)
    def _():
        o_ref[...] = jnp.zeros_like(o_ref)


def grouped_experts(xs, tile_exp, tile_valid, wg, wu, wd, layer):
    p, half = xs.shape
    d = 2 * half
    n_tiles = p // EXPERT_TILE
    first = jnp.concatenate([jnp.ones((1,), bool), tile_exp[1:] != tile_exp[:-1]])
    slot = (jnp.cumsum(first.astype(i32)) - 1) % 2
    nxt_at = jnp.sum((tile_exp[None, :] <= tile_exp[:, None]).astype(i32), axis=1)
    nxt = jnp.where(nxt_at < n_tiles, tile_exp[jnp.minimum(nxt_at, n_tiles - 1)], -1)
    tile = pl.BlockSpec((EXPERT_TILE, half), lambda i, *_: (i, 0))
    hbm = pl.BlockSpec(memory_space=pl.ANY)
    grid_spec = pltpu.PrefetchScalarGridSpec(
        num_scalar_prefetch=4,
        grid=(n_tiles,),
        in_specs=[tile, hbm, hbm, hbm],
        out_specs=tile,
        scratch_shapes=[pltpu.VMEM((2, d, D_EXPERT), f32), pltpu.VMEM((2, d, D_EXPERT), f32),
                        pltpu.VMEM((2, D_EXPERT, d), f32),
                        pltpu.VMEM((d, D_EXPERT), bf16), pltpu.VMEM((d, D_EXPERT), bf16),
                        pltpu.VMEM((D_EXPERT, d), bf16), pltpu.SemaphoreType.DMA((2, 3))],
    )
    return pl.pallas_call(
        functools.partial(_expert_kernel, layer=layer),
        out_shape=jax.ShapeDtypeStruct((p, half), i32),
        grid_spec=grid_spec,
        compiler_params=_params(("arbitrary",)),
        name="grouped_experts",
    )(tile_exp, tile_valid, nxt.astype(i32), slot.astype(i32), xs, wg, wu, wd)


def _combine_kernel(x_ref, yg_ref, wt_ref, g2_ref, sh2_ref, sc2_ref, n2_ref, wsg_ref, wsu_ref, wsd_ref, fg_ref, o_ref,
                    *, final):
    x = x_ref[...]
    hb = _rms_mod(x, n2_ref[...], sc2_ref[0], sh2_ref[0]).astype(bf16)
    a = jnp.dot(hb, wsg_ref[...], preferred_element_type=f32)
    b = jnp.dot(hb, wsu_ref[...], preferred_element_type=f32)
    f = jnp.dot((_silu(a) * b).astype(bf16), wsd_ref[...], preferred_element_type=f32)
    wt = wt_ref[...]
    half = x.shape[1] // 2
    f_lo, f_hi = f[:, :half], f[:, half:]
    for k in range(TOP_K):
        lo, hi = _unpack_bf16_pairs(yg_ref[k])
        w = wt[:, k:k + 1]
        f_lo = f_lo + w * lo.astype(f32)
        f_hi = f_hi + w * hi.astype(f32)
    xo = x + g2_ref[0] * jnp.concatenate([f_lo, f_hi], axis=1)
    if final:
        ms = jnp.mean(xo * xo, axis=-1, keepdims=True)
        xo = xo * lax.rsqrt(ms + RMS_EPS) * fg_ref[...]
    o_ref[...] = xo


def combine(x2, yg, wt_rows, row_offset, mods, rows_per_mod, tile, lw, final_g, final):
    m, d = x2.shape
    off = row_offset // tile
    row = lambda w: pl.BlockSpec((tile, w), lambda i: (i, 0))
    modspec = pl.BlockSpec((1, 1, d), lambda i: ((i * tile) // rows_per_mod, 0, 0))
    return pl.pallas_call(
        functools.partial(_combine_kernel, final=final),
        out_shape=jax.ShapeDtypeStruct((m, d), f32),
        grid=(m // tile,),
        in_specs=[row(d), pl.BlockSpec((TOP_K, tile, d // 2), lambda i: (0, i + off, 0)),
                  pl.BlockSpec((tile, 8), lambda i: (i + off, 0)), modspec, modspec, modspec,
                  _const_spec((1, d)), _const_spec((d, D_EXPERT)), _const_spec((d, D_EXPERT)),
                  _const_spec((D_EXPERT, d)), _const_spec((1, d))],
        out_specs=row(d),
        compiler_params=_params(("parallel",)),
        name="combine",
    )(x2, yg, wt_rows, mods["g2"], mods["sh2"], mods["sc2"], lw["norm2_g"], lw["w_sh_gate"], lw["w_sh_up"],
      lw["w_sh_down"], final_g)


def routed_experts(h2p, eid, rank, counts, lw, layer):
    m = h2p.shape[0]
    p_max = m * TOP_K + N_EXPERTS * EXPERT_TILE
    pos, tile_exp, tile_valid = routing_plan(eid, rank, counts, p_max)
    xs = sc_scatter_rows(h2p, pos, p_max)
    ys = grouped_experts(xs, tile_exp, tile_valid, lw["w_exp_gate"], lw["w_exp_up"], lw["w_exp_down"], layer)
    return sc_gather_rows(ys, pos[:TOP_K].reshape(TOP_K * m)).reshape(TOP_K, m, D_MODEL // 2)


def rope_tables(seq):
    rows = seq // GRID_W
    row = jnp.repeat(jnp.arange(rows), GRID_W).astype(f32)
    col = jnp.tile(jnp.arange(GRID_W), rows).astype(f32)
    axis_dim = HEAD_DIM // 2
    inv_freq = 1.0 / (ROPE_THETA ** (jnp.arange(0, axis_dim, 2, dtype=f32) / axis_dim))
    ang_r = row[:, None] * inv_freq
    ang_c = col[:, None] * inv_freq
    ang = jnp.concatenate([ang_r, ang_r, ang_c, ang_c], axis=-1)
    cos, sin = jnp.cos(ang), jnp.sin(ang)
    seg = (jnp.arange(HEAD_DIM) // 16) % 2
    sa = jnp.where(seg == 0, -sin, 0.0)
    sb = jnp.where(seg == 1, sin, 0.0)
    rep = lambda t: jnp.tile(t, (1, 4))
    return rep(cos), rep(sa), rep(sb)


def identity_rope_tables(rows):
    return jnp.ones((rows, 256), f32), jnp.zeros((rows, 256), f32), jnp.zeros((rows, 256), f32)


def kernel(x, c, ctx, c_ctx, w_mod, b_mod, norm1_g, norm2_g, w_in, q_norm_g, k_norm_g, sink, gm_norm_g, gm_ws, gm_b, w_br, w_o, w_router, router_bias, w_exp_gate, w_exp_up, w_exp_down, w_sh_gate, w_sh_up, w_sh_down, final_norm_g):
    bsz_all, seq, d = x.shape
    n_ctx = ctx.shape[1]
    depth = w_mod.shape[0]

    cc = jnp.concatenate([c, c_ctx[None, :], jnp.zeros((MOD_ROWS - bsz_all - 1, d), f32)], axis=0)
    mod_all = compute_mod(cc, w_mod, b_mod)

    lat_tables = rope_tables(seq)
    ctx_tables = identity_rope_tables(n_ctx)
    wc_lat, ws_lat = dft_tables(seq)
    wc_ctx, ws_ctx = dft_tables(n_ctx)
    cs64 = channel_dft_table()
    bd = jnp.asarray(np.kron(np.eye(4), np.full((HEAD_DIM, HEAD_DIM), 1.0 / HEAD_DIM)), dtype=bf16)
    final_g = final_norm_g.reshape(1, d)
    tri = jnp.asarray(np.triu(np.ones((ROW_TILE, ROW_TILE)), 1), dtype=bf16)

    lws = []
    for l in range(depth):
        lws.append({
            "norm1_g": norm1_g[l].reshape(1, d),
            "norm2_g": norm2_g[l].reshape(1, d),
            "w_z": w_in[l, :, :OFF_GATE].astype(bf16),
            "w_gate": w_in[l, :, OFF_GATE:].astype(bf16),
            "qn": jnp.tile(q_norm_g[l], 4).reshape(1, 256),
            "kn": jnp.tile(k_norm_g[l], 2).reshape(1, 128),
            "bd": bd,
            "gm_norm_g": gm_norm_g[l].reshape(1, 256),
            "gm_ws": gm_ws[l].astype(bf16),
            "gm_bias": jnp.repeat(gm_b[l].T, 64, axis=1),
            "cs64": cs64,
            "w_br": w_br[l].astype(bf16),
            "w_o": w_o[l].astype(bf16),
            "w_router_t": w_router[l].T,
            "router_bias": router_bias[l].reshape(N_EXPERTS, 1),
            "tri": tri,
            "w_exp_gate": w_exp_gate,
            "w_exp_up": w_exp_up,
            "w_exp_down": w_exp_down,
            "w_sh_gate": w_sh_gate[l].astype(bf16),
            "w_sh_up": w_sh_up[l].astype(bf16),
            "w_sh_down": w_sh_down[l].astype(bf16),
        })

    bsz = bsz_all // BATCH_CHUNKS
    outs = []
    for ch in range(BATCH_CHUNKS):
        b0 = ch * bsz
        outs.append(_forward_chunk(x[b0:b0 + bsz], ctx[b0:b0 + bsz], mod_all[:, b0:b0 + bsz], mod_all[:, bsz_all],
                                   lws, sink, lat_tables, ctx_tables, (wc_lat, ws_lat), (wc_ctx, ws_ctx), final_g))
    return jnp.concatenate(outs, axis=0)


def _forward_chunk(x, ctx, mod_lat, mod_ctx, lws, sink, lat_tables, ctx_tables, dft_lat, dft_ctx, final_g):
    bsz, seq, d = x.shape
    n_ctx = ctx.shape[1]
    depth = len(lws)
    n_lat = bsz * seq
    n_cx = bsz * n_ctx
    wc_lat, ws_lat = dft_lat
    wc_ctx, ws_ctx = dft_ctx
    xl = x.reshape(n_lat, d)
    xc = ctx.reshape(n_cx, d)
    for l in range(depth):
        ctx_out = l < depth - 1
        lw = lws[l]
        names = ("sh1", "sc1", "g1", "sh2", "sc2", "g2")
        mods_lat = {n: mod_lat[l, :, i * d:(i + 1) * d].reshape(bsz, 1, d) for i, n in enumerate(names)}
        mods_ctx = {n: mod_ctx[l, i * d:(i + 1) * d].reshape(1, 1, d) for i, n in enumerate(names)}
        sink_l = sink[l]

        qg, kg, vg, qw, kw, vw, fxc, fxs, o_d = in_projection(
            xl, mods_lat["sh1"], mods_lat["sc1"], seq, lat_tables, seq, ROW_TILE, lw)
        cqg, ckg, cvg, cqw, ckw, cvw, cfxc, cfxs, co_d = in_projection(
            xc, mods_ctx["sh1"], mods_ctx["sc1"], n_cx, ctx_tables, n_ctx, n_ctx, lw)
        b3 = lambda t, rows: t.reshape(bsz, rows, t.shape[-1])
        ckg3, cvg3, ckw3, cvw3 = b3(ckg, n_ctx), b3(cvg, n_ctx), b3(ckw, n_ctx), b3(cvw, n_ctx)

        o_a = dft_mix(wc_lat, ws_lat, b3(fxc, seq), b3(fxs, seq), ROW_TILE).reshape(n_lat, BRANCH_W)
        o_b = full_attention(b3(qg, seq), [(ckg3, cvg3), (b3(kg, seq), b3(vg, seq))], None, 256).reshape(n_lat, BRANCH_W)
        o_c = window_attention(b3(qw, seq), b3(kw, seq), b3(vw, seq), ckw3, cvw3, sink_l).reshape(n_lat, BRANCH_W)
        m_total = n_lat + (n_cx if ctx_out else 0)
        route = None
        if ctx_out:
            route = [jnp.zeros((m_total, d // 2), i32), jnp.zeros((8, m_total), i32), jnp.zeros((8, m_total), f32),
                     jnp.zeros((8, m_total), i32), jnp.zeros((N_EXPERTS, 1), i32)]
        xl, *route = merge_and_route(xl, mods_lat, seq, (o_a, o_b, o_c, o_d), ROW_TILE, lw, m_total, 0, route)
        if ctx_out:
            co_a = dft_mix(wc_ctx, ws_ctx, b3(cfxc, n_ctx), b3(cfxs, n_ctx), n_ctx).reshape(n_cx, BRANCH_W)
            co_b = full_attention(b3(cqg, n_ctx), [(ckg3, cvg3)], None, n_ctx).reshape(n_cx, BRANCH_W)
            co_c = full_attention(b3(cqw, n_ctx), [(ckw3, cvw3)], sink_l, n_ctx).reshape(n_cx, BRANCH_W)
            xc, *route = merge_and_route(xc, mods_ctx, n_cx, (co_a, co_b, co_c, co_d), ROW_TILE, lw, m_total, n_lat,
                                         route)
        h2p, eid, wt, rank, counts = route
        yg = routed_experts(h2p, eid, rank, counts, lw, l)
        wt_rows = wt.T
        xl = combine(xl, yg, wt_rows, 0, mods_lat, seq, ROW_TILE, lw, final_g, not ctx_out)
        if ctx_out:
            xc = combine(xc, yg, wt_rows, n_lat, mods_ctx, n_cx, ROW_TILE, lw, final_g, False)
    return xl.reshape(bsz, seq, d)
```

```python
import functools
import math

import jax
import jax.numpy as jnp
import numpy as np
from jax import lax
from jax.experimental import pallas as pl
from jax.experimental.pallas import tpu as pltpu
from jax.experimental.pallas import tpu_sc as plsc

f32 = jnp.float32
bf16 = jnp.bfloat16
i32 = jnp.int32

D_MODEL = 1024
HEAD_DIM = 64
GRID_W = 64
ROPE_THETA = 10000.0
ATTN_SCALE = HEAD_DIM ** -0.5
RMS_EPS = 1e-6
NEG_INF = -1e30
Q_BLOCK = 128
WINDOW = 128
GM_CHUNK = 128
N_BRANCHES = 4
BRANCH_W = 256
KV_W = 128
OFF_KV = 0
OFF_Q = 512
OFF_FN = 1024
OFF_GM = 1280
OFF_GATE = 1792
N_EXPERTS = 64
TOP_K = 6
N_GROUPS = 8
GROUP_SIZE = N_EXPERTS // N_GROUPS
TOPK_GROUPS = 4
D_EXPERT = 256
ROUTED_SCALE = 2.5

SC_CORES = 2
SC_SUBCORES = 16
SC_WORKERS = SC_CORES * SC_SUBCORES
SC_IDX_CHUNK = 128

ROW_TILE = 512
EXPERT_TILE = 512
BATCH_CHUNKS = 1
WINDOW_BLOCKS_PER_STEP = 4
MOD_ROWS = 24
VMEM_LIMIT = 56 * 1024 * 1024


def _params(sem, vmem=VMEM_LIMIT):
    return pltpu.CompilerParams(dimension_semantics=sem, vmem_limit_bytes=vmem)


def _const_spec(shape):
    nd = len(shape)
    return pl.BlockSpec(shape, lambda *_: (0,) * nd, pipeline_mode=pl.Buffered(1))


def _rms_mod(x, g, sc, sh):
    ms = jnp.mean(x * x, axis=-1, keepdims=True)
    return (x * lax.rsqrt(ms + RMS_EPS) * g) * (1.0 + sc) + sh


def _gelu(x):
    return 0.5 * x * (1.0 + jnp.tanh(math.sqrt(2.0 / math.pi) * (x + 0.044715 * (x * x * x))))


def _silu(x):
    return x * jax.nn.sigmoid(x)


def _mod_kernel(a_ref, w_ref, b_ref, o_ref):
    a = _silu(a_ref[...]).astype(bf16)
    o_ref[0] = jnp.dot(a, w_ref[0].astype(bf16), preferred_element_type=f32) + b_ref[0]


def compute_mod(cc, w_mod, b_mod):
    depth, d, n = w_mod.shape
    tn = 1536
    return pl.pallas_call(
        _mod_kernel,
        out_shape=jax.ShapeDtypeStruct((depth, MOD_ROWS, n), f32),
        grid=(depth, n // tn),
        in_specs=[
            pl.BlockSpec((MOD_ROWS, d), lambda l, j: (0, 0)),
            pl.BlockSpec((1, d, tn), lambda l, j: (l, 0, j)),
            pl.BlockSpec((1, 1, tn), lambda l, j: (l, 0, j)),
        ],
        out_specs=pl.BlockSpec((1, MOD_ROWS, tn), lambda l, j: (l, 0, j)),
        compiler_params=_params(("parallel", "parallel")),
        name="mod_proj",
    )(cc, w_mod, b_mod.reshape(depth, 1, n))


def _inproj_kernel(x_ref, sh_ref, sc_ref, g_ref, w_ref, qn_ref, kn_ref, bd_ref, gmg_ref, ws_ref, gb_ref, cs_ref,
                   cos_ref, sa_ref, sb_ref,
                   qg_ref, kg_ref, vg_ref, qw_ref, kw_ref, vw_ref, xc_ref, xs_ref, od_ref):
    tile = x_ref.shape[0]
    hb = _rms_mod(x_ref[...], g_ref[...], sc_ref[0], sh_ref[0]).astype(bf16)

    def proj(a, b):
        return jnp.dot(hb, w_ref[:, a:b], preferred_element_type=f32)

    def headnorm(t, gain):
        w = t.shape[1]
        sq = t * t
        hi = sq.astype(bf16)
        lo = (sq - hi.astype(f32)).astype(bf16)
        b = bd_ref[:w, :w]
        ms = jnp.dot(hi, b, preferred_element_type=f32) + jnp.dot(lo, b, preferred_element_type=f32)
        return t * lax.rsqrt(ms + RMS_EPS) * gain

    def rope(t):
        w = t.shape[1]
        return (t * cos_ref[:, :w] + pltpu.roll(t, w - 16, 1) * sa_ref[:, :w]
                + pltpu.roll(t, 16, 1) * sb_ref[:, :w])

    def expand_heads(q):
        lane = lax.broadcasted_iota(i32, (1, KV_W), 1)
        low = lane < HEAD_DIM
        blocks = []
        for kv in range(2):
            pair = q[:, KV_W * kv:KV_W * (kv + 1)]
            swapped = pltpu.roll(pair, HEAD_DIM, 1)
            keep = low if kv == 0 else jnp.logical_not(low)
            g0, g1 = (pair, swapped) if kv == 0 else (swapped, pair)
            blocks.append(jnp.where(keep, g0, 0.0))
            blocks.append(jnp.where(keep, g1, 0.0))
        return jnp.concatenate(blocks, axis=1)

    kv = proj(OFF_KV, OFF_Q)
    kg_ref[...] = rope(headnorm(kv[:, 0:128], kn_ref[...])).astype(bf16)
    vg_ref[...] = kv[:, 128:256].astype(bf16)
    kw_ref[...] = rope(kv[:, 256:384]).astype(bf16)
    vw_ref[...] = kv[:, 384:512].astype(bf16)

    qq = proj(OFF_Q, OFF_FN)
    qg = rope(headnorm(qq[:, :256], qn_ref[...])) * ATTN_SCALE
    qg_ref[...] = expand_heads(qg).astype(bf16)
    qw = rope(qq[:, 256:]) * ATTN_SCALE
    qw_ref[...] = expand_heads(qw).astype(bf16)

    fn = proj(OFF_FN, OFF_GM).astype(bf16)
    xcs = jnp.dot(fn, cs_ref[...], preferred_element_type=f32)
    xc_ref[...] = xcs[:, :256].astype(bf16)
    xs_ref[...] = xcs[:, 256:].astype(bf16)

    uv = proj(OFF_GM, OFF_GATE)
    u = _gelu(uv[:, :256])
    v = _gelu(uv[:, 256:])
    vms = jnp.mean(v * v, axis=-1, keepdims=True)
    vn = (v * lax.rsqrt(vms + RMS_EPS) * gmg_ref[...]).astype(bf16)
    lane_grp = lax.broadcasted_iota(i32, (1, 256), 1) // 64
    for c in range(tile // GM_CHUNK):
        rows = slice(c * GM_CHUNK, (c + 1) * GM_CHUNK)
        vch = vn[rows]
        sv = gb_ref[...]
        for g in range(4):
            r = jnp.dot(ws_ref[g], vch, preferred_element_type=f32)
            sv = sv + jnp.where(lane_grp == g, r, 0.0)
        od_ref[rows, :] = (u[rows] * sv).astype(bf16)


def in_projection(x2, sh, sc, rows_per_mod, tables, rows_per_seq, tile, lw):
    m, d = x2.shape
    cos_t, sa_t, sb_t = tables
    seq_blocks = rows_per_seq // tile
    row = lambda w: pl.BlockSpec((tile, w), lambda i: (i, 0))
    modspec = pl.BlockSpec((1, 1, d), lambda i: ((i * tile) // rows_per_mod, 0, 0))
    tabspec = pl.BlockSpec((tile, 256), lambda i: (i % seq_blocks, 0))
    out_w = [512, 128, 128, 512, 128, 128, 256, 256, 256]
    return pl.pallas_call(
        _inproj_kernel,
        out_shape=[jax.ShapeDtypeStruct((m, w), bf16) for w in out_w],
        grid=(m // tile,),
        in_specs=[
            row(d), modspec, modspec, _const_spec((1, d)), _const_spec((d, OFF_GATE)),
            _const_spec((1, 256)), _const_spec((1, 128)), _const_spec((256, 256)), _const_spec((1, 256)),
            _const_spec((4, GM_CHUNK, GM_CHUNK)), _const_spec((GM_CHUNK, 256)), _const_spec((256, 512)),
            tabspec, tabspec, tabspec,
        ],
        out_specs=[row(w) for w in out_w],
        compiler_params=_params(("parallel",)),
        name="in_projection",
    )(x2, sh, sc, lw["norm1_g"], lw["w_z"], lw["qn"], lw["kn"], lw["bd"], lw["gm_norm_g"], lw["gm_ws"], lw["gm_bias"],
      lw["cs64"], cos_t, sa_t, sb_t)


def _dft_kernel(wc_ref, ws_ref, xc_ref, xs_ref, o_ref):
    acc = jnp.dot(wc_ref[...], xc_ref[0], preferred_element_type=f32)
    acc = acc + jnp.dot(ws_ref[...], xs_ref[0], preferred_element_type=f32)
    o_ref[0] = acc.astype(bf16)


def dft_mix(wc, ws, xc, xs, tile):
    nb, length, w = xc.shape
    return pl.pallas_call(
        _dft_kernel,
        out_shape=jax.ShapeDtypeStruct((nb, length, w), bf16),
        grid=(length // tile, nb),
        in_specs=[
            pl.BlockSpec((tile, length), lambda i, b: (i, 0)),
            pl.BlockSpec((tile, length), lambda i, b: (i, 0)),
            pl.BlockSpec((1, length, w), lambda i, b: (b, 0, 0)),
            pl.BlockSpec((1, length, w), lambda i, b: (b, 0, 0)),
        ],
        out_specs=pl.BlockSpec((1, tile, w), lambda i, b: (b, i, 0)),
        compiler_params=_params(("parallel", "parallel")),
        name="dft_mix",
    )(wc, ws, xc, xs)


def dft_tables(length):
    jk = (np.arange(length)[:, None] * np.arange(length)[None, :]) % length
    ang = 2.0 * np.pi * jk / length
    s = 1.0 / math.sqrt(length)
    return jnp.asarray(np.cos(ang) * s, dtype=bf16), jnp.asarray(-np.sin(ang) * s, dtype=bf16)


def channel_dft_table():
    jk = (np.arange(64)[:, None] * np.arange(64)[None, :]) % 64
    ang = 2.0 * np.pi * jk / 64
    eye = np.eye(4)
    c = np.kron(eye, np.cos(ang) / 8.0)
    s = np.kron(eye, np.sin(ang) / 8.0)
    return jnp.asarray(np.concatenate([c, s], axis=1), dtype=bf16)


def _attend_pieces(q, pieces, sink_col, kv):
    lane = lax.broadcasted_iota(i32, (1, KV_W), 1)
    own = (lane < HEAD_DIM) if kv == 0 else (lane >= HEAD_DIM)
    scores = []
    for k, _, mask in pieces:
        s = lax.dot_general(q, k, (((1,), (1,)), ((), ())), preferred_element_type=f32)
        if mask is not None:
            s = jnp.where(mask, s, NEG_INF)
        scores.append(s)
    m = scores[0].max(axis=-1, keepdims=True)
    for s in scores[1:]:
        m = jnp.maximum(m, s.max(axis=-1, keepdims=True))
    if sink_col is not None:
        m = jnp.maximum(m, sink_col)
    acc = None
    for s, (_, v, _) in zip(scores, pieces):
        p = jnp.exp((s - m).astype(bf16))
        pv = jnp.dot(p, jnp.where(own, v, jnp.ones_like(v)), preferred_element_type=f32)
        acc = pv if acc is None else acc + pv
    denom = pltpu.roll(acc, HEAD_DIM, 1)
    if sink_col is not None:
        denom = denom + jnp.exp(sink_col - m)
    return acc * (1.0 / denom)


def _attend_heads(q_all, piece_fn, sink_ref, qb):
    lane = lax.broadcasted_iota(i32, (1, KV_W), 1)
    low = lane < HEAD_DIM
    halves = []
    for kv in range(2):
        q = jnp.concatenate([q_all[:, KV_W * (2 * kv):KV_W * (2 * kv + 1)],
                             q_all[:, KV_W * (2 * kv + 1):KV_W * (2 * kv + 2)]], axis=0)
        if sink_ref is not None:
            s0 = jnp.full((qb, 1), sink_ref[2 * kv], f32)
            s1 = jnp.full((qb, 1), sink_ref[2 * kv + 1], f32)
            sink_col = jnp.concatenate([s0, s1], axis=0)
        else:
            sink_col = None
        res = _attend_pieces(q, piece_fn(), sink_col, kv)
        r0, r1 = res[:qb], res[qb:]
        if kv == 0:
            halves.append(jnp.where(low, r0, pltpu.roll(r1, HEAD_DIM, 1)))
        else:
            halves.append(jnp.where(low, pltpu.roll(r0, HEAD_DIM, 1), r1))
    return jnp.concatenate(halves, axis=1)


def _full_attn_kernel(*refs, n_pieces, has_sink, qb):
    pos = 0
    sink_ref = None
    if has_sink:
        sink_ref = refs[0]
        pos = 1
    q_ref = refs[pos]
    kv_refs = refs[pos + 1:pos + 1 + 2 * n_pieces]
    o_ref = refs[pos + 1 + 2 * n_pieces]

    def piece_fn():
        return [(kv_refs[2 * i][0], kv_refs[2 * i + 1][0], None) for i in range(n_pieces)]

    o_ref[0] = _attend_heads(q_ref[0], piece_fn, sink_ref, qb).astype(bf16)


def full_attention(q, pieces, sink, qb):
    nb, lq, _ = q.shape
    in_specs = []
    args = []
    if sink is not None:
        in_specs.append(pl.BlockSpec(memory_space=pltpu.SMEM))
        args.append(sink)
    in_specs.append(pl.BlockSpec((1, qb, 512), lambda b, i: (b, i, 0)))
    args.append(q)
    for k, v in pieces:
        spec = pl.BlockSpec((1, k.shape[1], KV_W), lambda b, i: (b, 0, 0))
        in_specs += [spec, spec]
        args += [k, v]
    return pl.pallas_call(
        functools.partial(_full_attn_kernel, n_pieces=len(pieces), has_sink=sink is not None, qb=qb),
        out_shape=jax.ShapeDtypeStruct((nb, lq, 256), bf16),
        grid=(nb, lq // qb),
        in_specs=in_specs,
        out_specs=pl.BlockSpec((1, qb, 256), lambda b, i: (b, i, 0)),
        compiler_params=_params(("parallel", "parallel")),
        name="full_attention",
    )(*args)


def _window_attn_kernel(sink_ref, q_ref, k_ref, v_ref, kc_ref, vc_ref, o_ref, *, seq, blocks):
    span = 3 * Q_BLOCK
    for j in range(blocks):
        n = pl.program_id(1) * blocks + j
        rows = slice(j * Q_BLOCK, (j + 1) * Q_BLOCK)
        start = pl.multiple_of(jnp.clip((n - 1) * Q_BLOCK, 0, seq - span), Q_BLOCK)
        kwin = k_ref[0, pl.ds(start, span), :]
        vwin = v_ref[0, pl.ds(start, span), :]
        row = lax.broadcasted_iota(i32, (2 * Q_BLOCK, span), 0) % Q_BLOCK + n * Q_BLOCK
        col = lax.broadcasted_iota(i32, (2 * Q_BLOCK, span), 1) + start
        mask = jnp.abs(row - col) <= WINDOW

        def piece_fn():
            return [(kc_ref[0], vc_ref[0], None), (kwin, vwin, mask)]

        o_ref[0, rows, :] = _attend_heads(q_ref[0, rows, :], piece_fn, sink_ref, Q_BLOCK).astype(bf16)


def window_attention(q, k, v, kc, vc, sink):
    nb, seq, _ = q.shape
    n_ctx = kc.shape[1]
    blocks = WINDOW_BLOCKS_PER_STEP
    full = lambda l: pl.BlockSpec((1, l, KV_W), lambda b, i: (b, 0, 0))
    return pl.pallas_call(
        functools.partial(_window_attn_kernel, seq=seq, blocks=blocks),
        out_shape=jax.ShapeDtypeStruct((nb, seq, 256), bf16),
        grid=(nb, seq // (Q_BLOCK * blocks)),
        in_specs=[pl.BlockSpec(memory_space=pltpu.SMEM),
                  pl.BlockSpec((1, Q_BLOCK * blocks, 512), lambda b, i: (b, i, 0)),
                  full(seq), full(seq), full(n_ctx), full(n_ctx)],
        out_specs=pl.BlockSpec((1, Q_BLOCK * blocks, 256), lambda b, i: (b, i, 0)),
        compiler_params=_params(("parallel", "parallel")),
        name="window_attention",
    )(sink, q, k, v, kc, vc)


def _route(logits_t, bias_col):
    t = logits_t.shape[1]
    scores = jax.nn.sigmoid(logits_t)
    choice = scores + bias_col
    sub = lax.broadcasted_iota(i32, (GROUP_SIZE, t), 0)
    grp_score = []
    for g in range(N_GROUPS):
        cg = choice[g * GROUP_SIZE:(g + 1) * GROUP_SIZE]
        m1 = cg.max(axis=0, keepdims=True)
        first = jnp.min(jnp.where(cg == m1, sub, GROUP_SIZE), axis=0, keepdims=True)
        m2 = jnp.where(sub == first, -jnp.inf, cg).max(axis=0, keepdims=True)
        grp_score.append(m1 + m2)
    keep = []
    for g in range(N_GROUPS):
        beaten = jnp.zeros((1, t), i32)
        for o in range(N_GROUPS):
            if o == g:
                continue
            wins = (grp_score[o] > grp_score[g]) | ((grp_score[o] == grp_score[g]) & (o < g))
            beaten = beaten + wins.astype(i32)
        keep.append(jnp.broadcast_to(beaten < TOPK_GROUPS, (GROUP_SIZE, t)))
    masked = jnp.where(jnp.concatenate(keep, axis=0), choice, NEG_INF)
    eid = lax.broadcasted_iota(i32, (N_EXPERTS, t), 0)
    ids, wts = [], []
    for _ in range(TOP_K):
        m = masked.max(axis=0, keepdims=True)
        pick = jnp.min(jnp.where(masked == m, eid, N_EXPERTS), axis=0, keepdims=True)
        sel = eid == pick
        ids.append(pick)
        wts.append(jnp.sum(jnp.where(sel, scores, 0.0), axis=0, keepdims=True))
        masked = jnp.where(sel, -jnp.inf, masked)
    total = wts[0]
    for w in wts[1:]:
        total = total + w
    norm = ROUTED_SCALE / total
    return ids, [w * norm for w in wts]


def _pack_bf16_pairs(x):
    w = x.shape[1] // 2
    lo = lax.bitcast_convert_type(x[:, :w].astype(bf16).astype(f32), i32)
    hi = lax.bitcast_convert_type(x[:, w:].astype(bf16).astype(f32), i32)
    return lax.shift_right_logical(lo, 16) | (hi & jnp.int32(-65536))


def _unpack_bf16_pairs(p):
    lo = lax.bitcast_convert_type(lax.shift_left(p, 16), f32)
    hi = lax.bitcast_convert_type(p & jnp.int32(-65536), f32)
    return lo.astype(bf16), hi.astype(bf16)


def _merge_kernel(x_ref, sh_ref, sc_ref, g1_ref, sh2_ref, sc2_ref, n1_ref, n2_ref, oa_ref, ob_ref, oc_ref, od_ref,
                  wg_ref, wbr_ref, wo_ref, wr_ref, rb_ref, tri_ref, *rest, extends):
    if extends:
        cnt_in_ref = rest[0]
        rest = rest[5:]
    xo_ref, h2_ref, eid_ref, wt_ref, rank_ref, cnt_ref, run_ref = rest

    @pl.when(pl.program_id(0) == 0)
    def _():
        run_ref[...] = cnt_in_ref[...].astype(f32) if extends else jnp.zeros_like(run_ref)

    x = x_ref[...]
    hb = _rms_mod(x, n1_ref[...], sc_ref[0], sh_ref[0]).astype(bf16)
    y = None
    for i, o_ref in enumerate((oa_ref, ob_ref, oc_ref, od_ref)):
        logit = jnp.dot(hb, wg_ref[:, i * D_MODEL:(i + 1) * D_MODEL], preferred_element_type=f32)
        proj = jnp.dot(o_ref[...], wbr_ref[i], preferred_element_type=f32)
        term = jax.nn.sigmoid(logit.astype(bf16)) * proj.astype(bf16)
        y = term if y is None else y + term
    xn = x + g1_ref[0] * jnp.dot(y, wo_ref[...], preferred_element_type=f32)
    xo_ref[...] = xn
    h2 = _rms_mod(xn, n2_ref[...], sc2_ref[0], sh2_ref[0])
    h2_ref[...] = _pack_bf16_pairs(h2)
    logits_t = lax.dot_general(wr_ref[...], h2, (((1,), (1,)), ((), ())), preferred_element_type=f32,
                               precision=lax.Precision.HIGHEST)
    ids, wts = _route(logits_t, rb_ref[...])

    t = x.shape[0]
    eid = lax.broadcasted_iota(i32, (N_EXPERTS, t), 0)
    hits = [eid == pick for pick in ids]
    chosen = hits[0]
    for h in hits[1:]:
        chosen = chosen | h
    chosen = jnp.where(chosen, 1.0, 0.0)
    prefix = jnp.dot(chosen.astype(bf16), tri_ref[...], preferred_element_type=f32)
    offset = run_ref[...] + prefix
    ranks = [jnp.sum(jnp.where(h, offset, 0.0), axis=0, keepdims=True).astype(i32) for h in hits]
    run_ref[...] += jnp.sum(chosen, axis=1, keepdims=True)
    cnt_ref[...] = run_ref[...].astype(i32)

    pad_i = [jnp.zeros((1, t), i32)] * (8 - TOP_K)
    eid_ref[...] = jnp.concatenate(ids + pad_i, axis=0)
    rank_ref[...] = jnp.concatenate(ranks + pad_i, axis=0)
    wt_ref[...] = jnp.concatenate(wts + [jnp.zeros((1, t), f32)] * (8 - TOP_K), axis=0)


def merge_and_route(x2, mods, rows_per_mod, branches, tile, lw, m_total, row_offset, prior):
    m, d = x2.shape
    off = row_offset // tile
    row = lambda w: pl.BlockSpec((tile, w), lambda i: (i, 0))
    row_off = lambda w: pl.BlockSpec((tile, w), lambda i: (i + off, 0))
    modspec = pl.BlockSpec((1, 1, d), lambda i: ((i * tile) // rows_per_mod, 0, 0))
    col = pl.BlockSpec((8, tile), lambda i: (0, i + off))
    in_specs = [row(d)] + [modspec] * 5 + [_const_spec((1, d)), _const_spec((1, d))] + [row(BRANCH_W)] * 4 + [
        _const_spec((d, N_BRANCHES * d)), _const_spec((N_BRANCHES, BRANCH_W, d)), _const_spec((d, d)),
        _const_spec((N_EXPERTS, d)), _const_spec((N_EXPERTS, 1)), _const_spec((tile, tile))]
    args = [x2, mods["sh1"], mods["sc1"], mods["g1"], mods["sh2"], mods["sc2"], lw["norm1_g"], lw["norm2_g"],
            *branches, lw["w_gate"], lw["w_br"], lw["w_o"], lw["w_router_t"], lw["router_bias"], lw["tri"]]
    aliases = {}
    if prior is not None:
        h2p, eid, wt, rank, counts = prior
        n_in = len(args)
        in_specs += [_const_spec((N_EXPERTS, 1))] + [pl.BlockSpec(memory_space=pl.ANY)] * 4
        args += [counts, h2p, eid, wt, rank]
        aliases = {n_in + 1 + j: 1 + j for j in range(4)}
    return pl.pallas_call(
        functools.partial(_merge_kernel, extends=prior is not None),
        out_shape=[jax.ShapeDtypeStruct((m, d), f32), jax.ShapeDtypeStruct((m_total, d // 2), i32),
                   jax.ShapeDtypeStruct((8, m_total), i32), jax.ShapeDtypeStruct((8, m_total), f32),
                   jax.ShapeDtypeStruct((8, m_total), i32), jax.ShapeDtypeStruct((N_EXPERTS, 1), i32)],
        grid=(m // tile,),
        in_specs=in_specs,
        out_specs=[row(d), row_off(d // 2), col, col, col, pl.BlockSpec((N_EXPERTS, 1), lambda i: (0, 0))],
        scratch_shapes=[pltpu.VMEM((N_EXPERTS, 1), f32)],
        input_output_aliases=aliases,
        compiler_params=_params(("arbitrary",)),
        name="merge_and_route",
    )(*args)


def routing_plan(eid, rank, counts, p_max):
    counts = counts.reshape(N_EXPERTS)
    padded = ((counts + EXPERT_TILE - 1) // EXPERT_TILE) * EXPERT_TILE
    ends = jnp.cumsum(padded)
    starts = ends - padded
    onehot = eid[:, :, None] == jnp.arange(N_EXPERTS, dtype=i32)[None, None, :]
    pos = rank + jnp.sum(jnp.where(onehot, starts[None, None, :], 0), axis=-1)
    n_tiles = p_max // EXPERT_TILE
    tile_start = jnp.arange(n_tiles, dtype=i32) * EXPERT_TILE
    tile_valid = tile_start < ends[-1]
    tile_exp = jnp.sum((ends[None, :] <= tile_start[:, None]).astype(i32), axis=1)
    return pos.astype(i32), jnp.minimum(tile_exp, N_EXPERTS - 1), tile_valid.astype(i32)


def _sc_worker_id():
    return lax.axis_index("subcore") * SC_CORES + lax.axis_index("core")


def sc_scatter_rows(table, pos, p_rows):
    m, w = table.shape
    n_chunks = m // SC_IDX_CHUNK
    steps = -(-n_chunks // SC_WORKERS)
    pos3 = pos.reshape(8, n_chunks, SC_IDX_CHUNK).transpose(1, 0, 2)
    mesh = plsc.VectorSubcoreMesh(core_axis_name="core", subcore_axis_name="subcore")

    @functools.partial(
        pl.kernel,
        out_type=jax.ShapeDtypeStruct((p_rows, w), table.dtype),
        mesh=mesh,
        scratch_types=[
            pltpu.VMEM((8, SC_IDX_CHUNK), i32),
            pltpu.VMEM((SC_IDX_CHUNK, w), table.dtype),
            pltpu.SemaphoreType.DMA,
        ],
    )
    def scatter(x_hbm, p_hbm, o_hbm, idx_v, rows_v, sem):
        wid = _sc_worker_id()

        @pl.loop(0, steps)
        def _(si):
            chunk = si * SC_WORKERS + wid

            @pl.when(chunk < n_chunks)
            def _():
                pltpu.sync_copy(p_hbm.at[chunk], idx_v)
                pltpu.sync_copy(x_hbm.at[pl.ds(chunk * SC_IDX_CHUNK, SC_IDX_CHUNK)], rows_v)
                copies = [pltpu.async_copy(rows_v, o_hbm.at[idx_v.at[k]], sem) for k in range(TOP_K)]
                for cp in copies:
                    cp.wait()

    return scatter(table, pos3)


def sc_gather_rows(table, idx):
    n_idx = idx.shape[0]
    w = table.shape[1]
    n_chunks = n_idx // SC_IDX_CHUNK
    steps = -(-n_chunks // SC_WORKERS)
    half = SC_IDX_CHUNK // 2
    mesh = plsc.VectorSubcoreMesh(core_axis_name="core", subcore_axis_name="subcore")

    @functools.partial(
        pl.kernel,
        out_type=jax.ShapeDtypeStruct((n_idx, w), table.dtype),
        mesh=mesh,
        scratch_types=[
            pltpu.VMEM((SC_IDX_CHUNK,), i32),
            pltpu.VMEM((half, w), table.dtype),
            pltpu.VMEM((half, w), table.dtype),
            pltpu.SemaphoreType.DMA,
            pltpu.SemaphoreType.DMA,
            pltpu.SemaphoreType.DMA,
            pltpu.SemaphoreType.DMA,
        ],
    )
    def gather(x_hbm, i_hbm, o_hbm, idx_v, buf0, buf1, g0_sem, g1_sem, w0_sem, w1_sem):
        wid = _sc_worker_id()

        @pl.loop(0, steps)
        def _(si):
            chunk = si * SC_WORKERS + wid

            @pl.when(chunk < n_chunks)
            def _():
                cbase = chunk * SC_IDX_CHUNK
                pltpu.sync_copy(i_hbm.at[pl.ds(cbase, SC_IDX_CHUNK)], idx_v)
                g0 = pltpu.async_copy(x_hbm.at[idx_v.at[pl.ds(0, half)]], buf0, g0_sem)
                g1 = pltpu.async_copy(x_hbm.at[idx_v.at[pl.ds(half, half)]], buf1, g1_sem)
                g0.wait()
                w0 = pltpu.async_copy(buf0, o_hbm.at[pl.ds(cbase, half)], w0_sem)
                g1.wait()
                w1 = pltpu.async_copy(buf1, o_hbm.at[pl.ds(cbase + half, half)], w1_sem)
                w0.wait()
                w1.wait()

    return gather(table, idx)


def _expert_kernel(te_ref, tv_ref, nx_ref, sl_ref, x_ref, wg_hbm, wu_hbm, wd_hbm, o_ref,
                   wg_f, wu_f, wd_f, wg_b, wu_b, wd_b, sems, *, layer):
    i = pl.program_id(0)

    def weight_copies(expert, slot):
        return [pltpu.make_async_copy(hbm.at[layer, expert], buf.at[slot], sems.at[slot, j])
                for j, (hbm, buf) in enumerate(((wg_hbm, wg_f), (wu_hbm, wu_f), (wd_hbm, wd_f)))]

    @pl.when(i == 0)
    def _():
        for cp in weight_copies(te_ref[0], 0):
            cp.start()

    @pl.when((i == 0) | (te_ref[i] != te_ref[jnp.maximum(i - 1, 0)]))
    def _():
        slot = sl_ref[i]
        for cp in weight_copies(te_ref[i], slot):
            cp.wait()
        wg_b[...] = wg_f[slot].astype(bf16)
        wu_b[...] = wu_f[slot].astype(bf16)
        wd_b[...] = wd_f[slot].astype(bf16)

        @pl.when(nx_ref[i] >= 0)
        def _():
            for cp in weight_copies(nx_ref[i], 1 - slot):
                cp.start()

    @pl.when(tv_ref[i] != 0)
    def _():
        lo, hi = _unpack_bf16_pairs(x_ref[...])
        half = lo.shape[1]
        a = (jnp.dot(lo, wg_b[:half], preferred_element_type=f32)
             + jnp.dot(hi, wg_b[half:], preferred_element_type=f32))
        b = (jnp.dot(lo, wu_b[:half], preferred_element_type=f32)
             + jnp.dot(hi, wu_b[half:], preferred_element_type=f32))
        hid = (_silu(a) * b).astype(bf16)
        o_ref[...] = _pack_bf16_pairs(jnp.dot(hid, wd_b[...], preferred_element_type=f32))

    @pl.when(tv_ref[i] == 0)
    def _():
        o_ref[...] = jnp.zeros_like(o_ref)


def grouped_experts(xs, tile_exp, tile_valid, wg, wu, wd, layer):
    p, half = xs.shape
    d = 2 * half
    n_tiles = p // EXPERT_TILE
    first = jnp.concatenate([jnp.ones((1,), bool), tile_exp[1:] != tile_exp[:-1]])
    slot = (jnp.cumsum(first.astype(i32)) - 1) % 2
    nxt_at = jnp.sum((tile_exp[None, :] <= tile_exp[:, None]).astype(i32), axis=1)
    nxt = jnp.where(nxt_at < n_tiles, tile_exp[jnp.minimum(nxt_at, n_tiles - 1)], -1)
    tile = pl.BlockSpec((EXPERT_TILE, half), lambda i, *_: (i, 0))
    hbm = pl.BlockSpec(memory_space=pl.ANY)
    grid_spec = pltpu.PrefetchScalarGridSpec(
        num_scalar_prefetch=4,
        grid=(n_tiles,),
        in_specs=[tile, hbm, hbm, hbm],
        out_specs=tile,
        scratch_shapes=[pltpu.VMEM((2, d, D_EXPERT), f32), pltpu.VMEM((2, d, D_EXPERT), f32),
                        pltpu.VMEM((2, D_EXPERT, d), f32),
                        pltpu.VMEM((d, D_EXPERT), bf16), pltpu.VMEM((d, D_EXPERT), bf16),
                        pltpu.VMEM((D_EXPERT, d), bf16), pltpu.SemaphoreType.DMA((2, 3))],
    )
    return pl.pallas_call(
        functools.partial(_expert_kernel, layer=layer),
        out_shape=jax.ShapeDtypeStruct((p, half), i32),
        grid_spec=grid_spec,
        compiler_params=_params(("arbitrary",)),
        name="grouped_experts",
    )(tile_exp, tile_valid, nxt.astype(i32), slot.astype(i32), xs, wg, wu, wd)


def _combine_kernel(x_ref, yg_ref, wt_ref, g2_ref, sh2_ref, sc2_ref, n2_ref, wsg_ref, wsu_ref, wsd_ref, fg_ref, o_ref,
                    *, final):
    x = x_ref[...]
    hb = _rms_mod(x, n2_ref[...], sc2_ref[0], sh2_ref[0]).astype(bf16)
    a = jnp.dot(hb, wsg_ref[...], preferred_element_type=f32)
    b = jnp.dot(hb, wsu_ref[...], preferred_element_type=f32)
    f = jnp.dot((_silu(a) * b).astype(bf16), wsd_ref[...], preferred_element_type=f32)
    wt = wt_ref[...]
    half = x.shape[1] // 2
    f_lo, f_hi = f[:, :half], f[:, half:]
    for k in range(TOP_K):
        lo, hi = _unpack_bf16_pairs(yg_ref[k])
        w = wt[:, k:k + 1]
        f_lo = f_lo + w * lo.astype(f32)
        f_hi = f_hi + w * hi.astype(f32)
    xo = x + g2_ref[0] * jnp.concatenate([f_lo, f_hi], axis=1)
    if final:
        ms = jnp.mean(xo * xo, axis=-1, keepdims=True)
        xo = xo * lax.rsqrt(ms + RMS_EPS) * fg_ref[...]
    o_ref[...] = xo


def combine(x2, yg, wt_rows, row_offset, mods, rows_per_mod, tile, lw, final_g, final):
    m, d = x2.shape
    off = row_offset // tile
    row = lambda w: pl.BlockSpec((tile, w), lambda i: (i, 0))
    modspec = pl.BlockSpec((1, 1, d), lambda i: ((i * tile) // rows_per_mod, 0, 0))
    return pl.pallas_call(
        functools.partial(_combine_kernel, final=final),
        out_shape=jax.ShapeDtypeStruct((m, d), f32),
        grid=(m // tile,),
        in_specs=[row(d), pl.BlockSpec((TOP_K, tile, d // 2), lambda i: (0, i + off, 0)),
                  pl.BlockSpec((tile, 8), lambda i: (i + off, 0)), modspec, modspec, modspec,
                  _const_spec((1, d)), _const_spec((d, D_EXPERT)), _const_spec((d, D_EXPERT)),
                  _const_spec((D_EXPERT, d)), _const_spec((1, d))],
        out_specs=row(d),
        compiler_params=_params(("parallel",)),
        name="combine",
    )(x2, yg, wt_rows, mods["g2"], mods["sh2"], mods["sc2"], lw["norm2_g"], lw["w_sh_gate"], lw["w_sh_up"],
      lw["w_sh_down"], final_g)


def routed_experts(h2p, eid, rank, counts, lw, layer):
    m = h2p.shape[0]
    p_max = m * TOP_K + N_EXPERTS * EXPERT_TILE
    pos, tile_exp, tile_valid = routing_plan(eid, rank, counts, p_max)
    xs = sc_scatter_rows(h2p, pos, p_max)
    ys = grouped_experts(xs, tile_exp, tile_valid, lw["w_exp_gate"], lw["w_exp_up"], lw["w_exp_down"], layer)
    return sc_gather_rows(ys, pos[:TOP_K].reshape(TOP_K * m)).reshape(TOP_K, m, D_MODEL // 2)


def rope_tables(seq):
    rows = seq // GRID_W
    row = jnp.repeat(jnp.arange(rows), GRID_W).astype(f32)
    col = jnp.tile(jnp.arange(GRID_W), rows).astype(f32)
    axis_dim = HEAD_DIM // 2
    inv_freq = 1.0 / (ROPE_THETA ** (jnp.arange(0, axis_dim, 2, dtype=f32) / axis_dim))
    ang_r = row[:, None] * inv_freq
    ang_c = col[:, None] * inv_freq
    ang = jnp.concatenate([ang_r, ang_r, ang_c, ang_c], axis=-1)
    cos, sin = jnp.cos(ang), jnp.sin(ang)
    seg = (jnp.arange(HEAD_DIM) // 16) % 2
    sa = jnp.where(seg == 0, -sin, 0.0)
    sb = jnp.where(seg == 1, sin, 0.0)
    rep = lambda t: jnp.tile(t, (1, 4))
    return rep(cos), rep(sa), rep(sb)


def identity_rope_tables(rows):
    return jnp.ones((rows, 256), f32), jnp.zeros((rows, 256), f32), jnp.zeros((rows, 256), f32)


def kernel(x, c, ctx, c_ctx, w_mod, b_mod, norm1_g, norm2_g, w_in, q_norm_g, k_norm_g, sink, gm_norm_g, gm_ws, gm_b, w_br, w_o, w_router, router_bias, w_exp_gate, w_exp_up, w_exp_down, w_sh_gate, w_sh_up, w_sh_down, final_norm_g):
    bsz_all, seq, d = x.shape
    n_ctx = ctx.shape[1]
    depth = w_mod.shape[0]

    cc = jnp.concatenate([c, c_ctx[None, :], jnp.zeros((MOD_ROWS - bsz_all - 1, d), f32)], axis=0)
    mod_all = compute_mod(cc, w_mod, b_mod)

    lat_tables = rope_tables(seq)
    ctx_tables = identity_rope_tables(n_ctx)
    wc_lat, ws_lat = dft_tables(seq)
    wc_ctx, ws_ctx = dft_tables(n_ctx)
    cs64 = channel_dft_table()
    bd = jnp.asarray(np.kron(np.eye(4), np.full((HEAD_DIM, HEAD_DIM), 1.0 / HEAD_DIM)), dtype=bf16)
    final_g = final_norm_g.reshape(1, d)
    tri = jnp.asarray(np.triu(np.ones((ROW_TILE, ROW_TILE)), 1), dtype=bf16)

    lws = []
    for l in range(depth):
        lws.append({
            "norm1_g": norm1_g[l].reshape(1, d),
            "norm2_g": norm2_g[l].reshape(1, d),
            "w_z": w_in[l, :, :OFF_GATE].astype(bf16),
            "w_gate": w_in[l, :, OFF_GATE:].astype(bf16),
            "qn": jnp.tile(q_norm_g[l], 4).reshape(1, 256),
            "kn": jnp.tile(k_norm_g[l], 2).reshape(1, 128),
            "bd": bd,
            "gm_norm_g": gm_norm_g[l].reshape(1, 256),
            "gm_ws": gm_ws[l].astype(bf16),
            "gm_bias": jnp.repeat(gm_b[l].T, 64, axis=1),
            "cs64": cs64,
            "w_br": w_br[l].astype(bf16),
            "w_o": w_o[l].astype(bf16),
            "w_router_t": w_router[l].T,
            "router_bias": router_bias[l].reshape(N_EXPERTS, 1),
            "tri": tri,
            "w_exp_gate": w_exp_gate,
            "w_exp_up": w_exp_up,
            "w_exp_down": w_exp_down,
            "w_sh_gate": w_sh_gate[l].astype(bf16),
            "w_sh_up": w_sh_up[l].astype(bf16),
            "w_sh_down": w_sh_down[l].astype(bf16),
        })

    bsz = bsz_all // BATCH_CHUNKS
    outs = []
    for ch in range(BATCH_CHUNKS):
        b0 = ch * bsz
        outs.append(_forward_chunk(x[b0:b0 + bsz], ctx[b0:b0 + bsz], mod_all[:, b0:b0 + bsz], mod_all[:, bsz_all],
                                   lws, sink, lat_tables, ctx_tables, (wc_lat, ws_lat), (wc_ctx, ws_ctx), final_g))
    return jnp.concatenate(outs, axis=0)


def _forward_chunk(x, ctx, mod_lat, mod_ctx, lws, sink, lat_tables, ctx_tables, dft_lat, dft_ctx, final_g):
    bsz, seq, d = x.shape
    n_ctx = ctx.shape[1]
    depth = len(lws)
    n_lat = bsz * seq
    n_cx = bsz * n_ctx
    wc_lat, ws_lat = dft_lat
    wc_ctx, ws_ctx = dft_ctx
    xl = x.reshape(n_lat, d)
    xc = ctx.reshape(n_cx, d)
    for l in range(depth):
        ctx_out = l < depth - 1
        lw = lws[l]
        names = ("sh1", "sc1", "g1", "sh2", "sc2", "g2")
        mods_lat = {n: mod_lat[l, :, i * d:(i + 1) * d].reshape(bsz, 1, d) for i, n in enumerate(names)}
        mods_ctx = {n: mod_ctx[l, i * d:(i + 1) * d].reshape(1, 1, d) for i, n in enumerate(names)}
        sink_l = sink[l]

        qg, kg, vg, qw, kw, vw, fxc, fxs, o_d = in_projection(
            xl, mods_lat["sh1"], mods_lat["sc1"], seq, lat_tables, seq, ROW_TILE, lw)
        cqg, ckg, cvg, cqw, ckw, cvw, cfxc, cfxs, co_d = in_projection(
            xc, mods_ctx["sh1"], mods_ctx["sc1"], n_cx, ctx_tables, n_ctx, n_ctx, lw)
        b3 = lambda t, rows: t.reshape(bsz, rows, t.shape[-1])
        ckg3, cvg3, ckw3, cvw3 = b3(ckg, n_ctx), b3(cvg, n_ctx), b3(ckw, n_ctx), b3(cvw, n_ctx)

        o_a = dft_mix(wc_lat, ws_lat, b3(fxc, seq), b3(fxs, seq), ROW_TILE).reshape(n_lat, BRANCH_W)
        o_b = full_attention(b3(qg, seq), [(ckg3, cvg3), (b3(kg, seq), b3(vg, seq))], None, 256).reshape(n_lat, BRANCH_W)
        o_c = window_attention(b3(qw, seq), b3(kw, seq), b3(vw, seq), ckw3, cvw3, sink_l).reshape(n_lat, BRANCH_W)
        m_total = n_lat + (n_cx if ctx_out else 0)
        route = None
        if ctx_out:
            route = [jnp.zeros((m_total, d // 2), i32), jnp.zeros((8, m_total), i32), jnp.zeros((8, m_total), f32),
                     jnp.zeros((8, m_total), i32), jnp.zeros((N_EXPERTS, 1), i32)]
        xl, *route = merge_and_route(xl, mods_lat, seq, (o_a, o_b, o_c, o_d), ROW_TILE, lw, m_total, 0, route)
        if ctx_out:
            co_a = dft_mix(wc_ctx, ws_ctx, b3(cfxc, n_ctx), b3(cfxs, n_ctx), n_ctx).reshape(n_cx, BRANCH_W)
            co_b = full_attention(b3(cqg, n_ctx), [(ckg3, cvg3)], None, n_ctx).reshape(n_cx, BRANCH_W)
            co_c = full_attention(b3(cqw, n_ctx), [(ckw3, cvw3)], sink_l, n_ctx).reshape(n_cx, BRANCH_W)
            xc, *route = merge_and_route(xc, mods_ctx, n_cx, (co_a, co_b, co_c, co_d), ROW_TILE, lw, m_total, n_lat,
                                         route)
        h2p, eid, wt, rank, counts = route
        yg = routed_experts(h2p, eid, rank, counts, lw, l)
        wt_rows = wt.T
        xl = combine(xl, yg, wt_rows, 0, mods_lat, seq, ROW_TILE, lw, final_g, not ctx_out)
        if ctx_out:
            xc = combine(xc, yg, wt_rows, n_lat, mods_ctx, n_cx, ROW_TILE, lw, final_g, False)
    return xl.reshape(bsz, seq, d)
```

```python
import functools
import math

import jax
import jax.numpy as jnp
import numpy as np
from jax import lax
from jax.experimental import pallas as pl
from jax.experimental.pallas import tpu as pltpu
from jax.experimental.pallas import tpu_sc as plsc

f32 = jnp.float32
bf16 = jnp.bfloat16
i32 = jnp.int32

D_MODEL = 1024
HEAD_DIM = 64
GRID_W = 64
ROPE_THETA = 10000.0
ATTN_SCALE = HEAD_DIM ** -0.5
RMS_EPS = 1e-6
NEG_INF = -1e30
Q_BLOCK = 128
WINDOW = 128
GM_CHUNK = 128
N_BRANCHES = 4
BRANCH_W = 256
KV_W = 128
OFF_KV = 0
OFF_Q = 512
OFF_FN = 1024
OFF_GM = 1280
OFF_GATE = 1792
N_EXPERTS = 64
TOP_K = 6
N_GROUPS = 8
GROUP_SIZE = N_EXPERTS // N_GROUPS
TOPK_GROUPS = 4
D_EXPERT = 256
ROUTED_SCALE = 2.5

SC_CORES = 2
SC_SUBCORES = 16
SC_WORKERS = SC_CORES * SC_SUBCORES
SC_IDX_CHUNK = 128

ROW_TILE = 512
EXPERT_TILE = 512
BATCH_CHUNKS = 1
FULL_ATTN_Q_ROWS = 512
FULL_ATTN_ROW_BLOCKS = 2
WINDOW_BLOCKS_PER_STEP = 4
MOD_ROWS = 24
VMEM_LIMIT = 56 * 1024 * 1024


def _params(sem, vmem=VMEM_LIMIT):
    return pltpu.CompilerParams(dimension_semantics=sem, vmem_limit_bytes=vmem)


def _const_spec(shape):
    nd = len(shape)
    return pl.BlockSpec(shape, lambda *_: (0,) * nd, pipeline_mode=pl.Buffered(1))


def _rms_mod(x, g, sc, sh):
    ms = jnp.mean(x * x, axis=-1, keepdims=True)
    return (x * lax.rsqrt(ms + RMS_EPS) * g) * (1.0 + sc) + sh


def _gelu(x):
    return 0.5 * x * (1.0 + jnp.tanh(math.sqrt(2.0 / math.pi) * (x + 0.044715 * (x * x * x))))


def _silu(x):
    return x * jax.nn.sigmoid(x)


def _mod_kernel(a_ref, w_ref, b_ref, o_ref):
    a = _silu(a_ref[...]).astype(bf16)
    o_ref[0] = jnp.dot(a, w_ref[0].astype(bf16), preferred_element_type=f32) + b_ref[0]


def compute_mod(cc, w_mod, b_mod):
    depth, d, n = w_mod.shape
    tn = 1536
    return pl.pallas_call(
        _mod_kernel,
        out_shape=jax.ShapeDtypeStruct((depth, MOD_ROWS, n), f32),
        grid=(depth, n // tn),
        in_specs=[
            pl.BlockSpec((MOD_ROWS, d), lambda l, j: (0, 0)),
            pl.BlockSpec((1, d, tn), lambda l, j: (l, 0, j)),
            pl.BlockSpec((1, 1, tn), lambda l, j: (l, 0, j)),
        ],
        out_specs=pl.BlockSpec((1, MOD_ROWS, tn), lambda l, j: (l, 0, j)),
        compiler_params=_params(("parallel", "parallel")),
        name="mod_proj",
    )(cc, w_mod, b_mod.reshape(depth, 1, n))


def _inproj_kernel(x_ref, sh_ref, sc_ref, g_ref, w_ref, qn_ref, kn_ref, bd_ref, gmg_ref, ws_ref, gb_ref, cs_ref,
                   cos_ref, sa_ref, sb_ref,
                   qg_ref, kg_ref, vg_ref, qw_ref, kw_ref, vw_ref, xc_ref, xs_ref, od_ref):
    tile = x_ref.shape[0]
    hb = _rms_mod(x_ref[...], g_ref[...], sc_ref[0], sh_ref[0]).astype(bf16)

    def proj(a, b):
        return jnp.dot(hb, w_ref[:, a:b], preferred_element_type=f32)

    def headnorm(t, gain):
        w = t.shape[1]
        sq = t * t
        hi = sq.astype(bf16)
        lo = (sq - hi.astype(f32)).astype(bf16)
        b = bd_ref[:w, :w]
        ms = jnp.dot(hi, b, preferred_element_type=f32) + jnp.dot(lo, b, preferred_element_type=f32)
        return t * lax.rsqrt(ms + RMS_EPS) * gain

    def rope(t):
        w = t.shape[1]
        return (t * cos_ref[:, :w] + pltpu.roll(t, w - 16, 1) * sa_ref[:, :w]
                + pltpu.roll(t, 16, 1) * sb_ref[:, :w])

    def expand_heads(q):
        lane = lax.broadcasted_iota(i32, (1, KV_W), 1)
        low = lane < HEAD_DIM
        blocks = []
        for kv in range(2):
            pair = q[:, KV_W * kv:KV_W * (kv + 1)]
            swapped = pltpu.roll(pair, HEAD_DIM, 1)
            keep = low if kv == 0 else jnp.logical_not(low)
            g0, g1 = (pair, swapped) if kv == 0 else (swapped, pair)
            blocks.append(jnp.where(keep, g0, 0.0))
            blocks.append(jnp.where(keep, g1, 0.0))
        return jnp.concatenate(blocks, axis=1)

    kv = proj(OFF_KV, OFF_Q)
    kg_ref[...] = rope(headnorm(kv[:, 0:128], kn_ref[...])).astype(bf16)
    vg_ref[...] = kv[:, 128:256].astype(bf16)
    kw_ref[...] = rope(kv[:, 256:384]).astype(bf16)
    vw_ref[...] = kv[:, 384:512].astype(bf16)

    qq = proj(OFF_Q, OFF_FN)
    qg = rope(headnorm(qq[:, :256], qn_ref[...])) * ATTN_SCALE
    qg_ref[...] = expand_heads(qg).astype(bf16)
    qw = rope(qq[:, 256:]) * ATTN_SCALE
    qw_ref[...] = expand_heads(qw).astype(bf16)

    fn = proj(OFF_FN, OFF_GM).astype(bf16)
    xcs = jnp.dot(fn, cs_ref[...], preferred_element_type=f32)
    xc_ref[...] = xcs[:, :256].astype(bf16)
    xs_ref[...] = xcs[:, 256:].astype(bf16)

    uv = proj(OFF_GM, OFF_GATE)
    u = _gelu(uv[:, :256])
    v = _gelu(uv[:, 256:])
    vms = jnp.mean(v * v, axis=-1, keepdims=True)
    vn = (v * lax.rsqrt(vms + RMS_EPS) * gmg_ref[...]).astype(bf16)
    lane_grp = lax.broadcasted_iota(i32, (1, 256), 1) // 64
    for c in range(tile // GM_CHUNK):
        rows = slice(c * GM_CHUNK, (c + 1) * GM_CHUNK)
        vch = vn[rows]
        sv = gb_ref[...]
        for g in range(4):
            r = jnp.dot(ws_ref[g], vch, preferred_element_type=f32)
            sv = sv + jnp.where(lane_grp == g, r, 0.0)
        od_ref[rows, :] = (u[rows] * sv).astype(bf16)


def in_projection(x2, sh, sc, rows_per_mod, tables, rows_per_seq, tile, lw):
    m, d = x2.shape
    cos_t, sa_t, sb_t = tables
    seq_blocks = rows_per_seq // tile
    row = lambda w: pl.BlockSpec((tile, w), lambda i: (i, 0))
    modspec = pl.BlockSpec((1, 1, d), lambda i: ((i * tile) // rows_per_mod, 0, 0))
    tabspec = pl.BlockSpec((tile, 256), lambda i: (i % seq_blocks, 0))
    out_w = [512, 128, 128, 512, 128, 128, 256, 256, 256]
    return pl.pallas_call(
        _inproj_kernel,
        out_shape=[jax.ShapeDtypeStruct((m, w), bf16) for w in out_w],
        grid=(m // tile,),
        in_specs=[
            row(d), modspec, modspec, _const_spec((1, d)), _const_spec((d, OFF_GATE)),
            _const_spec((1, 256)), _const_spec((1, 128)), _const_spec((256, 256)), _const_spec((1, 256)),
            _const_spec((4, GM_CHUNK, GM_CHUNK)), _const_spec((GM_CHUNK, 256)), _const_spec((256, 512)),
            tabspec, tabspec, tabspec,
        ],
        out_specs=[row(w) for w in out_w],
        compiler_params=_params(("parallel",)),
        name="in_projection",
    )(x2, sh, sc, lw["norm1_g"], lw["w_z"], lw["qn"], lw["kn"], lw["bd"], lw["gm_norm_g"], lw["gm_ws"], lw["gm_bias"],
      lw["cs64"], cos_t, sa_t, sb_t)


def _dft_kernel(wc_ref, ws_ref, xc_ref, xs_ref, o_ref):
    acc = jnp.dot(wc_ref[...], xc_ref[0], preferred_element_type=f32)
    acc = acc + jnp.dot(ws_ref[...], xs_ref[0], preferred_element_type=f32)
    o_ref[0] = acc.astype(bf16)


def dft_mix(wc, ws, xc, xs, tile):
    nb, length, w = xc.shape
    return pl.pallas_call(
        _dft_kernel,
        out_shape=jax.ShapeDtypeStruct((nb, length, w), bf16),
        grid=(length // tile, nb),
        in_specs=[
            pl.BlockSpec((tile, length), lambda i, b: (i, 0)),
            pl.BlockSpec((tile, length), lambda i, b: (i, 0)),
            pl.BlockSpec((1, length, w), lambda i, b: (b, 0, 0)),
            pl.BlockSpec((1, length, w), lambda i, b: (b, 0, 0)),
        ],
        out_specs=pl.BlockSpec((1, tile, w), lambda i, b: (b, i, 0)),
        compiler_params=_params(("parallel", "parallel")),
        name="dft_mix",
    )(wc, ws, xc, xs)


def dft_tables(length):
    jk = (np.arange(length)[:, None] * np.arange(length)[None, :]) % length
    ang = 2.0 * np.pi * jk / length
    s = 1.0 / math.sqrt(length)
    return jnp.asarray(np.cos(ang) * s, dtype=bf16), jnp.asarray(-np.sin(ang) * s, dtype=bf16)


def channel_dft_table():
    jk = (np.arange(64)[:, None] * np.arange(64)[None, :]) % 64
    ang = 2.0 * np.pi * jk / 64
    eye = np.eye(4)
    c = np.kron(eye, np.cos(ang) / 8.0)
    s = np.kron(eye, np.sin(ang) / 8.0)
    return jnp.asarray(np.concatenate([c, s], axis=1), dtype=bf16)


def _attend_blocks(blocks, sink_ref):
    lane = lax.broadcasted_iota(i32, (1, KV_W), 1)
    low = lane < HEAD_DIM
    units = []
    for q_all, pieces in blocks:
        qb = q_all.shape[0]
        for kv in range(2):
            q = jnp.concatenate([q_all[:, KV_W * (2 * kv):KV_W * (2 * kv + 1)],
                                 q_all[:, KV_W * (2 * kv + 1):KV_W * (2 * kv + 2)]], axis=0)
            sink_col = None
            if sink_ref is not None:
                sink_col = jnp.concatenate([jnp.full((qb, 1), sink_ref[2 * kv], f32),
                                            jnp.full((qb, 1), sink_ref[2 * kv + 1], f32)], axis=0)
            units.append((q, pieces, sink_col, low if kv == 0 else jnp.logical_not(low)))

    scores = []
    for q, pieces, _, _ in units:
        unit_scores = []
        for k, _, mask in pieces:
            s = lax.dot_general(q, k, (((1,), (1,)), ((), ())), preferred_element_type=f32)
            unit_scores.append(s if mask is None else jnp.where(mask, s, NEG_INF))
        scores.append(unit_scores)

    maxes = []
    for (_, _, sink_col, _), unit_scores in zip(units, scores):
        m = unit_scores[0].max(axis=-1, keepdims=True)
        for s in unit_scores[1:]:
            m = jnp.maximum(m, s.max(axis=-1, keepdims=True))
        maxes.append(m if sink_col is None else jnp.maximum(m, sink_col))

    probs = [[jnp.exp((s - m).astype(bf16)) for s in unit_scores] for unit_scores, m in zip(scores, maxes)]

    results = []
    for (_, pieces, sink_col, own), unit_probs, m in zip(units, probs, maxes):
        acc = None
        for p, (_, v, _) in zip(unit_probs, pieces):
            pv = jnp.dot(p, jnp.where(own, v, jnp.ones_like(v)), preferred_element_type=f32)
            acc = pv if acc is None else acc + pv
        denom = pltpu.roll(acc, HEAD_DIM, 1)
        if sink_col is not None:
            denom = denom + jnp.exp(sink_col - m)
        results.append(acc * (1.0 / denom))

    outs = []
    for i, (q_all, _) in enumerate(blocks):
        qb = q_all.shape[0]
        r_kv0, r_kv1 = results[2 * i], results[2 * i + 1]
        lo = jnp.where(low, r_kv0[:qb], pltpu.roll(r_kv0[qb:], HEAD_DIM, 1))
        hi = jnp.where(low, pltpu.roll(r_kv1[:qb], HEAD_DIM, 1), r_kv1[qb:])
        outs.append(jnp.concatenate([lo, hi], axis=1))
    return outs


def _full_attn_kernel(*refs, n_pieces, has_sink, row_blocks):
    pos = 0
    sink_ref = None
    if has_sink:
        sink_ref = refs[0]
        pos = 1
    q_ref = refs[pos]
    kv_refs = refs[pos + 1:pos + 1 + 2 * n_pieces]
    o_ref = refs[pos + 1 + 2 * n_pieces]
    pieces = [(kv_refs[2 * i][0], kv_refs[2 * i + 1][0], None) for i in range(n_pieces)]
    rows = q_ref.shape[1] // row_blocks
    blocks = [(q_ref[0, j * rows:(j + 1) * rows, :], pieces) for j in range(row_blocks)]
    for j, out in enumerate(_attend_blocks(blocks, sink_ref)):
        o_ref[0, j * rows:(j + 1) * rows, :] = out.astype(bf16)


def full_attention(q, pieces, sink, qb):
    nb, lq, _ = q.shape
    in_specs = []
    args = []
    if sink is not None:
        in_specs.append(pl.BlockSpec(memory_space=pltpu.SMEM))
        args.append(sink)
    in_specs.append(pl.BlockSpec((1, qb, 512), lambda b, i: (b, i, 0)))
    args.append(q)
    for k, v in pieces:
        spec = pl.BlockSpec((1, k.shape[1], KV_W), lambda b, i: (b, 0, 0))
        in_specs += [spec, spec]
        args += [k, v]
    return pl.pallas_call(
        functools.partial(_full_attn_kernel, n_pieces=len(pieces), has_sink=sink is not None,
                          row_blocks=FULL_ATTN_ROW_BLOCKS),
        out_shape=jax.ShapeDtypeStruct((nb, lq, 256), bf16),
        grid=(nb, lq // qb),
        in_specs=in_specs,
        out_specs=pl.BlockSpec((1, qb, 256), lambda b, i: (b, i, 0)),
        compiler_params=_params(("parallel", "parallel")),
        name="full_attention",
    )(*args)


def _window_attn_kernel(sink_ref, q_ref, k_ref, v_ref, kc_ref, vc_ref, o_ref, *, seq, blocks):
    span = 3 * Q_BLOCK
    ctx_piece = (kc_ref[0], vc_ref[0], None)
    work = []
    for j in range(blocks):
        n = pl.program_id(1) * blocks + j
        start = pl.multiple_of(jnp.clip((n - 1) * Q_BLOCK, 0, seq - span), Q_BLOCK)
        kwin = k_ref[0, pl.ds(start, span), :]
        vwin = v_ref[0, pl.ds(start, span), :]
        row = lax.broadcasted_iota(i32, (2 * Q_BLOCK, span), 0) % Q_BLOCK + n * Q_BLOCK
        col = lax.broadcasted_iota(i32, (2 * Q_BLOCK, span), 1) + start
        mask = jnp.abs(row - col) <= WINDOW
        work.append((q_ref[0, j * Q_BLOCK:(j + 1) * Q_BLOCK, :], [ctx_piece, (kwin, vwin, mask)]))
    for j, out in enumerate(_attend_blocks(work, sink_ref)):
        o_ref[0, j * Q_BLOCK:(j + 1) * Q_BLOCK, :] = out.astype(bf16)


def window_attention(q, k, v, kc, vc, sink):
    nb, seq, _ = q.shape
    n_ctx = kc.shape[1]
    blocks = WINDOW_BLOCKS_PER_STEP
    full = lambda l: pl.BlockSpec((1, l, KV_W), lambda b, i: (b, 0, 0))
    return pl.pallas_call(
        functools.partial(_window_attn_kernel, seq=seq, blocks=blocks),
        out_shape=jax.ShapeDtypeStruct((nb, seq, 256), bf16),
        grid=(nb, seq // (Q_BLOCK * blocks)),
        in_specs=[pl.BlockSpec(memory_space=pltpu.SMEM),
                  pl.BlockSpec((1, Q_BLOCK * blocks, 512), lambda b, i: (b, i, 0)),
                  full(seq), full(seq), full(n_ctx), full(n_ctx)],
        out_specs=pl.BlockSpec((1, Q_BLOCK * blocks, 256), lambda b, i: (b, i, 0)),
        compiler_params=_params(("parallel", "parallel")),
        name="window_attention",
    )(sink, q, k, v, kc, vc)


def _route(logits_t, bias_col):
    t = logits_t.shape[1]
    scores = jax.nn.sigmoid(logits_t)
    choice = scores + bias_col
    sub = lax.broadcasted_iota(i32, (GROUP_SIZE, t), 0)
    grp_score = []
    for g in range(N_GROUPS):
        cg = choice[g * GROUP_SIZE:(g + 1) * GROUP_SIZE]
        m1 = cg.max(axis=0, keepdims=True)
        first = jnp.min(jnp.where(cg == m1, sub, GROUP_SIZE), axis=0, keepdims=True)
        m2 = jnp.where(sub == first, -jnp.inf, cg).max(axis=0, keepdims=True)
        grp_score.append(m1 + m2)
    keep = []
    for g in range(N_GROUPS):
        beaten = jnp.zeros((1, t), i32)
        for o in range(N_GROUPS):
            if o == g:
                continue
            wins = (grp_score[o] > grp_score[g]) | ((grp_score[o] == grp_score[g]) & (o < g))
            beaten = beaten + wins.astype(i32)
        keep.append(jnp.broadcast_to(beaten < TOPK_GROUPS, (GROUP_SIZE, t)))
    masked = jnp.where(jnp.concatenate(keep, axis=0), choice, NEG_INF)
    eid = lax.broadcasted_iota(i32, (N_EXPERTS, t), 0)
    ids, wts = [], []
    for _ in range(TOP_K):
        m = masked.max(axis=0, keepdims=True)
        pick = jnp.min(jnp.where(masked == m, eid, N_EXPERTS), axis=0, keepdims=True)
        sel = eid == pick
        ids.append(pick)
        wts.append(jnp.sum(jnp.where(sel, scores, 0.0), axis=0, keepdims=True))
        masked = jnp.where(sel, -jnp.inf, masked)
    total = wts[0]
    for w in wts[1:]:
        total = total + w
    norm = ROUTED_SCALE / total
    return ids, [w * norm for w in wts]


def _pack_bf16_pairs(x):
    w = x.shape[1] // 2
    lo = lax.bitcast_convert_type(x[:, :w].astype(bf16).astype(f32), i32)
    hi = lax.bitcast_convert_type(x[:, w:].astype(bf16).astype(f32), i32)
    return lax.shift_right_logical(lo, 16) | (hi & jnp.int32(-65536))


def _unpack_bf16_pairs(p):
    lo = lax.bitcast_convert_type(lax.shift_left(p, 16), f32)
    hi = lax.bitcast_convert_type(p & jnp.int32(-65536), f32)
    return lo.astype(bf16), hi.astype(bf16)


def _merge_kernel(x_ref, sh_ref, sc_ref, g1_ref, sh2_ref, sc2_ref, n1_ref, n2_ref, oa_ref, ob_ref, oc_ref, od_ref,
                  wg_ref, wbr_ref, wo_ref, wr_ref, rb_ref, tri_ref, *rest, extends):
    if extends:
        cnt_in_ref = rest[0]
        rest = rest[5:]
    xo_ref, h2_ref, eid_ref, wt_ref, rank_ref, cnt_ref, run_ref = rest

    @pl.when(pl.program_id(0) == 0)
    def _():
        run_ref[...] = cnt_in_ref[...].astype(f32) if extends else jnp.zeros_like(run_ref)

    x = x_ref[...]
    hb = _rms_mod(x, n1_ref[...], sc_ref[0], sh_ref[0]).astype(bf16)
    y = None
    for i, o_ref in enumerate((oa_ref, ob_ref, oc_ref, od_ref)):
        logit = jnp.dot(hb, wg_ref[:, i * D_MODEL:(i + 1) * D_MODEL], preferred_element_type=f32)
        proj = jnp.dot(o_ref[...], wbr_ref[i], preferred_element_type=f32)
        term = jax.nn.sigmoid(logit.astype(bf16)) * proj.astype(bf16)
        y = term if y is None else y + term
    xn = x + g1_ref[0] * jnp.dot(y, wo_ref[...], preferred_element_type=f32)
    xo_ref[...] = xn
    h2 = _rms_mod(xn, n2_ref[...], sc2_ref[0], sh2_ref[0])
    h2_ref[...] = _pack_bf16_pairs(h2)
    logits_t = lax.dot_general(wr_ref[...], h2, (((1,), (1,)), ((), ())), preferred_element_type=f32,
                               precision=lax.Precision.HIGHEST)
    ids, wts = _route(logits_t, rb_ref[...])

    t = x.shape[0]
    eid = lax.broadcasted_iota(i32, (N_EXPERTS, t), 0)
    hits = [eid == pick for pick in ids]
    chosen = hits[0]
    for h in hits[1:]:
        chosen = chosen | h
    chosen = jnp.where(chosen, 1.0, 0.0)
    prefix = jnp.dot(chosen.astype(bf16), tri_ref[...], preferred_element_type=f32)
    offset = run_ref[...] + prefix
    ranks = [jnp.sum(jnp.where(h, offset, 0.0), axis=0, keepdims=True).astype(i32) for h in hits]
    run_ref[...] += jnp.sum(chosen, axis=1, keepdims=True)
    cnt_ref[...] = run_ref[...].astype(i32)

    pad_i = [jnp.zeros((1, t), i32)] * (8 - TOP_K)
    eid_ref[...] = jnp.concatenate(ids + pad_i, axis=0)
    rank_ref[...] = jnp.concatenate(ranks + pad_i, axis=0)
    wt_ref[...] = jnp.concatenate(wts + [jnp.zeros((1, t), f32)] * (8 - TOP_K), axis=0)


def merge_and_route(x2, mods, rows_per_mod, branches, tile, lw, m_total, row_offset, prior):
    m, d = x2.shape
    off = row_offset // tile
    row = lambda w: pl.BlockSpec((tile, w), lambda i: (i, 0))
    row_off = lambda w: pl.BlockSpec((tile, w), lambda i: (i + off, 0))
    modspec = pl.BlockSpec((1, 1, d), lambda i: ((i * tile) // rows_per_mod, 0, 0))
    col = pl.BlockSpec((8, tile), lambda i: (0, i + off))
    in_specs = [row(d)] + [modspec] * 5 + [_const_spec((1, d)), _const_spec((1, d))] + [row(BRANCH_W)] * 4 + [
        _const_spec((d, N_BRANCHES * d)), _const_spec((N_BRANCHES, BRANCH_W, d)), _const_spec((d, d)),
        _const_spec((N_EXPERTS, d)), _const_spec((N_EXPERTS, 1)), _const_spec((tile, tile))]
    args = [x2, mods["sh1"], mods["sc1"], mods["g1"], mods["sh2"], mods["sc2"], lw["norm1_g"], lw["norm2_g"],
            *branches, lw["w_gate"], lw["w_br"], lw["w_o"], lw["w_router_t"], lw["router_bias"], lw["tri"]]
    aliases = {}
    if prior is not None:
        h2p, eid, wt, rank, counts = prior
        n_in = len(args)
        in_specs += [_const_spec((N_EXPERTS, 1))] + [pl.BlockSpec(memory_space=pl.ANY)] * 4
        args += [counts, h2p, eid, wt, rank]
        aliases = {n_in + 1 + j: 1 + j for j in range(4)}
    return pl.pallas_call(
        functools.partial(_merge_kernel, extends=prior is not None),
        out_shape=[jax.ShapeDtypeStruct((m, d), f32), jax.ShapeDtypeStruct((m_total, d // 2), i32),
                   jax.ShapeDtypeStruct((8, m_total), i32), jax.ShapeDtypeStruct((8, m_total), f32),
                   jax.ShapeDtypeStruct((8, m_total), i32), jax.ShapeDtypeStruct((N_EXPERTS, 1), i32)],
        grid=(m // tile,),
        in_specs=in_specs,
        out_specs=[row(d), row_off(d // 2), col, col, col, pl.BlockSpec((N_EXPERTS, 1), lambda i: (0, 0))],
        scratch_shapes=[pltpu.VMEM((N_EXPERTS, 1), f32)],
        input_output_aliases=aliases,
        compiler_params=_params(("arbitrary",)),
        name="merge_and_route",
    )(*args)


def routing_plan(eid, rank, counts, p_max):
    counts = counts.reshape(N_EXPERTS)
    padded = ((counts + EXPERT_TILE - 1) // EXPERT_TILE) * EXPERT_TILE
    ends = jnp.cumsum(padded)
    starts = ends - padded
    onehot = eid[:, :, None] == jnp.arange(N_EXPERTS, dtype=i32)[None, None, :]
    pos = rank + jnp.sum(jnp.where(onehot, starts[None, None, :], 0), axis=-1)
    n_tiles = p_max // EXPERT_TILE
    tile_start = jnp.arange(n_tiles, dtype=i32) * EXPERT_TILE
    tile_valid = tile_start < ends[-1]
    tile_exp = jnp.sum((ends[None, :] <= tile_start[:, None]).astype(i32), axis=1)
    return pos.astype(i32), jnp.minimum(tile_exp, N_EXPERTS - 1), tile_valid.astype(i32)


def _sc_worker_id():
    return lax.axis_index("subcore") * SC_CORES + lax.axis_index("core")


def sc_scatter_rows(table, pos, p_rows):
    m, w = table.shape
    n_chunks = m // SC_IDX_CHUNK
    steps = -(-n_chunks // SC_WORKERS)
    pos3 = pos.reshape(8, n_chunks, SC_IDX_CHUNK).transpose(1, 0, 2)
    mesh = plsc.VectorSubcoreMesh(core_axis_name="core", subcore_axis_name="subcore")

    @functools.partial(
        pl.kernel,
        out_type=jax.ShapeDtypeStruct((p_rows, w), table.dtype),
        mesh=mesh,
        scratch_types=[
            pltpu.VMEM((8, SC_IDX_CHUNK), i32),
            pltpu.VMEM((SC_IDX_CHUNK, w), table.dtype),
            pltpu.SemaphoreType.DMA,
        ],
    )
    def scatter(x_hbm, p_hbm, o_hbm, idx_v, rows_v, sem):
        wid = _sc_worker_id()

        @pl.loop(0, steps)
        def _(si):
            chunk = si * SC_WORKERS + wid

            @pl.when(chunk < n_chunks)
            def _():
                pltpu.sync_copy(p_hbm.at[chunk], idx_v)
                pltpu.sync_copy(x_hbm.at[pl.ds(chunk * SC_IDX_CHUNK, SC_IDX_CHUNK)], rows_v)
                copies = [pltpu.async_copy(rows_v, o_hbm.at[idx_v.at[k]], sem) for k in range(TOP_K)]
                for cp in copies:
                    cp.wait()

    return scatter(table, pos3)


def sc_gather_rows(table, idx):
    n_idx = idx.shape[0]
    w = table.shape[1]
    n_chunks = n_idx // SC_IDX_CHUNK
    steps = -(-n_chunks // SC_WORKERS)
    half = SC_IDX_CHUNK // 2
    mesh = plsc.VectorSubcoreMesh(core_axis_name="core", subcore_axis_name="subcore")

    @functools.partial(
        pl.kernel,
        out_type=jax.ShapeDtypeStruct((n_idx, w), table.dtype),
        mesh=mesh,
        scratch_types=[
            pltpu.VMEM((SC_IDX_CHUNK,), i32),
            pltpu.VMEM((half, w), table.dtype),
            pltpu.VMEM((half, w), table.dtype),
            pltpu.SemaphoreType.DMA,
            pltpu.SemaphoreType.DMA,
            pltpu.SemaphoreType.DMA,
            pltpu.SemaphoreType.DMA,
        ],
    )
    def gather(x_hbm, i_hbm, o_hbm, idx_v, buf0, buf1, g0_sem, g1_sem, w0_sem, w1_sem):
        wid = _sc_worker_id()

        @pl.loop(0, steps)
        def _(si):
            chunk = si * SC_WORKERS + wid

            @pl.when(chunk < n_chunks)
            def _():
                cbase = chunk * SC_IDX_CHUNK
                pltpu.sync_copy(i_hbm.at[pl.ds(cbase, SC_IDX_CHUNK)], idx_v)
                g0 = pltpu.async_copy(x_hbm.at[idx_v.at[pl.ds(0, half)]], buf0, g0_sem)
                g1 = pltpu.async_copy(x_hbm.at[idx_v.at[pl.ds(half, half)]], buf1, g1_sem)
                g0.wait()
                w0 = pltpu.async_copy(buf0, o_hbm.at[pl.ds(cbase, half)], w0_sem)
                g1.wait()
                w1 = pltpu.async_copy(buf1, o_hbm.at[pl.ds(cbase + half, half)], w1_sem)
                w0.wait()
                w1.wait()

    return gather(table, idx)


def _expert_kernel(te_ref, tv_ref, nx_ref, sl_ref, x_ref, wg_hbm, wu_hbm, wd_hbm, o_ref,
                   wg_f, wu_f, wd_f, wg_b, wu_b, wd_b, sems, *, layer):
    i = pl.program_id(0)

    def weight_copies(expert, slot):
        return [pltpu.make_async_copy(hbm.at[layer, expert], buf.at[slot], sems.at[slot, j])
                for j, (hbm, buf) in enumerate(((wg_hbm, wg_f), (wu_hbm, wu_f), (wd_hbm, wd_f)))]

    @pl.when(i == 0)
    def _():
        for cp in weight_copies(te_ref[0], 0):
            cp.start()

    @pl.when((i == 0) | (te_ref[i] != te_ref[jnp.maximum(i - 1, 0)]))
    def _():
        slot = sl_ref[i]
        for cp in weight_copies(te_ref[i], slot):
            cp.wait()
        wg_b[...] = wg_f[slot].astype(bf16)
        wu_b[...] = wu_f[slot].astype(bf16)
        wd_b[...] = wd_f[slot].astype(bf16)

        @pl.when(nx_ref[i] >= 0)
        def _():
            for cp in weight_copies(nx_ref[i], 1 - slot):
                cp.start()

    @pl.when(tv_ref[i] != 0)
    def _():
        lo, hi = _unpack_bf16_pairs(x_ref[...])
        half = lo.shape[1]
        a = (jnp.dot(lo, wg_b[:half], preferred_element_type=f32)
             + jnp.dot(hi, wg_b[half:], preferred_element_type=f32))
        b = (jnp.dot(lo, wu_b[:half], preferred_element_type=f32)
             + jnp.dot(hi, wu_b[half:], preferred_element_type=f32))
        hid = (_silu(a) * b).astype(bf16)
        o_ref[...] = _pack_bf16_pairs(jnp.dot(hid, wd_b[...], preferred_element_type=f32))

    @pl.when(tv_ref[i] == 0)
    def _():
        o_ref[...] = jnp.zeros_like(o_ref)


def grouped_experts(xs, tile_exp, tile_valid, wg, wu, wd, layer):
    p, half = xs.shape
    d = 2 * half
    n_tiles = p // EXPERT_TILE
    first = jnp.concatenate([jnp.ones((1,), bool), tile_exp[1:] != tile_exp[:-1]])
    slot = (jnp.cumsum(first.astype(i32)) - 1) % 2
    nxt_at = jnp.sum((tile_exp[None, :] <= tile_exp[:, None]).astype(i32), axis=1)
    nxt = jnp.where(nxt_at < n_tiles, tile_exp[jnp.minimum(nxt_at, n_tiles - 1)], -1)
    tile = pl.BlockSpec((EXPERT_TILE, half), lambda i, *_: (i, 0))
    hbm = pl.BlockSpec(memory_space=pl.ANY)
    grid_spec = pltpu.PrefetchScalarGridSpec(
        num_scalar_prefetch=4,
        grid=(n_tiles,),
        in_specs=[tile, hbm, hbm, hbm],
        out_specs=tile,
        scratch_shapes=[pltpu.VMEM((2, d, D_EXPERT), f32), pltpu.VMEM((2, d, D_EXPERT), f32),
                        pltpu.VMEM((2, D_EXPERT, d), f32),
                        pltpu.VMEM((d, D_EXPERT), bf16), pltpu.VMEM((d, D_EXPERT), bf16),
                        pltpu.VMEM((D_EXPERT, d), bf16), pltpu.SemaphoreType.DMA((2, 3))],
    )
    return pl.pallas_call(
        functools.partial(_expert_kernel, layer=layer),
        out_shape=jax.ShapeDtypeStruct((p, half), i32),
        grid_spec=grid_spec,
        compiler_params=_params(("arbitrary",)),
        name="grouped_experts",
    )(tile_exp, tile_valid, nxt.astype(i32), slot.astype(i32), xs, wg, wu, wd)


def _combine_kernel(x_ref, yg_ref, wt_ref, g2_ref, sh2_ref, sc2_ref, n2_ref, wsg_ref, wsu_ref, wsd_ref, fg_ref, o_ref,
                    *, final):
    x = x_ref[...]
    hb = _rms_mod(x, n2_ref[...], sc2_ref[0], sh2_ref[0]).astype(bf16)
    a = jnp.dot(hb, wsg_ref[...], preferred_element_type=f32)
    b = jnp.dot(hb, wsu_ref[...], preferred_element_type=f32)
    f = jnp.dot((_silu(a) * b).astype(bf16), wsd_ref[...], preferred_element_type=f32)
    wt = wt_ref[...]
    half = x.shape[1] // 2
    f_lo, f_hi = f[:, :half], f[:, half:]
    for k in range(TOP_K):
        lo, hi = _unpack_bf16_pairs(yg_ref[k])
        w = wt[:, k:k + 1]
        f_lo = f_lo + w * lo.astype(f32)
        f_hi = f_hi + w * hi.astype(f32)
    xo = x + g2_ref[0] * jnp.concatenate([f_lo, f_hi], axis=1)
    if final:
        ms = jnp.mean(xo * xo, axis=-1, keepdims=True)
        xo = xo * lax.rsqrt(ms + RMS_EPS) * fg_ref[...]
    o_ref[...] = xo


def combine(x2, yg, wt_rows, row_offset, mods, rows_per_mod, tile, lw, final_g, final):
    m, d = x2.shape
    off = row_offset // tile
    row = lambda w: pl.BlockSpec((tile, w), lambda i: (i, 0))
    modspec = pl.BlockSpec((1, 1, d), lambda i: ((i * tile) // rows_per_mod, 0, 0))
    return pl.pallas_call(
        functools.partial(_combine_kernel, final=final),
        out_shape=jax.ShapeDtypeStruct((m, d), f32),
        grid=(m // tile,),
        in_specs=[row(d), pl.BlockSpec((TOP_K, tile, d // 2), lambda i: (0, i + off, 0)),
                  pl.BlockSpec((tile, 8), lambda i: (i + off, 0)), modspec, modspec, modspec,
                  _const_spec((1, d)), _const_spec((d, D_EXPERT)), _const_spec((d, D_EXPERT)),
                  _const_spec((D_EXPERT, d)), _const_spec((1, d))],
        out_specs=row(d),
        compiler_params=_params(("parallel",)),
        name="combine",
    )(x2, yg, wt_rows, mods["g2"], mods["sh2"], mods["sc2"], lw["norm2_g"], lw["w_sh_gate"], lw["w_sh_up"],
      lw["w_sh_down"], final_g)


def routed_experts(h2p, eid, rank, counts, lw, layer):
    m = h2p.shape[0]
    p_max = m * TOP_K + N_EXPERTS * EXPERT_TILE
    pos, tile_exp, tile_valid = routing_plan(eid, rank, counts, p_max)
    xs = sc_scatter_rows(h2p, pos, p_max)
    ys = grouped_experts(xs, tile_exp, tile_valid, lw["w_exp_gate"], lw["w_exp_up"], lw["w_exp_down"], layer)
    return sc_gather_rows(ys, pos[:TOP_K].reshape(TOP_K * m)).reshape(TOP_K, m, D_MODEL // 2)


def rope_tables(seq):
    rows = seq // GRID_W
    row = jnp.repeat(jnp.arange(rows), GRID_W).astype(f32)
    col = jnp.tile(jnp.arange(GRID_W), rows).astype(f32)
    axis_dim = HEAD_DIM // 2
    inv_freq = 1.0 / (ROPE_THETA ** (jnp.arange(0, axis_dim, 2, dtype=f32) / axis_dim))
    ang_r = row[:, None] * inv_freq
    ang_c = col[:, None] * inv_freq
    ang = jnp.concatenate([ang_r, ang_r, ang_c, ang_c], axis=-1)
    cos, sin = jnp.cos(ang), jnp.sin(ang)
    seg = (jnp.arange(HEAD_DIM) // 16) % 2
    sa = jnp.where(seg == 0, -sin, 0.0)
    sb = jnp.where(seg == 1, sin, 0.0)
    rep = lambda t: jnp.tile(t, (1, 4))
    return rep(cos), rep(sa), rep(sb)


def identity_rope_tables(rows):
    return jnp.ones((rows, 256), f32), jnp.zeros((rows, 256), f32), jnp.zeros((rows, 256), f32)


def kernel(x, c, ctx, c_ctx, w_mod, b_mod, norm1_g, norm2_g, w_in, q_norm_g, k_norm_g, sink, gm_norm_g, gm_ws, gm_b, w_br, w_o, w_router, router_bias, w_exp_gate, w_exp_up, w_exp_down, w_sh_gate, w_sh_up, w_sh_down, final_norm_g):
    bsz_all, seq, d = x.shape
    n_ctx = ctx.shape[1]
    depth = w_mod.shape[0]

    cc = jnp.concatenate([c, c_ctx[None, :], jnp.zeros((MOD_ROWS - bsz_all - 1, d), f32)], axis=0)
    mod_all = compute_mod(cc, w_mod, b_mod)

    lat_tables = rope_tables(seq)
    ctx_tables = identity_rope_tables(n_ctx)
    wc_lat, ws_lat = dft_tables(seq)
    wc_ctx, ws_ctx = dft_tables(n_ctx)
    cs64 = channel_dft_table()
    bd = jnp.asarray(np.kron(np.eye(4), np.full((HEAD_DIM, HEAD_DIM), 1.0 / HEAD_DIM)), dtype=bf16)
    final_g = final_norm_g.reshape(1, d)
    tri = jnp.asarray(np.triu(np.ones((ROW_TILE, ROW_TILE)), 1), dtype=bf16)

    lws = []
    for l in range(depth):
        lws.append({
            "norm1_g": norm1_g[l].reshape(1, d),
            "norm2_g": norm2_g[l].reshape(1, d),
            "w_z": w_in[l, :, :OFF_GATE].astype(bf16),
            "w_gate": w_in[l, :, OFF_GATE:].astype(bf16),
            "qn": jnp.tile(q_norm_g[l], 4).reshape(1, 256),
            "kn": jnp.tile(k_norm_g[l], 2).reshape(1, 128),
            "bd": bd,
            "gm_norm_g": gm_norm_g[l].reshape(1, 256),
            "gm_ws": gm_ws[l].astype(bf16),
            "gm_bias": jnp.repeat(gm_b[l].T, 64, axis=1),
            "cs64": cs64,
            "w_br": w_br[l].astype(bf16),
            "w_o": w_o[l].astype(bf16),
            "w_router_t": w_router[l].T,
            "router_bias": router_bias[l].reshape(N_EXPERTS, 1),
            "tri": tri,
            "w_exp_gate": w_exp_gate,
            "w_exp_up": w_exp_up,
            "w_exp_down": w_exp_down,
            "w_sh_gate": w_sh_gate[l].astype(bf16),
            "w_sh_up": w_sh_up[l].astype(bf16),
            "w_sh_down": w_sh_down[l].astype(bf16),
        })

    bsz = bsz_all // BATCH_CHUNKS
    outs = []
    for ch in range(BATCH_CHUNKS):
        b0 = ch * bsz
        outs.append(_forward_chunk(x[b0:b0 + bsz], ctx[b0:b0 + bsz], mod_all[:, b0:b0 + bsz], mod_all[:, bsz_all],
                                   lws, sink, lat_tables, ctx_tables, (wc_lat, ws_lat), (wc_ctx, ws_ctx), final_g))
    return jnp.concatenate(outs, axis=0)


def _forward_chunk(x, ctx, mod_lat, mod_ctx, lws, sink, lat_tables, ctx_tables, dft_lat, dft_ctx, final_g):
    bsz, seq, d = x.shape
    n_ctx = ctx.shape[1]
    depth = len(lws)
    n_lat = bsz * seq
    n_cx = bsz * n_ctx
    wc_lat, ws_lat = dft_lat
    wc_ctx, ws_ctx = dft_ctx
    xl = x.reshape(n_lat, d)
    xc = ctx.reshape(n_cx, d)
    for l in range(depth):
        ctx_out = l < depth - 1
        lw = lws[l]
        names = ("sh1", "sc1", "g1", "sh2", "sc2", "g2")
        mods_lat = {n: mod_lat[l, :, i * d:(i + 1) * d].reshape(bsz, 1, d) for i, n in enumerate(names)}
        mods_ctx = {n: mod_ctx[l, i * d:(i + 1) * d].reshape(1, 1, d) for i, n in enumerate(names)}
        sink_l = sink[l]

        qg, kg, vg, qw, kw, vw, fxc, fxs, o_d = in_projection(
            xl, mods_lat["sh1"], mods_lat["sc1"], seq, lat_tables, seq, ROW_TILE, lw)
        cqg, ckg, cvg, cqw, ckw, cvw, cfxc, cfxs, co_d = in_projection(
            xc, mods_ctx["sh1"], mods_ctx["sc1"], n_cx, ctx_tables, n_ctx, n_ctx, lw)
        b3 = lambda t, rows: t.reshape(bsz, rows, t.shape[-1])
        ckg3, cvg3, ckw3, cvw3 = b3(ckg, n_ctx), b3(cvg, n_ctx), b3(ckw, n_ctx), b3(cvw, n_ctx)

        o_a = dft_mix(wc_lat, ws_lat, b3(fxc, seq), b3(fxs, seq), ROW_TILE).reshape(n_lat, BRANCH_W)
        o_b = full_attention(b3(qg, seq), [(ckg3, cvg3), (b3(kg, seq), b3(vg, seq))], None, FULL_ATTN_Q_ROWS)
        o_b = o_b.reshape(n_lat, BRANCH_W)
        o_c = window_attention(b3(qw, seq), b3(kw, seq), b3(vw, seq), ckw3, cvw3, sink_l).reshape(n_lat, BRANCH_W)
        m_total = n_lat + (n_cx if ctx_out else 0)
        route = None
        if ctx_out:
            route = [jnp.zeros((m_total, d // 2), i32), jnp.zeros((8, m_total), i32), jnp.zeros((8, m_total), f32),
                     jnp.zeros((8, m_total), i32), jnp.zeros((N_EXPERTS, 1), i32)]
        xl, *route = merge_and_route(xl, mods_lat, seq, (o_a, o_b, o_c, o_d), ROW_TILE, lw, m_total, 0, route)
        if ctx_out:
            co_a = dft_mix(wc_ctx, ws_ctx, b3(cfxc, n_ctx), b3(cfxs, n_ctx), n_ctx).reshape(n_cx, BRANCH_W)
            co_b = full_attention(b3(cqg, n_ctx), [(ckg3, cvg3)], None, n_ctx).reshape(n_cx, BRANCH_W)
            co_c = full_attention(b3(cqw, n_ctx), [(ckw3, cvw3)], sink_l, n_ctx).reshape(n_cx, BRANCH_W)
            xc, *route = merge_and_route(xc, mods_ctx, n_cx, (co_a, co_b, co_c, co_d), ROW_TILE, lw, m_total, n_lat,
                                         route)
        h2p, eid, wt, rank, counts = route
        yg = routed_experts(h2p, eid, rank, counts, lw, l)
        wt_rows = wt.T
        xl = combine(xl, yg, wt_rows, 0, mods_lat, seq, ROW_TILE, lw, final_g, not ctx_out)
        if ctx_out:
            xc = combine(xc, yg, wt_rows, n_lat, mods_ctx, n_cx, ROW_TILE, lw, final_g, False)
    return xl.reshape(bsz, seq, d)
```

```python
import functools
import math

import jax
import jax.numpy as jnp
import numpy as np
from jax import lax
from jax.experimental import pallas as pl
from jax.experimental.pallas import tpu as pltpu
from jax.experimental.pallas import tpu_sc as plsc

f32 = jnp.float32
bf16 = jnp.bfloat16
i32 = jnp.int32

D_MODEL = 1024
HEAD_DIM = 64
GRID_W = 64
ROPE_THETA = 10000.0
ATTN_SCALE = HEAD_DIM ** -0.5
RMS_EPS = 1e-6
NEG_INF = -1e30
Q_BLOCK = 128
WINDOW = 128
GM_CHUNK = 128
N_BRANCHES = 4
BRANCH_W = 256
KV_W = 128
OFF_KV = 0
OFF_Q = 512
OFF_FN = 1024
OFF_GM = 1280
OFF_GATE = 1792
N_EXPERTS = 64
TOP_K = 6
N_GROUPS = 8
GROUP_SIZE = N_EXPERTS // N_GROUPS
TOPK_GROUPS = 4
D_EXPERT = 256
ROUTED_SCALE = 2.5

SC_CORES = 2
SC_SUBCORES = 16
SC_WORKERS = SC_CORES * SC_SUBCORES
SC_IDX_CHUNK = 128

ROW_TILE = 512
MERGE_TILE = 512
EXPERT_TILE = 1024
BATCH_CHUNKS = 1
FULL_ATTN_Q_ROWS = 512
FULL_ATTN_ROW_BLOCKS = 2
WINDOW_BLOCKS_PER_STEP = 4
MOD_ROWS = 24
VMEM_LIMIT = 56 * 1024 * 1024


def _params(sem, vmem=VMEM_LIMIT):
    return pltpu.CompilerParams(dimension_semantics=sem, vmem_limit_bytes=vmem)


def _const_spec(shape):
    nd = len(shape)
    return pl.BlockSpec(shape, lambda *_: (0,) * nd, pipeline_mode=pl.Buffered(1))


def _rms_mod(x, g, sc, sh):
    ms = jnp.mean(x * x, axis=-1, keepdims=True)
    return (x * lax.rsqrt(ms + RMS_EPS) * g) * (1.0 + sc) + sh


def _gelu(x):
    return 0.5 * x * (1.0 + jnp.tanh(math.sqrt(2.0 / math.pi) * (x + 0.044715 * (x * x * x))))


def _silu(x):
    return x * jax.nn.sigmoid(x)


def _mod_kernel(a_ref, w_ref, b_ref, o_ref):
    a = _silu(a_ref[...]).astype(bf16)
    o_ref[0] = jnp.dot(a, w_ref[0].astype(bf16), preferred_element_type=f32) + b_ref[0]


def compute_mod(cc, w_mod, b_mod):
    depth, d, n = w_mod.shape
    tn = 1536
    return pl.pallas_call(
        _mod_kernel,
        out_shape=jax.ShapeDtypeStruct((depth, MOD_ROWS, n), f32),
        grid=(depth, n // tn),
        in_specs=[
            pl.BlockSpec((MOD_ROWS, d), lambda l, j: (0, 0)),
            pl.BlockSpec((1, d, tn), lambda l, j: (l, 0, j)),
            pl.BlockSpec((1, 1, tn), lambda l, j: (l, 0, j)),
        ],
        out_specs=pl.BlockSpec((1, MOD_ROWS, tn), lambda l, j: (l, 0, j)),
        compiler_params=_params(("parallel", "parallel")),
        name="mod_proj",
    )(cc, w_mod, b_mod.reshape(depth, 1, n))


def _inproj_kernel(x_ref, sh_ref, sc_ref, g_ref, w_ref, qn_ref, kn_ref, bd_ref, gmg_ref, ws_ref, gb_ref, cs_ref,
                   cos_ref, sa_ref, sb_ref,
                   qg_ref, kg_ref, vg_ref, qw_ref, kw_ref, vw_ref, xc_ref, xs_ref, od_ref):
    tile = x_ref.shape[0]
    hb = _rms_mod(x_ref[...], g_ref[...], sc_ref[0], sh_ref[0]).astype(bf16)

    def proj(a, b):
        return jnp.dot(hb, w_ref[:, a:b], preferred_element_type=f32)

    def headnorm(t, gain):
        w = t.shape[1]
        sq = t * t
        hi = sq.astype(bf16)
        lo = (sq - hi.astype(f32)).astype(bf16)
        b = bd_ref[:w, :w]
        ms = jnp.dot(hi, b, preferred_element_type=f32) + jnp.dot(lo, b, preferred_element_type=f32)
        return t * lax.rsqrt(ms + RMS_EPS) * gain

    def rope(t):
        w = t.shape[1]
        return (t * cos_ref[:, :w] + pltpu.roll(t, w - 16, 1) * sa_ref[:, :w]
                + pltpu.roll(t, 16, 1) * sb_ref[:, :w])

    def expand_heads(q):
        lane = lax.broadcasted_iota(i32, (1, KV_W), 1)
        low = lane < HEAD_DIM
        blocks = []
        for kv in range(2):
            pair = q[:, KV_W * kv:KV_W * (kv + 1)]
            swapped = pltpu.roll(pair, HEAD_DIM, 1)
            keep = low if kv == 0 else jnp.logical_not(low)
            g0, g1 = (pair, swapped) if kv == 0 else (swapped, pair)
            blocks.append(jnp.where(keep, g0, 0.0))
            blocks.append(jnp.where(keep, g1, 0.0))
        return jnp.concatenate(blocks, axis=1)

    kv = proj(OFF_KV, OFF_Q)
    kg_ref[...] = rope(headnorm(kv[:, 0:128], kn_ref[...])).astype(bf16)
    vg_ref[...] = kv[:, 128:256].astype(bf16)
    kw_ref[...] = rope(kv[:, 256:384]).astype(bf16)
    vw_ref[...] = kv[:, 384:512].astype(bf16)

    qq = proj(OFF_Q, OFF_FN)
    qg = rope(headnorm(qq[:, :256], qn_ref[...])) * ATTN_SCALE
    qg_ref[...] = expand_heads(qg).astype(bf16)
    qw = rope(qq[:, 256:]) * ATTN_SCALE
    qw_ref[...] = expand_heads(qw).astype(bf16)

    fn = proj(OFF_FN, OFF_GM).astype(bf16)
    xcs = jnp.dot(fn, cs_ref[...], preferred_element_type=f32)
    xc_ref[...] = xcs[:, :256].astype(bf16)
    xs_ref[...] = xcs[:, 256:].astype(bf16)

    uv = proj(OFF_GM, OFF_GATE)
    u = _gelu(uv[:, :256])
    v = _gelu(uv[:, 256:])
    vms = jnp.mean(v * v, axis=-1, keepdims=True)
    vn = (v * lax.rsqrt(vms + RMS_EPS) * gmg_ref[...]).astype(bf16)
    lane_grp = lax.broadcasted_iota(i32, (1, 256), 1) // 64
    for c in range(tile // GM_CHUNK):
        rows = slice(c * GM_CHUNK, (c + 1) * GM_CHUNK)
        vch = vn[rows]
        sv = gb_ref[...]
        for g in range(4):
            r = jnp.dot(ws_ref[g], vch, preferred_element_type=f32)
            sv = sv + jnp.where(lane_grp == g, r, 0.0)
        od_ref[rows, :] = (u[rows] * sv).astype(bf16)


def in_projection(x2, sh, sc, rows_per_mod, tables, rows_per_seq, tile, lw):
    m, d = x2.shape
    cos_t, sa_t, sb_t = tables
    seq_blocks = rows_per_seq // tile
    row = lambda w: pl.BlockSpec((tile, w), lambda i: (i, 0))
    modspec = pl.BlockSpec((1, 1, d), lambda i: ((i * tile) // rows_per_mod, 0, 0))
    tabspec = pl.BlockSpec((tile, 256), lambda i: (i % seq_blocks, 0))
    out_w = [512, 128, 128, 512, 128, 128, 256, 256, 256]
    return pl.pallas_call(
        _inproj_kernel,
        out_shape=[jax.ShapeDtypeStruct((m, w), bf16) for w in out_w],
        grid=(m // tile,),
        in_specs=[
            row(d), modspec, modspec, _const_spec((1, d)), _const_spec((d, OFF_GATE)),
            _const_spec((1, 256)), _const_spec((1, 128)), _const_spec((256, 256)), _const_spec((1, 256)),
            _const_spec((4, GM_CHUNK, GM_CHUNK)), _const_spec((GM_CHUNK, 256)), _const_spec((256, 512)),
            tabspec, tabspec, tabspec,
        ],
        out_specs=[row(w) for w in out_w],
        compiler_params=_params(("parallel",)),
        name="in_projection",
    )(x2, sh, sc, lw["norm1_g"], lw["w_z"], lw["qn"], lw["kn"], lw["bd"], lw["gm_norm_g"], lw["gm_ws"], lw["gm_bias"],
      lw["cs64"], cos_t, sa_t, sb_t)


def _dft_kernel(wc_ref, ws_ref, xc_ref, xs_ref, o_ref):
    acc = jnp.dot(wc_ref[...], xc_ref[0], preferred_element_type=f32)
    acc = acc + jnp.dot(ws_ref[...], xs_ref[0], preferred_element_type=f32)
    o_ref[0] = acc.astype(bf16)


def dft_mix(wc, ws, xc, xs, tile):
    nb, length, w = xc.shape
    return pl.pallas_call(
        _dft_kernel,
        out_shape=jax.ShapeDtypeStruct((nb, length, w), bf16),
        grid=(length // tile, nb),
        in_specs=[
            pl.BlockSpec((tile, length), lambda i, b: (i, 0)),
            pl.BlockSpec((tile, length), lambda i, b: (i, 0)),
            pl.BlockSpec((1, length, w), lambda i, b: (b, 0, 0)),
            pl.BlockSpec((1, length, w), lambda i, b: (b, 0, 0)),
        ],
        out_specs=pl.BlockSpec((1, tile, w), lambda i, b: (b, i, 0)),
        compiler_params=_params(("parallel", "parallel")),
        name="dft_mix",
    )(wc, ws, xc, xs)


def dft_tables(length):
    jk = (np.arange(length)[:, None] * np.arange(length)[None, :]) % length
    ang = 2.0 * np.pi * jk / length
    s = 1.0 / math.sqrt(length)
    return jnp.asarray(np.cos(ang) * s, dtype=bf16), jnp.asarray(-np.sin(ang) * s, dtype=bf16)


def channel_dft_table():
    jk = (np.arange(64)[:, None] * np.arange(64)[None, :]) % 64
    ang = 2.0 * np.pi * jk / 64
    eye = np.eye(4)
    c = np.kron(eye, np.cos(ang) / 8.0)
    s = np.kron(eye, np.sin(ang) / 8.0)
    return jnp.asarray(np.concatenate([c, s], axis=1), dtype=bf16)


def _attend_blocks(blocks, sink_ref):
    lane = lax.broadcasted_iota(i32, (1, KV_W), 1)
    low = lane < HEAD_DIM
    units = []
    for q_all, pieces in blocks:
        qb = q_all.shape[0]
        for kv in range(2):
            q = jnp.concatenate([q_all[:, KV_W * (2 * kv):KV_W * (2 * kv + 1)],
                                 q_all[:, KV_W * (2 * kv + 1):KV_W * (2 * kv + 2)]], axis=0)
            sink_col = None
            if sink_ref is not None:
                sink_col = jnp.concatenate([jnp.full((qb, 1), sink_ref[2 * kv], f32),
                                            jnp.full((qb, 1), sink_ref[2 * kv + 1], f32)], axis=0)
            units.append((q, pieces, sink_col, low if kv == 0 else jnp.logical_not(low)))

    scores = []
    for q, pieces, _, _ in units:
        unit_scores = []
        for k, _, mask in pieces:
            s = lax.dot_general(q, k, (((1,), (1,)), ((), ())), preferred_element_type=f32)
            unit_scores.append(s if mask is None else jnp.where(mask, s, NEG_INF))
        scores.append(unit_scores)

    maxes = []
    for (_, _, sink_col, _), unit_scores in zip(units, scores):
        m = unit_scores[0].max(axis=-1, keepdims=True)
        for s in unit_scores[1:]:
            m = jnp.maximum(m, s.max(axis=-1, keepdims=True))
        maxes.append(m if sink_col is None else jnp.maximum(m, sink_col))

    probs = [[jnp.exp((s - m).astype(bf16)) for s in unit_scores] for unit_scores, m in zip(scores, maxes)]

    results = []
    for (_, pieces, sink_col, own), unit_probs, m in zip(units, probs, maxes):
        acc = None
        for p, (_, v, _) in zip(unit_probs, pieces):
            pv = jnp.dot(p, jnp.where(own, v, jnp.ones_like(v)), preferred_element_type=f32)
            acc = pv if acc is None else acc + pv
        denom = pltpu.roll(acc, HEAD_DIM, 1)
        if sink_col is not None:
            denom = denom + jnp.exp(sink_col - m)
        results.append(acc * (1.0 / denom))

    outs = []
    for i, (q_all, _) in enumerate(blocks):
        qb = q_all.shape[0]
        r_kv0, r_kv1 = results[2 * i], results[2 * i + 1]
        lo = jnp.where(low, r_kv0[:qb], pltpu.roll(r_kv0[qb:], HEAD_DIM, 1))
        hi = jnp.where(low, pltpu.roll(r_kv1[:qb], HEAD_DIM, 1), r_kv1[qb:])
        outs.append(jnp.concatenate([lo, hi], axis=1))
    return outs


def _full_attn_kernel(*refs, n_pieces, has_sink, row_blocks):
    pos = 0
    sink_ref = None
    if has_sink:
        sink_ref = refs[0]
        pos = 1
    q_ref = refs[pos]
    kv_refs = refs[pos + 1:pos + 1 + 2 * n_pieces]
    o_ref = refs[pos + 1 + 2 * n_pieces]
    pieces = [(kv_refs[2 * i][0], kv_refs[2 * i + 1][0], None) for i in range(n_pieces)]
    rows = q_ref.shape[1] // row_blocks
    blocks = [(q_ref[0, j * rows:(j + 1) * rows, :], pieces) for j in range(row_blocks)]
    for j, out in enumerate(_attend_blocks(blocks, sink_ref)):
        o_ref[0, j * rows:(j + 1) * rows, :] = out.astype(bf16)


def full_attention(q, pieces, sink, qb):
    nb, lq, _ = q.shape
    in_specs = []
    args = []
    if sink is not None:
        in_specs.append(pl.BlockSpec(memory_space=pltpu.SMEM))
        args.append(sink)
    in_specs.append(pl.BlockSpec((1, qb, 512), lambda b, i: (b, i, 0)))
    args.append(q)
    for k, v in pieces:
        spec = pl.BlockSpec((1, k.shape[1], KV_W), lambda b, i: (b, 0, 0))
        in_specs += [spec, spec]
        args += [k, v]
    return pl.pallas_call(
        functools.partial(_full_attn_kernel, n_pieces=len(pieces), has_sink=sink is not None,
                          row_blocks=FULL_ATTN_ROW_BLOCKS),
        out_shape=jax.ShapeDtypeStruct((nb, lq, 256), bf16),
        grid=(nb, lq // qb),
        in_specs=in_specs,
        out_specs=pl.BlockSpec((1, qb, 256), lambda b, i: (b, i, 0)),
        compiler_params=_params(("parallel", "parallel")),
        name="full_attention",
    )(*args)


def _window_attn_kernel(sink_ref, q_ref, k_ref, v_ref, kc_ref, vc_ref, o_ref, *, seq, blocks):
    span = 3 * Q_BLOCK
    ctx_piece = (kc_ref[0], vc_ref[0], None)
    work = []
    for j in range(blocks):
        n = pl.program_id(1) * blocks + j
        start = pl.multiple_of(jnp.clip((n - 1) * Q_BLOCK, 0, seq - span), Q_BLOCK)
        kwin = k_ref[0, pl.ds(start, span), :]
        vwin = v_ref[0, pl.ds(start, span), :]
        row = lax.broadcasted_iota(i32, (2 * Q_BLOCK, span), 0) % Q_BLOCK + n * Q_BLOCK
        col = lax.broadcasted_iota(i32, (2 * Q_BLOCK, span), 1) + start
        mask = jnp.abs(row - col) <= WINDOW
        work.append((q_ref[0, j * Q_BLOCK:(j + 1) * Q_BLOCK, :], [ctx_piece, (kwin, vwin, mask)]))
    for j, out in enumerate(_attend_blocks(work, sink_ref)):
        o_ref[0, j * Q_BLOCK:(j + 1) * Q_BLOCK, :] = out.astype(bf16)


def window_attention(q, k, v, kc, vc, sink):
    nb, seq, _ = q.shape
    n_ctx = kc.shape[1]
    blocks = WINDOW_BLOCKS_PER_STEP
    full = lambda l: pl.BlockSpec((1, l, KV_W), lambda b, i: (b, 0, 0))
    return pl.pallas_call(
        functools.partial(_window_attn_kernel, seq=seq, blocks=blocks),
        out_shape=jax.ShapeDtypeStruct((nb, seq, 256), bf16),
        grid=(nb, seq // (Q_BLOCK * blocks)),
        in_specs=[pl.BlockSpec(memory_space=pltpu.SMEM),
                  pl.BlockSpec((1, Q_BLOCK * blocks, 512), lambda b, i: (b, i, 0)),
                  full(seq), full(seq), full(n_ctx), full(n_ctx)],
        out_specs=pl.BlockSpec((1, Q_BLOCK * blocks, 256), lambda b, i: (b, i, 0)),
        compiler_params=_params(("parallel", "parallel")),
        name="window_attention",
    )(sink, q, k, v, kc, vc)


def _route(logits_t, bias_col):
    t = logits_t.shape[1]
    scores = jax.nn.sigmoid(logits_t)
    choice = scores + bias_col
    sub = lax.broadcasted_iota(i32, (GROUP_SIZE, t), 0)
    grp_score = []
    for g in range(N_GROUPS):
        cg = choice[g * GROUP_SIZE:(g + 1) * GROUP_SIZE]
        m1 = cg.max(axis=0, keepdims=True)
        first = jnp.min(jnp.where(cg == m1, sub, GROUP_SIZE), axis=0, keepdims=True)
        m2 = jnp.where(sub == first, -jnp.inf, cg).max(axis=0, keepdims=True)
        grp_score.append(m1 + m2)
    keep = []
    for g in range(N_GROUPS):
        beaten = jnp.zeros((1, t), i32)
        for o in range(N_GROUPS):
            if o == g:
                continue
            wins = (grp_score[o] > grp_score[g]) | ((grp_score[o] == grp_score[g]) & (o < g))
            beaten = beaten + wins.astype(i32)
        keep.append(jnp.broadcast_to(beaten < TOPK_GROUPS, (GROUP_SIZE, t)))
    masked = jnp.where(jnp.concatenate(keep, axis=0), choice, NEG_INF)
    eid = lax.broadcasted_iota(i32, (N_EXPERTS, t), 0)
    ids, wts = [], []
    for _ in range(TOP_K):
        m = masked.max(axis=0, keepdims=True)
        pick = jnp.min(jnp.where(masked == m, eid, N_EXPERTS), axis=0, keepdims=True)
        sel = eid == pick
        ids.append(pick)
        wts.append(jnp.sum(jnp.where(sel, scores, 0.0), axis=0, keepdims=True))
        masked = jnp.where(sel, -jnp.inf, masked)
    total = wts[0]
    for w in wts[1:]:
        total = total + w
    norm = ROUTED_SCALE / total
    return ids, [w * norm for w in wts]


def _pack_bf16_pairs(x):
    w = x.shape[1] // 2
    lo = lax.bitcast_convert_type(x[:, :w].astype(bf16).astype(f32), i32)
    hi = lax.bitcast_convert_type(x[:, w:].astype(bf16).astype(f32), i32)
    return lax.shift_right_logical(lo, 16) | (hi & jnp.int32(-65536))


def _unpack_bf16_pairs(p):
    lo = lax.bitcast_convert_type(lax.shift_left(p, 16), f32)
    hi = lax.bitcast_convert_type(p & jnp.int32(-65536), f32)
    return lo.astype(bf16), hi.astype(bf16)


def _merge_kernel(x_ref, sh_ref, sc_ref, g1_ref, sh2_ref, sc2_ref, n1_ref, n2_ref, oa_ref, ob_ref, oc_ref, od_ref,
                  wg_ref, wbr_ref, wo_ref, wr_ref, rb_ref, tri_ref, *rest, extends):
    if extends:
        cnt_in_ref = rest[0]
        rest = rest[5:]
    xo_ref, h2_ref, eid_ref, wt_ref, rank_ref, cnt_ref, run_ref = rest

    @pl.when(pl.program_id(0) == 0)
    def _():
        run_ref[...] = cnt_in_ref[...].astype(f32) if extends else jnp.zeros_like(run_ref)

    x = x_ref[...]
    hb = _rms_mod(x, n1_ref[...], sc_ref[0], sh_ref[0]).astype(bf16)
    y = None
    for i, o_ref in enumerate((oa_ref, ob_ref, oc_ref, od_ref)):
        logit = jnp.dot(hb, wg_ref[:, i * D_MODEL:(i + 1) * D_MODEL], preferred_element_type=f32)
        proj = jnp.dot(o_ref[...], wbr_ref[i], preferred_element_type=f32)
        term = jax.nn.sigmoid(logit.astype(bf16)) * proj.astype(bf16)
        y = term if y is None else y + term
    xn = x + g1_ref[0] * jnp.dot(y, wo_ref[...], preferred_element_type=f32)
    xo_ref[...] = xn
    h2 = _rms_mod(xn, n2_ref[...], sc2_ref[0], sh2_ref[0])
    h2_ref[...] = _pack_bf16_pairs(h2)
    logits_t = lax.dot_general(wr_ref[...], h2, (((1,), (1,)), ((), ())), preferred_element_type=f32,
                               precision=lax.Precision.HIGHEST)
    ids, wts = _route(logits_t, rb_ref[...])

    t = x.shape[0]
    eid = lax.broadcasted_iota(i32, (N_EXPERTS, t), 0)
    hits = [eid == pick for pick in ids]
    chosen = hits[0]
    for h in hits[1:]:
        chosen = chosen | h
    chosen = jnp.where(chosen, 1.0, 0.0)
    prefix = jnp.dot(chosen.astype(bf16), tri_ref[...], preferred_element_type=f32)
    offset = run_ref[...] + prefix
    ranks = [jnp.sum(jnp.where(h, offset, 0.0), axis=0, keepdims=True).astype(i32) for h in hits]
    run_ref[...] += jnp.sum(chosen, axis=1, keepdims=True)
    cnt_ref[...] = run_ref[...].astype(i32)

    pad_i = [jnp.zeros((1, t), i32)] * (8 - TOP_K)
    eid_ref[...] = jnp.concatenate(ids + pad_i, axis=0)
    rank_ref[...] = jnp.concatenate(ranks + pad_i, axis=0)
    wt_ref[...] = jnp.concatenate(wts + [jnp.zeros((1, t), f32)] * (8 - TOP_K), axis=0)


def merge_and_route(x2, mods, rows_per_mod, branches, tile, lw, m_total, row_offset, prior):
    m, d = x2.shape
    off = row_offset // tile
    row = lambda w: pl.BlockSpec((tile, w), lambda i: (i, 0))
    row_off = lambda w: pl.BlockSpec((tile, w), lambda i: (i + off, 0))
    modspec = pl.BlockSpec((1, 1, d), lambda i: ((i * tile) // rows_per_mod, 0, 0))
    col = pl.BlockSpec((8, tile), lambda i: (0, i + off))
    in_specs = [row(d)] + [modspec] * 5 + [_const_spec((1, d)), _const_spec((1, d))] + [row(BRANCH_W)] * 4 + [
        _const_spec((d, N_BRANCHES * d)), _const_spec((N_BRANCHES, BRANCH_W, d)), _const_spec((d, d)),
        _const_spec((N_EXPERTS, d)), _const_spec((N_EXPERTS, 1)), _const_spec((tile, tile))]
    args = [x2, mods["sh1"], mods["sc1"], mods["g1"], mods["sh2"], mods["sc2"], lw["norm1_g"], lw["norm2_g"],
            *branches, lw["w_gate"], lw["w_br"], lw["w_o"], lw["w_router_t"], lw["router_bias"], lw["tri"]]
    aliases = {}
    if prior is not None:
        h2p, eid, wt, rank, counts = prior
        n_in = len(args)
        in_specs += [_const_spec((N_EXPERTS, 1))] + [pl.BlockSpec(memory_space=pl.ANY)] * 4
        args += [counts, h2p, eid, wt, rank]
        aliases = {n_in + 1 + j: 1 + j for j in range(4)}
    return pl.pallas_call(
        functools.partial(_merge_kernel, extends=prior is not None),
        out_shape=[jax.ShapeDtypeStruct((m, d), f32), jax.ShapeDtypeStruct((m_total, d // 2), i32),
                   jax.ShapeDtypeStruct((8, m_total), i32), jax.ShapeDtypeStruct((8, m_total), f32),
                   jax.ShapeDtypeStruct((8, m_total), i32), jax.ShapeDtypeStruct((N_EXPERTS, 1), i32)],
        grid=(m // tile,),
        in_specs=in_specs,
        out_specs=[row(d), row_off(d // 2), col, col, col, pl.BlockSpec((N_EXPERTS, 1), lambda i: (0, 0))],
        scratch_shapes=[pltpu.VMEM((N_EXPERTS, 1), f32)],
        input_output_aliases=aliases,
        compiler_params=_params(("arbitrary",)),
        name="merge_and_route",
    )(*args)


def routing_plan(eid, rank, counts, p_max):
    counts = counts.reshape(N_EXPERTS)
    padded = ((counts + EXPERT_TILE - 1) // EXPERT_TILE) * EXPERT_TILE
    ends = jnp.cumsum(padded)
    starts = ends - padded
    onehot = eid[:, :, None] == jnp.arange(N_EXPERTS, dtype=i32)[None, None, :]
    pos = rank + jnp.sum(jnp.where(onehot, starts[None, None, :], 0), axis=-1)
    n_tiles = p_max // EXPERT_TILE
    tile_start = jnp.arange(n_tiles, dtype=i32) * EXPERT_TILE
    tile_valid = tile_start < ends[-1]
    tile_exp = jnp.sum((ends[None, :] <= tile_start[:, None]).astype(i32), axis=1)
    return pos.astype(i32), jnp.minimum(tile_exp, N_EXPERTS - 1), tile_valid.astype(i32)


def _sc_worker_id():
    return lax.axis_index("subcore") * SC_CORES + lax.axis_index("core")


def sc_scatter_rows(table, pos, p_rows):
    m, w = table.shape
    n_chunks = m // SC_IDX_CHUNK
    steps = -(-n_chunks // SC_WORKERS)
    pos3 = pos.reshape(8, n_chunks, SC_IDX_CHUNK).transpose(1, 0, 2)
    mesh = plsc.VectorSubcoreMesh(core_axis_name="core", subcore_axis_name="subcore")

    @functools.partial(
        pl.kernel,
        out_type=jax.ShapeDtypeStruct((p_rows, w), table.dtype),
        mesh=mesh,
        scratch_types=[
            pltpu.VMEM((8, SC_IDX_CHUNK), i32),
            pltpu.VMEM((SC_IDX_CHUNK, w), table.dtype),
            pltpu.SemaphoreType.DMA,
        ],
    )
    def scatter(x_hbm, p_hbm, o_hbm, idx_v, rows_v, sem):
        wid = _sc_worker_id()

        @pl.loop(0, steps)
        def _(si):
            chunk = si * SC_WORKERS + wid

            @pl.when(chunk < n_chunks)
            def _():
                pltpu.sync_copy(p_hbm.at[chunk], idx_v)
                pltpu.sync_copy(x_hbm.at[pl.ds(chunk * SC_IDX_CHUNK, SC_IDX_CHUNK)], rows_v)
                copies = [pltpu.async_copy(rows_v, o_hbm.at[idx_v.at[k]], sem) for k in range(TOP_K)]
                for cp in copies:
                    cp.wait()

    return scatter(table, pos3)


def sc_gather_rows(table, idx):
    n_idx = idx.shape[0]
    w = table.shape[1]
    n_chunks = n_idx // SC_IDX_CHUNK
    steps = -(-n_chunks // SC_WORKERS)
    half = SC_IDX_CHUNK // 2
    mesh = plsc.VectorSubcoreMesh(core_axis_name="core", subcore_axis_name="subcore")

    @functools.partial(
        pl.kernel,
        out_type=jax.ShapeDtypeStruct((n_idx, w), table.dtype),
        mesh=mesh,
        scratch_types=[
            pltpu.VMEM((SC_IDX_CHUNK,), i32),
            pltpu.VMEM((half, w), table.dtype),
            pltpu.VMEM((half, w), table.dtype),
            pltpu.SemaphoreType.DMA,
            pltpu.SemaphoreType.DMA,
            pltpu.SemaphoreType.DMA,
            pltpu.SemaphoreType.DMA,
        ],
    )
    def gather(x_hbm, i_hbm, o_hbm, idx_v, buf0, buf1, g0_sem, g1_sem, w0_sem, w1_sem):
        wid = _sc_worker_id()

        @pl.loop(0, steps)
        def _(si):
            chunk = si * SC_WORKERS + wid

            @pl.when(chunk < n_chunks)
            def _():
                cbase = chunk * SC_IDX_CHUNK
                pltpu.sync_copy(i_hbm.at[pl.ds(cbase, SC_IDX_CHUNK)], idx_v)
                g0 = pltpu.async_copy(x_hbm.at[idx_v.at[pl.ds(0, half)]], buf0, g0_sem)
                g1 = pltpu.async_copy(x_hbm.at[idx_v.at[pl.ds(half, half)]], buf1, g1_sem)
                g0.wait()
                w0 = pltpu.async_copy(buf0, o_hbm.at[pl.ds(cbase, half)], w0_sem)
                g1.wait()
                w1 = pltpu.async_copy(buf1, o_hbm.at[pl.ds(cbase + half, half)], w1_sem)
                w0.wait()
                w1.wait()

    return gather(table, idx)


def _expert_kernel(te_ref, tv_ref, nx_ref, sl_ref, x_ref, wg_hbm, wu_hbm, wd_hbm, o_ref,
                   wg_f, wu_f, wd_f, wg_b, wu_b, wd_b, sems, *, layer):
    i = pl.program_id(0)

    def weight_copies(expert, slot):
        return [pltpu.make_async_copy(hbm.at[layer, expert], buf.at[slot], sems.at[slot, j])
                for j, (hbm, buf) in enumerate(((wg_hbm, wg_f), (wu_hbm, wu_f), (wd_hbm, wd_f)))]

    @pl.when(i == 0)
    def _():
        for cp in weight_copies(te_ref[0], 0):
            cp.start()

    @pl.when((i == 0) | (te_ref[i] != te_ref[jnp.maximum(i - 1, 0)]))
    def _():
        slot = sl_ref[i]
        for cp in weight_copies(te_ref[i], slot):
            cp.wait()
        wg_b[...] = wg_f[slot].astype(bf16)
        wu_b[...] = wu_f[slot].astype(bf16)
        wd_b[...] = wd_f[slot].astype(bf16)

        @pl.when(nx_ref[i] >= 0)
        def _():
            for cp in weight_copies(nx_ref[i], 1 - slot):
                cp.start()

    @pl.when(tv_ref[i] != 0)
    def _():
        lo, hi = _unpack_bf16_pairs(x_ref[...])
        half = lo.shape[1]
        a = (jnp.dot(lo, wg_b[:half], preferred_element_type=f32)
             + jnp.dot(hi, wg_b[half:], preferred_element_type=f32))
        b = (jnp.dot(lo, wu_b[:half], preferred_element_type=f32)
             + jnp.dot(hi, wu_b[half:], preferred_element_type=f32))
        hid = (_silu(a) * b).astype(bf16)
        o_ref[...] = _pack_bf16_pairs(jnp.dot(hid, wd_b[...], preferred_element_type=f32))

    @pl.when(tv_ref[i] == 0)
    def _():
        o_ref[...] = jnp.zeros_like(o_ref)


def grouped_experts(xs, tile_exp, tile_valid, wg, wu, wd, layer):
    p, half = xs.shape
    d = 2 * half
    n_tiles = p // EXPERT_TILE
    first = jnp.concatenate([jnp.ones((1,), bool), tile_exp[1:] != tile_exp[:-1]])
    slot = (jnp.cumsum(first.astype(i32)) - 1) % 2
    nxt_at = jnp.sum((tile_exp[None, :] <= tile_exp[:, None]).astype(i32), axis=1)
    nxt = jnp.where(nxt_at < n_tiles, tile_exp[jnp.minimum(nxt_at, n_tiles - 1)], -1)
    tile = pl.BlockSpec((EXPERT_TILE, half), lambda i, *_: (i, 0))
    hbm = pl.BlockSpec(memory_space=pl.ANY)
    grid_spec = pltpu.PrefetchScalarGridSpec(
        num_scalar_prefetch=4,
        grid=(n_tiles,),
        in_specs=[tile, hbm, hbm, hbm],
        out_specs=tile,
        scratch_shapes=[pltpu.VMEM((2, d, D_EXPERT), f32), pltpu.VMEM((2, d, D_EXPERT), f32),
                        pltpu.VMEM((2, D_EXPERT, d), f32),
                        pltpu.VMEM((d, D_EXPERT), bf16), pltpu.VMEM((d, D_EXPERT), bf16),
                        pltpu.VMEM((D_EXPERT, d), bf16), pltpu.SemaphoreType.DMA((2, 3))],
    )
    return pl.pallas_call(
        functools.partial(_expert_kernel, layer=layer),
        out_shape=jax.ShapeDtypeStruct((p, half), i32),
        grid_spec=grid_spec,
        compiler_params=_params(("arbitrary",)),
        name="grouped_experts",
    )(tile_exp, tile_valid, nxt.astype(i32), slot.astype(i32), xs, wg, wu, wd)


def _combine_kernel(x_ref, yg_ref, wt_ref, g2_ref, sh2_ref, sc2_ref, n2_ref, wsg_ref, wsu_ref, wsd_ref, fg_ref, o_ref,
                    *, final):
    x = x_ref[...]
    hb = _rms_mod(x, n2_ref[...], sc2_ref[0], sh2_ref[0]).astype(bf16)
    a = jnp.dot(hb, wsg_ref[...], preferred_element_type=f32)
    b = jnp.dot(hb, wsu_ref[...], preferred_element_type=f32)
    f = jnp.dot((_silu(a) * b).astype(bf16), wsd_ref[...], preferred_element_type=f32)
    wt = wt_ref[...]
    half = x.shape[1] // 2
    f_lo, f_hi = f[:, :half], f[:, half:]
    for k in range(TOP_K):
        packed = yg_ref[k]
        w = wt[:, k:k + 1]
        f_lo = f_lo + w * lax.bitcast_convert_type(lax.shift_left(packed, 16), f32)
        f_hi = f_hi + w * lax.bitcast_convert_type(packed & jnp.int32(-65536), f32)
    xo = x + g2_ref[0] * jnp.concatenate([f_lo, f_hi], axis=1)
    if final:
        ms = jnp.mean(xo * xo, axis=-1, keepdims=True)
        xo = xo * lax.rsqrt(ms + RMS_EPS) * fg_ref[...]
    o_ref[...] = xo


def combine(x2, yg, wt_rows, row_offset, mods, rows_per_mod, tile, lw, final_g, final):
    m, d = x2.shape
    off = row_offset // tile
    row = lambda w: pl.BlockSpec((tile, w), lambda i: (i, 0))
    modspec = pl.BlockSpec((1, 1, d), lambda i: ((i * tile) // rows_per_mod, 0, 0))
    return pl.pallas_call(
        functools.partial(_combine_kernel, final=final),
        out_shape=jax.ShapeDtypeStruct((m, d), f32),
        grid=(m // tile,),
        in_specs=[row(d), pl.BlockSpec((TOP_K, tile, d // 2), lambda i: (0, i + off, 0)),
                  pl.BlockSpec((tile, 8), lambda i: (i + off, 0)), modspec, modspec, modspec,
                  _const_spec((1, d)), _const_spec((d, D_EXPERT)), _const_spec((d, D_EXPERT)),
                  _const_spec((D_EXPERT, d)), _const_spec((1, d))],
        out_specs=row(d),
        compiler_params=_params(("parallel",)),
        name="combine",
    )(x2, yg, wt_rows, mods["g2"], mods["sh2"], mods["sc2"], lw["norm2_g"], lw["w_sh_gate"], lw["w_sh_up"],
      lw["w_sh_down"], final_g)


def routed_experts(h2p, eid, rank, counts, lw, layer):
    m = h2p.shape[0]
    p_max = m * TOP_K + N_EXPERTS * EXPERT_TILE
    pos, tile_exp, tile_valid = routing_plan(eid, rank, counts, p_max)
    xs = sc_scatter_rows(h2p, pos, p_max)
    ys = grouped_experts(xs, tile_exp, tile_valid, lw["w_exp_gate"], lw["w_exp_up"], lw["w_exp_down"], layer)
    return sc_gather_rows(ys, pos[:TOP_K].reshape(TOP_K * m)).reshape(TOP_K, m, D_MODEL // 2)


def rope_tables(seq):
    rows = seq // GRID_W
    row = jnp.repeat(jnp.arange(rows), GRID_W).astype(f32)
    col = jnp.tile(jnp.arange(GRID_W), rows).astype(f32)
    axis_dim = HEAD_DIM // 2
    inv_freq = 1.0 / (ROPE_THETA ** (jnp.arange(0, axis_dim, 2, dtype=f32) / axis_dim))
    ang_r = row[:, None] * inv_freq
    ang_c = col[:, None] * inv_freq
    ang = jnp.concatenate([ang_r, ang_r, ang_c, ang_c], axis=-1)
    cos, sin = jnp.cos(ang), jnp.sin(ang)
    seg = (jnp.arange(HEAD_DIM) // 16) % 2
    sa = jnp.where(seg == 0, -sin, 0.0)
    sb = jnp.where(seg == 1, sin, 0.0)
    rep = lambda t: jnp.tile(t, (1, 4))
    return rep(cos), rep(sa), rep(sb)


def identity_rope_tables(rows):
    return jnp.ones((rows, 256), f32), jnp.zeros((rows, 256), f32), jnp.zeros((rows, 256), f32)


def kernel(x, c, ctx, c_ctx, w_mod, b_mod, norm1_g, norm2_g, w_in, q_norm_g, k_norm_g, sink, gm_norm_g, gm_ws, gm_b, w_br, w_o, w_router, router_bias, w_exp_gate, w_exp_up, w_exp_down, w_sh_gate, w_sh_up, w_sh_down, final_norm_g):
    bsz_all, seq, d = x.shape
    n_ctx = ctx.shape[1]
    depth = w_mod.shape[0]

    cc = jnp.concatenate([c, c_ctx[None, :], jnp.zeros((MOD_ROWS - bsz_all - 1, d), f32)], axis=0)
    mod_all = compute_mod(cc, w_mod, b_mod)

    lat_tables = rope_tables(seq)
    ctx_tables = identity_rope_tables(n_ctx)
    wc_lat, ws_lat = dft_tables(seq)
    wc_ctx, ws_ctx = dft_tables(n_ctx)
    cs64 = channel_dft_table()
    bd = jnp.asarray(np.kron(np.eye(4), np.full((HEAD_DIM, HEAD_DIM), 1.0 / HEAD_DIM)), dtype=bf16)
    final_g = final_norm_g.reshape(1, d)
    tri = jnp.asarray(np.triu(np.ones((MERGE_TILE, MERGE_TILE)), 1), dtype=bf16)

    lws = []
    for l in range(depth):
        lws.append({
            "norm1_g": norm1_g[l].reshape(1, d),
            "norm2_g": norm2_g[l].reshape(1, d),
            "w_z": w_in[l, :, :OFF_GATE].astype(bf16),
            "w_gate": w_in[l, :, OFF_GATE:].astype(bf16),
            "qn": jnp.tile(q_norm_g[l], 4).reshape(1, 256),
            "kn": jnp.tile(k_norm_g[l], 2).reshape(1, 128),
            "bd": bd,
            "gm_norm_g": gm_norm_g[l].reshape(1, 256),
            "gm_ws": gm_ws[l].astype(bf16),
            "gm_bias": jnp.repeat(gm_b[l].T, 64, axis=1),
            "cs64": cs64,
            "w_br": w_br[l].astype(bf16),
            "w_o": w_o[l].astype(bf16),
            "w_router_t": w_router[l].T,
            "router_bias": router_bias[l].reshape(N_EXPERTS, 1),
            "tri": tri,
            "w_exp_gate": w_exp_gate,
            "w_exp_up": w_exp_up,
            "w_exp_down": w_exp_down,
            "w_sh_gate": w_sh_gate[l].astype(bf16),
            "w_sh_up": w_sh_up[l].astype(bf16),
            "w_sh_down": w_sh_down[l].astype(bf16),
        })

    bsz = bsz_all // BATCH_CHUNKS
    outs = []
    for ch in range(BATCH_CHUNKS):
        b0 = ch * bsz
        outs.append(_forward_chunk(x[b0:b0 + bsz], ctx[b0:b0 + bsz], mod_all[:, b0:b0 + bsz], mod_all[:, bsz_all],
                                   lws, sink, lat_tables, ctx_tables, (wc_lat, ws_lat), (wc_ctx, ws_ctx), final_g))
    return jnp.concatenate(outs, axis=0)


def _forward_chunk(x, ctx, mod_lat, mod_ctx, lws, sink, lat_tables, ctx_tables, dft_lat, dft_ctx, final_g):
    bsz, seq, d = x.shape
    n_ctx = ctx.shape[1]
    depth = len(lws)
    n_lat = bsz * seq
    n_cx = bsz * n_ctx
    wc_lat, ws_lat = dft_lat
    wc_ctx, ws_ctx = dft_ctx
    xl = x.reshape(n_lat, d)
    xc = ctx.reshape(n_cx, d)
    for l in range(depth):
        ctx_out = l < depth - 1
        lw = lws[l]
        names = ("sh1", "sc1", "g1", "sh2", "sc2", "g2")
        mods_lat = {n: mod_lat[l, :, i * d:(i + 1) * d].reshape(bsz, 1, d) for i, n in enumerate(names)}
        mods_ctx = {n: mod_ctx[l, i * d:(i + 1) * d].reshape(1, 1, d) for i, n in enumerate(names)}
        sink_l = sink[l]

        qg, kg, vg, qw, kw, vw, fxc, fxs, o_d = in_projection(
            xl, mods_lat["sh1"], mods_lat["sc1"], seq, lat_tables, seq, ROW_TILE, lw)
        cqg, ckg, cvg, cqw, ckw, cvw, cfxc, cfxs, co_d = in_projection(
            xc, mods_ctx["sh1"], mods_ctx["sc1"], n_cx, ctx_tables, n_ctx, n_ctx, lw)
        b3 = lambda t, rows: t.reshape(bsz, rows, t.shape[-1])
        ckg3, cvg3, ckw3, cvw3 = b3(ckg, n_ctx), b3(cvg, n_ctx), b3(ckw, n_ctx), b3(cvw, n_ctx)

        o_a = dft_mix(wc_lat, ws_lat, b3(fxc, seq), b3(fxs, seq), ROW_TILE).reshape(n_lat, BRANCH_W)
        o_b = full_attention(b3(qg, seq), [(ckg3, cvg3), (b3(kg, seq), b3(vg, seq))], None, FULL_ATTN_Q_ROWS)
        o_b = o_b.reshape(n_lat, BRANCH_W)
        o_c = window_attention(b3(qw, seq), b3(kw, seq), b3(vw, seq), ckw3, cvw3, sink_l).reshape(n_lat, BRANCH_W)
        m_total = n_lat + (n_cx if ctx_out else 0)
        route = None
        if ctx_out:
            route = [jnp.zeros((m_total, d // 2), i32), jnp.zeros((8, m_total), i32), jnp.zeros((8, m_total), f32),
                     jnp.zeros((8, m_total), i32), jnp.zeros((N_EXPERTS, 1), i32)]
        xl, *route = merge_and_route(xl, mods_lat, seq, (o_a, o_b, o_c, o_d), MERGE_TILE, lw, m_total, 0, route)
        if ctx_out:
            co_a = dft_mix(wc_ctx, ws_ctx, b3(cfxc, n_ctx), b3(cfxs, n_ctx), n_ctx).reshape(n_cx, BRANCH_W)
            co_b = full_attention(b3(cqg, n_ctx), [(ckg3, cvg3)], None, n_ctx).reshape(n_cx, BRANCH_W)
            co_c = full_attention(b3(cqw, n_ctx), [(ckw3, cvw3)], sink_l, n_ctx).reshape(n_cx, BRANCH_W)
            xc, *route = merge_and_route(xc, mods_ctx, n_cx, (co_a, co_b, co_c, co_d), MERGE_TILE, lw, m_total, n_lat,
                                         route)
        h2p, eid, wt, rank, counts = route
        yg = routed_experts(h2p, eid, rank, counts, lw, l)
        wt_rows = wt.T
        xl = combine(xl, yg, wt_rows, 0, mods_lat, seq, ROW_TILE, lw, final_g, not ctx_out)
        if ctx_out:
            xc = combine(xc, yg, wt_rows, n_lat, mods_ctx, n_cx, ROW_TILE, lw, final_g, False)
    return xl.reshape(bsz, seq, d)
```

```python
import functools
import math

import jax
import jax.numpy as jnp
import numpy as np
from jax import lax
from jax.experimental import pallas as pl
from jax.experimental.pallas import tpu as pltpu
from jax.experimental.pallas import tpu_sc as plsc

f32 = jnp.float32
bf16 = jnp.bfloat16
i32 = jnp.int32

D_MODEL = 1024
HEAD_DIM = 64
GRID_W = 64
ROPE_THETA = 10000.0
ATTN_SCALE = HEAD_DIM ** -0.5
RMS_EPS = 1e-6
NEG_INF = -1e30
Q_BLOCK = 128
WINDOW = 128
GM_CHUNK = 128
N_BRANCHES = 4
BRANCH_W = 256
KV_W = 128
OFF_KV = 0
OFF_Q = 512
OFF_FN = 1024
OFF_GM = 1280
OFF_GATE = 1792
N_EXPERTS = 64
TOP_K = 6
N_GROUPS = 8
GROUP_SIZE = N_EXPERTS // N_GROUPS
TOPK_GROUPS = 4
D_EXPERT = 256
ROUTED_SCALE = 2.5

SC_CORES = 2
SC_SUBCORES = 16
SC_WORKERS = SC_CORES * SC_SUBCORES
SC_IDX_CHUNK = 128

ROW_TILE = 512
MERGE_TILE = 512
ROUTER_AFTER_BRANCHES = 2
EXPERT_TILE = 1024
BATCH_CHUNKS = 1
FULL_ATTN_Q_ROWS = 512
FULL_ATTN_ROW_BLOCKS = 2
WINDOW_BLOCKS_PER_STEP = 4
MOD_ROWS = 24
VMEM_LIMIT = 56 * 1024 * 1024


def _params(sem, vmem=VMEM_LIMIT):
    return pltpu.CompilerParams(dimension_semantics=sem, vmem_limit_bytes=vmem)


def _const_spec(shape):
    nd = len(shape)
    return pl.BlockSpec(shape, lambda *_: (0,) * nd, pipeline_mode=pl.Buffered(1))


def _rms_mod(x, g, sc, sh):
    ms = jnp.mean(x * x, axis=-1, keepdims=True)
    return (x * lax.rsqrt(ms + RMS_EPS) * g) * (1.0 + sc) + sh


def _gelu(x):
    return 0.5 * x * (1.0 + jnp.tanh(math.sqrt(2.0 / math.pi) * (x + 0.044715 * (x * x * x))))


def _silu(x):
    return x * jax.nn.sigmoid(x)


def _mod_kernel(a_ref, w_ref, b_ref, o_ref):
    a = _silu(a_ref[...]).astype(bf16)
    o_ref[0] = jnp.dot(a, w_ref[0].astype(bf16), preferred_element_type=f32) + b_ref[0]


def compute_mod(cc, w_mod, b_mod):
    depth, d, n = w_mod.shape
    tn = 1536
    return pl.pallas_call(
        _mod_kernel,
        out_shape=jax.ShapeDtypeStruct((depth, MOD_ROWS, n), f32),
        grid=(depth, n // tn),
        in_specs=[
            pl.BlockSpec((MOD_ROWS, d), lambda l, j: (0, 0)),
            pl.BlockSpec((1, d, tn), lambda l, j: (l, 0, j)),
            pl.BlockSpec((1, 1, tn), lambda l, j: (l, 0, j)),
        ],
        out_specs=pl.BlockSpec((1, MOD_ROWS, tn), lambda l, j: (l, 0, j)),
        compiler_params=_params(("parallel", "parallel")),
        name="mod_proj",
    )(cc, w_mod, b_mod.reshape(depth, 1, n))


def _inproj_kernel(x_ref, sh_ref, sc_ref, g_ref, w_ref, qn_ref, kn_ref, bd_ref, gmg_ref, ws_ref, gb_ref, cs_ref,
                   cos_ref, sa_ref, sb_ref,
                   qg_ref, kg_ref, vg_ref, qw_ref, kw_ref, vw_ref, xc_ref, xs_ref, od_ref):
    tile = x_ref.shape[0]
    hb = _rms_mod(x_ref[...], g_ref[...], sc_ref[0], sh_ref[0]).astype(bf16)

    def proj(a, b):
        return jnp.dot(hb, w_ref[:, a:b], preferred_element_type=f32)

    def headnorm(t, gain):
        w = t.shape[1]
        sq = t * t
        hi = sq.astype(bf16)
        lo = (sq - hi.astype(f32)).astype(bf16)
        b = bd_ref[:w, :w]
        ms = jnp.dot(hi, b, preferred_element_type=f32) + jnp.dot(lo, b, preferred_element_type=f32)
        return t * lax.rsqrt(ms + RMS_EPS) * gain

    def rope(t):
        w = t.shape[1]
        return (t * cos_ref[:, :w] + pltpu.roll(t, w - 16, 1) * sa_ref[:, :w]
                + pltpu.roll(t, 16, 1) * sb_ref[:, :w])

    def expand_heads(q):
        lane = lax.broadcasted_iota(i32, (1, KV_W), 1)
        low = lane < HEAD_DIM
        blocks = []
        for kv in range(2):
            pair = q[:, KV_W * kv:KV_W * (kv + 1)]
            swapped = pltpu.roll(pair, HEAD_DIM, 1)
            keep = low if kv == 0 else jnp.logical_not(low)
            g0, g1 = (pair, swapped) if kv == 0 else (swapped, pair)
            blocks.append(jnp.where(keep, g0, 0.0))
            blocks.append(jnp.where(keep, g1, 0.0))
        return jnp.concatenate(blocks, axis=1)

    kv = proj(OFF_KV, OFF_Q)
    kg_ref[...] = rope(headnorm(kv[:, 0:128], kn_ref[...])).astype(bf16)
    vg_ref[...] = kv[:, 128:256].astype(bf16)
    kw_ref[...] = rope(kv[:, 256:384]).astype(bf16)
    vw_ref[...] = kv[:, 384:512].astype(bf16)

    qq = proj(OFF_Q, OFF_FN)
    qg = rope(headnorm(qq[:, :256], qn_ref[...])) * ATTN_SCALE
    qg_ref[...] = expand_heads(qg).astype(bf16)
    qw = rope(qq[:, 256:]) * ATTN_SCALE
    qw_ref[...] = expand_heads(qw).astype(bf16)

    fn = proj(OFF_FN, OFF_GM).astype(bf16)
    xcs = jnp.dot(fn, cs_ref[...], preferred_element_type=f32)
    xc_ref[...] = xcs[:, :256].astype(bf16)
    xs_ref[...] = xcs[:, 256:].astype(bf16)

    uv = proj(OFF_GM, OFF_GATE)
    u = _gelu(uv[:, :256])
    v = _gelu(uv[:, 256:])
    vms = jnp.mean(v * v, axis=-1, keepdims=True)
    vn = (v * lax.rsqrt(vms + RMS_EPS) * gmg_ref[...]).astype(bf16)
    lane_grp = lax.broadcasted_iota(i32, (1, 256), 1) // 64
    for c in range(tile // GM_CHUNK):
        rows = slice(c * GM_CHUNK, (c + 1) * GM_CHUNK)
        vch = vn[rows]
        sv = gb_ref[...]
        for g in range(4):
            r = jnp.dot(ws_ref[g], vch, preferred_element_type=f32)
            sv = sv + jnp.where(lane_grp == g, r, 0.0)
        od_ref[rows, :] = (u[rows] * sv).astype(bf16)


def in_projection(x2, sh, sc, rows_per_mod, tables, rows_per_seq, tile, lw):
    m, d = x2.shape
    cos_t, sa_t, sb_t = tables
    seq_blocks = rows_per_seq // tile
    row = lambda w: pl.BlockSpec((tile, w), lambda i: (i, 0))
    modspec = pl.BlockSpec((1, 1, d), lambda i: ((i * tile) // rows_per_mod, 0, 0))
    tabspec = pl.BlockSpec((tile, 256), lambda i: (i % seq_blocks, 0))
    out_w = [512, 128, 128, 512, 128, 128, 256, 256, 256]
    return pl.pallas_call(
        _inproj_kernel,
        out_shape=[jax.ShapeDtypeStruct((m, w), bf16) for w in out_w],
        grid=(m // tile,),
        in_specs=[
            row(d), modspec, modspec, _const_spec((1, d)), _const_spec((d, OFF_GATE)),
            _const_spec((1, 256)), _const_spec((1, 128)), _const_spec((256, 256)), _const_spec((1, 256)),
            _const_spec((4, GM_CHUNK, GM_CHUNK)), _const_spec((GM_CHUNK, 256)), _const_spec((256, 512)),
            tabspec, tabspec, tabspec,
        ],
        out_specs=[row(w) for w in out_w],
        compiler_params=_params(("parallel",)),
        name="in_projection",
    )(x2, sh, sc, lw["norm1_g"], lw["w_z"], lw["qn"], lw["kn"], lw["bd"], lw["gm_norm_g"], lw["gm_ws"], lw["gm_bias"],
      lw["cs64"], cos_t, sa_t, sb_t)


def _dft_kernel(wc_ref, ws_ref, xc_ref, xs_ref, o_ref):
    acc = jnp.dot(wc_ref[...], xc_ref[0], preferred_element_type=f32)
    acc = acc + jnp.dot(ws_ref[...], xs_ref[0], preferred_element_type=f32)
    o_ref[0] = acc.astype(bf16)


def dft_mix(wc, ws, xc, xs, tile):
    nb, length, w = xc.shape
    return pl.pallas_call(
        _dft_kernel,
        out_shape=jax.ShapeDtypeStruct((nb, length, w), bf16),
        grid=(length // tile, nb),
        in_specs=[
            pl.BlockSpec((tile, length), lambda i, b: (i, 0)),
            pl.BlockSpec((tile, length), lambda i, b: (i, 0)),
            pl.BlockSpec((1, length, w), lambda i, b: (b, 0, 0)),
            pl.BlockSpec((1, length, w), lambda i, b: (b, 0, 0)),
        ],
        out_specs=pl.BlockSpec((1, tile, w), lambda i, b: (b, i, 0)),
        compiler_params=_params(("parallel", "parallel")),
        name="dft_mix",
    )(wc, ws, xc, xs)


def dft_tables(length):
    jk = (np.arange(length)[:, None] * np.arange(length)[None, :]) % length
    ang = 2.0 * np.pi * jk / length
    s = 1.0 / math.sqrt(length)
    return jnp.asarray(np.cos(ang) * s, dtype=bf16), jnp.asarray(-np.sin(ang) * s, dtype=bf16)


def channel_dft_table():
    jk = (np.arange(64)[:, None] * np.arange(64)[None, :]) % 64
    ang = 2.0 * np.pi * jk / 64
    eye = np.eye(4)
    c = np.kron(eye, np.cos(ang) / 8.0)
    s = np.kron(eye, np.sin(ang) / 8.0)
    return jnp.asarray(np.concatenate([c, s], axis=1), dtype=bf16)


def _attend_blocks(blocks, sink_ref):
    lane = lax.broadcasted_iota(i32, (1, KV_W), 1)
    low = lane < HEAD_DIM
    units = []
    for q_all, pieces in blocks:
        qb = q_all.shape[0]
        for kv in range(2):
            q = jnp.concatenate([q_all[:, KV_W * (2 * kv):KV_W * (2 * kv + 1)],
                                 q_all[:, KV_W * (2 * kv + 1):KV_W * (2 * kv + 2)]], axis=0)
            sink_col = None
            if sink_ref is not None:
                sink_col = jnp.concatenate([jnp.full((qb, 1), sink_ref[2 * kv], f32),
                                            jnp.full((qb, 1), sink_ref[2 * kv + 1], f32)], axis=0)
            units.append((q, pieces, sink_col, low if kv == 0 else jnp.logical_not(low)))

    scores = []
    for q, pieces, _, _ in units:
        unit_scores = []
        for k, _, mask in pieces:
            s = lax.dot_general(q, k, (((1,), (1,)), ((), ())), preferred_element_type=f32)
            unit_scores.append(s if mask is None else jnp.where(mask, s, NEG_INF))
        scores.append(unit_scores)

    maxes = []
    for (_, _, sink_col, _), unit_scores in zip(units, scores):
        m = unit_scores[0].max(axis=-1, keepdims=True)
        for s in unit_scores[1:]:
            m = jnp.maximum(m, s.max(axis=-1, keepdims=True))
        maxes.append(m if sink_col is None else jnp.maximum(m, sink_col))

    probs = [[jnp.exp((s - m).astype(bf16)) for s in unit_scores] for unit_scores, m in zip(scores, maxes)]

    results = []
    for (_, pieces, sink_col, own), unit_probs, m in zip(units, probs, maxes):
        acc = None
        for p, (_, v, _) in zip(unit_probs, pieces):
            pv = jnp.dot(p, jnp.where(own, v, jnp.ones_like(v)), preferred_element_type=f32)
            acc = pv if acc is None else acc + pv
        denom = pltpu.roll(acc, HEAD_DIM, 1)
        if sink_col is not None:
            denom = denom + jnp.exp(sink_col - m)
        results.append(acc * (1.0 / denom))

    outs = []
    for i, (q_all, _) in enumerate(blocks):
        qb = q_all.shape[0]
        r_kv0, r_kv1 = results[2 * i], results[2 * i + 1]
        lo = jnp.where(low, r_kv0[:qb], pltpu.roll(r_kv0[qb:], HEAD_DIM, 1))
        hi = jnp.where(low, pltpu.roll(r_kv1[:qb], HEAD_DIM, 1), r_kv1[qb:])
        outs.append(jnp.concatenate([lo, hi], axis=1))
    return outs


def _full_attn_kernel(*refs, n_pieces, has_sink, row_blocks):
    pos = 0
    sink_ref = None
    if has_sink:
        sink_ref = refs[0]
        pos = 1
    q_ref = refs[pos]
    kv_refs = refs[pos + 1:pos + 1 + 2 * n_pieces]
    o_ref = refs[pos + 1 + 2 * n_pieces]
    pieces = [(kv_refs[2 * i][0], kv_refs[2 * i + 1][0], None) for i in range(n_pieces)]
    rows = q_ref.shape[1] // row_blocks
    blocks = [(q_ref[0, j * rows:(j + 1) * rows, :], pieces) for j in range(row_blocks)]
    for j, out in enumerate(_attend_blocks(blocks, sink_ref)):
        o_ref[0, j * rows:(j + 1) * rows, :] = out.astype(bf16)


def full_attention(q, pieces, sink, qb):
    nb, lq, _ = q.shape
    in_specs = []
    args = []
    if sink is not None:
        in_specs.append(pl.BlockSpec(memory_space=pltpu.SMEM))
        args.append(sink)
    in_specs.append(pl.BlockSpec((1, qb, 512), lambda b, i: (b, i, 0)))
    args.append(q)
    for k, v in pieces:
        spec = pl.BlockSpec((1, k.shape[1], KV_W), lambda b, i: (b, 0, 0))
        in_specs += [spec, spec]
        args += [k, v]
    return pl.pallas_call(
        functools.partial(_full_attn_kernel, n_pieces=len(pieces), has_sink=sink is not None,
                          row_blocks=FULL_ATTN_ROW_BLOCKS),
        out_shape=jax.ShapeDtypeStruct((nb, lq, 256), bf16),
        grid=(nb, lq // qb),
        in_specs=in_specs,
        out_specs=pl.BlockSpec((1, qb, 256), lambda b, i: (b, i, 0)),
        compiler_params=_params(("parallel", "parallel")),
        name="full_attention",
    )(*args)


def _window_attn_kernel(sink_ref, q_ref, k_ref, v_ref, kc_ref, vc_ref, o_ref, *, seq, blocks):
    span = 3 * Q_BLOCK
    ctx_piece = (kc_ref[0], vc_ref[0], None)
    work = []
    for j in range(blocks):
        n = pl.program_id(1) * blocks + j
        start = pl.multiple_of(jnp.clip((n - 1) * Q_BLOCK, 0, seq - span), Q_BLOCK)
        kwin = k_ref[0, pl.ds(start, span), :]
        vwin = v_ref[0, pl.ds(start, span), :]
        row = lax.broadcasted_iota(i32, (2 * Q_BLOCK, span), 0) % Q_BLOCK + n * Q_BLOCK
        col = lax.broadcasted_iota(i32, (2 * Q_BLOCK, span), 1) + start
        mask = jnp.abs(row - col) <= WINDOW
        work.append((q_ref[0, j * Q_BLOCK:(j + 1) * Q_BLOCK, :], [ctx_piece, (kwin, vwin, mask)]))
    for j, out in enumerate(_attend_blocks(work, sink_ref)):
        o_ref[0, j * Q_BLOCK:(j + 1) * Q_BLOCK, :] = out.astype(bf16)


def window_attention(q, k, v, kc, vc, sink):
    nb, seq, _ = q.shape
    n_ctx = kc.shape[1]
    blocks = WINDOW_BLOCKS_PER_STEP
    full = lambda l: pl.BlockSpec((1, l, KV_W), lambda b, i: (b, 0, 0))
    return pl.pallas_call(
        functools.partial(_window_attn_kernel, seq=seq, blocks=blocks),
        out_shape=jax.ShapeDtypeStruct((nb, seq, 256), bf16),
        grid=(nb, seq // (Q_BLOCK * blocks)),
        in_specs=[pl.BlockSpec(memory_space=pltpu.SMEM),
                  pl.BlockSpec((1, Q_BLOCK * blocks, 512), lambda b, i: (b, i, 0)),
                  full(seq), full(seq), full(n_ctx), full(n_ctx)],
        out_specs=pl.BlockSpec((1, Q_BLOCK * blocks, 256), lambda b, i: (b, i, 0)),
        compiler_params=_params(("parallel", "parallel")),
        name="window_attention",
    )(sink, q, k, v, kc, vc)


def _route(logits_t, bias_col):
    t = logits_t.shape[1]
    scores = jax.nn.sigmoid(logits_t)
    choice = scores + bias_col
    sub = lax.broadcasted_iota(i32, (GROUP_SIZE, t), 0)
    grp_score = []
    for g in range(N_GROUPS):
        cg = choice[g * GROUP_SIZE:(g + 1) * GROUP_SIZE]
        m1 = cg.max(axis=0, keepdims=True)
        first = jnp.min(jnp.where(cg == m1, sub, GROUP_SIZE), axis=0, keepdims=True)
        m2 = jnp.where(sub == first, -jnp.inf, cg).max(axis=0, keepdims=True)
        grp_score.append(m1 + m2)
    keep = []
    for g in range(N_GROUPS):
        beaten = jnp.zeros((1, t), i32)
        for o in range(N_GROUPS):
            if o == g:
                continue
            wins = (grp_score[o] > grp_score[g]) | ((grp_score[o] == grp_score[g]) & (o < g))
            beaten = beaten + wins.astype(i32)
        keep.append(jnp.broadcast_to(beaten < TOPK_GROUPS, (GROUP_SIZE, t)))
    masked = jnp.where(jnp.concatenate(keep, axis=0), choice, NEG_INF)
    eid = lax.broadcasted_iota(i32, (N_EXPERTS, t), 0)
    ids, wts = [], []
    for _ in range(TOP_K):
        m = masked.max(axis=0, keepdims=True)
        pick = jnp.min(jnp.where(masked == m, eid, N_EXPERTS), axis=0, keepdims=True)
        sel = eid == pick
        ids.append(pick)
        wts.append(jnp.sum(jnp.where(sel, scores, 0.0), axis=0, keepdims=True))
        masked = jnp.where(sel, -jnp.inf, masked)
    total = wts[0]
    for w in wts[1:]:
        total = total + w
    norm = ROUTED_SCALE / total
    return ids, [w * norm for w in wts]


def _pack_bf16_pairs(x):
    w = x.shape[1] // 2
    lo = lax.bitcast_convert_type(x[:, :w].astype(bf16).astype(f32), i32)
    hi = lax.bitcast_convert_type(x[:, w:].astype(bf16).astype(f32), i32)
    return lax.shift_right_logical(lo, 16) | (hi & jnp.int32(-65536))


def _unpack_bf16_pairs(p):
    lo = lax.bitcast_convert_type(lax.shift_left(p, 16), f32)
    hi = lax.bitcast_convert_type(p & jnp.int32(-65536), f32)
    return lo.astype(bf16), hi.astype(bf16)


def _merge_kernel(x_ref, sh_ref, sc_ref, g1_ref, sh2_ref, sc2_ref, n1_ref, n2_ref, oa_ref, ob_ref, oc_ref, od_ref,
                  wg_ref, wbr_ref, wo_ref, wr_ref, rb_ref, tri_ref, *rest, extends):
    if extends:
        cnt_in_ref = rest[0]
        rest = rest[5:]
    xo_ref, h2_ref, eid_ref, wt_ref, rank_ref, cnt_ref, run_ref, xn_prev_ref = rest
    step = pl.program_id(0)

    @pl.when(step == 0)
    def _():
        run_ref[...] = cnt_in_ref[...].astype(f32) if extends else jnp.zeros_like(run_ref)
        xn_prev_ref[...] = jnp.zeros_like(xn_prev_ref)

    x = x_ref[...]
    hb = _rms_mod(x, n1_ref[...], sc_ref[0], sh_ref[0]).astype(bf16)
    h2 = _rms_mod(xn_prev_ref[...], n2_ref[...], sc2_ref[0], sh2_ref[0])
    h2_ref[...] = _pack_bf16_pairs(h2)
    y = None
    for i, o_ref in enumerate((oa_ref, ob_ref, oc_ref, od_ref)):
        if i == ROUTER_AFTER_BRANCHES:
            ids, wts = _route(_router_logits(wr_ref[...], h2), rb_ref[...])
        logit = jnp.dot(hb, wg_ref[:, i * D_MODEL:(i + 1) * D_MODEL], preferred_element_type=f32)
        proj = jnp.dot(o_ref[...], wbr_ref[i], preferred_element_type=f32)
        term = jax.nn.sigmoid(logit.astype(bf16)) * proj.astype(bf16)
        y = term if y is None else y + term
    xn = x + g1_ref[0] * jnp.dot(y, wo_ref[...], preferred_element_type=f32)
    xo_ref[...] = xn
    xn_prev_ref[...] = xn
    _rank_entries(ids, wts, tri_ref, run_ref, (step > 0).astype(f32), eid_ref, wt_ref, rank_ref, cnt_ref)


def _split3(x):
    def head(v):
        return lax.bitcast_convert_type(lax.bitcast_convert_type(v, i32) & jnp.int32(-65536), f32)

    hi = head(x)
    r1 = x - hi
    mid = head(r1)
    lo = r1 - mid
    return hi.astype(bf16), mid.astype(bf16), lo.astype(bf16)


def _router_logits(w3, h2):
    prod = None
    for piece in _split3(h2):
        p = jnp.dot(piece, w3, preferred_element_type=f32)
        prod = p if prod is None else prod + p
    lane = lax.broadcasted_iota(i32, (1, 2 * N_EXPERTS), 1)
    low = prod[:, :2 * N_EXPERTS]
    logits = low + pltpu.roll(low, N_EXPERTS, 1) + prod[:, 2 * N_EXPERTS:]
    return jnp.where(lane < N_EXPERTS, logits, 0.0).T[:N_EXPERTS]


def _rank_entries(ids, wts, tri_ref, run_ref, live, eid_ref, wt_ref, rank_ref, cnt_ref):
    t = ids[0].shape[1]
    eid = lax.broadcasted_iota(i32, (N_EXPERTS, t), 0)
    hits = [eid == pick for pick in ids]
    chosen = hits[0]
    for h in hits[1:]:
        chosen = chosen | h
    chosen = jnp.where(chosen, 1.0, 0.0)
    prefix = jnp.dot(chosen.astype(bf16), tri_ref[...], preferred_element_type=f32)
    offset = run_ref[...] + prefix
    ranks = [jnp.sum(jnp.where(h, offset, 0.0), axis=0, keepdims=True).astype(i32) for h in hits]
    run_ref[...] += live * jnp.sum(chosen, axis=1, keepdims=True)
    cnt_ref[...] = run_ref[...].astype(i32)

    pad_i = [jnp.zeros((1, t), i32)] * (8 - TOP_K)
    eid_ref[...] = jnp.concatenate(ids + pad_i, axis=0)
    rank_ref[...] = jnp.concatenate(ranks + pad_i, axis=0)
    wt_ref[...] = jnp.concatenate(wts + [jnp.zeros((1, t), f32)] * (8 - TOP_K), axis=0)


def merge_and_route(x2, mods, rows_per_mod, branches, tile, lw, m_total, row_offset, prior):
    m, d = x2.shape
    off = row_offset // tile
    n_tiles = m // tile
    cur = lambda i: jnp.minimum(i, n_tiles - 1)
    prev = lambda i: jnp.maximum(i - 1, 0)
    row = lambda w: pl.BlockSpec((tile, w), lambda i: (cur(i), 0))
    row_prev = lambda w: pl.BlockSpec((tile, w), lambda i: (prev(i) + off, 0))
    modspec = pl.BlockSpec((1, 1, d), lambda i: ((cur(i) * tile) // rows_per_mod, 0, 0))
    modspec_prev = pl.BlockSpec((1, 1, d), lambda i: ((prev(i) * tile) // rows_per_mod, 0, 0))
    col = pl.BlockSpec((8, tile), lambda i: (0, prev(i) + off))
    in_specs = [row(d)] + [modspec] * 3 + [modspec_prev] * 2 + [_const_spec((1, d)), _const_spec((1, d))] + [
        row(BRANCH_W)] * 4 + [
        _const_spec((d, N_BRANCHES * d)), _const_spec((N_BRANCHES, BRANCH_W, d)), _const_spec((d, d)),
        _const_spec((d, 4 * N_EXPERTS)), _const_spec((N_EXPERTS, 1)), _const_spec((tile, tile))]
    args = [x2, mods["sh1"], mods["sc1"], mods["g1"], mods["sh2"], mods["sc2"], lw["norm1_g"], lw["norm2_g"],
            *branches, lw["w_gate"], lw["w_br"], lw["w_o"], lw["w_router3"], lw["router_bias"], lw["tri"]]
    aliases = {}
    if prior is not None:
        h2p, eid, wt, rank, counts = prior
        n_in = len(args)
        in_specs += [_const_spec((N_EXPERTS, 1))] + [pl.BlockSpec(memory_space=pl.ANY)] * 4
        args += [counts, h2p, eid, wt, rank]
        aliases = {n_in + 1 + j: 1 + j for j in range(4)}
    return pl.pallas_call(
        functools.partial(_merge_kernel, extends=prior is not None),
        out_shape=[jax.ShapeDtypeStruct((m, d), f32), jax.ShapeDtypeStruct((m_total, d // 2), i32),
                   jax.ShapeDtypeStruct((8, m_total), i32), jax.ShapeDtypeStruct((8, m_total), f32),
                   jax.ShapeDtypeStruct((8, m_total), i32), jax.ShapeDtypeStruct((N_EXPERTS, 1), i32)],
        grid=(n_tiles + 1,),
        in_specs=in_specs,
        out_specs=[row(d), row_prev(d // 2), col, col, col, pl.BlockSpec((N_EXPERTS, 1), lambda i: (0, 0))],
        scratch_shapes=[pltpu.VMEM((N_EXPERTS, 1), f32), pltpu.VMEM((tile, d), f32)],
        input_output_aliases=aliases,
        compiler_params=_params(("arbitrary",)),
        name="merge_and_route",
    )(*args)


def routing_plan(eid, rank, counts, p_max):
    counts = counts.reshape(N_EXPERTS)
    padded = ((counts + EXPERT_TILE - 1) // EXPERT_TILE) * EXPERT_TILE
    ends = jnp.cumsum(padded)
    starts = ends - padded
    onehot = eid[:, :, None] == jnp.arange(N_EXPERTS, dtype=i32)[None, None, :]
    pos = rank + jnp.sum(jnp.where(onehot, starts[None, None, :], 0), axis=-1)
    n_tiles = p_max // EXPERT_TILE
    tile_start = jnp.arange(n_tiles, dtype=i32) * EXPERT_TILE
    tile_valid = tile_start < ends[-1]
    tile_exp = jnp.sum((ends[None, :] <= tile_start[:, None]).astype(i32), axis=1)
    return pos.astype(i32), jnp.minimum(tile_exp, N_EXPERTS - 1), tile_valid.astype(i32)


def _sc_worker_id():
    return lax.axis_index("subcore") * SC_CORES + lax.axis_index("core")


def sc_scatter_rows(table, pos, p_rows):
    m, w = table.shape
    n_chunks = m // SC_IDX_CHUNK
    steps = -(-n_chunks // SC_WORKERS)
    pos3 = pos.reshape(8, n_chunks, SC_IDX_CHUNK).transpose(1, 0, 2)
    mesh = plsc.VectorSubcoreMesh(core_axis_name="core", subcore_axis_name="subcore")

    @functools.partial(
        pl.kernel,
        out_type=jax.ShapeDtypeStruct((p_rows, w), table.dtype),
        mesh=mesh,
        scratch_types=[
            pltpu.VMEM((8, SC_IDX_CHUNK), i32),
            pltpu.VMEM((SC_IDX_CHUNK, w), table.dtype),
            pltpu.SemaphoreType.DMA,
        ],
    )
    def scatter(x_hbm, p_hbm, o_hbm, idx_v, rows_v, sem):
        wid = _sc_worker_id()

        @pl.loop(0, steps)
        def _(si):
            chunk = si * SC_WORKERS + wid

            @pl.when(chunk < n_chunks)
            def _():
                pltpu.sync_copy(p_hbm.at[chunk], idx_v)
                pltpu.sync_copy(x_hbm.at[pl.ds(chunk * SC_IDX_CHUNK, SC_IDX_CHUNK)], rows_v)
                copies = [pltpu.async_copy(rows_v, o_hbm.at[idx_v.at[k]], sem) for k in range(TOP_K)]
                for cp in copies:
                    cp.wait()

    return scatter(table, pos3)


def sc_gather_rows(table, idx):
    n_idx = idx.shape[0]
    w = table.shape[1]
    n_chunks = n_idx // SC_IDX_CHUNK
    steps = -(-n_chunks // SC_WORKERS)
    half = SC_IDX_CHUNK // 2
    mesh = plsc.VectorSubcoreMesh(core_axis_name="core", subcore_axis_name="subcore")

    @functools.partial(
        pl.kernel,
        out_type=jax.ShapeDtypeStruct((n_idx, w), table.dtype),
        mesh=mesh,
        scratch_types=[
            pltpu.VMEM((SC_IDX_CHUNK,), i32),
            pltpu.VMEM((half, w), table.dtype),
            pltpu.VMEM((half, w), table.dtype),
            pltpu.SemaphoreType.DMA,
            pltpu.SemaphoreType.DMA,
            pltpu.SemaphoreType.DMA,
            pltpu.SemaphoreType.DMA,
        ],
    )
    def gather(x_hbm, i_hbm, o_hbm, idx_v, buf0, buf1, g0_sem, g1_sem, w0_sem, w1_sem):
        wid = _sc_worker_id()

        @pl.loop(0, steps)
        def _(si):
            chunk = si * SC_WORKERS + wid

            @pl.when(chunk < n_chunks)
            def _():
                cbase = chunk * SC_IDX_CHUNK
                pltpu.sync_copy(i_hbm.at[pl.ds(cbase, SC_IDX_CHUNK)], idx_v)
                g0 = pltpu.async_copy(x_hbm.at[idx_v.at[pl.ds(0, half)]], buf0, g0_sem)
                g1 = pltpu.async_copy(x_hbm.at[idx_v.at[pl.ds(half, half)]], buf1, g1_sem)
                g0.wait()
                w0 = pltpu.async_copy(buf0, o_hbm.at[pl.ds(cbase, half)], w0_sem)
                g1.wait()
                w1 = pltpu.async_copy(buf1, o_hbm.at[pl.ds(cbase + half, half)], w1_sem)
                w0.wait()
                w1.wait()

    return gather(table, idx)


def _expert_kernel(te_ref, tv_ref, nx_ref, sl_ref, x_ref, wg_hbm, wu_hbm, wd_hbm, o_ref,
                   wg_f, wu_f, wd_f, wg_b, wu_b, wd_b, sems, *, layer):
    i = pl.program_id(0)

    def weight_copies(expert, slot):
        return [pltpu.make_async_copy(hbm.at[layer, expert], buf.at[slot], sems.at[slot, j])
                for j, (hbm, buf) in enumerate(((wg_hbm, wg_f), (wu_hbm, wu_f), (wd_hbm, wd_f)))]

    @pl.when(i == 0)
    def _():
        for cp in weight_copies(te_ref[0], 0):
            cp.start()

    @pl.when((i == 0) | (te_ref[i] != te_ref[jnp.maximum(i - 1, 0)]))
    def _():
        slot = sl_ref[i]
        for cp in weight_copies(te_ref[i], slot):
            cp.wait()
        wg_b[...] = wg_f[slot].astype(bf16)
        wu_b[...] = wu_f[slot].astype(bf16)
        wd_b[...] = wd_f[slot].astype(bf16)

        @pl.when(nx_ref[i] >= 0)
        def _():
            for cp in weight_copies(nx_ref[i], 1 - slot):
                cp.start()

    @pl.when(tv_ref[i] != 0)
    def _():
        lo, hi = _unpack_bf16_pairs(x_ref[...])
        half = lo.shape[1]
        a = (jnp.dot(lo, wg_b[:half], preferred_element_type=f32)
             + jnp.dot(hi, wg_b[half:], preferred_element_type=f32))
        b = (jnp.dot(lo, wu_b[:half], preferred_element_type=f32)
             + jnp.dot(hi, wu_b[half:], preferred_element_type=f32))
        hid = (_silu(a) * b).astype(bf16)
        o_ref[...] = _pack_bf16_pairs(jnp.dot(hid, wd_b[...], preferred_element_type=f32))

    @pl.when(tv_ref[i] == 0)
    def _():
        o_ref[...] = jnp.zeros_like(o_ref)


def grouped_experts(xs, tile_exp, tile_valid, wg, wu, wd, layer):
    p, half = xs.shape
    d = 2 * half
    n_tiles = p // EXPERT_TILE
    first = jnp.concatenate([jnp.ones((1,), bool), tile_exp[1:] != tile_exp[:-1]])
    slot = (jnp.cumsum(first.astype(i32)) - 1) % 2
    nxt_at = jnp.sum((tile_exp[None, :] <= tile_exp[:, None]).astype(i32), axis=1)
    nxt = jnp.where(nxt_at < n_tiles, tile_exp[jnp.minimum(nxt_at, n_tiles - 1)], -1)
    tile = pl.BlockSpec((EXPERT_TILE, half), lambda i, *_: (i, 0))
    hbm = pl.BlockSpec(memory_space=pl.ANY)
    grid_spec = pltpu.PrefetchScalarGridSpec(
        num_scalar_prefetch=4,
        grid=(n_tiles,),
        in_specs=[tile, hbm, hbm, hbm],
        out_specs=tile,
        scratch_shapes=[pltpu.VMEM((2, d, D_EXPERT), f32), pltpu.VMEM((2, d, D_EXPERT), f32),
                        pltpu.VMEM((2, D_EXPERT, d), f32),
                        pltpu.VMEM((d, D_EXPERT), bf16), pltpu.VMEM((d, D_EXPERT), bf16),
                        pltpu.VMEM((D_EXPERT, d), bf16), pltpu.SemaphoreType.DMA((2, 3))],
    )
    return pl.pallas_call(
        functools.partial(_expert_kernel, layer=layer),
        out_shape=jax.ShapeDtypeStruct((p, half), i32),
        grid_spec=grid_spec,
        compiler_params=_params(("arbitrary",)),
        name="grouped_experts",
    )(tile_exp, tile_valid, nxt.astype(i32), slot.astype(i32), xs, wg, wu, wd)


def _combine_kernel(x_ref, yg_ref, wt_ref, g2_ref, sh2_ref, sc2_ref, n2_ref, wsg_ref, wsu_ref, wsd_ref, fg_ref, o_ref,
                    *, final):
    x = x_ref[...]
    hb = _rms_mod(x, n2_ref[...], sc2_ref[0], sh2_ref[0]).astype(bf16)
    a = jnp.dot(hb, wsg_ref[...], preferred_element_type=f32)
    b = jnp.dot(hb, wsu_ref[...], preferred_element_type=f32)
    f = jnp.dot((_silu(a) * b).astype(bf16), wsd_ref[...], preferred_element_type=f32)
    wt = wt_ref[...]
    half = x.shape[1] // 2
    f_lo, f_hi = f[:, :half], f[:, half:]
    for k in range(TOP_K):
        packed = yg_ref[k]
        w = wt[:, k:k + 1]
        f_lo = f_lo + w * lax.bitcast_convert_type(lax.shift_left(packed, 16), f32)
        f_hi = f_hi + w * lax.bitcast_convert_type(packed & jnp.int32(-65536), f32)
    xo = x + g2_ref[0] * jnp.concatenate([f_lo, f_hi], axis=1)
    if final:
        ms = jnp.mean(xo * xo, axis=-1, keepdims=True)
        xo = xo * lax.rsqrt(ms + RMS_EPS) * fg_ref[...]
    o_ref[...] = xo


def combine(x2, yg, wt_rows, row_offset, mods, rows_per_mod, tile, lw, final_g, final):
    m, d = x2.shape
    off = row_offset // tile
    row = lambda w: pl.BlockSpec((tile, w), lambda i: (i, 0))
    modspec = pl.BlockSpec((1, 1, d), lambda i: ((i * tile) // rows_per_mod, 0, 0))
    return pl.pallas_call(
        functools.partial(_combine_kernel, final=final),
        out_shape=jax.ShapeDtypeStruct((m, d), f32),
        grid=(m // tile,),
        in_specs=[row(d), pl.BlockSpec((TOP_K, tile, d // 2), lambda i: (0, i + off, 0)),
                  pl.BlockSpec((tile, 8), lambda i: (i + off, 0)), modspec, modspec, modspec,
                  _const_spec((1, d)), _const_spec((d, D_EXPERT)), _const_spec((d, D_EXPERT)),
                  _const_spec((D_EXPERT, d)), _const_spec((1, d))],
        out_specs=row(d),
        compiler_params=_params(("parallel",)),
        name="combine",
    )(x2, yg, wt_rows, mods["g2"], mods["sh2"], mods["sc2"], lw["norm2_g"], lw["w_sh_gate"], lw["w_sh_up"],
      lw["w_sh_down"], final_g)


def routed_experts(h2p, eid, rank, counts, lw, layer):
    m = h2p.shape[0]
    p_max = m * TOP_K + N_EXPERTS * EXPERT_TILE
    pos, tile_exp, tile_valid = routing_plan(eid, rank, counts, p_max)
    xs = sc_scatter_rows(h2p, pos, p_max)
    ys = grouped_experts(xs, tile_exp, tile_valid, lw["w_exp_gate"], lw["w_exp_up"], lw["w_exp_down"], layer)
    return sc_gather_rows(ys, pos[:TOP_K].reshape(TOP_K * m)).reshape(TOP_K, m, D_MODEL // 2)


def rope_tables(seq):
    rows = seq // GRID_W
    row = jnp.repeat(jnp.arange(rows), GRID_W).astype(f32)
    col = jnp.tile(jnp.arange(GRID_W), rows).astype(f32)
    axis_dim = HEAD_DIM // 2
    inv_freq = 1.0 / (ROPE_THETA ** (jnp.arange(0, axis_dim, 2, dtype=f32) / axis_dim))
    ang_r = row[:, None] * inv_freq
    ang_c = col[:, None] * inv_freq
    ang = jnp.concatenate([ang_r, ang_r, ang_c, ang_c], axis=-1)
    cos, sin = jnp.cos(ang), jnp.sin(ang)
    seg = (jnp.arange(HEAD_DIM) // 16) % 2
    sa = jnp.where(seg == 0, -sin, 0.0)
    sb = jnp.where(seg == 1, sin, 0.0)
    rep = lambda t: jnp.tile(t, (1, 4))
    return rep(cos), rep(sa), rep(sb)


def _router_weight_pieces(w):
    hi, mid, lo = _split3(w)
    return jnp.concatenate([hi, mid, lo, jnp.zeros_like(hi)], axis=1)


def identity_rope_tables(rows):
    return jnp.ones((rows, 256), f32), jnp.zeros((rows, 256), f32), jnp.zeros((rows, 256), f32)


def kernel(x, c, ctx, c_ctx, w_mod, b_mod, norm1_g, norm2_g, w_in, q_norm_g, k_norm_g, sink, gm_norm_g, gm_ws, gm_b, w_br, w_o, w_router, router_bias, w_exp_gate, w_exp_up, w_exp_down, w_sh_gate, w_sh_up, w_sh_down, final_norm_g):
    bsz_all, seq, d = x.shape
    n_ctx = ctx.shape[1]
    depth = w_mod.shape[0]

    cc = jnp.concatenate([c, c_ctx[None, :], jnp.zeros((MOD_ROWS - bsz_all - 1, d), f32)], axis=0)
    mod_all = compute_mod(cc, w_mod, b_mod)

    lat_tables = rope_tables(seq)
    ctx_tables = identity_rope_tables(n_ctx)
    wc_lat, ws_lat = dft_tables(seq)
    wc_ctx, ws_ctx = dft_tables(n_ctx)
    cs64 = channel_dft_table()
    bd = jnp.asarray(np.kron(np.eye(4), np.full((HEAD_DIM, HEAD_DIM), 1.0 / HEAD_DIM)), dtype=bf16)
    final_g = final_norm_g.reshape(1, d)
    tri = jnp.asarray(np.triu(np.ones((MERGE_TILE, MERGE_TILE)), 1), dtype=bf16)

    lws = []
    for l in range(depth):
        lws.append({
            "norm1_g": norm1_g[l].reshape(1, d),
            "norm2_g": norm2_g[l].reshape(1, d),
            "w_z": w_in[l, :, :OFF_GATE].astype(bf16),
            "w_gate": w_in[l, :, OFF_GATE:].astype(bf16),
            "qn": jnp.tile(q_norm_g[l], 4).reshape(1, 256),
            "kn": jnp.tile(k_norm_g[l], 2).reshape(1, 128),
            "bd": bd,
            "gm_norm_g": gm_norm_g[l].reshape(1, 256),
            "gm_ws": gm_ws[l].astype(bf16),
            "gm_bias": jnp.repeat(gm_b[l].T, 64, axis=1),
            "cs64": cs64,
            "w_br": w_br[l].astype(bf16),
            "w_o": w_o[l].astype(bf16),
            "w_router3": _router_weight_pieces(w_router[l]),
            "router_bias": router_bias[l].reshape(N_EXPERTS, 1),
            "tri": tri,
            "w_exp_gate": w_exp_gate,
            "w_exp_up": w_exp_up,
            "w_exp_down": w_exp_down,
            "w_sh_gate": w_sh_gate[l].astype(bf16),
            "w_sh_up": w_sh_up[l].astype(bf16),
            "w_sh_down": w_sh_down[l].astype(bf16),
        })

    bsz = bsz_all // BATCH_CHUNKS
    outs = []
    for ch in range(BATCH_CHUNKS):
        b0 = ch * bsz
        outs.append(_forward_chunk(x[b0:b0 + bsz], ctx[b0:b0 + bsz], mod_all[:, b0:b0 + bsz], mod_all[:, bsz_all],
                                   lws, sink, lat_tables, ctx_tables, (wc_lat, ws_lat), (wc_ctx, ws_ctx), final_g))
    return jnp.concatenate(outs, axis=0)


def _forward_chunk(x, ctx, mod_lat, mod_ctx, lws, sink, lat_tables, ctx_tables, dft_lat, dft_ctx, final_g):
    bsz, seq, d = x.shape
    n_ctx = ctx.shape[1]
    depth = len(lws)
    n_lat = bsz * seq
    n_cx = bsz * n_ctx
    wc_lat, ws_lat = dft_lat
    wc_ctx, ws_ctx = dft_ctx
    xl = x.reshape(n_lat, d)
    xc = ctx.reshape(n_cx, d)
    for l in range(depth):
        ctx_out = l < depth - 1
        lw = lws[l]
        names = ("sh1", "sc1", "g1", "sh2", "sc2", "g2")
        mods_lat = {n: mod_lat[l, :, i * d:(i + 1) * d].reshape(bsz, 1, d) for i, n in enumerate(names)}
        mods_ctx = {n: mod_ctx[l, i * d:(i + 1) * d].reshape(1, 1, d) for i, n in enumerate(names)}
        sink_l = sink[l]

        qg, kg, vg, qw, kw, vw, fxc, fxs, o_d = in_projection(
            xl, mods_lat["sh1"], mods_lat["sc1"], seq, lat_tables, seq, ROW_TILE, lw)
        cqg, ckg, cvg, cqw, ckw, cvw, cfxc, cfxs, co_d = in_projection(
            xc, mods_ctx["sh1"], mods_ctx["sc1"], n_cx, ctx_tables, n_ctx, n_ctx, lw)
        b3 = lambda t, rows: t.reshape(bsz, rows, t.shape[-1])
        ckg3, cvg3, ckw3, cvw3 = b3(ckg, n_ctx), b3(cvg, n_ctx), b3(ckw, n_ctx), b3(cvw, n_ctx)

        o_a = dft_mix(wc_lat, ws_lat, b3(fxc, seq), b3(fxs, seq), ROW_TILE).reshape(n_lat, BRANCH_W)
        o_b = full_attention(b3(qg, seq), [(ckg3, cvg3), (b3(kg, seq), b3(vg, seq))], None, FULL_ATTN_Q_ROWS)
        o_b = o_b.reshape(n_lat, BRANCH_W)
        o_c = window_attention(b3(qw, seq), b3(kw, seq), b3(vw, seq), ckw3, cvw3, sink_l).reshape(n_lat, BRANCH_W)
        m_total = n_lat + (n_cx if ctx_out else 0)
        route = None
        if ctx_out:
            route = [jnp.zeros((m_total, d // 2), i32), jnp.zeros((8, m_total), i32), jnp.zeros((8, m_total), f32),
                     jnp.zeros((8, m_total), i32), jnp.zeros((N_EXPERTS, 1), i32)]
        xl, *route = merge_and_route(xl, mods_lat, seq, (o_a, o_b, o_c, o_d), MERGE_TILE, lw, m_total, 0, route)
        if ctx_out:
            co_a = dft_mix(wc_ctx, ws_ctx, b3(cfxc, n_ctx), b3(cfxs, n_ctx), n_ctx).reshape(n_cx, BRANCH_W)
            co_b = full_attention(b3(cqg, n_ctx), [(ckg3, cvg3)], None, n_ctx).reshape(n_cx, BRANCH_W)
            co_c = full_attention(b3(cqw, n_ctx), [(ckw3, cvw3)], sink_l, n_ctx).reshape(n_cx, BRANCH_W)
            xc, *route = merge_and_route(xc, mods_ctx, n_cx, (co_a, co_b, co_c, co_d), MERGE_TILE, lw, m_total, n_lat,
                                         route)
        h2p, eid, wt, rank, counts = route
        yg = routed_experts(h2p, eid, rank, counts, lw, l)
        wt_rows = wt.T
        xl = combine(xl, yg, wt_rows, 0, mods_lat, seq, ROW_TILE, lw, final_g, not ctx_out)
        if ctx_out:
            xc = combine(xc, yg, wt_rows, n_lat, mods_ctx, n_cx, ROW_TILE, lw, final_g, False)
    return xl.reshape(bsz, seq, d)
```

```python
import functools
import math

import jax
import jax.numpy as jnp
import numpy as np
from jax import lax
from jax.experimental import pallas as pl
from jax.experimental.pallas import tpu as pltpu
from jax.experimental.pallas import tpu_sc as plsc

f32 = jnp.float32
bf16 = jnp.bfloat16
i32 = jnp.int32

D_MODEL = 1024
HEAD_DIM = 64
GRID_W = 64
ROPE_THETA = 10000.0
ATTN_SCALE = HEAD_DIM ** -0.5
RMS_EPS = 1e-6
NEG_INF = -1e30
Q_BLOCK = 128
WINDOW = 128
GM_CHUNK = 128
N_BRANCHES = 4
BRANCH_W = 256
KV_W = 128
OFF_KV = 0
OFF_Q = 512
OFF_FN = 1024
OFF_GM = 1280
OFF_GATE = 1792
N_EXPERTS = 64
TOP_K = 6
N_GROUPS = 8
GROUP_SIZE = N_EXPERTS // N_GROUPS
TOPK_GROUPS = 4
D_EXPERT = 256
ROUTED_SCALE = 2.5

SC_CORES = 2
SC_SUBCORES = 16
SC_WORKERS = SC_CORES * SC_SUBCORES
SC_IDX_CHUNK = 128

ROW_TILE = 512
MERGE_TILE = 512
ROUTER_AFTER_BRANCHES = 2
EXPERT_TILE = 1024
BATCH_CHUNKS = 1
FULL_ATTN_Q_ROWS = 512
FULL_ATTN_ROW_BLOCKS = 2
WINDOW_BLOCKS_PER_STEP = 8
MOD_ROWS = 24
VMEM_LIMIT = 56 * 1024 * 1024


def _params(sem, vmem=VMEM_LIMIT):
    return pltpu.CompilerParams(dimension_semantics=sem, vmem_limit_bytes=vmem)


def _const_spec(shape):
    nd = len(shape)
    return pl.BlockSpec(shape, lambda *_: (0,) * nd, pipeline_mode=pl.Buffered(1))


def _rms_mod(x, g, sc, sh):
    ms = jnp.mean(x * x, axis=-1, keepdims=True)
    return (x * lax.rsqrt(ms + RMS_EPS) * g) * (1.0 + sc) + sh


def _gelu(x):
    return 0.5 * x * (1.0 + jnp.tanh(math.sqrt(2.0 / math.pi) * (x + 0.044715 * (x * x * x))))


def _silu(x):
    return x * jax.nn.sigmoid(x)


def _mod_kernel(a_ref, w_ref, b_ref, o_ref):
    a = _silu(a_ref[...]).astype(bf16)
    o_ref[0] = jnp.dot(a, w_ref[0].astype(bf16), preferred_element_type=f32) + b_ref[0]


def compute_mod(cc, w_mod, b_mod):
    depth, d, n = w_mod.shape
    tn = 1536
    return pl.pallas_call(
        _mod_kernel,
        out_shape=jax.ShapeDtypeStruct((depth, MOD_ROWS, n), f32),
        grid=(depth, n // tn),
        in_specs=[
            pl.BlockSpec((MOD_ROWS, d), lambda l, j: (0, 0)),
            pl.BlockSpec((1, d, tn), lambda l, j: (l, 0, j)),
            pl.BlockSpec((1, 1, tn), lambda l, j: (l, 0, j)),
        ],
        out_specs=pl.BlockSpec((1, MOD_ROWS, tn), lambda l, j: (l, 0, j)),
        compiler_params=_params(("parallel", "parallel")),
        name="mod_proj",
    )(cc, w_mod, b_mod.reshape(depth, 1, n))


def _inproj_kernel(x_ref, sh_ref, sc_ref, g_ref, w_ref, qn_ref, kn_ref, bd_ref, gmg_ref, ws_ref, gb_ref, cs_ref,
                   cos_ref, sa_ref, sb_ref,
                   qg_ref, kg_ref, vg_ref, qw_ref, kw_ref, vw_ref, xc_ref, xs_ref, od_ref):
    tile = x_ref.shape[0]
    hb = _rms_mod(x_ref[...], g_ref[...], sc_ref[0], sh_ref[0]).astype(bf16)

    def proj(a, b):
        return jnp.dot(hb, w_ref[:, a:b], preferred_element_type=f32)

    def square_pieces(t):
        sq = t * t
        hi = sq.astype(bf16)
        return hi, (sq - hi.astype(f32)).astype(bf16)

    def headnorm(t, pieces, gain):
        w = t.shape[1]
        b = bd_ref[:w, :w]
        ms = (jnp.dot(pieces[0], b, preferred_element_type=f32) + jnp.dot(pieces[1], b, preferred_element_type=f32))
        return t * lax.rsqrt(ms + RMS_EPS) * gain

    def rope(t):
        w = t.shape[1]
        return (t * cos_ref[:, :w] + pltpu.roll(t, w - 16, 1) * sa_ref[:, :w]
                + pltpu.roll(t, 16, 1) * sb_ref[:, :w])

    def expand_heads(q):
        lane = lax.broadcasted_iota(i32, (1, KV_W), 1)
        low = lane < HEAD_DIM
        blocks = []
        for kv in range(2):
            pair = q[:, KV_W * kv:KV_W * (kv + 1)]
            swapped = pltpu.roll(pair, HEAD_DIM, 1)
            keep = low if kv == 0 else jnp.logical_not(low)
            g0, g1 = (pair, swapped) if kv == 0 else (swapped, pair)
            blocks.append(jnp.where(keep, g0, 0.0))
            blocks.append(jnp.where(keep, g1, 0.0))
        return jnp.concatenate(blocks, axis=1)

    kv = proj(OFF_KV, OFF_Q)
    qq = proj(OFF_Q, OFF_FN)
    fn = proj(OFF_FN, OFF_GM).astype(bf16)
    uv = proj(OFF_GM, OFF_GATE)

    k_sq = square_pieces(kv[:, 0:128])
    q_sq = square_pieces(qq[:, :256])
    u = _gelu(uv[:, :256])
    v = _gelu(uv[:, 256:])
    vms = jnp.mean(v * v, axis=-1, keepdims=True)
    vn = (v * lax.rsqrt(vms + RMS_EPS) * gmg_ref[...]).astype(bf16)

    kg_ref[...] = rope(headnorm(kv[:, 0:128], k_sq, kn_ref[...])).astype(bf16)
    vg_ref[...] = kv[:, 128:256].astype(bf16)
    kw_ref[...] = rope(kv[:, 256:384]).astype(bf16)
    vw_ref[...] = kv[:, 384:512].astype(bf16)

    qg = rope(headnorm(qq[:, :256], q_sq, qn_ref[...])) * ATTN_SCALE
    qg_ref[...] = expand_heads(qg).astype(bf16)
    qw = rope(qq[:, 256:]) * ATTN_SCALE
    qw_ref[...] = expand_heads(qw).astype(bf16)

    xcs = jnp.dot(fn, cs_ref[...], preferred_element_type=f32)
    xc_ref[...] = xcs[:, :256].astype(bf16)
    xs_ref[...] = xcs[:, 256:].astype(bf16)

    lane_grp = lax.broadcasted_iota(i32, (1, 256), 1) // 64
    for c in range(tile // GM_CHUNK):
        rows = slice(c * GM_CHUNK, (c + 1) * GM_CHUNK)
        vch = vn[rows]
        sv = gb_ref[...]
        for g in range(4):
            r = jnp.dot(ws_ref[g], vch, preferred_element_type=f32)
            sv = sv + jnp.where(lane_grp == g, r, 0.0)
        od_ref[rows, :] = (u[rows] * sv).astype(bf16)


def in_projection(x2, sh, sc, rows_per_mod, tables, rows_per_seq, tile, lw):
    m, d = x2.shape
    cos_t, sa_t, sb_t = tables
    seq_blocks = rows_per_seq // tile
    row = lambda w: pl.BlockSpec((tile, w), lambda i: (i, 0))
    modspec = pl.BlockSpec((1, 1, d), lambda i: ((i * tile) // rows_per_mod, 0, 0))
    tabspec = pl.BlockSpec((tile, 256), lambda i: (i % seq_blocks, 0))
    out_w = [512, 128, 128, 512, 128, 128, 256, 256, 256]
    return pl.pallas_call(
        _inproj_kernel,
        out_shape=[jax.ShapeDtypeStruct((m, w), bf16) for w in out_w],
        grid=(m // tile,),
        in_specs=[
            row(d), modspec, modspec, _const_spec((1, d)), _const_spec((d, OFF_GATE)),
            _const_spec((1, 256)), _const_spec((1, 128)), _const_spec((256, 256)), _const_spec((1, 256)),
            _const_spec((4, GM_CHUNK, GM_CHUNK)), _const_spec((GM_CHUNK, 256)), _const_spec((256, 512)),
            tabspec, tabspec, tabspec,
        ],
        out_specs=[row(w) for w in out_w],
        compiler_params=_params(("parallel",)),
        name="in_projection",
    )(x2, sh, sc, lw["norm1_g"], lw["w_z"], lw["qn"], lw["kn"], lw["bd"], lw["gm_norm_g"], lw["gm_ws"], lw["gm_bias"],
      lw["cs64"], cos_t, sa_t, sb_t)


def _dft_kernel(wc_ref, ws_ref, xc_ref, xs_ref, o_ref):
    acc = jnp.dot(wc_ref[...], xc_ref[0], preferred_element_type=f32)
    acc = acc + jnp.dot(ws_ref[...], xs_ref[0], preferred_element_type=f32)
    o_ref[0] = acc.astype(bf16)


def dft_mix(wc, ws, xc, xs, tile):
    nb, length, w = xc.shape
    return pl.pallas_call(
        _dft_kernel,
        out_shape=jax.ShapeDtypeStruct((nb, length, w), bf16),
        grid=(length // tile, nb),
        in_specs=[
            pl.BlockSpec((tile, length), lambda i, b: (i, 0)),
            pl.BlockSpec((tile, length), lambda i, b: (i, 0)),
            pl.BlockSpec((1, length, w), lambda i, b: (b, 0, 0)),
            pl.BlockSpec((1, length, w), lambda i, b: (b, 0, 0)),
        ],
        out_specs=pl.BlockSpec((1, tile, w), lambda i, b: (b, i, 0)),
        compiler_params=_params(("parallel", "parallel")),
        name="dft_mix",
    )(wc, ws, xc, xs)


def dft_tables(length):
    jk = (np.arange(length)[:, None] * np.arange(length)[None, :]) % length
    ang = 2.0 * np.pi * jk / length
    s = 1.0 / math.sqrt(length)
    return jnp.asarray(np.cos(ang) * s, dtype=bf16), jnp.asarray(-np.sin(ang) * s, dtype=bf16)


def channel_dft_table():
    jk = (np.arange(64)[:, None] * np.arange(64)[None, :]) % 64
    ang = 2.0 * np.pi * jk / 64
    eye = np.eye(4)
    c = np.kron(eye, np.cos(ang) / 8.0)
    s = np.kron(eye, np.sin(ang) / 8.0)
    return jnp.asarray(np.concatenate([c, s], axis=1), dtype=bf16)


def _attend_blocks(blocks, sink_ref):
    lane = lax.broadcasted_iota(i32, (1, KV_W), 1)
    low = lane < HEAD_DIM
    units = []
    for q_all, pieces in blocks:
        qb = q_all.shape[0]
        for kv in range(2):
            q = jnp.concatenate([q_all[:, KV_W * (2 * kv):KV_W * (2 * kv + 1)],
                                 q_all[:, KV_W * (2 * kv + 1):KV_W * (2 * kv + 2)]], axis=0)
            sink_col = None
            if sink_ref is not None:
                sink_col = jnp.concatenate([jnp.full((qb, 1), sink_ref[2 * kv], f32),
                                            jnp.full((qb, 1), sink_ref[2 * kv + 1], f32)], axis=0)
            units.append((q, pieces, sink_col, low if kv == 0 else jnp.logical_not(low)))

    scores = []
    for q, pieces, _, _ in units:
        unit_scores = []
        for k, _, mask in pieces:
            s = lax.dot_general(q, k, (((1,), (1,)), ((), ())), preferred_element_type=f32)
            unit_scores.append(s if mask is None else jnp.where(mask, s, NEG_INF))
        scores.append(unit_scores)

    maxes = []
    for (_, _, sink_col, _), unit_scores in zip(units, scores):
        m = unit_scores[0].max(axis=-1, keepdims=True)
        for s in unit_scores[1:]:
            m = jnp.maximum(m, s.max(axis=-1, keepdims=True))
        maxes.append(m if sink_col is None else jnp.maximum(m, sink_col))

    probs = [[jnp.exp((s - m).astype(bf16)) for s in unit_scores] for unit_scores, m in zip(scores, maxes)]

    results = []
    for (_, pieces, sink_col, own), unit_probs, m in zip(units, probs, maxes):
        acc = None
        for p, (_, v, _) in zip(unit_probs, pieces):
            pv = jnp.dot(p, jnp.where(own, v, jnp.ones_like(v)), preferred_element_type=f32)
            acc = pv if acc is None else acc + pv
        denom = pltpu.roll(acc, HEAD_DIM, 1)
        if sink_col is not None:
            denom = denom + jnp.exp(sink_col - m)
        results.append(acc * (1.0 / denom))

    outs = []
    for i, (q_all, _) in enumerate(blocks):
        qb = q_all.shape[0]
        r_kv0, r_kv1 = results[2 * i], results[2 * i + 1]
        lo = jnp.where(low, r_kv0[:qb], pltpu.roll(r_kv0[qb:], HEAD_DIM, 1))
        hi = jnp.where(low, pltpu.roll(r_kv1[:qb], HEAD_DIM, 1), r_kv1[qb:])
        outs.append(jnp.concatenate([lo, hi], axis=1))
    return outs


def _full_attn_kernel(*refs, n_pieces, has_sink, row_blocks):
    pos = 0
    sink_ref = None
    if has_sink:
        sink_ref = refs[0]
        pos = 1
    q_ref = refs[pos]
    kv_refs = refs[pos + 1:pos + 1 + 2 * n_pieces]
    o_ref = refs[pos + 1 + 2 * n_pieces]
    pieces = [(kv_refs[2 * i][0], kv_refs[2 * i + 1][0], None) for i in range(n_pieces)]
    rows = q_ref.shape[1] // row_blocks
    blocks = [(q_ref[0, j * rows:(j + 1) * rows, :], pieces) for j in range(row_blocks)]
    for j, out in enumerate(_attend_blocks(blocks, sink_ref)):
        o_ref[0, j * rows:(j + 1) * rows, :] = out.astype(bf16)


def full_attention(q, pieces, sink, qb):
    nb, lq, _ = q.shape
    in_specs = []
    args = []
    if sink is not None:
        in_specs.append(pl.BlockSpec(memory_space=pltpu.SMEM))
        args.append(sink)
    in_specs.append(pl.BlockSpec((1, qb, 512), lambda b, i: (b, i, 0)))
    args.append(q)
    for k, v in pieces:
        spec = pl.BlockSpec((1, k.shape[1], KV_W), lambda b, i: (b, 0, 0))
        in_specs += [spec, spec]
        args += [k, v]
    return pl.pallas_call(
        functools.partial(_full_attn_kernel, n_pieces=len(pieces), has_sink=sink is not None,
                          row_blocks=FULL_ATTN_ROW_BLOCKS),
        out_shape=jax.ShapeDtypeStruct((nb, lq, 256), bf16),
        grid=(nb, lq // qb),
        in_specs=in_specs,
        out_specs=pl.BlockSpec((1, qb, 256), lambda b, i: (b, i, 0)),
        compiler_params=_params(("parallel", "parallel")),
        name="full_attention",
    )(*args)


def _window_attn_kernel(sink_ref, q_ref, k_ref, v_ref, kc_ref, vc_ref, o_ref, *, seq, blocks):
    span = 3 * Q_BLOCK
    ctx_piece = (kc_ref[0], vc_ref[0], None)
    work = []
    for j in range(blocks):
        n = pl.program_id(1) * blocks + j
        start = pl.multiple_of(jnp.clip((n - 1) * Q_BLOCK, 0, seq - span), Q_BLOCK)
        kwin = k_ref[0, pl.ds(start, span), :]
        vwin = v_ref[0, pl.ds(start, span), :]
        row = lax.broadcasted_iota(i32, (2 * Q_BLOCK, span), 0) % Q_BLOCK + n * Q_BLOCK
        col = lax.broadcasted_iota(i32, (2 * Q_BLOCK, span), 1) + start
        mask = jnp.abs(row - col) <= WINDOW
        work.append((q_ref[0, j * Q_BLOCK:(j + 1) * Q_BLOCK, :], [ctx_piece, (kwin, vwin, mask)]))
    for j, out in enumerate(_attend_blocks(work, sink_ref)):
        o_ref[0, j * Q_BLOCK:(j + 1) * Q_BLOCK, :] = out.astype(bf16)


def window_attention(q, k, v, kc, vc, sink):
    nb, seq, _ = q.shape
    n_ctx = kc.shape[1]
    blocks = WINDOW_BLOCKS_PER_STEP
    full = lambda l: pl.BlockSpec((1, l, KV_W), lambda b, i: (b, 0, 0))
    return pl.pallas_call(
        functools.partial(_window_attn_kernel, seq=seq, blocks=blocks),
        out_shape=jax.ShapeDtypeStruct((nb, seq, 256), bf16),
        grid=(nb, seq // (Q_BLOCK * blocks)),
        in_specs=[pl.BlockSpec(memory_space=pltpu.SMEM),
                  pl.BlockSpec((1, Q_BLOCK * blocks, 512), lambda b, i: (b, i, 0)),
                  full(seq), full(seq), full(n_ctx), full(n_ctx)],
        out_specs=pl.BlockSpec((1, Q_BLOCK * blocks, 256), lambda b, i: (b, i, 0)),
        compiler_params=_params(("parallel", "parallel")),
        name="window_attention",
    )(sink, q, k, v, kc, vc)


def _route(logits_t, bias_col):
    t = logits_t.shape[1]
    scores = jax.nn.sigmoid(logits_t)
    choice = scores + bias_col
    sub = lax.broadcasted_iota(i32, (GROUP_SIZE, t), 0)
    grp_score = []
    for g in range(N_GROUPS):
        cg = choice[g * GROUP_SIZE:(g + 1) * GROUP_SIZE]
        m1 = cg.max(axis=0, keepdims=True)
        first = jnp.min(jnp.where(cg == m1, sub, GROUP_SIZE), axis=0, keepdims=True)
        m2 = jnp.where(sub == first, -jnp.inf, cg).max(axis=0, keepdims=True)
        grp_score.append(m1 + m2)
    keep = []
    for g in range(N_GROUPS):
        beaten = jnp.zeros((1, t), i32)
        for o in range(N_GROUPS):
            if o == g:
                continue
            wins = (grp_score[o] > grp_score[g]) | ((grp_score[o] == grp_score[g]) & (o < g))
            beaten = beaten + wins.astype(i32)
        keep.append(jnp.broadcast_to(beaten < TOPK_GROUPS, (GROUP_SIZE, t)))
    masked = jnp.where(jnp.concatenate(keep, axis=0), choice, NEG_INF)
    eid = lax.broadcasted_iota(i32, (N_EXPERTS, t), 0)
    ids, wts = [], []
    for _ in range(TOP_K):
        m = masked.max(axis=0, keepdims=True)
        pick = jnp.min(jnp.where(masked == m, eid, N_EXPERTS), axis=0, keepdims=True)
        sel = eid == pick
        ids.append(pick)
        wts.append(jnp.sum(jnp.where(sel, scores, 0.0), axis=0, keepdims=True))
        masked = jnp.where(sel, -jnp.inf, masked)
    total = wts[0]
    for w in wts[1:]:
        total = total + w
    norm = ROUTED_SCALE / total
    return ids, [w * norm for w in wts]


def _pack_bf16_pairs(x):
    w = x.shape[1] // 2
    lo = lax.bitcast_convert_type(x[:, :w].astype(bf16).astype(f32), i32)
    hi = lax.bitcast_convert_type(x[:, w:].astype(bf16).astype(f32), i32)
    return lax.shift_right_logical(lo, 16) | (hi & jnp.int32(-65536))


def _unpack_bf16_pairs(p):
    lo = lax.bitcast_convert_type(lax.shift_left(p, 16), f32)
    hi = lax.bitcast_convert_type(p & jnp.int32(-65536), f32)
    return lo.astype(bf16), hi.astype(bf16)


def _merge_kernel(x_ref, sh_ref, sc_ref, g1_ref, sh2_ref, sc2_ref, n1_ref, n2_ref, oa_ref, ob_ref, oc_ref, od_ref,
                  wg_ref, wbr_ref, wo_ref, wr_ref, rb_ref, tri_ref, *rest, extends):
    if extends:
        cnt_in_ref = rest[0]
        rest = rest[5:]
    xo_ref, h2_ref, eid_ref, wt_ref, rank_ref, cnt_ref, run_ref, xn_prev_ref = rest
    step = pl.program_id(0)

    @pl.when(step == 0)
    def _():
        run_ref[...] = cnt_in_ref[...].astype(f32) if extends else jnp.zeros_like(run_ref)
        xn_prev_ref[...] = jnp.zeros_like(xn_prev_ref)

    x = x_ref[...]
    hb = _rms_mod(x, n1_ref[...], sc_ref[0], sh_ref[0]).astype(bf16)
    h2 = _rms_mod(xn_prev_ref[...], n2_ref[...], sc2_ref[0], sh2_ref[0])
    h2_ref[...] = _pack_bf16_pairs(h2)
    y = None
    for i, o_ref in enumerate((oa_ref, ob_ref, oc_ref, od_ref)):
        if i == ROUTER_AFTER_BRANCHES:
            ids, wts = _route(_router_logits(wr_ref[...], h2), rb_ref[...])
        logit = jnp.dot(hb, wg_ref[:, i * D_MODEL:(i + 1) * D_MODEL], preferred_element_type=f32)
        proj = jnp.dot(o_ref[...], wbr_ref[i], preferred_element_type=f32)
        term = jax.nn.sigmoid(logit.astype(bf16)) * proj.astype(bf16)
        y = term if y is None else y + term
    xn = x + g1_ref[0] * jnp.dot(y, wo_ref[...], preferred_element_type=f32)
    xo_ref[...] = xn
    xn_prev_ref[...] = xn
    _rank_entries(ids, wts, tri_ref, run_ref, (step > 0).astype(f32), eid_ref, wt_ref, rank_ref, cnt_ref)


def _split3(x):
    def head(v):
        return lax.bitcast_convert_type(lax.bitcast_convert_type(v, i32) & jnp.int32(-65536), f32)

    hi = head(x)
    r1 = x - hi
    mid = head(r1)
    lo = r1 - mid
    return hi.astype(bf16), mid.astype(bf16), lo.astype(bf16)


def _router_logits(w3, h2):
    prod = None
    for piece in _split3(h2):
        p = jnp.dot(piece, w3, preferred_element_type=f32)
        prod = p if prod is None else prod + p
    lane = lax.broadcasted_iota(i32, (1, 2 * N_EXPERTS), 1)
    low = prod[:, :2 * N_EXPERTS]
    logits = low + pltpu.roll(low, N_EXPERTS, 1) + prod[:, 2 * N_EXPERTS:]
    return jnp.where(lane < N_EXPERTS, logits, 0.0).T[:N_EXPERTS]


def _rank_entries(ids, wts, tri_ref, run_ref, live, eid_ref, wt_ref, rank_ref, cnt_ref):
    t = ids[0].shape[1]
    eid = lax.broadcasted_iota(i32, (N_EXPERTS, t), 0)
    hits = [eid == pick for pick in ids]
    chosen = hits[0]
    for h in hits[1:]:
        chosen = chosen | h
    chosen = jnp.where(chosen, 1.0, 0.0)
    prefix = jnp.dot(chosen.astype(bf16), tri_ref[...], preferred_element_type=f32)
    offset = run_ref[...] + prefix
    ranks = [jnp.sum(jnp.where(h, offset, 0.0), axis=0, keepdims=True).astype(i32) for h in hits]
    run_ref[...] += live * jnp.sum(chosen, axis=1, keepdims=True)
    cnt_ref[...] = run_ref[...].astype(i32)

    pad_i = [jnp.zeros((1, t), i32)] * (8 - TOP_K)
    eid_ref[...] = jnp.concatenate(ids + pad_i, axis=0)
    rank_ref[...] = jnp.concatenate(ranks + pad_i, axis=0)
    wt_ref[...] = jnp.concatenate(wts + [jnp.zeros((1, t), f32)] * (8 - TOP_K), axis=0)


def merge_and_route(x2, mods, rows_per_mod, branches, tile, lw, m_total, row_offset, prior):
    m, d = x2.shape
    off = row_offset // tile
    n_tiles = m // tile
    cur = lambda i: jnp.minimum(i, n_tiles - 1)
    prev = lambda i: jnp.maximum(i - 1, 0)
    row = lambda w: pl.BlockSpec((tile, w), lambda i: (cur(i), 0))
    row_prev = lambda w: pl.BlockSpec((tile, w), lambda i: (prev(i) + off, 0))
    modspec = pl.BlockSpec((1, 1, d), lambda i: ((cur(i) * tile) // rows_per_mod, 0, 0))
    modspec_prev = pl.BlockSpec((1, 1, d), lambda i: ((prev(i) * tile) // rows_per_mod, 0, 0))
    col = pl.BlockSpec((8, tile), lambda i: (0, prev(i) + off))
    in_specs = [row(d)] + [modspec] * 3 + [modspec_prev] * 2 + [_const_spec((1, d)), _const_spec((1, d))] + [
        row(BRANCH_W)] * 4 + [
        _const_spec((d, N_BRANCHES * d)), _const_spec((N_BRANCHES, BRANCH_W, d)), _const_spec((d, d)),
        _const_spec((d, 4 * N_EXPERTS)), _const_spec((N_EXPERTS, 1)), _const_spec((tile, tile))]
    args = [x2, mods["sh1"], mods["sc1"], mods["g1"], mods["sh2"], mods["sc2"], lw["norm1_g"], lw["norm2_g"],
            *branches, lw["w_gate"], lw["w_br"], lw["w_o"], lw["w_router3"], lw["router_bias"], lw["tri"]]
    aliases = {}
    if prior is not None:
        h2p, eid, wt, rank, counts = prior
        n_in = len(args)
        in_specs += [_const_spec((N_EXPERTS, 1))] + [pl.BlockSpec(memory_space=pl.ANY)] * 4
        args += [counts, h2p, eid, wt, rank]
        aliases = {n_in + 1 + j: 1 + j for j in range(4)}
    return pl.pallas_call(
        functools.partial(_merge_kernel, extends=prior is not None),
        out_shape=[jax.ShapeDtypeStruct((m, d), f32), jax.ShapeDtypeStruct((m_total, d // 2), i32),
                   jax.ShapeDtypeStruct((8, m_total), i32), jax.ShapeDtypeStruct((8, m_total), f32),
                   jax.ShapeDtypeStruct((8, m_total), i32), jax.ShapeDtypeStruct((N_EXPERTS, 1), i32)],
        grid=(n_tiles + 1,),
        in_specs=in_specs,
        out_specs=[row(d), row_prev(d // 2), col, col, col, pl.BlockSpec((N_EXPERTS, 1), lambda i: (0, 0))],
        scratch_shapes=[pltpu.VMEM((N_EXPERTS, 1), f32), pltpu.VMEM((tile, d), f32)],
        input_output_aliases=aliases,
        compiler_params=_params(("arbitrary",)),
        name="merge_and_route",
    )(*args)


def routing_plan(eid, rank, counts, p_max):
    counts = counts.reshape(N_EXPERTS)
    padded = ((counts + EXPERT_TILE - 1) // EXPERT_TILE) * EXPERT_TILE
    ends = jnp.cumsum(padded)
    starts = ends - padded
    onehot = eid[:, :, None] == jnp.arange(N_EXPERTS, dtype=i32)[None, None, :]
    pos = rank + jnp.sum(jnp.where(onehot, starts[None, None, :], 0), axis=-1)
    n_tiles = p_max // EXPERT_TILE
    tile_start = jnp.arange(n_tiles, dtype=i32) * EXPERT_TILE
    tile_valid = tile_start < ends[-1]
    tile_exp = jnp.sum((ends[None, :] <= tile_start[:, None]).astype(i32), axis=1)
    return pos.astype(i32), jnp.minimum(tile_exp, N_EXPERTS - 1), tile_valid.astype(i32)


def _sc_worker_id():
    return lax.axis_index("subcore") * SC_CORES + lax.axis_index("core")


def sc_scatter_rows(table, pos, p_rows):
    m, w = table.shape
    n_chunks = m // SC_IDX_CHUNK
    steps = -(-n_chunks // SC_WORKERS)
    pos3 = pos.reshape(8, n_chunks, SC_IDX_CHUNK).transpose(1, 0, 2)
    mesh = plsc.VectorSubcoreMesh(core_axis_name="core", subcore_axis_name="subcore")

    @functools.partial(
        pl.kernel,
        out_type=jax.ShapeDtypeStruct((p_rows, w), table.dtype),
        mesh=mesh,
        scratch_types=[
            pltpu.VMEM((8, SC_IDX_CHUNK), i32),
            pltpu.VMEM((SC_IDX_CHUNK, w), table.dtype),
            pltpu.SemaphoreType.DMA,
        ],
    )
    def scatter(x_hbm, p_hbm, o_hbm, idx_v, rows_v, sem):
        wid = _sc_worker_id()

        @pl.loop(0, steps)
        def _(si):
            chunk = si * SC_WORKERS + wid

            @pl.when(chunk < n_chunks)
            def _():
                pltpu.sync_copy(p_hbm.at[chunk], idx_v)
                pltpu.sync_copy(x_hbm.at[pl.ds(chunk * SC_IDX_CHUNK, SC_IDX_CHUNK)], rows_v)
                copies = [pltpu.async_copy(rows_v, o_hbm.at[idx_v.at[k]], sem) for k in range(TOP_K)]
                for cp in copies:
                    cp.wait()

    return scatter(table, pos3)


def sc_gather_rows(table, idx):
    n_idx = idx.shape[0]
    w = table.shape[1]
    n_chunks = n_idx // SC_IDX_CHUNK
    steps = -(-n_chunks // SC_WORKERS)
    half = SC_IDX_CHUNK // 2
    mesh = plsc.VectorSubcoreMesh(core_axis_name="core", subcore_axis_name="subcore")

    @functools.partial(
        pl.kernel,
        out_type=jax.ShapeDtypeStruct((n_idx, w), table.dtype),
        mesh=mesh,
        scratch_types=[
            pltpu.VMEM((SC_IDX_CHUNK,), i32),
            pltpu.VMEM((half, w), table.dtype),
            pltpu.VMEM((half, w), table.dtype),
            pltpu.SemaphoreType.DMA,
            pltpu.SemaphoreType.DMA,
            pltpu.SemaphoreType.DMA,
            pltpu.SemaphoreType.DMA,
        ],
    )
    def gather(x_hbm, i_hbm, o_hbm, idx_v, buf0, buf1, g0_sem, g1_sem, w0_sem, w1_sem):
        wid = _sc_worker_id()

        @pl.loop(0, steps)
        def _(si):
            chunk = si * SC_WORKERS + wid

            @pl.when(chunk < n_chunks)
            def _():
                cbase = chunk * SC_IDX_CHUNK
                pltpu.sync_copy(i_hbm.at[pl.ds(cbase, SC_IDX_CHUNK)], idx_v)
                g0 = pltpu.async_copy(x_hbm.at[idx_v.at[pl.ds(0, half)]], buf0, g0_sem)
                g1 = pltpu.async_copy(x_hbm.at[idx_v.at[pl.ds(half, half)]], buf1, g1_sem)
                g0.wait()
                w0 = pltpu.async_copy(buf0, o_hbm.at[pl.ds(cbase, half)], w0_sem)
                g1.wait()
                w1 = pltpu.async_copy(buf1, o_hbm.at[pl.ds(cbase + half, half)], w1_sem)
                w0.wait()
                w1.wait()

    return gather(table, idx)


def _expert_kernel(te_ref, tv_ref, nx_ref, sl_ref, nu_ref, x_ref, wg_hbm, wu_hbm, wd_hbm, o_ref,
                   wg_f, wu_f, wd_f, wg_b, wu_b, wd_b, sems, *, layer):
    i = pl.program_id(0)

    def weight_copies(expert, slot):
        return [pltpu.make_async_copy(hbm.at[layer, expert], buf.at[slot], sems.at[slot, j])
                for j, (hbm, buf) in enumerate(((wg_hbm, wg_f), (wu_hbm, wu_f), (wd_hbm, wd_f)))]

    @pl.when(i == 0)
    def _():
        for cp in weight_copies(te_ref[0], 0):
            cp.start()

    @pl.when((i == 0) | (te_ref[i] != te_ref[jnp.maximum(i - 1, 0)]))
    def _():
        slot = sl_ref[i]
        for cp in weight_copies(te_ref[i], slot):
            cp.wait()
        wg_b[...] = wg_f[slot].astype(bf16)
        wu_b[...] = wu_f[slot].astype(bf16)
        wd_b[...] = wd_f[slot].astype(bf16)

        @pl.when(nx_ref[i] >= 0)
        def _():
            for cp in weight_copies(nx_ref[i], 1 - slot):
                cp.start()

    @pl.when(tv_ref[i] != 0)
    def _():
        lo, hi = _unpack_bf16_pairs(x_ref[...])
        half = lo.shape[1]
        a = (jnp.dot(lo, wg_b[:half], preferred_element_type=f32)
             + jnp.dot(hi, wg_b[half:], preferred_element_type=f32))
        b = (jnp.dot(lo, wu_b[:half], preferred_element_type=f32)
             + jnp.dot(hi, wu_b[half:], preferred_element_type=f32))
        hid = (_silu(a) * b).astype(bf16)
        o_ref[...] = _pack_bf16_pairs(jnp.dot(hid, wd_b[...], preferred_element_type=f32))


def grouped_experts(xs, tile_exp, tile_valid, wg, wu, wd, layer):
    p, half = xs.shape
    d = 2 * half
    n_tiles = p // EXPERT_TILE
    first = jnp.concatenate([jnp.ones((1,), bool), tile_exp[1:] != tile_exp[:-1]])
    slot = (jnp.cumsum(first.astype(i32)) - 1) % 2
    nxt_at = jnp.sum((tile_exp[None, :] <= tile_exp[:, None]).astype(i32), axis=1)
    nxt = jnp.where(nxt_at < n_tiles, tile_exp[jnp.minimum(nxt_at, n_tiles - 1)], -1)
    n_used = jnp.sum(tile_valid).reshape(1)
    tile = pl.BlockSpec((EXPERT_TILE, half), lambda i, te, tv, nx, sl, nu: (jnp.minimum(i, nu[0] - 1), 0))
    hbm = pl.BlockSpec(memory_space=pl.ANY)
    grid_spec = pltpu.PrefetchScalarGridSpec(
        num_scalar_prefetch=5,
        grid=(n_tiles,),
        in_specs=[tile, hbm, hbm, hbm],
        out_specs=tile,
        scratch_shapes=[pltpu.VMEM((2, d, D_EXPERT), f32), pltpu.VMEM((2, d, D_EXPERT), f32),
                        pltpu.VMEM((2, D_EXPERT, d), f32),
                        pltpu.VMEM((d, D_EXPERT), bf16), pltpu.VMEM((d, D_EXPERT), bf16),
                        pltpu.VMEM((D_EXPERT, d), bf16), pltpu.SemaphoreType.DMA((2, 3))],
    )
    return pl.pallas_call(
        functools.partial(_expert_kernel, layer=layer),
        out_shape=jax.ShapeDtypeStruct((p, half), i32),
        grid_spec=grid_spec,
        compiler_params=_params(("arbitrary",)),
        name="grouped_experts",
    )(tile_exp, tile_valid, nxt.astype(i32), slot.astype(i32), n_used.astype(i32), xs, wg, wu, wd)


def _combine_kernel(x_ref, yg_ref, wt_ref, g2_ref, sh2_ref, sc2_ref, n2_ref, wsg_ref, wsu_ref, wsd_ref, fg_ref, o_ref,
                    *, final):
    x = x_ref[...]
    hb = _rms_mod(x, n2_ref[...], sc2_ref[0], sh2_ref[0]).astype(bf16)
    a = jnp.dot(hb, wsg_ref[...], preferred_element_type=f32)
    b = jnp.dot(hb, wsu_ref[...], preferred_element_type=f32)
    f = jnp.dot((_silu(a) * b).astype(bf16), wsd_ref[...], preferred_element_type=f32)
    wt = wt_ref[...]
    half = x.shape[1] // 2
    f_lo, f_hi = f[:, :half], f[:, half:]
    for k in range(TOP_K):
        packed = yg_ref[k]
        w = wt[:, k:k + 1]
        f_lo = f_lo + w * lax.bitcast_convert_type(lax.shift_left(packed, 16), f32)
        f_hi = f_hi + w * lax.bitcast_convert_type(packed & jnp.int32(-65536), f32)
    xo = x + g2_ref[0] * jnp.concatenate([f_lo, f_hi], axis=1)
    if final:
        ms = jnp.mean(xo * xo, axis=-1, keepdims=True)
        xo = xo * lax.rsqrt(ms + RMS_EPS) * fg_ref[...]
    o_ref[...] = xo


def combine(x2, yg, wt_rows, row_offset, mods, rows_per_mod, tile, lw, final_g, final):
    m, d = x2.shape
    off = row_offset // tile
    row = lambda w: pl.BlockSpec((tile, w), lambda i: (i, 0))
    modspec = pl.BlockSpec((1, 1, d), lambda i: ((i * tile) // rows_per_mod, 0, 0))
    return pl.pallas_call(
        functools.partial(_combine_kernel, final=final),
        out_shape=jax.ShapeDtypeStruct((m, d), f32),
        grid=(m // tile,),
        in_specs=[row(d), pl.BlockSpec((TOP_K, tile, d // 2), lambda i: (0, i + off, 0)),
                  pl.BlockSpec((tile, 8), lambda i: (i + off, 0)), modspec, modspec, modspec,
                  _const_spec((1, d)), _const_spec((d, D_EXPERT)), _const_spec((d, D_EXPERT)),
                  _const_spec((D_EXPERT, d)), _const_spec((1, d))],
        out_specs=row(d),
        compiler_params=_params(("parallel",)),
        name="combine",
    )(x2, yg, wt_rows, mods["g2"], mods["sh2"], mods["sc2"], lw["norm2_g"], lw["w_sh_gate"], lw["w_sh_up"],
      lw["w_sh_down"], final_g)


def routed_experts(h2p, eid, rank, counts, lw, layer):
    m = h2p.shape[0]
    p_max = m * TOP_K + N_EXPERTS * EXPERT_TILE
    pos, tile_exp, tile_valid = routing_plan(eid, rank, counts, p_max)
    xs = sc_scatter_rows(h2p, pos, p_max)
    ys = grouped_experts(xs, tile_exp, tile_valid, lw["w_exp_gate"], lw["w_exp_up"], lw["w_exp_down"], layer)
    return sc_gather_rows(ys, pos[:TOP_K].reshape(TOP_K * m)).reshape(TOP_K, m, D_MODEL // 2)


def rope_tables(seq):
    rows = seq // GRID_W
    row = jnp.repeat(jnp.arange(rows), GRID_W).astype(f32)
    col = jnp.tile(jnp.arange(GRID_W), rows).astype(f32)
    axis_dim = HEAD_DIM // 2
    inv_freq = 1.0 / (ROPE_THETA ** (jnp.arange(0, axis_dim, 2, dtype=f32) / axis_dim))
    ang_r = row[:, None] * inv_freq
    ang_c = col[:, None] * inv_freq
    ang = jnp.concatenate([ang_r, ang_r, ang_c, ang_c], axis=-1)
    cos, sin = jnp.cos(ang), jnp.sin(ang)
    seg = (jnp.arange(HEAD_DIM) // 16) % 2
    sa = jnp.where(seg == 0, -sin, 0.0)
    sb = jnp.where(seg == 1, sin, 0.0)
    rep = lambda t: jnp.tile(t, (1, 4))
    return rep(cos), rep(sa), rep(sb)


def _router_weight_pieces(w):
    hi, mid, lo = _split3(w)
    return jnp.concatenate([hi, mid, lo, jnp.zeros_like(hi)], axis=1)


def identity_rope_tables(rows):
    return jnp.ones((rows, 256), f32), jnp.zeros((rows, 256), f32), jnp.zeros((rows, 256), f32)


def kernel(x, c, ctx, c_ctx, w_mod, b_mod, norm1_g, norm2_g, w_in, q_norm_g, k_norm_g, sink, gm_norm_g, gm_ws, gm_b, w_br, w_o, w_router, router_bias, w_exp_gate, w_exp_up, w_exp_down, w_sh_gate, w_sh_up, w_sh_down, final_norm_g):
    bsz_all, seq, d = x.shape
    n_ctx = ctx.shape[1]
    depth = w_mod.shape[0]

    cc = jnp.concatenate([c, c_ctx[None, :], jnp.zeros((MOD_ROWS - bsz_all - 1, d), f32)], axis=0)
    mod_all = compute_mod(cc, w_mod, b_mod)

    lat_tables = rope_tables(seq)
    ctx_tables = identity_rope_tables(n_ctx)
    wc_lat, ws_lat = dft_tables(seq)
    wc_ctx, ws_ctx = dft_tables(n_ctx)
    cs64 = channel_dft_table()
    bd = jnp.asarray(np.kron(np.eye(4), np.full((HEAD_DIM, HEAD_DIM), 1.0 / HEAD_DIM)), dtype=bf16)
    final_g = final_norm_g.reshape(1, d)
    tri = jnp.asarray(np.triu(np.ones((MERGE_TILE, MERGE_TILE)), 1), dtype=bf16)

    lws = []
    for l in range(depth):
        lws.append({
            "norm1_g": norm1_g[l].reshape(1, d),
            "norm2_g": norm2_g[l].reshape(1, d),
            "w_z": w_in[l, :, :OFF_GATE].astype(bf16),
            "w_gate": w_in[l, :, OFF_GATE:].astype(bf16),
            "qn": jnp.tile(q_norm_g[l], 4).reshape(1, 256),
            "kn": jnp.tile(k_norm_g[l], 2).reshape(1, 128),
            "bd": bd,
            "gm_norm_g": gm_norm_g[l].reshape(1, 256),
            "gm_ws": gm_ws[l].astype(bf16),
            "gm_bias": jnp.repeat(gm_b[l].T, 64, axis=1),
            "cs64": cs64,
            "w_br": w_br[l].astype(bf16),
            "w_o": w_o[l].astype(bf16),
            "w_router3": _router_weight_pieces(w_router[l]),
            "router_bias": router_bias[l].reshape(N_EXPERTS, 1),
            "tri": tri,
            "w_exp_gate": w_exp_gate,
            "w_exp_up": w_exp_up,
            "w_exp_down": w_exp_down,
            "w_sh_gate": w_sh_gate[l].astype(bf16),
            "w_sh_up": w_sh_up[l].astype(bf16),
            "w_sh_down": w_sh_down[l].astype(bf16),
        })

    bsz = bsz_all // BATCH_CHUNKS
    outs = []
    for ch in range(BATCH_CHUNKS):
        b0 = ch * bsz
        outs.append(_forward_chunk(x[b0:b0 + bsz], ctx[b0:b0 + bsz], mod_all[:, b0:b0 + bsz], mod_all[:, bsz_all],
                                   lws, sink, lat_tables, ctx_tables, (wc_lat, ws_lat), (wc_ctx, ws_ctx), final_g))
    return jnp.concatenate(outs, axis=0)


def _forward_chunk(x, ctx, mod_lat, mod_ctx, lws, sink, lat_tables, ctx_tables, dft_lat, dft_ctx, final_g):
    bsz, seq, d = x.shape
    n_ctx = ctx.shape[1]
    depth = len(lws)
    n_lat = bsz * seq
    n_cx = bsz * n_ctx
    wc_lat, ws_lat = dft_lat
    wc_ctx, ws_ctx = dft_ctx
    xl = x.reshape(n_lat, d)
    xc = ctx.reshape(n_cx, d)
    for l in range(depth):
        ctx_out = l < depth - 1
        lw = lws[l]
        names = ("sh1", "sc1", "g1", "sh2", "sc2", "g2")
        mods_lat = {n: mod_lat[l, :, i * d:(i + 1) * d].reshape(bsz, 1, d) for i, n in enumerate(names)}
        mods_ctx = {n: mod_ctx[l, i * d:(i + 1) * d].reshape(1, 1, d) for i, n in enumerate(names)}
        sink_l = sink[l]

        qg, kg, vg, qw, kw, vw, fxc, fxs, o_d = in_projection(
            xl, mods_lat["sh1"], mods_lat["sc1"], seq, lat_tables, seq, ROW_TILE, lw)
        cqg, ckg, cvg, cqw, ckw, cvw, cfxc, cfxs, co_d = in_projection(
            xc, mods_ctx["sh1"], mods_ctx["sc1"], n_cx, ctx_tables, n_ctx, n_ctx, lw)
        b3 = lambda t, rows: t.reshape(bsz, rows, t.shape[-1])
        ckg3, cvg3, ckw3, cvw3 = b3(ckg, n_ctx), b3(cvg, n_ctx), b3(ckw, n_ctx), b3(cvw, n_ctx)

        o_a = dft_mix(wc_lat, ws_lat, b3(fxc, seq), b3(fxs, seq), ROW_TILE).reshape(n_lat, BRANCH_W)
        o_b = full_attention(b3(qg, seq), [(ckg3, cvg3), (b3(kg, seq), b3(vg, seq))], None, FULL_ATTN_Q_ROWS)
        o_b = o_b.reshape(n_lat, BRANCH_W)
        o_c = window_attention(b3(qw, seq), b3(kw, seq), b3(vw, seq), ckw3, cvw3, sink_l).reshape(n_lat, BRANCH_W)
        m_total = n_lat + (n_cx if ctx_out else 0)
        route = None
        if ctx_out:
            route = [jnp.zeros((m_total, d // 2), i32), jnp.zeros((8, m_total), i32), jnp.zeros((8, m_total), f32),
                     jnp.zeros((8, m_total), i32), jnp.zeros((N_EXPERTS, 1), i32)]
        xl, *route = merge_and_route(xl, mods_lat, seq, (o_a, o_b, o_c, o_d), MERGE_TILE, lw, m_total, 0, route)
        if ctx_out:
            co_a = dft_mix(wc_ctx, ws_ctx, b3(cfxc, n_ctx), b3(cfxs, n_ctx), n_ctx).reshape(n_cx, BRANCH_W)
            co_b = full_attention(b3(cqg, n_ctx), [(ckg3, cvg3)], None, n_ctx).reshape(n_cx, BRANCH_W)
            co_c = full_attention(b3(cqw, n_ctx), [(ckw3, cvw3)], sink_l, n_ctx).reshape(n_cx, BRANCH_W)
            xc, *route = merge_and_route(xc, mods_ctx, n_cx, (co_a, co_b, co_c, co_d), MERGE_TILE, lw, m_total, n_lat,
                                         route)
        h2p, eid, wt, rank, counts = route
        yg = routed_experts(h2p, eid, rank, counts, lw, l)
        wt_rows = wt.T
        xl = combine(xl, yg, wt_rows, 0, mods_lat, seq, ROW_TILE, lw, final_g, not ctx_out)
        if ctx_out:
            xc = combine(xc, yg, wt_rows, n_lat, mods_ctx, n_cx, ROW_TILE, lw, final_g, False)
    return xl.reshape(bsz, seq, d)
```

```python
import functools
import math

import jax
import jax.numpy as jnp
import numpy as np
from jax import lax
from jax.experimental import pallas as pl
from jax.experimental.pallas import tpu as pltpu
from jax.experimental.pallas import tpu_sc as plsc

f32 = jnp.float32
bf16 = jnp.bfloat16
i32 = jnp.int32

D_MODEL = 1024
HEAD_DIM = 64
GRID_W = 64
ROPE_THETA = 10000.0
ATTN_SCALE = HEAD_DIM ** -0.5
RMS_EPS = 1e-6
NEG_INF = -1e30
Q_BLOCK = 128
WINDOW = 128
GM_CHUNK = 128
N_BRANCHES = 4
BRANCH_W = 256
KV_W = 128
OFF_KV = 0
OFF_Q = 512
OFF_FN = 1024
OFF_GM = 1280
OFF_GATE = 1792
N_EXPERTS = 64
TOP_K = 6
N_GROUPS = 8
GROUP_SIZE = N_EXPERTS // N_GROUPS
TOPK_GROUPS = 4
D_EXPERT = 256
ROUTED_SCALE = 2.5

SC_CORES = 2
SC_SUBCORES = 16
SC_WORKERS = SC_CORES * SC_SUBCORES
SC_IDX_CHUNK = 128

ROW_TILE = 1024
MERGE_TILE = 512
ROUTER_AFTER_BRANCHES = 2
EXPERT_TILE = 1024
FULL_ATTN_Q_ROWS = 1024
FULL_ATTN_ROW_BLOCKS = 4
WINDOW_BLOCKS_PER_STEP = 8
MOD_ROWS = 24
VMEM_LIMIT = 56 * 1024 * 1024


def _params(sem, vmem=VMEM_LIMIT):
    return pltpu.CompilerParams(dimension_semantics=sem, vmem_limit_bytes=vmem)


def _const_spec(shape):
    nd = len(shape)
    return pl.BlockSpec(shape, lambda *_: (0,) * nd, pipeline_mode=pl.Buffered(1))


def _rms_mod(x, g, sc, sh):
    ms = jnp.mean(x * x, axis=-1, keepdims=True)
    return (x * lax.rsqrt(ms + RMS_EPS) * g) * (1.0 + sc) + sh


def _gelu(x):
    return 0.5 * x * (1.0 + jnp.tanh(math.sqrt(2.0 / math.pi) * (x + 0.044715 * (x * x * x))))


def _silu(x):
    return x * jax.nn.sigmoid(x)


def _mod_kernel(a_ref, w_ref, b_ref, o_ref):
    a = _silu(a_ref[...]).astype(bf16)
    o_ref[0] = jnp.dot(a, w_ref[0].astype(bf16), preferred_element_type=f32) + b_ref[0]


def compute_mod(cc, w_mod, b_mod):
    depth, d, n = w_mod.shape
    tn = 1536
    return pl.pallas_call(
        _mod_kernel,
        out_shape=jax.ShapeDtypeStruct((depth, MOD_ROWS, n), f32),
        grid=(depth, n // tn),
        in_specs=[
            pl.BlockSpec((MOD_ROWS, d), lambda l, j: (0, 0)),
            pl.BlockSpec((1, d, tn), lambda l, j: (l, 0, j)),
            pl.BlockSpec((1, 1, tn), lambda l, j: (l, 0, j)),
        ],
        out_specs=pl.BlockSpec((1, MOD_ROWS, tn), lambda l, j: (l, 0, j)),
        compiler_params=_params(("parallel", "parallel")),
        name="mod_proj",
    )(cc, w_mod, b_mod.reshape(depth, 1, n))


def _inproj_kernel(x_ref, sh_ref, sc_ref, g_ref, w_ref, qn_ref, kn_ref, bd_ref, gmg_ref, ws_ref, gb_ref, cs_ref,
                   cos_ref, sa_ref, sb_ref,
                   qg_ref, kg_ref, vg_ref, qw_ref, kw_ref, vw_ref, xc_ref, xs_ref, od_ref):
    tile = x_ref.shape[0]
    hb = _rms_mod(x_ref[...], g_ref[...], sc_ref[0], sh_ref[0]).astype(bf16)

    def proj(a, b):
        return jnp.dot(hb, w_ref[:, a:b], preferred_element_type=f32)

    def square_pieces(t):
        sq = t * t
        hi = sq.astype(bf16)
        return hi, (sq - hi.astype(f32)).astype(bf16)

    def headnorm(t, pieces, gain):
        w = t.shape[1]
        b = bd_ref[:w, :w]
        ms = (jnp.dot(pieces[0], b, preferred_element_type=f32) + jnp.dot(pieces[1], b, preferred_element_type=f32))
        return t * lax.rsqrt(ms + RMS_EPS) * gain

    def rope(t):
        w = t.shape[1]
        return (t * cos_ref[:, :w] + pltpu.roll(t, w - 16, 1) * sa_ref[:, :w]
                + pltpu.roll(t, 16, 1) * sb_ref[:, :w])

    def expand_heads(q):
        lane = lax.broadcasted_iota(i32, (1, KV_W), 1)
        low = lane < HEAD_DIM
        blocks = []
        for kv in range(2):
            pair = q[:, KV_W * kv:KV_W * (kv + 1)]
            swapped = pltpu.roll(pair, HEAD_DIM, 1)
            keep = low if kv == 0 else jnp.logical_not(low)
            g0, g1 = (pair, swapped) if kv == 0 else (swapped, pair)
            blocks.append(jnp.where(keep, g0, 0.0))
            blocks.append(jnp.where(keep, g1, 0.0))
        return jnp.concatenate(blocks, axis=1)

    kv = proj(OFF_KV, OFF_Q)
    qq = proj(OFF_Q, OFF_FN)
    fn = proj(OFF_FN, OFF_GM).astype(bf16)
    uv = proj(OFF_GM, OFF_GATE)

    k_sq = square_pieces(kv[:, 0:128])
    q_sq = square_pieces(qq[:, :256])
    u = _gelu(uv[:, :256])
    v = _gelu(uv[:, 256:])
    vms = jnp.mean(v * v, axis=-1, keepdims=True)
    vn = (v * lax.rsqrt(vms + RMS_EPS) * gmg_ref[...]).astype(bf16)

    kg_ref[...] = rope(headnorm(kv[:, 0:128], k_sq, kn_ref[...])).astype(bf16)
    vg_ref[...] = kv[:, 128:256].astype(bf16)
    kw_ref[...] = rope(kv[:, 256:384]).astype(bf16)
    vw_ref[...] = kv[:, 384:512].astype(bf16)

    qg = rope(headnorm(qq[:, :256], q_sq, qn_ref[...])) * ATTN_SCALE
    qg_ref[...] = expand_heads(qg).astype(bf16)
    qw = rope(qq[:, 256:]) * ATTN_SCALE
    qw_ref[...] = expand_heads(qw).astype(bf16)

    xcs = jnp.dot(fn, cs_ref[...], preferred_element_type=f32)
    xc_ref[...] = xcs[:, :256].astype(bf16)
    xs_ref[...] = xcs[:, 256:].astype(bf16)

    lane_grp = lax.broadcasted_iota(i32, (1, 256), 1) // 64
    for c in range(tile // GM_CHUNK):
        rows = slice(c * GM_CHUNK, (c + 1) * GM_CHUNK)
        vch = vn[rows]
        sv = gb_ref[...]
        for g in range(4):
            r = jnp.dot(ws_ref[g], vch, preferred_element_type=f32)
            sv = sv + jnp.where(lane_grp == g, r, 0.0)
        od_ref[rows, :] = (u[rows] * sv).astype(bf16)


def in_projection(x2, sh, sc, rows_per_mod, tables, rows_per_seq, tile, lw):
    m, d = x2.shape
    cos_t, sa_t, sb_t = tables
    seq_blocks = rows_per_seq // tile
    row = lambda w: pl.BlockSpec((tile, w), lambda i: (i, 0))
    modspec = pl.BlockSpec((1, 1, d), lambda i: ((i * tile) // rows_per_mod, 0, 0))
    tabspec = pl.BlockSpec((tile, 256), lambda i: (i % seq_blocks, 0))
    out_w = [512, 128, 128, 512, 128, 128, 256, 256, 256]
    return pl.pallas_call(
        _inproj_kernel,
        out_shape=[jax.ShapeDtypeStruct((m, w), bf16) for w in out_w],
        grid=(m // tile,),
        in_specs=[
            row(d), modspec, modspec, _const_spec((1, d)), _const_spec((d, OFF_GATE)),
            _const_spec((1, 256)), _const_spec((1, 128)), _const_spec((256, 256)), _const_spec((1, 256)),
            _const_spec((4, GM_CHUNK, GM_CHUNK)), _const_spec((GM_CHUNK, 256)), _const_spec((256, 512)),
            tabspec, tabspec, tabspec,
        ],
        out_specs=[row(w) for w in out_w],
        compiler_params=_params(("parallel",)),
        name="in_projection",
    )(x2, sh, sc, lw["norm1_g"], lw["w_z"], lw["qn"], lw["kn"], lw["bd"], lw["gm_norm_g"], lw["gm_ws"], lw["gm_bias"],
      lw["cs64"], cos_t, sa_t, sb_t)


def _dft_kernel(wc_ref, ws_ref, xc_ref, xs_ref, o_ref):
    acc = jnp.dot(wc_ref[...], xc_ref[0], preferred_element_type=f32)
    acc = acc + jnp.dot(ws_ref[...], xs_ref[0], preferred_element_type=f32)
    o_ref[0] = acc.astype(bf16)


def dft_mix(wc, ws, xc, xs, tile):
    nb, length, w = xc.shape
    return pl.pallas_call(
        _dft_kernel,
        out_shape=jax.ShapeDtypeStruct((nb, length, w), bf16),
        grid=(length // tile, nb),
        in_specs=[
            pl.BlockSpec((tile, length), lambda i, b: (i, 0)),
            pl.BlockSpec((tile, length), lambda i, b: (i, 0)),
            pl.BlockSpec((1, length, w), lambda i, b: (b, 0, 0)),
            pl.BlockSpec((1, length, w), lambda i, b: (b, 0, 0)),
        ],
        out_specs=pl.BlockSpec((1, tile, w), lambda i, b: (b, i, 0)),
        compiler_params=_params(("parallel", "parallel")),
        name="dft_mix",
    )(wc, ws, xc, xs)


def dft_tables(length):
    jk = (np.arange(length)[:, None] * np.arange(length)[None, :]) % length
    ang = 2.0 * np.pi * jk / length
    s = 1.0 / math.sqrt(length)
    return jnp.asarray(np.cos(ang) * s, dtype=bf16), jnp.asarray(-np.sin(ang) * s, dtype=bf16)


def channel_dft_table():
    jk = (np.arange(64)[:, None] * np.arange(64)[None, :]) % 64
    ang = 2.0 * np.pi * jk / 64
    eye = np.eye(4)
    c = np.kron(eye, np.cos(ang) / 8.0)
    s = np.kron(eye, np.sin(ang) / 8.0)
    return jnp.asarray(np.concatenate([c, s], axis=1), dtype=bf16)


def _attend_blocks(blocks, sink_ref):
    lane = lax.broadcasted_iota(i32, (1, KV_W), 1)
    low = lane < HEAD_DIM
    units = []
    for q_all, pieces in blocks:
        qb = q_all.shape[0]
        for kv in range(2):
            q = jnp.concatenate([q_all[:, KV_W * (2 * kv):KV_W * (2 * kv + 1)],
                                 q_all[:, KV_W * (2 * kv + 1):KV_W * (2 * kv + 2)]], axis=0)
            sink_col = None
            if sink_ref is not None:
                sink_col = jnp.concatenate([jnp.full((qb, 1), sink_ref[2 * kv], f32),
                                            jnp.full((qb, 1), sink_ref[2 * kv + 1], f32)], axis=0)
            units.append((q, pieces, sink_col, low if kv == 0 else jnp.logical_not(low)))

    scores = []
    for q, pieces, _, _ in units:
        unit_scores = []
        for k, _, mask in pieces:
            s = lax.dot_general(q, k, (((1,), (1,)), ((), ())), preferred_element_type=f32)
            unit_scores.append(s if mask is None else jnp.where(mask, s, NEG_INF))
        scores.append(unit_scores)

    maxes = []
    for (_, _, sink_col, _), unit_scores in zip(units, scores):
        m = unit_scores[0].max(axis=-1, keepdims=True)
        for s in unit_scores[1:]:
            m = jnp.maximum(m, s.max(axis=-1, keepdims=True))
        maxes.append(m if sink_col is None else jnp.maximum(m, sink_col))

    probs = [[jnp.exp((s - m).astype(bf16)) for s in unit_scores] for unit_scores, m in zip(scores, maxes)]

    results = []
    for (_, pieces, sink_col, own), unit_probs, m in zip(units, probs, maxes):
        acc = None
        for p, (_, v, _) in zip(unit_probs, pieces):
            pv = jnp.dot(p, jnp.where(own, v, jnp.ones_like(v)), preferred_element_type=f32)
            acc = pv if acc is None else acc + pv
        denom = pltpu.roll(acc, HEAD_DIM, 1)
        if sink_col is not None:
            denom = denom + jnp.exp(sink_col - m)
        results.append(acc * (1.0 / denom))

    outs = []
    for i, (q_all, _) in enumerate(blocks):
        qb = q_all.shape[0]
        r_kv0, r_kv1 = results[2 * i], results[2 * i + 1]
        lo = jnp.where(low, r_kv0[:qb], pltpu.roll(r_kv0[qb:], HEAD_DIM, 1))
        hi = jnp.where(low, pltpu.roll(r_kv1[:qb], HEAD_DIM, 1), r_kv1[qb:])
        outs.append(jnp.concatenate([lo, hi], axis=1))
    return outs


def _full_attn_kernel(*refs, n_pieces, has_sink, row_blocks):
    pos = 0
    sink_ref = None
    if has_sink:
        sink_ref = refs[0]
        pos = 1
    q_ref = refs[pos]
    kv_refs = refs[pos + 1:pos + 1 + 2 * n_pieces]
    o_ref = refs[pos + 1 + 2 * n_pieces]
    pieces = [(kv_refs[2 * i][0], kv_refs[2 * i + 1][0], None) for i in range(n_pieces)]
    rows = q_ref.shape[1] // row_blocks
    blocks = [(q_ref[0, j * rows:(j + 1) * rows, :], pieces) for j in range(row_blocks)]
    for j, out in enumerate(_attend_blocks(blocks, sink_ref)):
        o_ref[0, j * rows:(j + 1) * rows, :] = out.astype(bf16)


def full_attention(q, pieces, sink, qb):
    nb, lq, _ = q.shape
    in_specs = []
    args = []
    if sink is not None:
        in_specs.append(pl.BlockSpec(memory_space=pltpu.SMEM))
        args.append(sink)
    in_specs.append(pl.BlockSpec((1, qb, 512), lambda b, i: (b, i, 0)))
    args.append(q)
    for k, v in pieces:
        spec = pl.BlockSpec((1, k.shape[1], KV_W), lambda b, i: (b, 0, 0))
        in_specs += [spec, spec]
        args += [k, v]
    return pl.pallas_call(
        functools.partial(_full_attn_kernel, n_pieces=len(pieces), has_sink=sink is not None,
                          row_blocks=FULL_ATTN_ROW_BLOCKS),
        out_shape=jax.ShapeDtypeStruct((nb, lq, 256), bf16),
        grid=(nb, lq // qb),
        in_specs=in_specs,
        out_specs=pl.BlockSpec((1, qb, 256), lambda b, i: (b, i, 0)),
        compiler_params=_params(("parallel", "parallel")),
        name="full_attention",
    )(*args)


def _window_attn_kernel(sink_ref, q_ref, k_ref, v_ref, kc_ref, vc_ref, o_ref, *, seq, blocks):
    span = 3 * Q_BLOCK
    ctx_piece = (kc_ref[0], vc_ref[0], None)
    work = []
    for j in range(blocks):
        n = pl.program_id(1) * blocks + j
        start = pl.multiple_of(jnp.clip((n - 1) * Q_BLOCK, 0, seq - span), Q_BLOCK)
        kwin = k_ref[0, pl.ds(start, span), :]
        vwin = v_ref[0, pl.ds(start, span), :]
        row = lax.broadcasted_iota(i32, (2 * Q_BLOCK, span), 0) % Q_BLOCK + n * Q_BLOCK
        col = lax.broadcasted_iota(i32, (2 * Q_BLOCK, span), 1) + start
        mask = jnp.abs(row - col) <= WINDOW
        work.append((q_ref[0, j * Q_BLOCK:(j + 1) * Q_BLOCK, :], [ctx_piece, (kwin, vwin, mask)]))
    for j, out in enumerate(_attend_blocks(work, sink_ref)):
        o_ref[0, j * Q_BLOCK:(j + 1) * Q_BLOCK, :] = out.astype(bf16)


def window_attention(q, k, v, kc, vc, sink):
    nb, seq, _ = q.shape
    n_ctx = kc.shape[1]
    blocks = WINDOW_BLOCKS_PER_STEP
    assert seq % (Q_BLOCK * blocks) == 0 and seq >= 3 * Q_BLOCK, seq
    full = lambda l: pl.BlockSpec((1, l, KV_W), lambda b, i: (b, 0, 0))
    return pl.pallas_call(
        functools.partial(_window_attn_kernel, seq=seq, blocks=blocks),
        out_shape=jax.ShapeDtypeStruct((nb, seq, 256), bf16),
        grid=(nb, seq // (Q_BLOCK * blocks)),
        in_specs=[pl.BlockSpec(memory_space=pltpu.SMEM),
                  pl.BlockSpec((1, Q_BLOCK * blocks, 512), lambda b, i: (b, i, 0)),
                  full(seq), full(seq), full(n_ctx), full(n_ctx)],
        out_specs=pl.BlockSpec((1, Q_BLOCK * blocks, 256), lambda b, i: (b, i, 0)),
        compiler_params=_params(("parallel", "parallel")),
        name="window_attention",
    )(sink, q, k, v, kc, vc)


def _route(logits_t, bias_col):
    t = logits_t.shape[1]
    scores = jax.nn.sigmoid(logits_t)
    choice = scores + bias_col
    sub = lax.broadcasted_iota(i32, (GROUP_SIZE, t), 0)
    grp_score = []
    for g in range(N_GROUPS):
        cg = choice[g * GROUP_SIZE:(g + 1) * GROUP_SIZE]
        m1 = cg.max(axis=0, keepdims=True)
        first = jnp.min(jnp.where(cg == m1, sub, GROUP_SIZE), axis=0, keepdims=True)
        m2 = jnp.where(sub == first, -jnp.inf, cg).max(axis=0, keepdims=True)
        grp_score.append(m1 + m2)
    keep = []
    for g in range(N_GROUPS):
        beaten = jnp.zeros((1, t), i32)
        for o in range(N_GROUPS):
            if o == g:
                continue
            wins = (grp_score[o] > grp_score[g]) | ((grp_score[o] == grp_score[g]) & (o < g))
            beaten = beaten + wins.astype(i32)
        keep.append(jnp.broadcast_to(beaten < TOPK_GROUPS, (GROUP_SIZE, t)))
    masked = jnp.where(jnp.concatenate(keep, axis=0), choice, NEG_INF)
    eid = lax.broadcasted_iota(i32, (N_EXPERTS, t), 0)
    ids, wts = [], []
    for _ in range(TOP_K):
        m = masked.max(axis=0, keepdims=True)
        pick = jnp.min(jnp.where(masked == m, eid, N_EXPERTS), axis=0, keepdims=True)
        sel = eid == pick
        ids.append(pick)
        wts.append(jnp.sum(jnp.where(sel, scores, 0.0), axis=0, keepdims=True))
        masked = jnp.where(sel, -jnp.inf, masked)
    total = wts[0]
    for w in wts[1:]:
        total = total + w
    norm = ROUTED_SCALE / total
    return ids, [w * norm for w in wts]


def _pack_bf16_pairs(x):
    w = x.shape[1] // 2
    lo = lax.bitcast_convert_type(x[:, :w].astype(bf16).astype(f32), i32)
    hi = lax.bitcast_convert_type(x[:, w:].astype(bf16).astype(f32), i32)
    return lax.shift_right_logical(lo, 16) | (hi & jnp.int32(-65536))


def _unpack_bf16_pairs(p):
    lo = lax.bitcast_convert_type(lax.shift_left(p, 16), f32)
    hi = lax.bitcast_convert_type(p & jnp.int32(-65536), f32)
    return lo.astype(bf16), hi.astype(bf16)


def _merge_kernel(x_ref, sh_ref, sc_ref, g1_ref, sh2_ref, sc2_ref, n1_ref, n2_ref, oa_ref, ob_ref, oc_ref, od_ref,
                  wg_ref, wbr_ref, wo_ref, wr_ref, rb_ref, tri_ref, *rest, extends):
    if extends:
        cnt_in_ref = rest[0]
        rest = rest[5:]
    xo_ref, h2_ref, eid_ref, wt_ref, rank_ref, cnt_ref, run_ref, xn_prev_ref = rest
    step = pl.program_id(0)

    @pl.when(step == 0)
    def _():
        run_ref[...] = cnt_in_ref[...].astype(f32) if extends else jnp.zeros_like(run_ref)
        xn_prev_ref[...] = jnp.zeros_like(xn_prev_ref)

    x = x_ref[...]
    hb = _rms_mod(x, n1_ref[...], sc_ref[0], sh_ref[0]).astype(bf16)
    h2 = _rms_mod(xn_prev_ref[...], n2_ref[...], sc2_ref[0], sh2_ref[0])
    h2_ref[...] = _pack_bf16_pairs(h2)
    y = None
    for i, o_ref in enumerate((oa_ref, ob_ref, oc_ref, od_ref)):
        if i == ROUTER_AFTER_BRANCHES:
            ids, wts = _route(_router_logits(wr_ref[...], h2), rb_ref[...])
        logit = jnp.dot(hb, wg_ref[:, i * D_MODEL:(i + 1) * D_MODEL], preferred_element_type=f32)
        proj = jnp.dot(o_ref[...], wbr_ref[i], preferred_element_type=f32)
        term = jax.nn.sigmoid(logit.astype(bf16)) * proj.astype(bf16)
        y = term if y is None else y + term
    xn = x + g1_ref[0] * jnp.dot(y, wo_ref[...], preferred_element_type=f32)
    xo_ref[...] = xn
    xn_prev_ref[...] = xn
    _rank_entries(ids, wts, tri_ref, run_ref, (step > 0).astype(f32), eid_ref, wt_ref, rank_ref, cnt_ref)


def _split3(x):
    def head(v):
        return lax.bitcast_convert_type(lax.bitcast_convert_type(v, i32) & jnp.int32(-65536), f32)

    hi = head(x)
    r1 = x - hi
    mid = head(r1)
    lo = r1 - mid
    return hi.astype(bf16), mid.astype(bf16), lo.astype(bf16)


def _router_logits(w3, h2):
    prod = None
    for piece in _split3(h2):
        p = jnp.dot(piece, w3, preferred_element_type=f32)
        prod = p if prod is None else prod + p
    lane = lax.broadcasted_iota(i32, (1, 2 * N_EXPERTS), 1)
    low = prod[:, :2 * N_EXPERTS]
    logits = low + pltpu.roll(low, N_EXPERTS, 1) + prod[:, 2 * N_EXPERTS:]
    return jnp.where(lane < N_EXPERTS, logits, 0.0).T[:N_EXPERTS]


def _rank_entries(ids, wts, tri_ref, run_ref, live, eid_ref, wt_ref, rank_ref, cnt_ref):
    t = ids[0].shape[1]
    eid = lax.broadcasted_iota(i32, (N_EXPERTS, t), 0)
    hits = [eid == pick for pick in ids]
    chosen = hits[0]
    for h in hits[1:]:
        chosen = chosen | h
    chosen = jnp.where(chosen, 1.0, 0.0)
    prefix = jnp.dot(chosen.astype(bf16), tri_ref[...], preferred_element_type=f32)
    offset = run_ref[...] + prefix
    ranks = [jnp.sum(jnp.where(h, offset, 0.0), axis=0, keepdims=True).astype(i32) for h in hits]
    run_ref[...] += live * jnp.sum(chosen, axis=1, keepdims=True)
    cnt_ref[...] = run_ref[...].astype(i32)

    pad_i = [jnp.zeros((1, t), i32)] * (8 - TOP_K)
    eid_ref[...] = jnp.concatenate(ids + pad_i, axis=0)
    rank_ref[...] = jnp.concatenate(ranks + pad_i, axis=0)
    wt_ref[...] = jnp.concatenate(wts + [jnp.zeros((1, t), f32)] * (8 - TOP_K), axis=0)


def merge_and_route(x2, mods, rows_per_mod, branches, tile, lw, m_total, row_offset, prior):
    m, d = x2.shape
    off = row_offset // tile
    n_tiles = m // tile
    cur = lambda i: jnp.minimum(i, n_tiles - 1)
    prev = lambda i: jnp.maximum(i - 1, 0)
    row = lambda w: pl.BlockSpec((tile, w), lambda i: (cur(i), 0))
    row_prev = lambda w: pl.BlockSpec((tile, w), lambda i: (prev(i) + off, 0))
    modspec = pl.BlockSpec((1, 1, d), lambda i: ((cur(i) * tile) // rows_per_mod, 0, 0))
    modspec_prev = pl.BlockSpec((1, 1, d), lambda i: ((prev(i) * tile) // rows_per_mod, 0, 0))
    col = pl.BlockSpec((8, tile), lambda i: (0, prev(i) + off))
    in_specs = [row(d)] + [modspec] * 3 + [modspec_prev] * 2 + [_const_spec((1, d)), _const_spec((1, d))] + [
        row(BRANCH_W)] * 4 + [
        _const_spec((d, N_BRANCHES * d)), _const_spec((N_BRANCHES, BRANCH_W, d)), _const_spec((d, d)),
        _const_spec((d, 4 * N_EXPERTS)), _const_spec((N_EXPERTS, 1)), _const_spec((tile, tile))]
    args = [x2, mods["sh1"], mods["sc1"], mods["g1"], mods["sh2"], mods["sc2"], lw["norm1_g"], lw["norm2_g"],
            *branches, lw["w_gate"], lw["w_br"], lw["w_o"], lw["w_router3"], lw["router_bias"], lw["tri"]]
    aliases = {}
    if prior is not None:
        h2p, eid, wt, rank, counts = prior
        n_in = len(args)
        in_specs += [_const_spec((N_EXPERTS, 1))] + [pl.BlockSpec(memory_space=pl.ANY)] * 4
        args += [counts, h2p, eid, wt, rank]
        aliases = {n_in + 1 + j: 1 + j for j in range(4)}
    return pl.pallas_call(
        functools.partial(_merge_kernel, extends=prior is not None),
        out_shape=[jax.ShapeDtypeStruct((m, d), f32), jax.ShapeDtypeStruct((m_total, d // 2), i32),
                   jax.ShapeDtypeStruct((8, m_total), i32), jax.ShapeDtypeStruct((8, m_total), f32),
                   jax.ShapeDtypeStruct((8, m_total), i32), jax.ShapeDtypeStruct((N_EXPERTS, 1), i32)],
        grid=(n_tiles + 1,),
        in_specs=in_specs,
        out_specs=[row(d), row_prev(d // 2), col, col, col, pl.BlockSpec((N_EXPERTS, 1), lambda i: (0, 0))],
        scratch_shapes=[pltpu.VMEM((N_EXPERTS, 1), f32), pltpu.VMEM((tile, d), f32)],
        input_output_aliases=aliases,
        compiler_params=_params(("arbitrary",)),
        name="merge_and_route",
    )(*args)


def routing_plan(eid, rank, counts, p_max):
    counts = counts.reshape(N_EXPERTS)
    padded = ((counts + EXPERT_TILE - 1) // EXPERT_TILE) * EXPERT_TILE
    ends = jnp.cumsum(padded)
    starts = ends - padded
    onehot = eid[:, :, None] == jnp.arange(N_EXPERTS, dtype=i32)[None, None, :]
    pos = rank + jnp.sum(jnp.where(onehot, starts[None, None, :], 0), axis=-1)
    n_tiles = p_max // EXPERT_TILE
    tile_start = jnp.arange(n_tiles, dtype=i32) * EXPERT_TILE
    tile_valid = tile_start < ends[-1]
    tile_exp = jnp.sum((ends[None, :] <= tile_start[:, None]).astype(i32), axis=1)
    return pos.astype(i32), jnp.minimum(tile_exp, N_EXPERTS - 1), tile_valid.astype(i32)


def _sc_worker_id():
    return lax.axis_index("subcore") * SC_CORES + lax.axis_index("core")


def sc_scatter_rows(table, pos, p_rows):
    m, w = table.shape
    n_chunks = m // SC_IDX_CHUNK
    steps = -(-n_chunks // SC_WORKERS)
    pos3 = pos.reshape(8, n_chunks, SC_IDX_CHUNK).transpose(1, 0, 2)
    mesh = plsc.VectorSubcoreMesh(core_axis_name="core", subcore_axis_name="subcore")

    @functools.partial(
        pl.kernel,
        out_type=jax.ShapeDtypeStruct((p_rows, w), table.dtype),
        mesh=mesh,
        scratch_types=[
            pltpu.VMEM((8, SC_IDX_CHUNK), i32),
            pltpu.VMEM((SC_IDX_CHUNK, w), table.dtype),
            pltpu.SemaphoreType.DMA,
        ],
    )
    def scatter(x_hbm, p_hbm, o_hbm, idx_v, rows_v, sem):
        wid = _sc_worker_id()

        @pl.loop(0, steps)
        def _(si):
            chunk = si * SC_WORKERS + wid

            @pl.when(chunk < n_chunks)
            def _():
                pltpu.sync_copy(p_hbm.at[chunk], idx_v)
                pltpu.sync_copy(x_hbm.at[pl.ds(chunk * SC_IDX_CHUNK, SC_IDX_CHUNK)], rows_v)
                copies = [pltpu.async_copy(rows_v, o_hbm.at[idx_v.at[k]], sem) for k in range(TOP_K)]
                for cp in copies:
                    cp.wait()

    return scatter(table, pos3)


def sc_gather_rows(table, idx):
    n_idx = idx.shape[0]
    w = table.shape[1]
    n_chunks = n_idx // SC_IDX_CHUNK
    steps = -(-n_chunks // SC_WORKERS)
    half = SC_IDX_CHUNK // 2
    mesh = plsc.VectorSubcoreMesh(core_axis_name="core", subcore_axis_name="subcore")

    @functools.partial(
        pl.kernel,
        out_type=jax.ShapeDtypeStruct((n_idx, w), table.dtype),
        mesh=mesh,
        scratch_types=[
            pltpu.VMEM((SC_IDX_CHUNK,), i32),
            pltpu.VMEM((half, w), table.dtype),
            pltpu.VMEM((half, w), table.dtype),
            pltpu.SemaphoreType.DMA,
            pltpu.SemaphoreType.DMA,
            pltpu.SemaphoreType.DMA,
            pltpu.SemaphoreType.DMA,
        ],
    )
    def gather(x_hbm, i_hbm, o_hbm, idx_v, buf0, buf1, g0_sem, g1_sem, w0_sem, w1_sem):
        wid = _sc_worker_id()

        @pl.loop(0, steps)
        def _(si):
            chunk = si * SC_WORKERS + wid

            @pl.when(chunk < n_chunks)
            def _():
                cbase = chunk * SC_IDX_CHUNK
                pltpu.sync_copy(i_hbm.at[pl.ds(cbase, SC_IDX_CHUNK)], idx_v)
                g0 = pltpu.async_copy(x_hbm.at[idx_v.at[pl.ds(0, half)]], buf0, g0_sem)
                g1 = pltpu.async_copy(x_hbm.at[idx_v.at[pl.ds(half, half)]], buf1, g1_sem)
                g0.wait()
                w0 = pltpu.async_copy(buf0, o_hbm.at[pl.ds(cbase, half)], w0_sem)
                g1.wait()
                w1 = pltpu.async_copy(buf1, o_hbm.at[pl.ds(cbase + half, half)], w1_sem)
                w0.wait()
                w1.wait()

    return gather(table, idx)


def _expert_kernel(te_ref, tv_ref, nx_ref, sl_ref, nu_ref, x_ref, wg_hbm, wu_hbm, wd_hbm, o_ref,
                   wg_f, wu_f, wd_f, wg_b, wu_b, wd_b, sems, *, layer):
    i = pl.program_id(0)

    def weight_copies(expert, slot):
        return [pltpu.make_async_copy(hbm.at[layer, expert], buf.at[slot], sems.at[slot, j])
                for j, (hbm, buf) in enumerate(((wg_hbm, wg_f), (wu_hbm, wu_f), (wd_hbm, wd_f)))]

    @pl.when(i == 0)
    def _():
        for cp in weight_copies(te_ref[0], 0):
            cp.start()

    @pl.when((i == 0) | (te_ref[i] != te_ref[jnp.maximum(i - 1, 0)]))
    def _():
        slot = sl_ref[i]
        for cp in weight_copies(te_ref[i], slot):
            cp.wait()
        wg_b[...] = wg_f[slot].astype(bf16)
        wu_b[...] = wu_f[slot].astype(bf16)
        wd_b[...] = wd_f[slot].astype(bf16)

        @pl.when(nx_ref[i] >= 0)
        def _():
            for cp in weight_copies(nx_ref[i], 1 - slot):
                cp.start()

    @pl.when(tv_ref[i] != 0)
    def _():
        lo, hi = _unpack_bf16_pairs(x_ref[...])
        half = lo.shape[1]
        a = (jnp.dot(lo, wg_b[:half], preferred_element_type=f32)
             + jnp.dot(hi, wg_b[half:], preferred_element_type=f32))
        b = (jnp.dot(lo, wu_b[:half], preferred_element_type=f32)
             + jnp.dot(hi, wu_b[half:], preferred_element_type=f32))
        hid = (_silu(a) * b).astype(bf16)
        o_ref[...] = _pack_bf16_pairs(jnp.dot(hid, wd_b[...], preferred_element_type=f32))


def grouped_experts(xs, tile_exp, tile_valid, wg, wu, wd, layer):
    p, half = xs.shape
    d = 2 * half
    n_tiles = p // EXPERT_TILE
    first = jnp.concatenate([jnp.ones((1,), bool), tile_exp[1:] != tile_exp[:-1]])
    slot = (jnp.cumsum(first.astype(i32)) - 1) % 2
    nxt_at = jnp.sum((tile_exp[None, :] <= tile_exp[:, None]).astype(i32), axis=1)
    nxt = jnp.where(nxt_at < n_tiles, tile_exp[jnp.minimum(nxt_at, n_tiles - 1)], -1)
    n_used = jnp.sum(tile_valid).reshape(1)
    tile = pl.BlockSpec((EXPERT_TILE, half), lambda i, te, tv, nx, sl, nu: (jnp.minimum(i, nu[0] - 1), 0))
    hbm = pl.BlockSpec(memory_space=pl.ANY)
    grid_spec = pltpu.PrefetchScalarGridSpec(
        num_scalar_prefetch=5,
        grid=(n_tiles,),
        in_specs=[tile, hbm, hbm, hbm],
        out_specs=tile,
        scratch_shapes=[pltpu.VMEM((2, d, D_EXPERT), f32), pltpu.VMEM((2, d, D_EXPERT), f32),
                        pltpu.VMEM((2, D_EXPERT, d), f32),
                        pltpu.VMEM((d, D_EXPERT), bf16), pltpu.VMEM((d, D_EXPERT), bf16),
                        pltpu.VMEM((D_EXPERT, d), bf16), pltpu.SemaphoreType.DMA((2, 3))],
    )
    return pl.pallas_call(
        functools.partial(_expert_kernel, layer=layer),
        out_shape=jax.ShapeDtypeStruct((p, half), i32),
        grid_spec=grid_spec,
        compiler_params=_params(("arbitrary",)),
        name="grouped_experts",
    )(tile_exp, tile_valid, nxt.astype(i32), slot.astype(i32), n_used.astype(i32), xs, wg, wu, wd)


def _combine_kernel(x_ref, yg_ref, wt_ref, g2_ref, sh2_ref, sc2_ref, n2_ref, wsg_ref, wsu_ref, wsd_ref, fg_ref, o_ref,
                    *, final):
    x = x_ref[...]
    hb = _rms_mod(x, n2_ref[...], sc2_ref[0], sh2_ref[0]).astype(bf16)
    a = jnp.dot(hb, wsg_ref[...], preferred_element_type=f32)
    b = jnp.dot(hb, wsu_ref[...], preferred_element_type=f32)
    f = jnp.dot((_silu(a) * b).astype(bf16), wsd_ref[...], preferred_element_type=f32)
    wt = wt_ref[...]
    half = x.shape[1] // 2
    f_lo, f_hi = f[:, :half], f[:, half:]
    for k in range(TOP_K):
        packed = yg_ref[k]
        w = wt[:, k:k + 1]
        f_lo = f_lo + w * lax.bitcast_convert_type(lax.shift_left(packed, 16), f32)
        f_hi = f_hi + w * lax.bitcast_convert_type(packed & jnp.int32(-65536), f32)
    xo = x + g2_ref[0] * jnp.concatenate([f_lo, f_hi], axis=1)
    if final:
        ms = jnp.mean(xo * xo, axis=-1, keepdims=True)
        xo = xo * lax.rsqrt(ms + RMS_EPS) * fg_ref[...]
    o_ref[...] = xo


def combine(x2, yg, wt_rows, row_offset, mods, rows_per_mod, tile, lw, final_g, final):
    m, d = x2.shape
    off = row_offset // tile
    row = lambda w: pl.BlockSpec((tile, w), lambda i: (i, 0))
    modspec = pl.BlockSpec((1, 1, d), lambda i: ((i * tile) // rows_per_mod, 0, 0))
    return pl.pallas_call(
        functools.partial(_combine_kernel, final=final),
        out_shape=jax.ShapeDtypeStruct((m, d), f32),
        grid=(m // tile,),
        in_specs=[row(d), pl.BlockSpec((TOP_K, tile, d // 2), lambda i: (0, i + off, 0)),
                  pl.BlockSpec((tile, 8), lambda i: (i + off, 0)), modspec, modspec, modspec,
                  _const_spec((1, d)), _const_spec((d, D_EXPERT)), _const_spec((d, D_EXPERT)),
                  _const_spec((D_EXPERT, d)), _const_spec((1, d))],
        out_specs=row(d),
        compiler_params=_params(("parallel",)),
        name="combine",
    )(x2, yg, wt_rows, mods["g2"], mods["sh2"], mods["sc2"], lw["norm2_g"], lw["w_sh_gate"], lw["w_sh_up"],
      lw["w_sh_down"], final_g)


def routed_experts(h2p, eid, rank, counts, lw, layer):
    m = h2p.shape[0]
    p_max = m * TOP_K + N_EXPERTS * EXPERT_TILE
    pos, tile_exp, tile_valid = routing_plan(eid, rank, counts, p_max)
    xs = sc_scatter_rows(h2p, pos, p_max)
    ys = grouped_experts(xs, tile_exp, tile_valid, lw["w_exp_gate"], lw["w_exp_up"], lw["w_exp_down"], layer)
    return sc_gather_rows(ys, pos[:TOP_K].reshape(TOP_K * m)).reshape(TOP_K, m, D_MODEL // 2)


def rope_tables(seq):
    rows = seq // GRID_W
    row = jnp.repeat(jnp.arange(rows), GRID_W).astype(f32)
    col = jnp.tile(jnp.arange(GRID_W), rows).astype(f32)
    axis_dim = HEAD_DIM // 2
    inv_freq = 1.0 / (ROPE_THETA ** (jnp.arange(0, axis_dim, 2, dtype=f32) / axis_dim))
    ang_r = row[:, None] * inv_freq
    ang_c = col[:, None] * inv_freq
    ang = jnp.concatenate([ang_r, ang_r, ang_c, ang_c], axis=-1)
    cos, sin = jnp.cos(ang), jnp.sin(ang)
    seg = (jnp.arange(HEAD_DIM) // 16) % 2
    sa = jnp.where(seg == 0, -sin, 0.0)
    sb = jnp.where(seg == 1, sin, 0.0)
    rep = lambda t: jnp.tile(t, (1, 4))
    return rep(cos), rep(sa), rep(sb)


def _router_weight_pieces(w):
    hi, mid, lo = _split3(w)
    return jnp.concatenate([hi, mid, lo, jnp.zeros_like(hi)], axis=1)


def identity_rope_tables(rows):
    return jnp.ones((rows, 256), f32), jnp.zeros((rows, 256), f32), jnp.zeros((rows, 256), f32)


def kernel(x, c, ctx, c_ctx, w_mod, b_mod, norm1_g, norm2_g, w_in, q_norm_g, k_norm_g, sink, gm_norm_g, gm_ws, gm_b, w_br, w_o, w_router, router_bias, w_exp_gate, w_exp_up, w_exp_down, w_sh_gate, w_sh_up, w_sh_down, final_norm_g):
    bsz_all, seq, d = x.shape
    n_ctx = ctx.shape[1]
    depth = w_mod.shape[0]

    cc = jnp.concatenate([c, c_ctx[None, :], jnp.zeros((MOD_ROWS - bsz_all - 1, d), f32)], axis=0)
    mod_all = compute_mod(cc, w_mod, b_mod)

    lat_tables = rope_tables(seq)
    ctx_tables = identity_rope_tables(n_ctx)
    wc_lat, ws_lat = dft_tables(seq)
    wc_ctx, ws_ctx = dft_tables(n_ctx)
    cs64 = channel_dft_table()
    bd = jnp.asarray(np.kron(np.eye(4), np.full((HEAD_DIM, HEAD_DIM), 1.0 / HEAD_DIM)), dtype=bf16)
    final_g = final_norm_g.reshape(1, d)
    tri = jnp.asarray(np.triu(np.ones((MERGE_TILE, MERGE_TILE)), 1), dtype=bf16)

    lws = []
    for l in range(depth):
        lws.append({
            "norm1_g": norm1_g[l].reshape(1, d),
            "norm2_g": norm2_g[l].reshape(1, d),
            "w_z": w_in[l, :, :OFF_GATE].astype(bf16),
            "w_gate": w_in[l, :, OFF_GATE:].astype(bf16),
            "qn": jnp.tile(q_norm_g[l], 4).reshape(1, 256),
            "kn": jnp.tile(k_norm_g[l], 2).reshape(1, 128),
            "bd": bd,
            "gm_norm_g": gm_norm_g[l].reshape(1, 256),
            "gm_ws": gm_ws[l].astype(bf16),
            "gm_bias": jnp.repeat(gm_b[l].T, 64, axis=1),
            "cs64": cs64,
            "w_br": w_br[l].astype(bf16),
            "w_o": w_o[l].astype(bf16),
            "w_router3": _router_weight_pieces(w_router[l]),
            "router_bias": router_bias[l].reshape(N_EXPERTS, 1),
            "tri": tri,
            "w_exp_gate": w_exp_gate,
            "w_exp_up": w_exp_up,
            "w_exp_down": w_exp_down,
            "w_sh_gate": w_sh_gate[l].astype(bf16),
            "w_sh_up": w_sh_up[l].astype(bf16),
            "w_sh_down": w_sh_down[l].astype(bf16),
        })

    return _layers(x, ctx, mod_all[:, :bsz_all], mod_all[:, bsz_all], lws, sink, lat_tables, ctx_tables,
                   (wc_lat, ws_lat), (wc_ctx, ws_ctx), final_g)


def _layers(x, ctx, mod_lat, mod_ctx, lws, sink, lat_tables, ctx_tables, dft_lat, dft_ctx, final_g):
    bsz, seq, d = x.shape
    n_ctx = ctx.shape[1]
    depth = len(lws)
    n_lat = bsz * seq
    n_cx = bsz * n_ctx
    wc_lat, ws_lat = dft_lat
    wc_ctx, ws_ctx = dft_ctx
    xl = x.reshape(n_lat, d)
    xc = ctx.reshape(n_cx, d)
    for l in range(depth):
        ctx_out = l < depth - 1
        lw = lws[l]
        names = ("sh1", "sc1", "g1", "sh2", "sc2", "g2")
        mods_lat = {n: mod_lat[l, :, i * d:(i + 1) * d].reshape(bsz, 1, d) for i, n in enumerate(names)}
        mods_ctx = {n: mod_ctx[l, i * d:(i + 1) * d].reshape(1, 1, d) for i, n in enumerate(names)}
        sink_l = sink[l]

        qg, kg, vg, qw, kw, vw, fxc, fxs, o_d = in_projection(
            xl, mods_lat["sh1"], mods_lat["sc1"], seq, lat_tables, seq, ROW_TILE, lw)
        cqg, ckg, cvg, cqw, ckw, cvw, cfxc, cfxs, co_d = in_projection(
            xc, mods_ctx["sh1"], mods_ctx["sc1"], n_cx, ctx_tables, n_ctx, n_ctx, lw)
        b3 = lambda t, rows: t.reshape(bsz, rows, t.shape[-1])
        ckg3, cvg3, ckw3, cvw3 = b3(ckg, n_ctx), b3(cvg, n_ctx), b3(ckw, n_ctx), b3(cvw, n_ctx)

        o_a = dft_mix(wc_lat, ws_lat, b3(fxc, seq), b3(fxs, seq), ROW_TILE).reshape(n_lat, BRANCH_W)
        o_b = full_attention(b3(qg, seq), [(ckg3, cvg3), (b3(kg, seq), b3(vg, seq))], None, FULL_ATTN_Q_ROWS)
        o_b = o_b.reshape(n_lat, BRANCH_W)
        o_c = window_attention(b3(qw, seq), b3(kw, seq), b3(vw, seq), ckw3, cvw3, sink_l).reshape(n_lat, BRANCH_W)
        m_total = n_lat + (n_cx if ctx_out else 0)
        route = None
        if ctx_out:
            route = [jnp.zeros((m_total, d // 2), i32), jnp.zeros((8, m_total), i32), jnp.zeros((8, m_total), f32),
                     jnp.zeros((8, m_total), i32), jnp.zeros((N_EXPERTS, 1), i32)]
        xl, *route = merge_and_route(xl, mods_lat, seq, (o_a, o_b, o_c, o_d), MERGE_TILE, lw, m_total, 0, route)
        if ctx_out:
            co_a = dft_mix(wc_ctx, ws_ctx, b3(cfxc, n_ctx), b3(cfxs, n_ctx), n_ctx).reshape(n_cx, BRANCH_W)
            co_b = full_attention(b3(cqg, n_ctx), [(ckg3, cvg3)], None, n_ctx).reshape(n_cx, BRANCH_W)
            co_c = full_attention(b3(cqw, n_ctx), [(ckw3, cvw3)], sink_l, n_ctx).reshape(n_cx, BRANCH_W)
            xc, *route = merge_and_route(xc, mods_ctx, n_cx, (co_a, co_b, co_c, co_d), MERGE_TILE, lw, m_total, n_lat,
                                         route)
        h2p, eid, wt, rank, counts = route
        yg = routed_experts(h2p, eid, rank, counts, lw, l)
        wt_rows = wt.T
        xl = combine(xl, yg, wt_rows, 0, mods_lat, seq, ROW_TILE, lw, final_g, not ctx_out)
        if ctx_out:
            xc = combine(xc, yg, wt_rows, n_lat, mods_ctx, n_cx, ROW_TILE, lw, final_g, False)
    return xl.reshape(bsz, seq, d)
```

```python
import functools
import math

import jax
import jax.numpy as jnp
import numpy as np
from jax import lax
from jax.experimental import pallas as pl
from jax.experimental.pallas import tpu as pltpu
from jax.experimental.pallas import tpu_sc as plsc

f32 = jnp.float32
bf16 = jnp.bfloat16
i32 = jnp.int32

D_MODEL = 1024
HEAD_DIM = 64
GRID_W = 64
ROPE_THETA = 10000.0
ATTN_SCALE = HEAD_DIM ** -0.5
RMS_EPS = 1e-6
NEG_INF = -1e30
Q_BLOCK = 128
WINDOW = 128
GM_CHUNK = 128
N_BRANCHES = 4
BRANCH_W = 256
KV_W = 128
OFF_KV = 0
OFF_Q = 512
OFF_FN = 1024
OFF_GM = 1280
OFF_GATE = 1792
N_EXPERTS = 64
TOP_K = 6
N_GROUPS = 8
GROUP_SIZE = N_EXPERTS // N_GROUPS
TOPK_GROUPS = 4
D_EXPERT = 256
ROUTED_SCALE = 2.5

SC_CORES = 2
SC_SUBCORES = 16
SC_WORKERS = SC_CORES * SC_SUBCORES
SC_IDX_CHUNK = 128

ROW_TILE = 1024
MERGE_TILE = 512
ROUTER_AFTER_BRANCHES = 2
EXPERT_TILE = 1024
EXPERT_TILE_PARTS = 2
FULL_ATTN_Q_ROWS = 1024
FULL_ATTN_ROW_BLOCKS = 4
WINDOW_BLOCKS_PER_STEP = 16
MOD_ROWS = 24
VMEM_LIMIT = 56 * 1024 * 1024


def _params(sem, vmem=VMEM_LIMIT):
    return pltpu.CompilerParams(dimension_semantics=sem, vmem_limit_bytes=vmem)


def _const_spec(shape):
    nd = len(shape)
    return pl.BlockSpec(shape, lambda *_: (0,) * nd, pipeline_mode=pl.Buffered(1))


def _rms_mod(x, g, sc, sh):
    ms = jnp.mean(x * x, axis=-1, keepdims=True)
    return (x * lax.rsqrt(ms + RMS_EPS) * g) * (1.0 + sc) + sh


def _gelu(x):
    return 0.5 * x * (1.0 + jnp.tanh(math.sqrt(2.0 / math.pi) * (x + 0.044715 * (x * x * x))))


def _silu(x):
    return x * jax.nn.sigmoid(x)


def _mod_kernel(a_ref, w_ref, b_ref, o_ref):
    a = _silu(a_ref[...]).astype(bf16)
    o_ref[0] = jnp.dot(a, w_ref[0].astype(bf16), preferred_element_type=f32) + b_ref[0]


def compute_mod(cc, w_mod, b_mod):
    depth, d, n = w_mod.shape
    tn = 1536
    return pl.pallas_call(
        _mod_kernel,
        out_shape=jax.ShapeDtypeStruct((depth, MOD_ROWS, n), f32),
        grid=(depth, n // tn),
        in_specs=[
            pl.BlockSpec((MOD_ROWS, d), lambda l, j: (0, 0)),
            pl.BlockSpec((1, d, tn), lambda l, j: (l, 0, j)),
            pl.BlockSpec((1, 1, tn), lambda l, j: (l, 0, j)),
        ],
        out_specs=pl.BlockSpec((1, MOD_ROWS, tn), lambda l, j: (l, 0, j)),
        compiler_params=_params(("parallel", "parallel")),
        name="mod_proj",
    )(cc, w_mod, b_mod.reshape(depth, 1, n))


def _inproj_kernel(x_ref, sh_ref, sc_ref, g_ref, w_ref, qn_ref, kn_ref, bd_ref, gmg_ref, ws_ref, gb_ref, cs_ref,
                   cos_ref, sa_ref, sb_ref,
                   qg_ref, kg_ref, vg_ref, qw_ref, kw_ref, vw_ref, xc_ref, xs_ref, od_ref):
    tile = x_ref.shape[0]
    hb = _rms_mod(x_ref[...], g_ref[...], sc_ref[0], sh_ref[0]).astype(bf16)

    def proj(a, b):
        return jnp.dot(hb, w_ref[:, a:b], preferred_element_type=f32)

    def square_pieces(t):
        sq = t * t
        hi = sq.astype(bf16)
        return hi, (sq - hi.astype(f32)).astype(bf16)

    def headnorm(t, pieces, gain):
        w = t.shape[1]
        b = bd_ref[:w, :w]
        ms = (jnp.dot(pieces[0], b, preferred_element_type=f32) + jnp.dot(pieces[1], b, preferred_element_type=f32))
        return t * lax.rsqrt(ms + RMS_EPS) * gain

    def rope(t):
        w = t.shape[1]
        return (t * cos_ref[:, :w] + pltpu.roll(t, w - 16, 1) * sa_ref[:, :w]
                + pltpu.roll(t, 16, 1) * sb_ref[:, :w])

    def expand_heads(q):
        lane = lax.broadcasted_iota(i32, (1, KV_W), 1)
        low = lane < HEAD_DIM
        blocks = []
        for kv in range(2):
            pair = q[:, KV_W * kv:KV_W * (kv + 1)]
            swapped = pltpu.roll(pair, HEAD_DIM, 1)
            keep = low if kv == 0 else jnp.logical_not(low)
            g0, g1 = (pair, swapped) if kv == 0 else (swapped, pair)
            blocks.append(jnp.where(keep, g0, 0.0))
            blocks.append(jnp.where(keep, g1, 0.0))
        return jnp.concatenate(blocks, axis=1)

    kv = proj(OFF_KV, OFF_Q)
    qq = proj(OFF_Q, OFF_FN)
    fn = proj(OFF_FN, OFF_GM).astype(bf16)
    uv = proj(OFF_GM, OFF_GATE)

    k_sq = square_pieces(kv[:, 0:128])
    q_sq = square_pieces(qq[:, :256])
    u = _gelu(uv[:, :256])
    v = _gelu(uv[:, 256:])
    vms = jnp.mean(v * v, axis=-1, keepdims=True)
    vn = (v * lax.rsqrt(vms + RMS_EPS) * gmg_ref[...]).astype(bf16)

    kg_ref[...] = rope(headnorm(kv[:, 0:128], k_sq, kn_ref[...])).astype(bf16)
    vg_ref[...] = kv[:, 128:256].astype(bf16)
    kw_ref[...] = rope(kv[:, 256:384]).astype(bf16)
    vw_ref[...] = kv[:, 384:512].astype(bf16)

    qg = rope(headnorm(qq[:, :256], q_sq, qn_ref[...])) * ATTN_SCALE
    qg_ref[...] = expand_heads(qg).astype(bf16)
    qw = rope(qq[:, 256:]) * ATTN_SCALE
    qw_ref[...] = expand_heads(qw).astype(bf16)

    xcs = jnp.dot(fn, cs_ref[...], preferred_element_type=f32)
    xc_ref[...] = xcs[:, :256].astype(bf16)
    xs_ref[...] = xcs[:, 256:].astype(bf16)

    lane_grp = lax.broadcasted_iota(i32, (1, 256), 1) // 64
    for c in range(tile // GM_CHUNK):
        rows = slice(c * GM_CHUNK, (c + 1) * GM_CHUNK)
        vch = vn[rows]
        sv = gb_ref[...]
        for g in range(4):
            r = jnp.dot(ws_ref[g], vch, preferred_element_type=f32)
            sv = sv + jnp.where(lane_grp == g, r, 0.0)
        od_ref[rows, :] = (u[rows] * sv).astype(bf16)


def in_projection(x2, sh, sc, rows_per_mod, tables, rows_per_seq, tile, lw):
    m, d = x2.shape
    cos_t, sa_t, sb_t = tables
    seq_blocks = rows_per_seq // tile
    row = lambda w: pl.BlockSpec((tile, w), lambda i: (i, 0))
    modspec = pl.BlockSpec((1, 1, d), lambda i: ((i * tile) // rows_per_mod, 0, 0))
    tabspec = pl.BlockSpec((tile, 256), lambda i: (i % seq_blocks, 0))
    out_w = [512, 128, 128, 512, 128, 128, 256, 256, 256]
    return pl.pallas_call(
        _inproj_kernel,
        out_shape=[jax.ShapeDtypeStruct((m, w), bf16) for w in out_w],
        grid=(m // tile,),
        in_specs=[
            row(d), modspec, modspec, _const_spec((1, d)), _const_spec((d, OFF_GATE)),
            _const_spec((1, 256)), _const_spec((1, 128)), _const_spec((256, 256)), _const_spec((1, 256)),
            _const_spec((4, GM_CHUNK, GM_CHUNK)), _const_spec((GM_CHUNK, 256)), _const_spec((256, 512)),
            tabspec, tabspec, tabspec,
        ],
        out_specs=[row(w) for w in out_w],
        compiler_params=_params(("parallel",)),
        name="in_projection",
    )(x2, sh, sc, lw["norm1_g"], lw["w_z"], lw["qn"], lw["kn"], lw["bd"], lw["gm_norm_g"], lw["gm_ws"], lw["gm_bias"],
      lw["cs64"], cos_t, sa_t, sb_t)


def _dft_kernel(wc_ref, ws_ref, xc_ref, xs_ref, o_ref):
    acc = jnp.dot(wc_ref[...], xc_ref[0], preferred_element_type=f32)
    acc = acc + jnp.dot(ws_ref[...], xs_ref[0], preferred_element_type=f32)
    o_ref[0] = acc.astype(bf16)


def dft_mix(wc, ws, xc, xs, tile):
    nb, length, w = xc.shape
    return pl.pallas_call(
        _dft_kernel,
        out_shape=jax.ShapeDtypeStruct((nb, length, w), bf16),
        grid=(length // tile, nb),
        in_specs=[
            pl.BlockSpec((tile, length), lambda i, b: (i, 0)),
            pl.BlockSpec((tile, length), lambda i, b: (i, 0)),
            pl.BlockSpec((1, length, w), lambda i, b: (b, 0, 0)),
            pl.BlockSpec((1, length, w), lambda i, b: (b, 0, 0)),
        ],
        out_specs=pl.BlockSpec((1, tile, w), lambda i, b: (b, i, 0)),
        compiler_params=_params(("parallel", "parallel")),
        name="dft_mix",
    )(wc, ws, xc, xs)


def dft_tables(length):
    jk = (np.arange(length)[:, None] * np.arange(length)[None, :]) % length
    ang = 2.0 * np.pi * jk / length
    s = 1.0 / math.sqrt(length)
    return jnp.asarray(np.cos(ang) * s, dtype=bf16), jnp.asarray(-np.sin(ang) * s, dtype=bf16)


def channel_dft_table():
    jk = (np.arange(64)[:, None] * np.arange(64)[None, :]) % 64
    ang = 2.0 * np.pi * jk / 64
    eye = np.eye(4)
    c = np.kron(eye, np.cos(ang) / 8.0)
    s = np.kron(eye, np.sin(ang) / 8.0)
    return jnp.asarray(np.concatenate([c, s], axis=1), dtype=bf16)


def _attend_blocks(blocks, sink_ref):
    lane = lax.broadcasted_iota(i32, (1, KV_W), 1)
    low = lane < HEAD_DIM
    units = []
    for q_all, pieces in blocks:
        qb = q_all.shape[0]
        for kv in range(2):
            q = jnp.concatenate([q_all[:, KV_W * (2 * kv):KV_W * (2 * kv + 1)],
                                 q_all[:, KV_W * (2 * kv + 1):KV_W * (2 * kv + 2)]], axis=0)
            sink_col = None
            if sink_ref is not None:
                sink_col = jnp.concatenate([jnp.full((qb, 1), sink_ref[2 * kv], f32),
                                            jnp.full((qb, 1), sink_ref[2 * kv + 1], f32)], axis=0)
            units.append((q, pieces, sink_col, low if kv == 0 else jnp.logical_not(low)))

    scores = []
    for q, pieces, _, _ in units:
        unit_scores = []
        for k, _, mask in pieces:
            s = lax.dot_general(q, k, (((1,), (1,)), ((), ())), preferred_element_type=f32)
            unit_scores.append(s if mask is None else jnp.where(mask, s, NEG_INF))
        scores.append(unit_scores)

    maxes = []
    for (_, _, sink_col, _), unit_scores in zip(units, scores):
        m = unit_scores[0].max(axis=-1, keepdims=True)
        for s in unit_scores[1:]:
            m = jnp.maximum(m, s.max(axis=-1, keepdims=True))
        maxes.append(m if sink_col is None else jnp.maximum(m, sink_col))

    probs = [[jnp.exp((s - m).astype(bf16)) for s in unit_scores] for unit_scores, m in zip(scores, maxes)]

    results = []
    for (_, pieces, sink_col, own), unit_probs, m in zip(units, probs, maxes):
        acc = None
        for p, (_, v, _) in zip(unit_probs, pieces):
            pv = jnp.dot(p, jnp.where(own, v, jnp.ones_like(v)), preferred_element_type=f32)
            acc = pv if acc is None else acc + pv
        denom = pltpu.roll(acc, HEAD_DIM, 1)
        if sink_col is not None:
            denom = denom + jnp.exp(sink_col - m)
        results.append(acc * (1.0 / denom))

    outs = []
    for i, (q_all, _) in enumerate(blocks):
        qb = q_all.shape[0]
        r_kv0, r_kv1 = results[2 * i], results[2 * i + 1]
        lo = jnp.where(low, r_kv0[:qb], pltpu.roll(r_kv0[qb:], HEAD_DIM, 1))
        hi = jnp.where(low, pltpu.roll(r_kv1[:qb], HEAD_DIM, 1), r_kv1[qb:])
        outs.append(jnp.concatenate([lo, hi], axis=1))
    return outs


def _full_attn_kernel(*refs, n_pieces, has_sink, row_blocks):
    pos = 0
    sink_ref = None
    if has_sink:
        sink_ref = refs[0]
        pos = 1
    q_ref = refs[pos]
    kv_refs = refs[pos + 1:pos + 1 + 2 * n_pieces]
    o_ref = refs[pos + 1 + 2 * n_pieces]
    pieces = [(kv_refs[2 * i][0], kv_refs[2 * i + 1][0], None) for i in range(n_pieces)]
    rows = q_ref.shape[1] // row_blocks
    blocks = [(q_ref[0, j * rows:(j + 1) * rows, :], pieces) for j in range(row_blocks)]
    for j, out in enumerate(_attend_blocks(blocks, sink_ref)):
        o_ref[0, j * rows:(j + 1) * rows, :] = out.astype(bf16)


def full_attention(q, pieces, sink, qb):
    nb, lq, _ = q.shape
    in_specs = []
    args = []
    if sink is not None:
        in_specs.append(pl.BlockSpec(memory_space=pltpu.SMEM))
        args.append(sink)
    in_specs.append(pl.BlockSpec((1, qb, 512), lambda b, i: (b, i, 0)))
    args.append(q)
    for k, v in pieces:
        spec = pl.BlockSpec((1, k.shape[1], KV_W), lambda b, i: (b, 0, 0))
        in_specs += [spec, spec]
        args += [k, v]
    return pl.pallas_call(
        functools.partial(_full_attn_kernel, n_pieces=len(pieces), has_sink=sink is not None,
                          row_blocks=FULL_ATTN_ROW_BLOCKS),
        out_shape=jax.ShapeDtypeStruct((nb, lq, 256), bf16),
        grid=(nb, lq // qb),
        in_specs=in_specs,
        out_specs=pl.BlockSpec((1, qb, 256), lambda b, i: (b, i, 0)),
        compiler_params=_params(("parallel", "parallel")),
        name="full_attention",
    )(*args)


def _window_attn_kernel(sink_ref, q_ref, k_ref, v_ref, kc_ref, vc_ref, o_ref, *, seq, blocks):
    span = 3 * Q_BLOCK
    ctx_piece = (kc_ref[0], vc_ref[0], None)
    work = []
    for j in range(blocks):
        n = pl.program_id(1) * blocks + j
        start = pl.multiple_of(jnp.clip((n - 1) * Q_BLOCK, 0, seq - span), Q_BLOCK)
        kwin = k_ref[0, pl.ds(start, span), :]
        vwin = v_ref[0, pl.ds(start, span), :]
        row = lax.broadcasted_iota(i32, (2 * Q_BLOCK, span), 0) % Q_BLOCK + n * Q_BLOCK
        col = lax.broadcasted_iota(i32, (2 * Q_BLOCK, span), 1) + start
        mask = jnp.abs(row - col) <= WINDOW
        work.append((q_ref[0, j * Q_BLOCK:(j + 1) * Q_BLOCK, :], [ctx_piece, (kwin, vwin, mask)]))
    for j, out in enumerate(_attend_blocks(work, sink_ref)):
        o_ref[0, j * Q_BLOCK:(j + 1) * Q_BLOCK, :] = out.astype(bf16)


def window_attention(q, k, v, kc, vc, sink):
    nb, seq, _ = q.shape
    n_ctx = kc.shape[1]
    blocks = WINDOW_BLOCKS_PER_STEP
    assert seq % (Q_BLOCK * blocks) == 0 and seq >= 3 * Q_BLOCK, seq
    full = lambda l: pl.BlockSpec((1, l, KV_W), lambda b, i: (b, 0, 0))
    return pl.pallas_call(
        functools.partial(_window_attn_kernel, seq=seq, blocks=blocks),
        out_shape=jax.ShapeDtypeStruct((nb, seq, 256), bf16),
        grid=(nb, seq // (Q_BLOCK * blocks)),
        in_specs=[pl.BlockSpec(memory_space=pltpu.SMEM),
                  pl.BlockSpec((1, Q_BLOCK * blocks, 512), lambda b, i: (b, i, 0)),
                  full(seq), full(seq), full(n_ctx), full(n_ctx)],
        out_specs=pl.BlockSpec((1, Q_BLOCK * blocks, 256), lambda b, i: (b, i, 0)),
        compiler_params=_params(("parallel", "parallel")),
        name="window_attention",
    )(sink, q, k, v, kc, vc)


def _route(logits_t, bias_col):
    t = logits_t.shape[1]
    scores = jax.nn.sigmoid(logits_t)
    choice = scores + bias_col
    sub = lax.broadcasted_iota(i32, (GROUP_SIZE, t), 0)
    grp_score = []
    for g in range(N_GROUPS):
        cg = choice[g * GROUP_SIZE:(g + 1) * GROUP_SIZE]
        m1 = cg.max(axis=0, keepdims=True)
        first = jnp.min(jnp.where(cg == m1, sub, GROUP_SIZE), axis=0, keepdims=True)
        m2 = jnp.where(sub == first, -jnp.inf, cg).max(axis=0, keepdims=True)
        grp_score.append(m1 + m2)
    keep = []
    for g in range(N_GROUPS):
        beaten = jnp.zeros((1, t), i32)
        for o in range(N_GROUPS):
            if o == g:
                continue
            wins = (grp_score[o] > grp_score[g]) | ((grp_score[o] == grp_score[g]) & (o < g))
            beaten = beaten + wins.astype(i32)
        keep.append(jnp.broadcast_to(beaten < TOPK_GROUPS, (GROUP_SIZE, t)))
    masked = jnp.where(jnp.concatenate(keep, axis=0), choice, NEG_INF)
    eid = lax.broadcasted_iota(i32, (N_EXPERTS, t), 0)
    ids, wts = [], []
    for _ in range(TOP_K):
        m = masked.max(axis=0, keepdims=True)
        pick = jnp.min(jnp.where(masked == m, eid, N_EXPERTS), axis=0, keepdims=True)
        sel = eid == pick
        ids.append(pick)
        wts.append(jnp.sum(jnp.where(sel, scores, 0.0), axis=0, keepdims=True))
        masked = jnp.where(sel, -jnp.inf, masked)
    total = wts[0]
    for w in wts[1:]:
        total = total + w
    norm = ROUTED_SCALE / total
    return ids, [w * norm for w in wts]


def _pack_bf16_pairs(x):
    w = x.shape[1] // 2
    lo = lax.bitcast_convert_type(x[:, :w].astype(bf16).astype(f32), i32)
    hi = lax.bitcast_convert_type(x[:, w:].astype(bf16).astype(f32), i32)
    return lax.shift_right_logical(lo, 16) | (hi & jnp.int32(-65536))


def _unpack_bf16_pairs(p):
    lo = lax.bitcast_convert_type(lax.shift_left(p, 16), f32)
    hi = lax.bitcast_convert_type(p & jnp.int32(-65536), f32)
    return lo.astype(bf16), hi.astype(bf16)


def _merge_kernel(x_ref, sh_ref, sc_ref, g1_ref, sh2_ref, sc2_ref, n1_ref, n2_ref, oa_ref, ob_ref, oc_ref, od_ref,
                  wg_ref, wbr_ref, wo_ref, wr_ref, rb_ref, tri_ref, *rest, extends):
    if extends:
        cnt_in_ref = rest[0]
        rest = rest[5:]
    xo_ref, h2_ref, eid_ref, wt_ref, rank_ref, cnt_ref, run_ref, xn_prev_ref = rest
    step = pl.program_id(0)

    @pl.when(step == 0)
    def _():
        run_ref[...] = cnt_in_ref[...].astype(f32) if extends else jnp.zeros_like(run_ref)
        xn_prev_ref[...] = jnp.zeros_like(xn_prev_ref)

    x = x_ref[...]
    hb = _rms_mod(x, n1_ref[...], sc_ref[0], sh_ref[0]).astype(bf16)
    h2 = _rms_mod(xn_prev_ref[...], n2_ref[...], sc2_ref[0], sh2_ref[0])
    h2_ref[...] = _pack_bf16_pairs(h2)
    y = None
    for i, o_ref in enumerate((oa_ref, ob_ref, oc_ref, od_ref)):
        if i == ROUTER_AFTER_BRANCHES:
            ids, wts = _route(_router_logits(wr_ref[...], h2), rb_ref[...])
        logit = jnp.dot(hb, wg_ref[:, i * D_MODEL:(i + 1) * D_MODEL], preferred_element_type=f32)
        proj = jnp.dot(o_ref[...], wbr_ref[i], preferred_element_type=f32)
        term = jax.nn.sigmoid(logit.astype(bf16)) * proj.astype(bf16)
        y = term if y is None else y + term
    xn = x + g1_ref[0] * jnp.dot(y, wo_ref[...], preferred_element_type=f32)
    xo_ref[...] = xn
    xn_prev_ref[...] = xn
    _rank_entries(ids, wts, tri_ref, run_ref, (step > 0).astype(f32), eid_ref, wt_ref, rank_ref, cnt_ref)


def _split3(x):
    def head(v):
        return lax.bitcast_convert_type(lax.bitcast_convert_type(v, i32) & jnp.int32(-65536), f32)

    hi = head(x)
    r1 = x - hi
    mid = head(r1)
    lo = r1 - mid
    return hi.astype(bf16), mid.astype(bf16), lo.astype(bf16)


def _router_logits(w3, h2):
    prod = None
    for piece in _split3(h2):
        p = jnp.dot(piece, w3, preferred_element_type=f32)
        prod = p if prod is None else prod + p
    lane = lax.broadcasted_iota(i32, (1, 2 * N_EXPERTS), 1)
    low = prod[:, :2 * N_EXPERTS]
    logits = low + pltpu.roll(low, N_EXPERTS, 1) + prod[:, 2 * N_EXPERTS:]
    return jnp.where(lane < N_EXPERTS, logits, 0.0).T[:N_EXPERTS]


def _rank_entries(ids, wts, tri_ref, run_ref, live, eid_ref, wt_ref, rank_ref, cnt_ref):
    t = ids[0].shape[1]
    eid = lax.broadcasted_iota(i32, (N_EXPERTS, t), 0)
    hits = [eid == pick for pick in ids]
    chosen = hits[0]
    for h in hits[1:]:
        chosen = chosen | h
    chosen = jnp.where(chosen, 1.0, 0.0)
    prefix = jnp.dot(chosen.astype(bf16), tri_ref[...], preferred_element_type=f32)
    offset = run_ref[...] + prefix
    ranks = [jnp.sum(jnp.where(h, offset, 0.0), axis=0, keepdims=True).astype(i32) for h in hits]
    run_ref[...] += live * jnp.sum(chosen, axis=1, keepdims=True)
    cnt_ref[...] = run_ref[...].astype(i32)

    pad_i = [jnp.zeros((1, t), i32)] * (8 - TOP_K)
    eid_ref[...] = jnp.concatenate(ids + pad_i, axis=0)
    rank_ref[...] = jnp.concatenate(ranks + pad_i, axis=0)
    wt_ref[...] = jnp.concatenate(wts + [jnp.zeros((1, t), f32)] * (8 - TOP_K), axis=0)


def merge_and_route(x2, mods, rows_per_mod, branches, tile, lw, m_total, row_offset, prior):
    m, d = x2.shape
    off = row_offset // tile
    n_tiles = m // tile
    cur = lambda i: jnp.minimum(i, n_tiles - 1)
    prev = lambda i: jnp.maximum(i - 1, 0)
    row = lambda w: pl.BlockSpec((tile, w), lambda i: (cur(i), 0))
    row_prev = lambda w: pl.BlockSpec((tile, w), lambda i: (prev(i) + off, 0))
    modspec = pl.BlockSpec((1, 1, d), lambda i: ((cur(i) * tile) // rows_per_mod, 0, 0))
    modspec_prev = pl.BlockSpec((1, 1, d), lambda i: ((prev(i) * tile) // rows_per_mod, 0, 0))
    col = pl.BlockSpec((8, tile), lambda i: (0, prev(i) + off))
    in_specs = [row(d)] + [modspec] * 3 + [modspec_prev] * 2 + [_const_spec((1, d)), _const_spec((1, d))] + [
        row(BRANCH_W)] * 4 + [
        _const_spec((d, N_BRANCHES * d)), _const_spec((N_BRANCHES, BRANCH_W, d)), _const_spec((d, d)),
        _const_spec((d, 4 * N_EXPERTS)), _const_spec((N_EXPERTS, 1)), _const_spec((tile, tile))]
    args = [x2, mods["sh1"], mods["sc1"], mods["g1"], mods["sh2"], mods["sc2"], lw["norm1_g"], lw["norm2_g"],
            *branches, lw["w_gate"], lw["w_br"], lw["w_o"], lw["w_router3"], lw["router_bias"], lw["tri"]]
    aliases = {}
    if prior is not None:
        h2p, eid, wt, rank, counts = prior
        n_in = len(args)
        in_specs += [_const_spec((N_EXPERTS, 1))] + [pl.BlockSpec(memory_space=pl.ANY)] * 4
        args += [counts, h2p, eid, wt, rank]
        aliases = {n_in + 1 + j: 1 + j for j in range(4)}
    return pl.pallas_call(
        functools.partial(_merge_kernel, extends=prior is not None),
        out_shape=[jax.ShapeDtypeStruct((m, d), f32), jax.ShapeDtypeStruct((m_total, d // 2), i32),
                   jax.ShapeDtypeStruct((8, m_total), i32), jax.ShapeDtypeStruct((8, m_total), f32),
                   jax.ShapeDtypeStruct((8, m_total), i32), jax.ShapeDtypeStruct((N_EXPERTS, 1), i32)],
        grid=(n_tiles + 1,),
        in_specs=in_specs,
        out_specs=[row(d), row_prev(d // 2), col, col, col, pl.BlockSpec((N_EXPERTS, 1), lambda i: (0, 0))],
        scratch_shapes=[pltpu.VMEM((N_EXPERTS, 1), f32), pltpu.VMEM((tile, d), f32)],
        input_output_aliases=aliases,
        compiler_params=_params(("arbitrary",)),
        name="merge_and_route",
    )(*args)


def routing_plan(eid, rank, counts, p_max):
    counts = counts.reshape(N_EXPERTS)
    padded = ((counts + EXPERT_TILE - 1) // EXPERT_TILE) * EXPERT_TILE
    ends = jnp.cumsum(padded)
    starts = ends - padded
    onehot = eid[:, :, None] == jnp.arange(N_EXPERTS, dtype=i32)[None, None, :]
    pos = rank + jnp.sum(jnp.where(onehot, starts[None, None, :], 0), axis=-1)
    n_tiles = p_max // EXPERT_TILE
    tile_start = jnp.arange(n_tiles, dtype=i32) * EXPERT_TILE
    tile_valid = tile_start < ends[-1]
    tile_exp = jnp.sum((ends[None, :] <= tile_start[:, None]).astype(i32), axis=1)
    return pos.astype(i32), jnp.minimum(tile_exp, N_EXPERTS - 1), tile_valid.astype(i32)


def _sc_worker_id():
    return lax.axis_index("subcore") * SC_CORES + lax.axis_index("core")


def sc_scatter_rows(table, pos, p_rows):
    m, w = table.shape
    n_chunks = m // SC_IDX_CHUNK
    steps = -(-n_chunks // SC_WORKERS)
    pos3 = pos.reshape(8, n_chunks, SC_IDX_CHUNK).transpose(1, 0, 2)
    mesh = plsc.VectorSubcoreMesh(core_axis_name="core", subcore_axis_name="subcore")

    @functools.partial(
        pl.kernel,
        out_type=jax.ShapeDtypeStruct((p_rows, w), table.dtype),
        mesh=mesh,
        scratch_types=[
            pltpu.VMEM((8, SC_IDX_CHUNK), i32),
            pltpu.VMEM((SC_IDX_CHUNK, w), table.dtype),
            pltpu.SemaphoreType.DMA,
        ],
    )
    def scatter(x_hbm, p_hbm, o_hbm, idx_v, rows_v, sem):
        wid = _sc_worker_id()

        @pl.loop(0, steps)
        def _(si):
            chunk = si * SC_WORKERS + wid

            @pl.when(chunk < n_chunks)
            def _():
                pltpu.sync_copy(p_hbm.at[chunk], idx_v)
                pltpu.sync_copy(x_hbm.at[pl.ds(chunk * SC_IDX_CHUNK, SC_IDX_CHUNK)], rows_v)
                copies = [pltpu.async_copy(rows_v, o_hbm.at[idx_v.at[k]], sem) for k in range(TOP_K)]
                for cp in copies:
                    cp.wait()

    return scatter(table, pos3)


def sc_gather_rows(table, idx):
    n_idx = idx.shape[0]
    w = table.shape[1]
    n_chunks = n_idx // SC_IDX_CHUNK
    steps = -(-n_chunks // SC_WORKERS)
    half = SC_IDX_CHUNK // 2
    mesh = plsc.VectorSubcoreMesh(core_axis_name="core", subcore_axis_name="subcore")

    @functools.partial(
        pl.kernel,
        out_type=jax.ShapeDtypeStruct((n_idx, w), table.dtype),
        mesh=mesh,
        scratch_types=[
            pltpu.VMEM((SC_IDX_CHUNK,), i32),
            pltpu.VMEM((half, w), table.dtype),
            pltpu.VMEM((half, w), table.dtype),
            pltpu.SemaphoreType.DMA,
            pltpu.SemaphoreType.DMA,
            pltpu.SemaphoreType.DMA,
            pltpu.SemaphoreType.DMA,
        ],
    )
    def gather(x_hbm, i_hbm, o_hbm, idx_v, buf0, buf1, g0_sem, g1_sem, w0_sem, w1_sem):
        wid = _sc_worker_id()

        @pl.loop(0, steps)
        def _(si):
            chunk = si * SC_WORKERS + wid

            @pl.when(chunk < n_chunks)
            def _():
                cbase = chunk * SC_IDX_CHUNK
                pltpu.sync_copy(i_hbm.at[pl.ds(cbase, SC_IDX_CHUNK)], idx_v)
                g0 = pltpu.async_copy(x_hbm.at[idx_v.at[pl.ds(0, half)]], buf0, g0_sem)
                g1 = pltpu.async_copy(x_hbm.at[idx_v.at[pl.ds(half, half)]], buf1, g1_sem)
                g0.wait()
                w0 = pltpu.async_copy(buf0, o_hbm.at[pl.ds(cbase, half)], w0_sem)
                g1.wait()
                w1 = pltpu.async_copy(buf1, o_hbm.at[pl.ds(cbase + half, half)], w1_sem)
                w0.wait()
                w1.wait()

    return gather(table, idx)


def _expert_kernel(te_ref, tv_ref, nx_ref, sl_ref, nu_ref, *refs, layer):
    x_refs = refs[:EXPERT_TILE_PARTS]
    wg_hbm, wu_hbm, wd_hbm, o_ref, wg_f, wu_f, wd_f, wg_b, wu_b, wd_b, sems = refs[EXPERT_TILE_PARTS:]
    i = pl.program_id(0)

    def weight_copies(expert, slot):
        return [pltpu.make_async_copy(hbm.at[layer, expert], buf.at[slot], sems.at[slot, j])
                for j, (hbm, buf) in enumerate(((wg_hbm, wg_f), (wu_hbm, wu_f), (wd_hbm, wd_f)))]

    @pl.when(i == 0)
    def _():
        for cp in weight_copies(te_ref[0], 0):
            cp.start()

    @pl.when((i == 0) | (te_ref[i] != te_ref[jnp.maximum(i - 1, 0)]))
    def _():
        slot = sl_ref[i]
        for cp in weight_copies(te_ref[i], slot):
            cp.wait()
        wg_b[...] = wg_f[slot].astype(bf16)
        wu_b[...] = wu_f[slot].astype(bf16)
        wd_b[...] = wd_f[slot].astype(bf16)

        @pl.when(nx_ref[i] >= 0)
        def _():
            for cp in weight_copies(nx_ref[i], 1 - slot):
                cp.start()

    @pl.when(tv_ref[i] != 0)
    def _():
        rows = x_refs[0].shape[0]
        half = x_refs[0].shape[1]
        for j, x_ref in enumerate(x_refs):
            lo, hi = _unpack_bf16_pairs(x_ref[...])
            a = (jnp.dot(lo, wg_b[:half], preferred_element_type=f32)
                 + jnp.dot(hi, wg_b[half:], preferred_element_type=f32))
            b = (jnp.dot(lo, wu_b[:half], preferred_element_type=f32)
                 + jnp.dot(hi, wu_b[half:], preferred_element_type=f32))
            hid = (_silu(a) * b).astype(bf16)
            y = jnp.dot(hid, wd_b[...], preferred_element_type=f32)
            o_ref[j * rows:(j + 1) * rows, :] = _pack_bf16_pairs(y)


def grouped_experts(xs, tile_exp, tile_valid, wg, wu, wd, layer):
    p, half = xs.shape
    d = 2 * half
    n_tiles = p // EXPERT_TILE
    first = jnp.concatenate([jnp.ones((1,), bool), tile_exp[1:] != tile_exp[:-1]])
    slot = (jnp.cumsum(first.astype(i32)) - 1) % 2
    nxt_at = jnp.sum((tile_exp[None, :] <= tile_exp[:, None]).astype(i32), axis=1)
    nxt = jnp.where(nxt_at < n_tiles, tile_exp[jnp.minimum(nxt_at, n_tiles - 1)], -1)
    n_used = jnp.sum(tile_valid).reshape(1)
    used = lambda i, nu: jnp.minimum(i, nu[0] - 1)
    tile = pl.BlockSpec((EXPERT_TILE, half), lambda i, te, tv, nx, sl, nu: (used(i, nu), 0))
    part_rows = EXPERT_TILE // EXPERT_TILE_PARTS
    parts = [pl.BlockSpec((part_rows, half), functools.partial(
        lambda j, i, te, tv, nx, sl, nu: (used(i, nu) * EXPERT_TILE_PARTS + j, 0), j))
        for j in range(EXPERT_TILE_PARTS)]
    hbm = pl.BlockSpec(memory_space=pl.ANY)
    grid_spec = pltpu.PrefetchScalarGridSpec(
        num_scalar_prefetch=5,
        grid=(n_tiles,),
        in_specs=parts + [hbm, hbm, hbm],
        out_specs=tile,
        scratch_shapes=[pltpu.VMEM((2, d, D_EXPERT), f32), pltpu.VMEM((2, d, D_EXPERT), f32),
                        pltpu.VMEM((2, D_EXPERT, d), f32),
                        pltpu.VMEM((d, D_EXPERT), bf16), pltpu.VMEM((d, D_EXPERT), bf16),
                        pltpu.VMEM((D_EXPERT, d), bf16), pltpu.SemaphoreType.DMA((2, 3))],
    )
    return pl.pallas_call(
        functools.partial(_expert_kernel, layer=layer),
        out_shape=jax.ShapeDtypeStruct((p, half), i32),
        grid_spec=grid_spec,
        compiler_params=_params(("arbitrary",)),
        name="grouped_experts",
    )(tile_exp, tile_valid, nxt.astype(i32), slot.astype(i32), n_used.astype(i32), *([xs] * EXPERT_TILE_PARTS),
      wg, wu, wd)


def _combine_kernel(x_ref, yg_ref, wt_ref, g2_ref, sh2_ref, sc2_ref, n2_ref, wsg_ref, wsu_ref, wsd_ref, fg_ref, o_ref,
                    *, final):
    x = x_ref[...]
    hb = _rms_mod(x, n2_ref[...], sc2_ref[0], sh2_ref[0]).astype(bf16)
    a = jnp.dot(hb, wsg_ref[...], preferred_element_type=f32)
    b = jnp.dot(hb, wsu_ref[...], preferred_element_type=f32)
    f = jnp.dot((_silu(a) * b).astype(bf16), wsd_ref[...], preferred_element_type=f32)
    wt = wt_ref[...]
    half = x.shape[1] // 2
    f_lo, f_hi = f[:, :half], f[:, half:]
    for k in range(TOP_K):
        packed = yg_ref[k]
        w = wt[:, k:k + 1]
        f_lo = f_lo + w * lax.bitcast_convert_type(lax.shift_left(packed, 16), f32)
        f_hi = f_hi + w * lax.bitcast_convert_type(packed & jnp.int32(-65536), f32)
    xo = x + g2_ref[0] * jnp.concatenate([f_lo, f_hi], axis=1)
    if final:
        ms = jnp.mean(xo * xo, axis=-1, keepdims=True)
        xo = xo * lax.rsqrt(ms + RMS_EPS) * fg_ref[...]
    o_ref[...] = xo


def combine(x2, yg, wt_rows, row_offset, mods, rows_per_mod, tile, lw, final_g, final):
    m, d = x2.shape
    off = row_offset // tile
    row = lambda w: pl.BlockSpec((tile, w), lambda i: (i, 0))
    modspec = pl.BlockSpec((1, 1, d), lambda i: ((i * tile) // rows_per_mod, 0, 0))
    return pl.pallas_call(
        functools.partial(_combine_kernel, final=final),
        out_shape=jax.ShapeDtypeStruct((m, d), f32),
        grid=(m // tile,),
        in_specs=[row(d), pl.BlockSpec((TOP_K, tile, d // 2), lambda i: (0, i + off, 0)),
                  pl.BlockSpec((tile, 8), lambda i: (i + off, 0)), modspec, modspec, modspec,
                  _const_spec((1, d)), _const_spec((d, D_EXPERT)), _const_spec((d, D_EXPERT)),
                  _const_spec((D_EXPERT, d)), _const_spec((1, d))],
        out_specs=row(d),
        compiler_params=_params(("parallel",)),
        name="combine",
    )(x2, yg, wt_rows, mods["g2"], mods["sh2"], mods["sc2"], lw["norm2_g"], lw["w_sh_gate"], lw["w_sh_up"],
      lw["w_sh_down"], final_g)


def routed_experts(h2p, eid, rank, counts, lw, layer):
    m = h2p.shape[0]
    p_max = m * TOP_K + N_EXPERTS * EXPERT_TILE
    pos, tile_exp, tile_valid = routing_plan(eid, rank, counts, p_max)
    xs = sc_scatter_rows(h2p, pos, p_max)
    ys = grouped_experts(xs, tile_exp, tile_valid, lw["w_exp_gate"], lw["w_exp_up"], lw["w_exp_down"], layer)
    return sc_gather_rows(ys, pos[:TOP_K].reshape(TOP_K * m)).reshape(TOP_K, m, D_MODEL // 2)


def rope_tables(seq):
    rows = seq // GRID_W
    row = jnp.repeat(jnp.arange(rows), GRID_W).astype(f32)
    col = jnp.tile(jnp.arange(GRID_W), rows).astype(f32)
    axis_dim = HEAD_DIM // 2
    inv_freq = 1.0 / (ROPE_THETA ** (jnp.arange(0, axis_dim, 2, dtype=f32) / axis_dim))
    ang_r = row[:, None] * inv_freq
    ang_c = col[:, None] * inv_freq
    ang = jnp.concatenate([ang_r, ang_r, ang_c, ang_c], axis=-1)
    cos, sin = jnp.cos(ang), jnp.sin(ang)
    seg = (jnp.arange(HEAD_DIM) // 16) % 2
    sa = jnp.where(seg == 0, -sin, 0.0)
    sb = jnp.where(seg == 1, sin, 0.0)
    rep = lambda t: jnp.tile(t, (1, 4))
    return rep(cos), rep(sa), rep(sb)


def _router_weight_pieces(w):
    hi, mid, lo = _split3(w)
    return jnp.concatenate([hi, mid, lo, jnp.zeros_like(hi)], axis=1)


def identity_rope_tables(rows):
    return jnp.ones((rows, 256), f32), jnp.zeros((rows, 256), f32), jnp.zeros((rows, 256), f32)


def kernel(x, c, ctx, c_ctx, w_mod, b_mod, norm1_g, norm2_g, w_in, q_norm_g, k_norm_g, sink, gm_norm_g, gm_ws, gm_b, w_br, w_o, w_router, router_bias, w_exp_gate, w_exp_up, w_exp_down, w_sh_gate, w_sh_up, w_sh_down, final_norm_g):
    bsz_all, seq, d = x.shape
    n_ctx = ctx.shape[1]
    depth = w_mod.shape[0]

    cc = jnp.concatenate([c, c_ctx[None, :], jnp.zeros((MOD_ROWS - bsz_all - 1, d), f32)], axis=0)
    mod_all = compute_mod(cc, w_mod, b_mod)

    lat_tables = rope_tables(seq)
    ctx_tables = identity_rope_tables(n_ctx)
    wc_lat, ws_lat = dft_tables(seq)
    wc_ctx, ws_ctx = dft_tables(n_ctx)
    cs64 = channel_dft_table()
    bd = jnp.asarray(np.kron(np.eye(4), np.full((HEAD_DIM, HEAD_DIM), 1.0 / HEAD_DIM)), dtype=bf16)
    final_g = final_norm_g.reshape(1, d)
    tri = jnp.asarray(np.triu(np.ones((MERGE_TILE, MERGE_TILE)), 1), dtype=bf16)

    lws = []
    for l in range(depth):
        lws.append({
            "norm1_g": norm1_g[l].reshape(1, d),
            "norm2_g": norm2_g[l].reshape(1, d),
            "w_z": w_in[l, :, :OFF_GATE].astype(bf16),
            "w_gate": w_in[l, :, OFF_GATE:].astype(bf16),
            "qn": jnp.tile(q_norm_g[l], 4).reshape(1, 256),
            "kn": jnp.tile(k_norm_g[l], 2).reshape(1, 128),
            "bd": bd,
            "gm_norm_g": gm_norm_g[l].reshape(1, 256),
            "gm_ws": gm_ws[l].astype(bf16),
            "gm_bias": jnp.repeat(gm_b[l].T, 64, axis=1),
            "cs64": cs64,
            "w_br": w_br[l].astype(bf16),
            "w_o": w_o[l].astype(bf16),
            "w_router3": _router_weight_pieces(w_router[l]),
            "router_bias": router_bias[l].reshape(N_EXPERTS, 1),
            "tri": tri,
            "w_exp_gate": w_exp_gate,
            "w_exp_up": w_exp_up,
            "w_exp_down": w_exp_down,
            "w_sh_gate": w_sh_gate[l].astype(bf16),
            "w_sh_up": w_sh_up[l].astype(bf16),
            "w_sh_down": w_sh_down[l].astype(bf16),
        })

    return _layers(x, ctx, mod_all[:, :bsz_all], mod_all[:, bsz_all], lws, sink, lat_tables, ctx_tables,
                   (wc_lat, ws_lat), (wc_ctx, ws_ctx), final_g)


def _layers(x, ctx, mod_lat, mod_ctx, lws, sink, lat_tables, ctx_tables, dft_lat, dft_ctx, final_g):
    bsz, seq, d = x.shape
    n_ctx = ctx.shape[1]
    depth = len(lws)
    n_lat = bsz * seq
    n_cx = bsz * n_ctx
    wc_lat, ws_lat = dft_lat
    wc_ctx, ws_ctx = dft_ctx
    xl = x.reshape(n_lat, d)
    xc = ctx.reshape(n_cx, d)
    for l in range(depth):
        ctx_out = l < depth - 1
        lw = lws[l]
        names = ("sh1", "sc1", "g1", "sh2", "sc2", "g2")
        mods_lat = {n: mod_lat[l, :, i * d:(i + 1) * d].reshape(bsz, 1, d) for i, n in enumerate(names)}
        mods_ctx = {n: mod_ctx[l, i * d:(i + 1) * d].reshape(1, 1, d) for i, n in enumerate(names)}
        sink_l = sink[l]

        qg, kg, vg, qw, kw, vw, fxc, fxs, o_d = in_projection(
            xl, mods_lat["sh1"], mods_lat["sc1"], seq, lat_tables, seq, ROW_TILE, lw)
        cqg, ckg, cvg, cqw, ckw, cvw, cfxc, cfxs, co_d = in_projection(
            xc, mods_ctx["sh1"], mods_ctx["sc1"], n_cx, ctx_tables, n_ctx, n_ctx, lw)
        b3 = lambda t, rows: t.reshape(bsz, rows, t.shape[-1])
        ckg3, cvg3, ckw3, cvw3 = b3(ckg, n_ctx), b3(cvg, n_ctx), b3(ckw, n_ctx), b3(cvw, n_ctx)

        o_a = dft_mix(wc_lat, ws_lat, b3(fxc, seq), b3(fxs, seq), ROW_TILE).reshape(n_lat, BRANCH_W)
        o_b = full_attention(b3(qg, seq), [(ckg3, cvg3), (b3(kg, seq), b3(vg, seq))], None, FULL_ATTN_Q_ROWS)
        o_b = o_b.reshape(n_lat, BRANCH_W)
        o_c = window_attention(b3(qw, seq), b3(kw, seq), b3(vw, seq), ckw3, cvw3, sink_l).reshape(n_lat, BRANCH_W)
        m_total = n_lat + (n_cx if ctx_out else 0)
        route = None
        if ctx_out:
            route = [jnp.zeros((m_total, d // 2), i32), jnp.zeros((8, m_total), i32), jnp.zeros((8, m_total), f32),
                     jnp.zeros((8, m_total), i32), jnp.zeros((N_EXPERTS, 1), i32)]
        xl, *route = merge_and_route(xl, mods_lat, seq, (o_a, o_b, o_c, o_d), MERGE_TILE, lw, m_total, 0, route)
        if ctx_out:
            co_a = dft_mix(wc_ctx, ws_ctx, b3(cfxc, n_ctx), b3(cfxs, n_ctx), n_ctx).reshape(n_cx, BRANCH_W)
            co_b = full_attention(b3(cqg, n_ctx), [(ckg3, cvg3)], None, n_ctx).reshape(n_cx, BRANCH_W)
            co_c = full_attention(b3(cqw, n_ctx), [(ckw3, cvw3)], sink_l, n_ctx).reshape(n_cx, BRANCH_W)
            xc, *route = merge_and_route(xc, mods_ctx, n_cx, (co_a, co_b, co_c, co_d), MERGE_TILE, lw, m_total, n_lat,
                                         route)
        h2p, eid, wt, rank, counts = route
        yg = routed_experts(h2p, eid, rank, counts, lw, l)
        wt_rows = wt.T
        xl = combine(xl, yg, wt_rows, 0, mods_lat, seq, ROW_TILE, lw, final_g, not ctx_out)
        if ctx_out:
            xc = combine(xc, yg, wt_rows, n_lat, mods_ctx, n_cx, ROW_TILE, lw, final_g, False)
    return xl.reshape(bsz, seq, d)
```

```python
import functools
import math

import jax
import jax.numpy as jnp
import numpy as np
from jax import lax
from jax.experimental import pallas as pl
from jax.experimental.pallas import tpu as pltpu
from jax.experimental.pallas import tpu_sc as plsc

f32 = jnp.float32
bf16 = jnp.bfloat16
i32 = jnp.int32

D_MODEL = 1024
HEAD_DIM = 64
GRID_W = 64
ROPE_THETA = 10000.0
ATTN_SCALE = HEAD_DIM ** -0.5
RMS_EPS = 1e-6
NEG_INF = -1e30
Q_BLOCK = 128
WINDOW = 128
GM_CHUNK = 128
N_BRANCHES = 4
BRANCH_W = 256
KV_W = 128
OFF_KV = 0
OFF_Q = 512
OFF_FN = 1024
OFF_GM = 1280
OFF_GATE = 1792
N_EXPERTS = 64
TOP_K = 6
N_GROUPS = 8
GROUP_SIZE = N_EXPERTS // N_GROUPS
TOPK_GROUPS = 4
D_EXPERT = 256
ROUTED_SCALE = 2.5

SC_CORES = 2
SC_SUBCORES = 16
SC_WORKERS = SC_CORES * SC_SUBCORES
SC_IDX_CHUNK = 128

ROW_TILE = 1024
MERGE_TILE = 512
ROUTER_AFTER_BRANCHES = 1
RANK_AFTER_BRANCHES = 3
EXPERT_TILE = 1024
FULL_ATTN_Q_ROWS = 1024
FULL_ATTN_ROW_BLOCKS = 4
WINDOW_BLOCKS_PER_STEP = 16
MOD_ROWS = 24
VMEM_LIMIT = 56 * 1024 * 1024


def _params(sem, vmem=VMEM_LIMIT):
    return pltpu.CompilerParams(dimension_semantics=sem, vmem_limit_bytes=vmem)


def _const_spec(shape):
    nd = len(shape)
    return pl.BlockSpec(shape, lambda *_: (0,) * nd, pipeline_mode=pl.Buffered(1))


def _rms_mod(x, g, sc, sh):
    ms = jnp.mean(x * x, axis=-1, keepdims=True)
    return (x * lax.rsqrt(ms + RMS_EPS) * g) * (1.0 + sc) + sh


def _gelu(x):
    return 0.5 * x * (1.0 + jnp.tanh(math.sqrt(2.0 / math.pi) * (x + 0.044715 * (x * x * x))))


def _silu(x):
    return x * jax.nn.sigmoid(x)


def _mod_kernel(a_ref, w_ref, b_ref, o_ref):
    a = _silu(a_ref[...]).astype(bf16)
    o_ref[0] = jnp.dot(a, w_ref[0].astype(bf16), preferred_element_type=f32) + b_ref[0]


def compute_mod(cc, w_mod, b_mod):
    depth, d, n = w_mod.shape
    tn = 1536
    return pl.pallas_call(
        _mod_kernel,
        out_shape=jax.ShapeDtypeStruct((depth, MOD_ROWS, n), f32),
        grid=(depth, n // tn),
        in_specs=[
            pl.BlockSpec((MOD_ROWS, d), lambda l, j: (0, 0)),
            pl.BlockSpec((1, d, tn), lambda l, j: (l, 0, j)),
            pl.BlockSpec((1, 1, tn), lambda l, j: (l, 0, j)),
        ],
        out_specs=pl.BlockSpec((1, MOD_ROWS, tn), lambda l, j: (l, 0, j)),
        compiler_params=_params(("parallel", "parallel")),
        name="mod_proj",
    )(cc, w_mod, b_mod.reshape(depth, 1, n))


def _inproj_kernel(x_ref, sh_ref, sc_ref, g_ref, w_ref, qn_ref, kn_ref, bd_ref, gmg_ref, ws_ref, gb_ref, cs_ref,
                   cos_ref, sa_ref, sb_ref,
                   qg_ref, kg_ref, vg_ref, qw_ref, kw_ref, vw_ref, xc_ref, xs_ref, od_ref):
    tile = x_ref.shape[0]
    hb = _rms_mod(x_ref[...], g_ref[...], sc_ref[0], sh_ref[0]).astype(bf16)

    def proj(a, b):
        return jnp.dot(hb, w_ref[:, a:b], preferred_element_type=f32)

    def square_pieces(t):
        sq = t * t
        hi = sq.astype(bf16)
        return hi, (sq - hi.astype(f32)).astype(bf16)

    def headnorm(t, pieces, gain):
        w = t.shape[1]
        b = bd_ref[:w, :w]
        ms = (jnp.dot(pieces[0], b, preferred_element_type=f32) + jnp.dot(pieces[1], b, preferred_element_type=f32))
        return t * lax.rsqrt(ms + RMS_EPS) * gain

    def rope(t):
        w = t.shape[1]
        return (t * cos_ref[:, :w] + pltpu.roll(t, w - 16, 1) * sa_ref[:, :w]
                + pltpu.roll(t, 16, 1) * sb_ref[:, :w])

    def expand_heads(q):
        lane = lax.broadcasted_iota(i32, (1, KV_W), 1)
        low = lane < HEAD_DIM
        blocks = []
        for kv in range(2):
            pair = q[:, KV_W * kv:KV_W * (kv + 1)]
            swapped = pltpu.roll(pair, HEAD_DIM, 1)
            keep = low if kv == 0 else jnp.logical_not(low)
            g0, g1 = (pair, swapped) if kv == 0 else (swapped, pair)
            blocks.append(jnp.where(keep, g0, 0.0))
            blocks.append(jnp.where(keep, g1, 0.0))
        return jnp.concatenate(blocks, axis=1)

    kv = proj(OFF_KV, OFF_Q)
    qq = proj(OFF_Q, OFF_FN)
    fn = proj(OFF_FN, OFF_GM).astype(bf16)
    uv = proj(OFF_GM, OFF_GATE)

    k_sq = square_pieces(kv[:, 0:128])
    q_sq = square_pieces(qq[:, :256])
    u = _gelu(uv[:, :256])
    v = _gelu(uv[:, 256:])
    vms = jnp.mean(v * v, axis=-1, keepdims=True)
    vn = (v * lax.rsqrt(vms + RMS_EPS) * gmg_ref[...]).astype(bf16)

    kg_ref[...] = rope(headnorm(kv[:, 0:128], k_sq, kn_ref[...])).astype(bf16)
    vg_ref[...] = kv[:, 128:256].astype(bf16)
    kw_ref[...] = rope(kv[:, 256:384]).astype(bf16)
    vw_ref[...] = kv[:, 384:512].astype(bf16)

    qg = rope(headnorm(qq[:, :256], q_sq, qn_ref[...])) * ATTN_SCALE
    qg_ref[...] = expand_heads(qg).astype(bf16)
    qw = rope(qq[:, 256:]) * ATTN_SCALE
    qw_ref[...] = expand_heads(qw).astype(bf16)

    xcs = jnp.dot(fn, cs_ref[...], preferred_element_type=f32)
    xc_ref[...] = xcs[:, :256].astype(bf16)
    xs_ref[...] = xcs[:, 256:].astype(bf16)

    lane_grp = lax.broadcasted_iota(i32, (1, 256), 1) // 64
    for c in range(tile // GM_CHUNK):
        rows = slice(c * GM_CHUNK, (c + 1) * GM_CHUNK)
        vch = vn[rows]
        sv = gb_ref[...]
        for g in range(4):
            r = jnp.dot(ws_ref[g], vch, preferred_element_type=f32)
            sv = sv + jnp.where(lane_grp == g, r, 0.0)
        od_ref[rows, :] = (u[rows] * sv).astype(bf16)


def in_projection(x2, sh, sc, rows_per_mod, tables, rows_per_seq, tile, lw):
    m, d = x2.shape
    cos_t, sa_t, sb_t = tables
    seq_blocks = rows_per_seq // tile
    row = lambda w: pl.BlockSpec((tile, w), lambda i: (i, 0))
    modspec = pl.BlockSpec((1, 1, d), lambda i: ((i * tile) // rows_per_mod, 0, 0))
    tabspec = pl.BlockSpec((tile, 256), lambda i: (i % seq_blocks, 0))
    out_w = [512, 128, 128, 512, 128, 128, 256, 256, 256]
    return pl.pallas_call(
        _inproj_kernel,
        out_shape=[jax.ShapeDtypeStruct((m, w), bf16) for w in out_w],
        grid=(m // tile,),
        in_specs=[
            row(d), modspec, modspec, _const_spec((1, d)), _const_spec((d, OFF_GATE)),
            _const_spec((1, 256)), _const_spec((1, 128)), _const_spec((256, 256)), _const_spec((1, 256)),
            _const_spec((4, GM_CHUNK, GM_CHUNK)), _const_spec((GM_CHUNK, 256)), _const_spec((256, 512)),
            tabspec, tabspec, tabspec,
        ],
        out_specs=[row(w) for w in out_w],
        compiler_params=_params(("parallel",)),
        name="in_projection",
    )(x2, sh, sc, lw["norm1_g"], lw["w_z"], lw["qn"], lw["kn"], lw["bd"], lw["gm_norm_g"], lw["gm_ws"], lw["gm_bias"],
      lw["cs64"], cos_t, sa_t, sb_t)


def _dft_kernel(wc_ref, ws_ref, xc_ref, xs_ref, o_ref):
    acc = jnp.dot(wc_ref[...], xc_ref[0], preferred_element_type=f32)
    acc = acc + jnp.dot(ws_ref[...], xs_ref[0], preferred_element_type=f32)
    o_ref[0] = acc.astype(bf16)


def dft_mix(wc, ws, xc, xs, tile):
    nb, length, w = xc.shape
    return pl.pallas_call(
        _dft_kernel,
        out_shape=jax.ShapeDtypeStruct((nb, length, w), bf16),
        grid=(length // tile, nb),
        in_specs=[
            pl.BlockSpec((tile, length), lambda i, b: (i, 0)),
            pl.BlockSpec((tile, length), lambda i, b: (i, 0)),
            pl.BlockSpec((1, length, w), lambda i, b: (b, 0, 0)),
            pl.BlockSpec((1, length, w), lambda i, b: (b, 0, 0)),
        ],
        out_specs=pl.BlockSpec((1, tile, w), lambda i, b: (b, i, 0)),
        compiler_params=_params(("parallel", "parallel")),
        name="dft_mix",
    )(wc, ws, xc, xs)


def dft_tables(length):
    jk = (np.arange(length)[:, None] * np.arange(length)[None, :]) % length
    ang = 2.0 * np.pi * jk / length
    s = 1.0 / math.sqrt(length)
    return jnp.asarray(np.cos(ang) * s, dtype=bf16), jnp.asarray(-np.sin(ang) * s, dtype=bf16)


def channel_dft_table():
    jk = (np.arange(64)[:, None] * np.arange(64)[None, :]) % 64
    ang = 2.0 * np.pi * jk / 64
    eye = np.eye(4)
    c = np.kron(eye, np.cos(ang) / 8.0)
    s = np.kron(eye, np.sin(ang) / 8.0)
    return jnp.asarray(np.concatenate([c, s], axis=1), dtype=bf16)


def _attend_blocks(blocks, sink_ref):
    lane = lax.broadcasted_iota(i32, (1, KV_W), 1)
    low = lane < HEAD_DIM
    units = []
    for q_all, pieces in blocks:
        qb = q_all.shape[0]
        for kv in range(2):
            q = jnp.concatenate([q_all[:, KV_W * (2 * kv):KV_W * (2 * kv + 1)],
                                 q_all[:, KV_W * (2 * kv + 1):KV_W * (2 * kv + 2)]], axis=0)
            sink_col = None
            if sink_ref is not None:
                sink_col = jnp.concatenate([jnp.full((qb, 1), sink_ref[2 * kv], f32),
                                            jnp.full((qb, 1), sink_ref[2 * kv + 1], f32)], axis=0)
            units.append((q, pieces, sink_col, low if kv == 0 else jnp.logical_not(low)))

    scores = []
    for q, pieces, _, _ in units:
        unit_scores = []
        for k, _, mask in pieces:
            s = lax.dot_general(q, k, (((1,), (1,)), ((), ())), preferred_element_type=f32)
            unit_scores.append(s if mask is None else jnp.where(mask, s, NEG_INF))
        scores.append(unit_scores)

    maxes = []
    for (_, _, sink_col, _), unit_scores in zip(units, scores):
        m = unit_scores[0].max(axis=-1, keepdims=True)
        for s in unit_scores[1:]:
            m = jnp.maximum(m, s.max(axis=-1, keepdims=True))
        maxes.append(m if sink_col is None else jnp.maximum(m, sink_col))

    probs = [[jnp.exp((s - m).astype(bf16)) for s in unit_scores] for unit_scores, m in zip(scores, maxes)]

    results = []
    for (_, pieces, sink_col, own), unit_probs, m in zip(units, probs, maxes):
        acc = None
        for p, (_, v, _) in zip(unit_probs, pieces):
            pv = jnp.dot(p, jnp.where(own, v, jnp.ones_like(v)), preferred_element_type=f32)
            acc = pv if acc is None else acc + pv
        denom = pltpu.roll(acc, HEAD_DIM, 1)
        if sink_col is not None:
            denom = denom + jnp.exp(sink_col - m)
        results.append(acc * (1.0 / denom))

    outs = []
    for i, (q_all, _) in enumerate(blocks):
        qb = q_all.shape[0]
        r_kv0, r_kv1 = results[2 * i], results[2 * i + 1]
        lo = jnp.where(low, r_kv0[:qb], pltpu.roll(r_kv0[qb:], HEAD_DIM, 1))
        hi = jnp.where(low, pltpu.roll(r_kv1[:qb], HEAD_DIM, 1), r_kv1[qb:])
        outs.append(jnp.concatenate([lo, hi], axis=1))
    return outs


def _full_attn_kernel(*refs, n_pieces, has_sink, row_blocks):
    pos = 0
    sink_ref = None
    if has_sink:
        sink_ref = refs[0]
        pos = 1
    q_ref = refs[pos]
    kv_refs = refs[pos + 1:pos + 1 + 2 * n_pieces]
    o_ref = refs[pos + 1 + 2 * n_pieces]
    pieces = [(kv_refs[2 * i][0], kv_refs[2 * i + 1][0], None) for i in range(n_pieces)]
    rows = q_ref.shape[1] // row_blocks
    blocks = [(q_ref[0, j * rows:(j + 1) * rows, :], pieces) for j in range(row_blocks)]
    for j, out in enumerate(_attend_blocks(blocks, sink_ref)):
        o_ref[0, j * rows:(j + 1) * rows, :] = out.astype(bf16)


def full_attention(q, pieces, sink, qb):
    nb, lq, _ = q.shape
    in_specs = []
    args = []
    if sink is not None:
        in_specs.append(pl.BlockSpec(memory_space=pltpu.SMEM))
        args.append(sink)
    in_specs.append(pl.BlockSpec((1, qb, 512), lambda b, i: (b, i, 0)))
    args.append(q)
    for k, v in pieces:
        spec = pl.BlockSpec((1, k.shape[1], KV_W), lambda b, i: (b, 0, 0))
        in_specs += [spec, spec]
        args += [k, v]
    return pl.pallas_call(
        functools.partial(_full_attn_kernel, n_pieces=len(pieces), has_sink=sink is not None,
                          row_blocks=FULL_ATTN_ROW_BLOCKS),
        out_shape=jax.ShapeDtypeStruct((nb, lq, 256), bf16),
        grid=(nb, lq // qb),
        in_specs=in_specs,
        out_specs=pl.BlockSpec((1, qb, 256), lambda b, i: (b, i, 0)),
        compiler_params=_params(("parallel", "parallel")),
        name="full_attention",
    )(*args)


def _window_attn_kernel(sink_ref, q_ref, k_ref, v_ref, kc_ref, vc_ref, o_ref, *, seq, blocks):
    span = 3 * Q_BLOCK
    ctx_piece = (kc_ref[0], vc_ref[0], None)
    work = []
    for j in range(blocks):
        n = pl.program_id(1) * blocks + j
        start = pl.multiple_of(jnp.clip((n - 1) * Q_BLOCK, 0, seq - span), Q_BLOCK)
        kwin = k_ref[0, pl.ds(start, span), :]
        vwin = v_ref[0, pl.ds(start, span), :]
        row = lax.broadcasted_iota(i32, (2 * Q_BLOCK, span), 0) % Q_BLOCK + n * Q_BLOCK
        col = lax.broadcasted_iota(i32, (2 * Q_BLOCK, span), 1) + start
        mask = jnp.abs(row - col) <= WINDOW
        work.append((q_ref[0, j * Q_BLOCK:(j + 1) * Q_BLOCK, :], [ctx_piece, (kwin, vwin, mask)]))
    for j, out in enumerate(_attend_blocks(work, sink_ref)):
        o_ref[0, j * Q_BLOCK:(j + 1) * Q_BLOCK, :] = out.astype(bf16)


def window_attention(q, k, v, kc, vc, sink):
    nb, seq, _ = q.shape
    n_ctx = kc.shape[1]
    blocks = WINDOW_BLOCKS_PER_STEP
    assert seq % (Q_BLOCK * blocks) == 0 and seq >= 3 * Q_BLOCK, seq
    full = lambda l: pl.BlockSpec((1, l, KV_W), lambda b, i: (b, 0, 0))
    return pl.pallas_call(
        functools.partial(_window_attn_kernel, seq=seq, blocks=blocks),
        out_shape=jax.ShapeDtypeStruct((nb, seq, 256), bf16),
        grid=(nb, seq // (Q_BLOCK * blocks)),
        in_specs=[pl.BlockSpec(memory_space=pltpu.SMEM),
                  pl.BlockSpec((1, Q_BLOCK * blocks, 512), lambda b, i: (b, i, 0)),
                  full(seq), full(seq), full(n_ctx), full(n_ctx)],
        out_specs=pl.BlockSpec((1, Q_BLOCK * blocks, 256), lambda b, i: (b, i, 0)),
        compiler_params=_params(("parallel", "parallel")),
        name="window_attention",
    )(sink, q, k, v, kc, vc)


def _route(logits_t, bias_col):
    t = logits_t.shape[1]
    scores = jax.nn.sigmoid(logits_t)
    choice = scores + bias_col
    sub = lax.broadcasted_iota(i32, (GROUP_SIZE, t), 0)
    grp_score = []
    for g in range(N_GROUPS):
        cg = choice[g * GROUP_SIZE:(g + 1) * GROUP_SIZE]
        m1 = cg.max(axis=0, keepdims=True)
        first = jnp.min(jnp.where(cg == m1, sub, GROUP_SIZE), axis=0, keepdims=True)
        m2 = jnp.where(sub == first, -jnp.inf, cg).max(axis=0, keepdims=True)
        grp_score.append(m1 + m2)
    keep = []
    for g in range(N_GROUPS):
        beaten = jnp.zeros((1, t), i32)
        for o in range(N_GROUPS):
            if o == g:
                continue
            wins = (grp_score[o] > grp_score[g]) | ((grp_score[o] == grp_score[g]) & (o < g))
            beaten = beaten + wins.astype(i32)
        keep.append(jnp.broadcast_to(beaten < TOPK_GROUPS, (GROUP_SIZE, t)))
    masked = jnp.where(jnp.concatenate(keep, axis=0), choice, NEG_INF)
    eid = lax.broadcasted_iota(i32, (N_EXPERTS, t), 0)
    ids, wts = [], []
    for _ in range(TOP_K):
        m = masked.max(axis=0, keepdims=True)
        pick = jnp.min(jnp.where(masked == m, eid, N_EXPERTS), axis=0, keepdims=True)
        sel = eid == pick
        ids.append(pick)
        wts.append(jnp.sum(jnp.where(sel, scores, 0.0), axis=0, keepdims=True))
        masked = jnp.where(sel, -jnp.inf, masked)
    total = wts[0]
    for w in wts[1:]:
        total = total + w
    norm = ROUTED_SCALE / total
    return ids, [w * norm for w in wts]


def _pack_bf16_pairs(x):
    w = x.shape[1] // 2
    lo = lax.bitcast_convert_type(x[:, :w].astype(bf16).astype(f32), i32)
    hi = lax.bitcast_convert_type(x[:, w:].astype(bf16).astype(f32), i32)
    return lax.shift_right_logical(lo, 16) | (hi & jnp.int32(-65536))


def _unpack_bf16_pairs(p):
    lo = lax.bitcast_convert_type(lax.shift_left(p, 16), f32)
    hi = lax.bitcast_convert_type(p & jnp.int32(-65536), f32)
    return lo.astype(bf16), hi.astype(bf16)


def _merge_kernel(x_ref, sh_ref, sc_ref, g1_ref, sh2_ref, sc2_ref, n1_ref, n2_ref, oa_ref, ob_ref, oc_ref, od_ref,
                  wg_ref, wbr_ref, wo_ref, wr_ref, rb_ref, tri_ref, *rest, extends):
    if extends:
        cnt_in_ref = rest[0]
        rest = rest[5:]
    xo_ref, h2_ref, eid_ref, wt_ref, rank_ref, cnt_ref, run_ref, xn_prev_ref = rest
    step = pl.program_id(0)

    @pl.when(step == 0)
    def _():
        run_ref[...] = cnt_in_ref[...].astype(f32) if extends else jnp.zeros_like(run_ref)
        xn_prev_ref[...] = jnp.zeros_like(xn_prev_ref)

    x = x_ref[...]
    hb = _rms_mod(x, n1_ref[...], sc_ref[0], sh_ref[0]).astype(bf16)
    h2 = _rms_mod(xn_prev_ref[...], n2_ref[...], sc2_ref[0], sh2_ref[0])
    h2_ref[...] = _pack_bf16_pairs(h2)
    y = None
    for i, o_ref in enumerate((oa_ref, ob_ref, oc_ref, od_ref)):
        if i == ROUTER_AFTER_BRANCHES:
            ids, wts = _route(_router_logits(wr_ref[...], h2), rb_ref[...])
        if i == RANK_AFTER_BRANCHES:
            _rank_entries(ids, wts, tri_ref, run_ref, (step > 0).astype(f32), eid_ref, wt_ref, rank_ref, cnt_ref)
        logit = jnp.dot(hb, wg_ref[:, i * D_MODEL:(i + 1) * D_MODEL], preferred_element_type=f32)
        proj = jnp.dot(o_ref[...], wbr_ref[i], preferred_element_type=f32)
        term = jax.nn.sigmoid(logit.astype(bf16)) * proj.astype(bf16)
        y = term if y is None else y + term
    xn = x + g1_ref[0] * jnp.dot(y, wo_ref[...], preferred_element_type=f32)
    xo_ref[...] = xn
    xn_prev_ref[...] = xn


def _split3(x):
    def head(v):
        return lax.bitcast_convert_type(lax.bitcast_convert_type(v, i32) & jnp.int32(-65536), f32)

    hi = head(x)
    r1 = x - hi
    mid = head(r1)
    lo = r1 - mid
    return hi.astype(bf16), mid.astype(bf16), lo.astype(bf16)


def _router_logits(w3, h2):
    prod = None
    for piece in _split3(h2):
        p = jnp.dot(piece, w3, preferred_element_type=f32)
        prod = p if prod is None else prod + p
    lane = lax.broadcasted_iota(i32, (1, 2 * N_EXPERTS), 1)
    low = prod[:, :2 * N_EXPERTS]
    logits = low + pltpu.roll(low, N_EXPERTS, 1) + prod[:, 2 * N_EXPERTS:]
    return jnp.where(lane < N_EXPERTS, logits, 0.0).T[:N_EXPERTS]


def _rank_entries(ids, wts, tri_ref, run_ref, live, eid_ref, wt_ref, rank_ref, cnt_ref):
    t = ids[0].shape[1]
    eid = lax.broadcasted_iota(i32, (N_EXPERTS, t), 0)
    hits = [eid == pick for pick in ids]
    chosen = hits[0]
    for h in hits[1:]:
        chosen = chosen | h
    chosen = jnp.where(chosen, 1.0, 0.0)
    prefix = jnp.dot(chosen.astype(bf16), tri_ref[...], preferred_element_type=f32)
    offset = run_ref[...] + prefix
    ranks = [jnp.sum(jnp.where(h, offset, 0.0), axis=0, keepdims=True).astype(i32) for h in hits]
    run_ref[...] += live * jnp.sum(chosen, axis=1, keepdims=True)
    cnt_ref[...] = run_ref[...].astype(i32)

    pad_i = [jnp.zeros((1, t), i32)] * (8 - TOP_K)
    eid_ref[...] = jnp.concatenate(ids + pad_i, axis=0)
    rank_ref[...] = jnp.concatenate(ranks + pad_i, axis=0)
    wt_ref[...] = jnp.concatenate(wts + [jnp.zeros((1, t), f32)] * (8 - TOP_K), axis=0)


def merge_and_route(x2, mods, rows_per_mod, branches, tile, lw, m_total, row_offset, prior):
    m, d = x2.shape
    off = row_offset // tile
    n_tiles = m // tile
    cur = lambda i: jnp.minimum(i, n_tiles - 1)
    prev = lambda i: jnp.maximum(i - 1, 0)
    row = lambda w: pl.BlockSpec((tile, w), lambda i: (cur(i), 0))
    row_prev = lambda w: pl.BlockSpec((tile, w), lambda i: (prev(i) + off, 0))
    modspec = pl.BlockSpec((1, 1, d), lambda i: ((cur(i) * tile) // rows_per_mod, 0, 0))
    modspec_prev = pl.BlockSpec((1, 1, d), lambda i: ((prev(i) * tile) // rows_per_mod, 0, 0))
    col = pl.BlockSpec((8, tile), lambda i: (0, prev(i) + off))
    in_specs = [row(d)] + [modspec] * 3 + [modspec_prev] * 2 + [_const_spec((1, d)), _const_spec((1, d))] + [
        row(BRANCH_W)] * 4 + [
        _const_spec((d, N_BRANCHES * d)), _const_spec((N_BRANCHES, BRANCH_W, d)), _const_spec((d, d)),
        _const_spec((d, 4 * N_EXPERTS)), _const_spec((N_EXPERTS, 1)), _const_spec((tile, tile))]
    args = [x2, mods["sh1"], mods["sc1"], mods["g1"], mods["sh2"], mods["sc2"], lw["norm1_g"], lw["norm2_g"],
            *branches, lw["w_gate"], lw["w_br"], lw["w_o"], lw["w_router3"], lw["router_bias"], lw["tri"]]
    aliases = {}
    if prior is not None:
        h2p, eid, wt, rank, counts = prior
        n_in = len(args)
        in_specs += [_const_spec((N_EXPERTS, 1))] + [pl.BlockSpec(memory_space=pl.ANY)] * 4
        args += [counts, h2p, eid, wt, rank]
        aliases = {n_in + 1 + j: 1 + j for j in range(4)}
    return pl.pallas_call(
        functools.partial(_merge_kernel, extends=prior is not None),
        out_shape=[jax.ShapeDtypeStruct((m, d), f32), jax.ShapeDtypeStruct((m_total, d // 2), i32),
                   jax.ShapeDtypeStruct((8, m_total), i32), jax.ShapeDtypeStruct((8, m_total), f32),
                   jax.ShapeDtypeStruct((8, m_total), i32), jax.ShapeDtypeStruct((N_EXPERTS, 1), i32)],
        grid=(n_tiles + 1,),
        in_specs=in_specs,
        out_specs=[row(d), row_prev(d // 2), col, col, col, pl.BlockSpec((N_EXPERTS, 1), lambda i: (0, 0))],
        scratch_shapes=[pltpu.VMEM((N_EXPERTS, 1), f32), pltpu.VMEM((tile, d), f32)],
        input_output_aliases=aliases,
        compiler_params=_params(("arbitrary",)),
        name="merge_and_route",
    )(*args)


def routing_plan(eid, rank, counts, p_max):
    counts = counts.reshape(N_EXPERTS)
    padded = ((counts + EXPERT_TILE - 1) // EXPERT_TILE) * EXPERT_TILE
    ends = jnp.cumsum(padded)
    starts = ends - padded
    onehot = eid[:, :, None] == jnp.arange(N_EXPERTS, dtype=i32)[None, None, :]
    pos = rank + jnp.sum(jnp.where(onehot, starts[None, None, :], 0), axis=-1)
    n_tiles = p_max // EXPERT_TILE
    tile_start = jnp.arange(n_tiles, dtype=i32) * EXPERT_TILE
    tile_valid = tile_start < ends[-1]
    tile_exp = jnp.sum((ends[None, :] <= tile_start[:, None]).astype(i32), axis=1)
    return pos.astype(i32), jnp.minimum(tile_exp, N_EXPERTS - 1), tile_valid.astype(i32)


def _sc_worker_id():
    return lax.axis_index("subcore") * SC_CORES + lax.axis_index("core")


def sc_scatter_rows(table, pos, p_rows):
    m, w = table.shape
    n_chunks = m // SC_IDX_CHUNK
    steps = -(-n_chunks // SC_WORKERS)
    pos3 = pos.reshape(8, n_chunks, SC_IDX_CHUNK).transpose(1, 0, 2)
    mesh = plsc.VectorSubcoreMesh(core_axis_name="core", subcore_axis_name="subcore")

    @functools.partial(
        pl.kernel,
        out_type=jax.ShapeDtypeStruct((p_rows, w), table.dtype),
        mesh=mesh,
        scratch_types=[
            pltpu.VMEM((8, SC_IDX_CHUNK), i32),
            pltpu.VMEM((SC_IDX_CHUNK, w), table.dtype),
            pltpu.SemaphoreType.DMA,
        ],
    )
    def scatter(x_hbm, p_hbm, o_hbm, idx_v, rows_v, sem):
        wid = _sc_worker_id()

        @pl.loop(0, steps)
        def _(si):
            chunk = si * SC_WORKERS + wid

            @pl.when(chunk < n_chunks)
            def _():
                pltpu.sync_copy(p_hbm.at[chunk], idx_v)
                pltpu.sync_copy(x_hbm.at[pl.ds(chunk * SC_IDX_CHUNK, SC_IDX_CHUNK)], rows_v)
                copies = [pltpu.async_copy(rows_v, o_hbm.at[idx_v.at[k]], sem) for k in range(TOP_K)]
                for cp in copies:
                    cp.wait()

    return scatter(table, pos3)


def sc_gather_rows(table, idx):
    n_idx = idx.shape[0]
    w = table.shape[1]
    n_chunks = n_idx // SC_IDX_CHUNK
    steps = -(-n_chunks // SC_WORKERS)
    half = SC_IDX_CHUNK // 2
    mesh = plsc.VectorSubcoreMesh(core_axis_name="core", subcore_axis_name="subcore")

    @functools.partial(
        pl.kernel,
        out_type=jax.ShapeDtypeStruct((n_idx, w), table.dtype),
        mesh=mesh,
        scratch_types=[
            pltpu.VMEM((SC_IDX_CHUNK,), i32),
            pltpu.VMEM((half, w), table.dtype),
            pltpu.VMEM((half, w), table.dtype),
            pltpu.SemaphoreType.DMA,
            pltpu.SemaphoreType.DMA,
            pltpu.SemaphoreType.DMA,
            pltpu.SemaphoreType.DMA,
        ],
    )
    def gather(x_hbm, i_hbm, o_hbm, idx_v, buf0, buf1, g0_sem, g1_sem, w0_sem, w1_sem):
        wid = _sc_worker_id()

        @pl.loop(0, steps)
        def _(si):
            chunk = si * SC_WORKERS + wid

            @pl.when(chunk < n_chunks)
            def _():
                cbase = chunk * SC_IDX_CHUNK
                pltpu.sync_copy(i_hbm.at[pl.ds(cbase, SC_IDX_CHUNK)], idx_v)
                g0 = pltpu.async_copy(x_hbm.at[idx_v.at[pl.ds(0, half)]], buf0, g0_sem)
                g1 = pltpu.async_copy(x_hbm.at[idx_v.at[pl.ds(half, half)]], buf1, g1_sem)
                g0.wait()
                w0 = pltpu.async_copy(buf0, o_hbm.at[pl.ds(cbase, half)], w0_sem)
                g1.wait()
                w1 = pltpu.async_copy(buf1, o_hbm.at[pl.ds(cbase + half, half)], w1_sem)
                w0.wait()
                w1.wait()

    return gather(table, idx)


def _expert_kernel(te_ref, tv_ref, nx_ref, sl_ref, nu_ref, x_ref, wg_hbm, wu_hbm, wd_hbm, o_ref,
                   wg_f, wu_f, wd_f, wg_b, wu_b, wd_b, sems, *, layer):
    i = pl.program_id(0)

    def weight_copies(expert, slot):
        return [pltpu.make_async_copy(hbm.at[layer, expert], buf.at[slot], sems.at[slot, j])
                for j, (hbm, buf) in enumerate(((wg_hbm, wg_f), (wu_hbm, wu_f), (wd_hbm, wd_f)))]

    @pl.when(i == 0)
    def _():
        for cp in weight_copies(te_ref[0], 0):
            cp.start()

    @pl.when((i == 0) | (te_ref[i] != te_ref[jnp.maximum(i - 1, 0)]))
    def _():
        slot = sl_ref[i]
        for cp in weight_copies(te_ref[i], slot):
            cp.wait()
        wg_b[...] = wg_f[slot].astype(bf16)
        wu_b[...] = wu_f[slot].astype(bf16)
        wd_b[...] = wd_f[slot].astype(bf16)

        @pl.when(nx_ref[i] >= 0)
        def _():
            for cp in weight_copies(nx_ref[i], 1 - slot):
                cp.start()

    @pl.when(tv_ref[i] != 0)
    def _():
        lo, hi = _unpack_bf16_pairs(x_ref[...])
        half = lo.shape[1]
        a = (jnp.dot(lo, wg_b[:half], preferred_element_type=f32)
             + jnp.dot(hi, wg_b[half:], preferred_element_type=f32))
        b = (jnp.dot(lo, wu_b[:half], preferred_element_type=f32)
             + jnp.dot(hi, wu_b[half:], preferred_element_type=f32))
        hid = (_silu(a) * b).astype(bf16)
        o_ref[...] = _pack_bf16_pairs(jnp.dot(hid, wd_b[...], preferred_element_type=f32))


def grouped_experts(xs, tile_exp, tile_valid, wg, wu, wd, layer):
    p, half = xs.shape
    d = 2 * half
    n_tiles = p // EXPERT_TILE
    first = jnp.concatenate([jnp.ones((1,), bool), tile_exp[1:] != tile_exp[:-1]])
    slot = (jnp.cumsum(first.astype(i32)) - 1) % 2
    nxt_at = jnp.sum((tile_exp[None, :] <= tile_exp[:, None]).astype(i32), axis=1)
    nxt = jnp.where(nxt_at < n_tiles, tile_exp[jnp.minimum(nxt_at, n_tiles - 1)], -1)
    n_used = jnp.sum(tile_valid).reshape(1)
    tile = pl.BlockSpec((EXPERT_TILE, half), lambda i, te, tv, nx, sl, nu: (jnp.minimum(i, nu[0] - 1), 0))
    hbm = pl.BlockSpec(memory_space=pl.ANY)
    grid_spec = pltpu.PrefetchScalarGridSpec(
        num_scalar_prefetch=5,
        grid=(n_tiles,),
        in_specs=[tile, hbm, hbm, hbm],
        out_specs=tile,
        scratch_shapes=[pltpu.VMEM((2, d, D_EXPERT), f32), pltpu.VMEM((2, d, D_EXPERT), f32),
                        pltpu.VMEM((2, D_EXPERT, d), f32),
                        pltpu.VMEM((d, D_EXPERT), bf16), pltpu.VMEM((d, D_EXPERT), bf16),
                        pltpu.VMEM((D_EXPERT, d), bf16), pltpu.SemaphoreType.DMA((2, 3))],
    )
    return pl.pallas_call(
        functools.partial(_expert_kernel, layer=layer),
        out_shape=jax.ShapeDtypeStruct((p, half), i32),
        grid_spec=grid_spec,
        compiler_params=_params(("arbitrary",)),
        name="grouped_experts",
    )(tile_exp, tile_valid, nxt.astype(i32), slot.astype(i32), n_used.astype(i32), xs, wg, wu, wd)


def _combine_kernel(x_ref, yg_ref, wt_ref, g2_ref, sh2_ref, sc2_ref, n2_ref, wsg_ref, wsu_ref, wsd_ref, fg_ref, o_ref,
                    *, final):
    x = x_ref[...]
    hb = _rms_mod(x, n2_ref[...], sc2_ref[0], sh2_ref[0]).astype(bf16)
    a = jnp.dot(hb, wsg_ref[...], preferred_element_type=f32)
    b = jnp.dot(hb, wsu_ref[...], preferred_element_type=f32)
    f = jnp.dot((_silu(a) * b).astype(bf16), wsd_ref[...], preferred_element_type=f32)
    wt = wt_ref[...]
    half = x.shape[1] // 2
    f_lo, f_hi = f[:, :half], f[:, half:]
    for k in range(TOP_K):
        packed = yg_ref[k]
        w = wt[:, k:k + 1]
        f_lo = f_lo + w * lax.bitcast_convert_type(lax.shift_left(packed, 16), f32)
        f_hi = f_hi + w * lax.bitcast_convert_type(packed & jnp.int32(-65536), f32)
    xo = x + g2_ref[0] * jnp.concatenate([f_lo, f_hi], axis=1)
    if final:
        ms = jnp.mean(xo * xo, axis=-1, keepdims=True)
        xo = xo * lax.rsqrt(ms + RMS_EPS) * fg_ref[...]
    o_ref[...] = xo


def combine(x2, yg, wt_rows, row_offset, mods, rows_per_mod, tile, lw, final_g, final):
    m, d = x2.shape
    off = row_offset // tile
    row = lambda w: pl.BlockSpec((tile, w), lambda i: (i, 0))
    modspec = pl.BlockSpec((1, 1, d), lambda i: ((i * tile) // rows_per_mod, 0, 0))
    return pl.pallas_call(
        functools.partial(_combine_kernel, final=final),
        out_shape=jax.ShapeDtypeStruct((m, d), f32),
        grid=(m // tile,),
        in_specs=[row(d), pl.BlockSpec((TOP_K, tile, d // 2), lambda i: (0, i + off, 0)),
                  pl.BlockSpec((tile, 8), lambda i: (i + off, 0)), modspec, modspec, modspec,
                  _const_spec((1, d)), _const_spec((d, D_EXPERT)), _const_spec((d, D_EXPERT)),
                  _const_spec((D_EXPERT, d)), _const_spec((1, d))],
        out_specs=row(d),
        compiler_params=_params(("parallel",)),
        name="combine",
    )(x2, yg, wt_rows, mods["g2"], mods["sh2"], mods["sc2"], lw["norm2_g"], lw["w_sh_gate"], lw["w_sh_up"],
      lw["w_sh_down"], final_g)


def routed_experts(h2p, eid, rank, counts, lw, layer):
    m = h2p.shape[0]
    p_max = m * TOP_K + N_EXPERTS * EXPERT_TILE
    pos, tile_exp, tile_valid = routing_plan(eid, rank, counts, p_max)
    xs = sc_scatter_rows(h2p, pos, p_max)
    ys = grouped_experts(xs, tile_exp, tile_valid, lw["w_exp_gate"], lw["w_exp_up"], lw["w_exp_down"], layer)
    return sc_gather_rows(ys, pos[:TOP_K].reshape(TOP_K * m)).reshape(TOP_K, m, D_MODEL // 2)


def rope_tables(seq):
    rows = seq // GRID_W
    row = jnp.repeat(jnp.arange(rows), GRID_W).astype(f32)
    col = jnp.tile(jnp.arange(GRID_W), rows).astype(f32)
    axis_dim = HEAD_DIM // 2
    inv_freq = 1.0 / (ROPE_THETA ** (jnp.arange(0, axis_dim, 2, dtype=f32) / axis_dim))
    ang_r = row[:, None] * inv_freq
    ang_c = col[:, None] * inv_freq
    ang = jnp.concatenate([ang_r, ang_r, ang_c, ang_c], axis=-1)
    cos, sin = jnp.cos(ang), jnp.sin(ang)
    seg = (jnp.arange(HEAD_DIM) // 16) % 2
    sa = jnp.where(seg == 0, -sin, 0.0)
    sb = jnp.where(seg == 1, sin, 0.0)
    rep = lambda t: jnp.tile(t, (1, 4))
    return rep(cos), rep(sa), rep(sb)


def _router_weight_pieces(w):
    hi, mid, lo = _split3(w)
    return jnp.concatenate([hi, mid, lo, jnp.zeros_like(hi)], axis=1)


def identity_rope_tables(rows):
    return jnp.ones((rows, 256), f32), jnp.zeros((rows, 256), f32), jnp.zeros((rows, 256), f32)


def kernel(x, c, ctx, c_ctx, w_mod, b_mod, norm1_g, norm2_g, w_in, q_norm_g, k_norm_g, sink, gm_norm_g, gm_ws, gm_b, w_br, w_o, w_router, router_bias, w_exp_gate, w_exp_up, w_exp_down, w_sh_gate, w_sh_up, w_sh_down, final_norm_g):
    bsz_all, seq, d = x.shape
    n_ctx = ctx.shape[1]
    depth = w_mod.shape[0]

    cc = jnp.concatenate([c, c_ctx[None, :], jnp.zeros((MOD_ROWS - bsz_all - 1, d), f32)], axis=0)
    mod_all = compute_mod(cc, w_mod, b_mod)

    lat_tables = rope_tables(seq)
    ctx_tables = identity_rope_tables(n_ctx)
    wc_lat, ws_lat = dft_tables(seq)
    wc_ctx, ws_ctx = dft_tables(n_ctx)
    cs64 = channel_dft_table()
    bd = jnp.asarray(np.kron(np.eye(4), np.full((HEAD_DIM, HEAD_DIM), 1.0 / HEAD_DIM)), dtype=bf16)
    final_g = final_norm_g.reshape(1, d)
    tri = jnp.asarray(np.triu(np.ones((MERGE_TILE, MERGE_TILE)), 1), dtype=bf16)

    lws = []
    for l in range(depth):
        lws.append({
            "norm1_g": norm1_g[l].reshape(1, d),
            "norm2_g": norm2_g[l].reshape(1, d),
            "w_z": w_in[l, :, :OFF_GATE].astype(bf16),
            "w_gate": w_in[l, :, OFF_GATE:].astype(bf16),
            "qn": jnp.tile(q_norm_g[l], 4).reshape(1, 256),
            "kn": jnp.tile(k_norm_g[l], 2).reshape(1, 128),
            "bd": bd,
            "gm_norm_g": gm_norm_g[l].reshape(1, 256),
            "gm_ws": gm_ws[l].astype(bf16),
            "gm_bias": jnp.repeat(gm_b[l].T, 64, axis=1),
            "cs64": cs64,
            "w_br": w_br[l].astype(bf16),
            "w_o": w_o[l].astype(bf16),
            "w_router3": _router_weight_pieces(w_router[l]),
            "router_bias": router_bias[l].reshape(N_EXPERTS, 1),
            "tri": tri,
            "w_exp_gate": w_exp_gate,
            "w_exp_up": w_exp_up,
            "w_exp_down": w_exp_down,
            "w_sh_gate": w_sh_gate[l].astype(bf16),
            "w_sh_up": w_sh_up[l].astype(bf16),
            "w_sh_down": w_sh_down[l].astype(bf16),
        })

    return _layers(x, ctx, mod_all[:, :bsz_all], mod_all[:, bsz_all], lws, sink, lat_tables, ctx_tables,
                   (wc_lat, ws_lat), (wc_ctx, ws_ctx), final_g)


def _layers(x, ctx, mod_lat, mod_ctx, lws, sink, lat_tables, ctx_tables, dft_lat, dft_ctx, final_g):
    bsz, seq, d = x.shape
    n_ctx = ctx.shape[1]
    depth = len(lws)
    n_lat = bsz * seq
    n_cx = bsz * n_ctx
    wc_lat, ws_lat = dft_lat
    wc_ctx, ws_ctx = dft_ctx
    xl = x.reshape(n_lat, d)
    xc = ctx.reshape(n_cx, d)
    for l in range(depth):
        ctx_out = l < depth - 1
        lw = lws[l]
        names = ("sh1", "sc1", "g1", "sh2", "sc2", "g2")
        mods_lat = {n: mod_lat[l, :, i * d:(i + 1) * d].reshape(bsz, 1, d) for i, n in enumerate(names)}
        mods_ctx = {n: mod_ctx[l, i * d:(i + 1) * d].reshape(1, 1, d) for i, n in enumerate(names)}
        sink_l = sink[l]

        qg, kg, vg, qw, kw, vw, fxc, fxs, o_d = in_projection(
            xl, mods_lat["sh1"], mods_lat["sc1"], seq, lat_tables, seq, ROW_TILE, lw)
        cqg, ckg, cvg, cqw, ckw, cvw, cfxc, cfxs, co_d = in_projection(
            xc, mods_ctx["sh1"], mods_ctx["sc1"], n_cx, ctx_tables, n_ctx, n_ctx, lw)
        b3 = lambda t, rows: t.reshape(bsz, rows, t.shape[-1])
        ckg3, cvg3, ckw3, cvw3 = b3(ckg, n_ctx), b3(cvg, n_ctx), b3(ckw, n_ctx), b3(cvw, n_ctx)

        o_a = dft_mix(wc_lat, ws_lat, b3(fxc, seq), b3(fxs, seq), ROW_TILE).reshape(n_lat, BRANCH_W)
        o_b = full_attention(b3(qg, seq), [(ckg3, cvg3), (b3(kg, seq), b3(vg, seq))], None, FULL_ATTN_Q_ROWS)
        o_b = o_b.reshape(n_lat, BRANCH_W)
        o_c = window_attention(b3(qw, seq), b3(kw, seq), b3(vw, seq), ckw3, cvw3, sink_l).reshape(n_lat, BRANCH_W)
        m_total = n_lat + (n_cx if ctx_out else 0)
        route = None
        if ctx_out:
            route = [jnp.zeros((m_total, d // 2), i32), jnp.zeros((8, m_total), i32), jnp.zeros((8, m_total), f32),
                     jnp.zeros((8, m_total), i32), jnp.zeros((N_EXPERTS, 1), i32)]
        xl, *route = merge_and_route(xl, mods_lat, seq, (o_a, o_b, o_c, o_d), MERGE_TILE, lw, m_total, 0, route)
        if ctx_out:
            co_a = dft_mix(wc_ctx, ws_ctx, b3(cfxc, n_ctx), b3(cfxs, n_ctx), n_ctx).reshape(n_cx, BRANCH_W)
            co_b = full_attention(b3(cqg, n_ctx), [(ckg3, cvg3)], None, n_ctx).reshape(n_cx, BRANCH_W)
            co_c = full_attention(b3(cqw, n_ctx), [(ckw3, cvw3)], sink_l, n_ctx).reshape(n_cx, BRANCH_W)
            xc, *route = merge_and_route(xc, mods_ctx, n_cx, (co_a, co_b, co_c, co_d), MERGE_TILE, lw, m_total, n_lat,
                                         route)
        h2p, eid, wt, rank, counts = route
        yg = routed_experts(h2p, eid, rank, counts, lw, l)
        wt_rows = wt.T
        xl = combine(xl, yg, wt_rows, 0, mods_lat, seq, ROW_TILE, lw, final_g, not ctx_out)
        if ctx_out:
            xc = combine(xc, yg, wt_rows, n_lat, mods_ctx, n_cx, ROW_TILE, lw, final_g, False)
    return xl.reshape(bsz, seq, d)
```

```python
import functools
import math

import jax
import jax.numpy as jnp
import numpy as np
from jax import lax
from jax.experimental import pallas as pl
from jax.experimental.pallas import tpu as pltpu
from jax.experimental.pallas import tpu_sc as plsc

f32 = jnp.float32
bf16 = jnp.bfloat16
i32 = jnp.int32

D_MODEL = 1024
HEAD_DIM = 64
GRID_W = 64
ROPE_THETA = 10000.0
ATTN_SCALE = HEAD_DIM ** -0.5
RMS_EPS = 1e-6
NEG_INF = -1e30
Q_BLOCK = 128
WINDOW = 128
GM_CHUNK = 128
N_BRANCHES = 4
BRANCH_W = 256
KV_W = 128
Q_EXP_W = 4 * KV_W
ROPE_SHIFT = HEAD_DIM // 4
GROUP_DIM = 64
OFF_KV = 0
OFF_Q = 512
OFF_FN = 1024
OFF_GM = 1280
OFF_GATE = 1792
N_EXPERTS = 64
TOP_K = 6
N_GROUPS = 8
GROUP_SIZE = N_EXPERTS // N_GROUPS
TOPK_GROUPS = 4
D_EXPERT = 256
ROUTED_SCALE = 2.5

SC_CORES = 2
SC_SUBCORES = 16
SC_WORKERS = SC_CORES * SC_SUBCORES
SC_IDX_CHUNK = 128

ROW_TILE = 1024
MERGE_TILE = 512
ROUTER_AFTER_BRANCHES = 1
RANK_AFTER_BRANCHES = 3
EXPERT_TILE = 1024
FULL_ATTN_Q_ROWS = 1024
FULL_ATTN_ROW_BLOCKS = 4
WINDOW_BLOCKS_PER_STEP = 16
MOD_ROWS = 24
VMEM_LIMIT = 56 * 1024 * 1024


def _params(sem, vmem=VMEM_LIMIT):
    return pltpu.CompilerParams(dimension_semantics=sem, vmem_limit_bytes=vmem)


def _const_spec(shape):
    nd = len(shape)
    return pl.BlockSpec(shape, lambda *_: (0,) * nd, pipeline_mode=pl.Buffered(1))


def _rms_mod(x, g, sc, sh):
    ms = jnp.mean(x * x, axis=-1, keepdims=True)
    return (x * lax.rsqrt(ms + RMS_EPS) * g) * (1.0 + sc) + sh


def _gelu(x):
    return 0.5 * x * (1.0 + jnp.tanh(math.sqrt(2.0 / math.pi) * (x + 0.044715 * (x * x * x))))


def _silu(x):
    return x * jax.nn.sigmoid(x)


def _mod_kernel(a_ref, w_ref, b_ref, o_ref):
    a = _silu(a_ref[...]).astype(bf16)
    o_ref[0] = jnp.dot(a, w_ref[0].astype(bf16), preferred_element_type=f32) + b_ref[0]


def compute_mod(cc, w_mod, b_mod):
    depth, d, n = w_mod.shape
    tn = 1536
    return pl.pallas_call(
        _mod_kernel,
        out_shape=jax.ShapeDtypeStruct((depth, MOD_ROWS, n), f32),
        grid=(depth, n // tn),
        in_specs=[
            pl.BlockSpec((MOD_ROWS, d), lambda l, j: (0, 0)),
            pl.BlockSpec((1, d, tn), lambda l, j: (l, 0, j)),
            pl.BlockSpec((1, 1, tn), lambda l, j: (l, 0, j)),
        ],
        out_specs=pl.BlockSpec((1, MOD_ROWS, tn), lambda l, j: (l, 0, j)),
        compiler_params=_params(("parallel", "parallel")),
        name="mod_proj",
    )(cc, w_mod, b_mod.reshape(depth, 1, n))


def _inproj_kernel(x_ref, sh_ref, sc_ref, g_ref, w_ref, qn_ref, kn_ref, bd_ref, gmg_ref, ws_ref, gb_ref, cs_ref,
                   cos_ref, sa_ref, sb_ref,
                   qg_ref, kg_ref, vg_ref, qw_ref, kw_ref, vw_ref, xc_ref, xs_ref, od_ref):
    tile = x_ref.shape[0]
    hb = _rms_mod(x_ref[...], g_ref[...], sc_ref[0], sh_ref[0]).astype(bf16)

    def proj(a, b):
        return jnp.dot(hb, w_ref[:, a:b], preferred_element_type=f32)

    def square_pieces(t):
        sq = t * t
        hi = sq.astype(bf16)
        return hi, (sq - hi.astype(f32)).astype(bf16)

    def headnorm(t, pieces, gain):
        w = t.shape[1]
        b = bd_ref[:w, :w]
        ms = (jnp.dot(pieces[0], b, preferred_element_type=f32) + jnp.dot(pieces[1], b, preferred_element_type=f32))
        return t * lax.rsqrt(ms + RMS_EPS) * gain

    def rope(t):
        w = t.shape[1]
        return (t * cos_ref[:, :w] + pltpu.roll(t, w - ROPE_SHIFT, 1) * sa_ref[:, :w]
                + pltpu.roll(t, ROPE_SHIFT, 1) * sb_ref[:, :w])

    def expand_heads(q):
        lane = lax.broadcasted_iota(i32, (1, KV_W), 1)
        low = lane < HEAD_DIM
        blocks = []
        for kv in range(2):
            pair = q[:, KV_W * kv:KV_W * (kv + 1)]
            swapped = pltpu.roll(pair, HEAD_DIM, 1)
            keep = low if kv == 0 else jnp.logical_not(low)
            g0, g1 = (pair, swapped) if kv == 0 else (swapped, pair)
            blocks.append(jnp.where(keep, g0, 0.0))
            blocks.append(jnp.where(keep, g1, 0.0))
        return jnp.concatenate(blocks, axis=1)

    kv = proj(OFF_KV, OFF_Q)
    qq = proj(OFF_Q, OFF_FN)
    fn = proj(OFF_FN, OFF_GM).astype(bf16)
    uv = proj(OFF_GM, OFF_GATE)

    k_sq = square_pieces(kv[:, :KV_W])
    q_sq = square_pieces(qq[:, :BRANCH_W])
    u = _gelu(uv[:, :BRANCH_W])
    v = _gelu(uv[:, BRANCH_W:])
    vms = jnp.mean(v * v, axis=-1, keepdims=True)
    vn = (v * lax.rsqrt(vms + RMS_EPS) * gmg_ref[...]).astype(bf16)

    kg_ref[...] = rope(headnorm(kv[:, :KV_W], k_sq, kn_ref[...])).astype(bf16)
    vg_ref[...] = kv[:, KV_W:2 * KV_W].astype(bf16)
    kw_ref[...] = rope(kv[:, 2 * KV_W:3 * KV_W]).astype(bf16)
    vw_ref[...] = kv[:, 3 * KV_W:].astype(bf16)

    qg = rope(headnorm(qq[:, :BRANCH_W], q_sq, qn_ref[...])) * ATTN_SCALE
    qg_ref[...] = expand_heads(qg).astype(bf16)
    qw = rope(qq[:, BRANCH_W:]) * ATTN_SCALE
    qw_ref[...] = expand_heads(qw).astype(bf16)

    xcs = jnp.dot(fn, cs_ref[...], preferred_element_type=f32)
    xc_ref[...] = xcs[:, :BRANCH_W].astype(bf16)
    xs_ref[...] = xcs[:, BRANCH_W:].astype(bf16)

    lane_grp = lax.broadcasted_iota(i32, (1, BRANCH_W), 1) // GROUP_DIM
    for c in range(tile // GM_CHUNK):
        rows = slice(c * GM_CHUNK, (c + 1) * GM_CHUNK)
        vch = vn[rows]
        sv = gb_ref[...]
        for g in range(4):
            r = jnp.dot(ws_ref[g], vch, preferred_element_type=f32)
            sv = sv + jnp.where(lane_grp == g, r, 0.0)
        od_ref[rows, :] = (u[rows] * sv).astype(bf16)


def in_projection(x2, sh, sc, rows_per_mod, tables, rows_per_seq, tile, lw):
    m, d = x2.shape
    cos_t, sa_t, sb_t = tables
    seq_blocks = rows_per_seq // tile
    row = lambda w: pl.BlockSpec((tile, w), lambda i: (i, 0))
    modspec = pl.BlockSpec((1, 1, d), lambda i: ((i * tile) // rows_per_mod, 0, 0))
    tabspec = pl.BlockSpec((tile, BRANCH_W), lambda i: (i % seq_blocks, 0))
    out_w = [Q_EXP_W, KV_W, KV_W, Q_EXP_W, KV_W, KV_W, BRANCH_W, BRANCH_W, BRANCH_W]
    return pl.pallas_call(
        _inproj_kernel,
        out_shape=[jax.ShapeDtypeStruct((m, w), bf16) for w in out_w],
        grid=(m // tile,),
        in_specs=[
            row(d), modspec, modspec, _const_spec((1, d)), _const_spec((d, OFF_GATE)),
            _const_spec((1, BRANCH_W)), _const_spec((1, KV_W)), _const_spec((BRANCH_W, BRANCH_W)),
            _const_spec((1, BRANCH_W)), _const_spec((4, GM_CHUNK, GM_CHUNK)), _const_spec((GM_CHUNK, BRANCH_W)),
            _const_spec((BRANCH_W, 2 * BRANCH_W)),
            tabspec, tabspec, tabspec,
        ],
        out_specs=[row(w) for w in out_w],
        compiler_params=_params(("parallel",)),
        name="in_projection",
    )(x2, sh, sc, lw["norm1_g"], lw["w_z"], lw["qn"], lw["kn"], lw["bd"], lw["gm_norm_g"], lw["gm_ws"], lw["gm_bias"],
      lw["cs64"], cos_t, sa_t, sb_t)


def _dft_kernel(wc_ref, ws_ref, xc_ref, xs_ref, o_ref):
    acc = jnp.dot(wc_ref[...], xc_ref[0], preferred_element_type=f32)
    acc = acc + jnp.dot(ws_ref[...], xs_ref[0], preferred_element_type=f32)
    o_ref[0] = acc.astype(bf16)


def dft_mix(wc, ws, xc, xs, tile):
    nb, length, w = xc.shape
    return pl.pallas_call(
        _dft_kernel,
        out_shape=jax.ShapeDtypeStruct((nb, length, w), bf16),
        grid=(length // tile, nb),
        in_specs=[
            pl.BlockSpec((tile, length), lambda i, b: (i, 0)),
            pl.BlockSpec((tile, length), lambda i, b: (i, 0)),
            pl.BlockSpec((1, length, w), lambda i, b: (b, 0, 0)),
            pl.BlockSpec((1, length, w), lambda i, b: (b, 0, 0)),
        ],
        out_specs=pl.BlockSpec((1, tile, w), lambda i, b: (b, i, 0)),
        compiler_params=_params(("parallel", "parallel")),
        name="dft_mix",
    )(wc, ws, xc, xs)


def dft_tables(length):
    jk = (np.arange(length)[:, None] * np.arange(length)[None, :]) % length
    ang = 2.0 * np.pi * jk / length
    s = 1.0 / math.sqrt(length)
    return jnp.asarray(np.cos(ang) * s, dtype=bf16), jnp.asarray(-np.sin(ang) * s, dtype=bf16)


def channel_dft_table():
    jk = (np.arange(64)[:, None] * np.arange(64)[None, :]) % 64
    ang = 2.0 * np.pi * jk / 64
    eye = np.eye(4)
    c = np.kron(eye, np.cos(ang) / 8.0)
    s = np.kron(eye, np.sin(ang) / 8.0)
    return jnp.asarray(np.concatenate([c, s], axis=1), dtype=bf16)


def _attend_blocks(blocks, sink_ref):
    lane = lax.broadcasted_iota(i32, (1, KV_W), 1)
    low = lane < HEAD_DIM
    units = []
    for q_all, pieces in blocks:
        qb = q_all.shape[0]
        for kv in range(2):
            q = jnp.concatenate([q_all[:, KV_W * (2 * kv):KV_W * (2 * kv + 1)],
                                 q_all[:, KV_W * (2 * kv + 1):KV_W * (2 * kv + 2)]], axis=0)
            sink_col = None
            if sink_ref is not None:
                sink_col = jnp.concatenate([jnp.full((qb, 1), sink_ref[2 * kv], f32),
                                            jnp.full((qb, 1), sink_ref[2 * kv + 1], f32)], axis=0)
            units.append((q, pieces, sink_col, low if kv == 0 else jnp.logical_not(low)))

    scores = []
    for q, pieces, _, _ in units:
        unit_scores = []
        for k, _, mask in pieces:
            s = lax.dot_general(q, k, (((1,), (1,)), ((), ())), preferred_element_type=f32)
            unit_scores.append(s if mask is None else jnp.where(mask, s, NEG_INF))
        scores.append(unit_scores)

    maxes = []
    for (_, _, sink_col, _), unit_scores in zip(units, scores):
        m = unit_scores[0].max(axis=-1, keepdims=True)
        for s in unit_scores[1:]:
            m = jnp.maximum(m, s.max(axis=-1, keepdims=True))
        maxes.append(m if sink_col is None else jnp.maximum(m, sink_col))

    probs = [[jnp.exp((s - m).astype(bf16)) for s in unit_scores] for unit_scores, m in zip(scores, maxes)]

    results = []
    for (_, pieces, sink_col, own), unit_probs, m in zip(units, probs, maxes):
        acc = None
        for p, (_, v, _) in zip(unit_probs, pieces):
            pv = jnp.dot(p, jnp.where(own, v, jnp.ones_like(v)), preferred_element_type=f32)
            acc = pv if acc is None else acc + pv
        denom = pltpu.roll(acc, HEAD_DIM, 1)
        if sink_col is not None:
            denom = denom + jnp.exp(sink_col - m)
        results.append(acc * (1.0 / denom))

    outs = []
    for i, (q_all, _) in enumerate(blocks):
        qb = q_all.shape[0]
        r_kv0, r_kv1 = results[2 * i], results[2 * i + 1]
        lo = jnp.where(low, r_kv0[:qb], pltpu.roll(r_kv0[qb:], HEAD_DIM, 1))
        hi = jnp.where(low, pltpu.roll(r_kv1[:qb], HEAD_DIM, 1), r_kv1[qb:])
        outs.append(jnp.concatenate([lo, hi], axis=1))
    return outs


def _full_attn_kernel(*refs, n_pieces, has_sink, row_blocks):
    pos = 0
    sink_ref = None
    if has_sink:
        sink_ref = refs[0]
        pos = 1
    q_ref = refs[pos]
    kv_refs = refs[pos + 1:pos + 1 + 2 * n_pieces]
    o_ref = refs[pos + 1 + 2 * n_pieces]
    pieces = [(kv_refs[2 * i][0], kv_refs[2 * i + 1][0], None) for i in range(n_pieces)]
    rows = q_ref.shape[1] // row_blocks
    blocks = [(q_ref[0, j * rows:(j + 1) * rows, :], pieces) for j in range(row_blocks)]
    for j, out in enumerate(_attend_blocks(blocks, sink_ref)):
        o_ref[0, j * rows:(j + 1) * rows, :] = out.astype(bf16)


def full_attention(q, pieces, sink, qb):
    nb, lq, _ = q.shape
    in_specs = []
    args = []
    if sink is not None:
        in_specs.append(pl.BlockSpec(memory_space=pltpu.SMEM))
        args.append(sink)
    in_specs.append(pl.BlockSpec((1, qb, Q_EXP_W), lambda b, i: (b, i, 0)))
    args.append(q)
    for k, v in pieces:
        spec = pl.BlockSpec((1, k.shape[1], KV_W), lambda b, i: (b, 0, 0))
        in_specs += [spec, spec]
        args += [k, v]
    return pl.pallas_call(
        functools.partial(_full_attn_kernel, n_pieces=len(pieces), has_sink=sink is not None,
                          row_blocks=FULL_ATTN_ROW_BLOCKS),
        out_shape=jax.ShapeDtypeStruct((nb, lq, BRANCH_W), bf16),
        grid=(nb, lq // qb),
        in_specs=in_specs,
        out_specs=pl.BlockSpec((1, qb, BRANCH_W), lambda b, i: (b, i, 0)),
        compiler_params=_params(("parallel", "parallel")),
        name="full_attention",
    )(*args)


def _window_attn_kernel(sink_ref, q_ref, k_ref, v_ref, kc_ref, vc_ref, o_ref, *, seq, blocks):
    span = 3 * Q_BLOCK
    ctx_piece = (kc_ref[0], vc_ref[0], None)
    work = []
    for j in range(blocks):
        n = pl.program_id(1) * blocks + j
        start = pl.multiple_of(jnp.clip((n - 1) * Q_BLOCK, 0, seq - span), Q_BLOCK)
        kwin = k_ref[0, pl.ds(start, span), :]
        vwin = v_ref[0, pl.ds(start, span), :]
        row = lax.broadcasted_iota(i32, (2 * Q_BLOCK, span), 0) % Q_BLOCK + n * Q_BLOCK
        col = lax.broadcasted_iota(i32, (2 * Q_BLOCK, span), 1) + start
        mask = jnp.abs(row - col) <= WINDOW
        work.append((q_ref[0, j * Q_BLOCK:(j + 1) * Q_BLOCK, :], [ctx_piece, (kwin, vwin, mask)]))
    for j, out in enumerate(_attend_blocks(work, sink_ref)):
        o_ref[0, j * Q_BLOCK:(j + 1) * Q_BLOCK, :] = out.astype(bf16)


def window_attention(q, k, v, kc, vc, sink):
    nb, seq, _ = q.shape
    n_ctx = kc.shape[1]
    blocks = WINDOW_BLOCKS_PER_STEP
    assert seq % (Q_BLOCK * blocks) == 0 and seq >= 3 * Q_BLOCK, seq
    full = lambda l: pl.BlockSpec((1, l, KV_W), lambda b, i: (b, 0, 0))
    return pl.pallas_call(
        functools.partial(_window_attn_kernel, seq=seq, blocks=blocks),
        out_shape=jax.ShapeDtypeStruct((nb, seq, BRANCH_W), bf16),
        grid=(nb, seq // (Q_BLOCK * blocks)),
        in_specs=[pl.BlockSpec(memory_space=pltpu.SMEM),
                  pl.BlockSpec((1, Q_BLOCK * blocks, Q_EXP_W), lambda b, i: (b, i, 0)),
                  full(seq), full(seq), full(n_ctx), full(n_ctx)],
        out_specs=pl.BlockSpec((1, Q_BLOCK * blocks, BRANCH_W), lambda b, i: (b, i, 0)),
        compiler_params=_params(("parallel", "parallel")),
        name="window_attention",
    )(sink, q, k, v, kc, vc)


def _route(logits_t, bias_col):
    t = logits_t.shape[1]
    scores = jax.nn.sigmoid(logits_t)
    choice = scores + bias_col
    sub = lax.broadcasted_iota(i32, (GROUP_SIZE, t), 0)
    grp_score = []
    for g in range(N_GROUPS):
        cg = choice[g * GROUP_SIZE:(g + 1) * GROUP_SIZE]
        m1 = cg.max(axis=0, keepdims=True)
        first = jnp.min(jnp.where(cg == m1, sub, GROUP_SIZE), axis=0, keepdims=True)
        m2 = jnp.where(sub == first, -jnp.inf, cg).max(axis=0, keepdims=True)
        grp_score.append(m1 + m2)
    keep = []
    for g in range(N_GROUPS):
        beaten = jnp.zeros((1, t), i32)
        for o in range(N_GROUPS):
            if o == g:
                continue
            wins = (grp_score[o] > grp_score[g]) | ((grp_score[o] == grp_score[g]) & (o < g))
            beaten = beaten + wins.astype(i32)
        keep.append(jnp.broadcast_to(beaten < TOPK_GROUPS, (GROUP_SIZE, t)))
    masked = jnp.where(jnp.concatenate(keep, axis=0), choice, NEG_INF)
    eid = lax.broadcasted_iota(i32, (N_EXPERTS, t), 0)
    ids, wts = [], []
    for _ in range(TOP_K):
        m = masked.max(axis=0, keepdims=True)
        pick = jnp.min(jnp.where(masked == m, eid, N_EXPERTS), axis=0, keepdims=True)
        sel = eid == pick
        ids.append(pick)
        wts.append(jnp.sum(jnp.where(sel, scores, 0.0), axis=0, keepdims=True))
        masked = jnp.where(sel, -jnp.inf, masked)
    total = wts[0]
    for w in wts[1:]:
        total = total + w
    norm = ROUTED_SCALE / total
    return ids, [w * norm for w in wts]


def _pack_bf16_pairs(x):
    w = x.shape[1] // 2
    lo = lax.bitcast_convert_type(x[:, :w].astype(bf16).astype(f32), i32)
    hi = lax.bitcast_convert_type(x[:, w:].astype(bf16).astype(f32), i32)
    return lax.shift_right_logical(lo, 16) | (hi & jnp.int32(-65536))


def _unpack_bf16_pairs(p):
    lo = lax.bitcast_convert_type(lax.shift_left(p, 16), f32)
    hi = lax.bitcast_convert_type(p & jnp.int32(-65536), f32)
    return lo.astype(bf16), hi.astype(bf16)


def _merge_kernel(x_ref, sh_ref, sc_ref, g1_ref, sh2_ref, sc2_ref, n1_ref, n2_ref, oa_ref, ob_ref, oc_ref, od_ref,
                  wg_ref, wbr_ref, wo_ref, wr_ref, rb_ref, tri_ref, *rest, extends):
    if extends:
        cnt_in_ref = rest[0]
        rest = rest[5:]
    xo_ref, h2_ref, eid_ref, wt_ref, rank_ref, cnt_ref, run_ref, xn_prev_ref = rest
    step = pl.program_id(0)

    @pl.when(step == 0)
    def _():
        run_ref[...] = cnt_in_ref[...].astype(f32) if extends else jnp.zeros_like(run_ref)
        xn_prev_ref[...] = jnp.zeros_like(xn_prev_ref)

    x = x_ref[...]
    hb = _rms_mod(x, n1_ref[...], sc_ref[0], sh_ref[0]).astype(bf16)
    h2 = _rms_mod(xn_prev_ref[...], n2_ref[...], sc2_ref[0], sh2_ref[0])
    h2_ref[...] = _pack_bf16_pairs(h2)
    y = None
    for i, o_ref in enumerate((oa_ref, ob_ref, oc_ref, od_ref)):
        if i == ROUTER_AFTER_BRANCHES:
            ids, wts = _route(_router_logits(wr_ref[...], h2), rb_ref[...])
        if i == RANK_AFTER_BRANCHES:
            _rank_entries(ids, wts, tri_ref, run_ref, (step > 0).astype(f32), eid_ref, wt_ref, rank_ref, cnt_ref)
        logit = jnp.dot(hb, wg_ref[:, i * D_MODEL:(i + 1) * D_MODEL], preferred_element_type=f32)
        proj = jnp.dot(o_ref[...], wbr_ref[i], preferred_element_type=f32)
        term = jax.nn.sigmoid(logit.astype(bf16)) * proj.astype(bf16)
        y = term if y is None else y + term
    xn = x + g1_ref[0] * jnp.dot(y, wo_ref[...], preferred_element_type=f32)
    xo_ref[...] = xn
    xn_prev_ref[...] = xn


def _split3(x):
    def head(v):
        return lax.bitcast_convert_type(lax.bitcast_convert_type(v, i32) & jnp.int32(-65536), f32)

    hi = head(x)
    r1 = x - hi
    mid = head(r1)
    lo = r1 - mid
    return hi.astype(bf16), mid.astype(bf16), lo.astype(bf16)


def _router_logits(w3, h2):
    prod = None
    for piece in _split3(h2):
        p = jnp.dot(piece, w3, preferred_element_type=f32)
        prod = p if prod is None else prod + p
    lane = lax.broadcasted_iota(i32, (1, 2 * N_EXPERTS), 1)
    low = prod[:, :2 * N_EXPERTS]
    logits = low + pltpu.roll(low, N_EXPERTS, 1) + prod[:, 2 * N_EXPERTS:]
    return jnp.where(lane < N_EXPERTS, logits, 0.0).T[:N_EXPERTS]


def _rank_entries(ids, wts, tri_ref, run_ref, live, eid_ref, wt_ref, rank_ref, cnt_ref):
    t = ids[0].shape[1]
    eid = lax.broadcasted_iota(i32, (N_EXPERTS, t), 0)
    hits = [eid == pick for pick in ids]
    chosen = hits[0]
    for h in hits[1:]:
        chosen = chosen | h
    chosen = jnp.where(chosen, 1.0, 0.0)
    prefix = jnp.dot(chosen.astype(bf16), tri_ref[...], preferred_element_type=f32)
    offset = run_ref[...] + prefix
    ranks = [jnp.sum(jnp.where(h, offset, 0.0), axis=0, keepdims=True).astype(i32) for h in hits]
    run_ref[...] += live * jnp.sum(chosen, axis=1, keepdims=True)
    cnt_ref[...] = run_ref[...].astype(i32)

    pad_i = [jnp.zeros((1, t), i32)] * (8 - TOP_K)
    eid_ref[...] = jnp.concatenate(ids + pad_i, axis=0)
    rank_ref[...] = jnp.concatenate(ranks + pad_i, axis=0)
    wt_ref[...] = jnp.concatenate(wts + [jnp.zeros((1, t), f32)] * (8 - TOP_K), axis=0)


def merge_and_route(x2, mods, rows_per_mod, branches, tile, lw, m_total, row_offset, prior):
    m, d = x2.shape
    off = row_offset // tile
    n_tiles = m // tile
    cur = lambda i: jnp.minimum(i, n_tiles - 1)
    prev = lambda i: jnp.maximum(i - 1, 0)
    row = lambda w: pl.BlockSpec((tile, w), lambda i: (cur(i), 0))
    row_prev = lambda w: pl.BlockSpec((tile, w), lambda i: (prev(i) + off, 0))
    modspec = pl.BlockSpec((1, 1, d), lambda i: ((cur(i) * tile) // rows_per_mod, 0, 0))
    modspec_prev = pl.BlockSpec((1, 1, d), lambda i: ((prev(i) * tile) // rows_per_mod, 0, 0))
    col = pl.BlockSpec((8, tile), lambda i: (0, prev(i) + off))
    in_specs = [row(d)] + [modspec] * 3 + [modspec_prev] * 2 + [_const_spec((1, d)), _const_spec((1, d))] + [
        row(BRANCH_W)] * 4 + [
        _const_spec((d, N_BRANCHES * d)), _const_spec((N_BRANCHES, BRANCH_W, d)), _const_spec((d, d)),
        _const_spec((d, 4 * N_EXPERTS)), _const_spec((N_EXPERTS, 1)), _const_spec((tile, tile))]
    args = [x2, mods["sh1"], mods["sc1"], mods["g1"], mods["sh2"], mods["sc2"], lw["norm1_g"], lw["norm2_g"],
            *branches, lw["w_gate"], lw["w_br"], lw["w_o"], lw["w_router3"], lw["router_bias"], lw["tri"]]
    aliases = {}
    if prior is not None:
        h2p, eid, wt, rank, counts = prior
        n_in = len(args)
        in_specs += [_const_spec((N_EXPERTS, 1))] + [pl.BlockSpec(memory_space=pl.ANY)] * 4
        args += [counts, h2p, eid, wt, rank]
        aliases = {n_in + 1 + j: 1 + j for j in range(4)}
    return pl.pallas_call(
        functools.partial(_merge_kernel, extends=prior is not None),
        out_shape=[jax.ShapeDtypeStruct((m, d), f32), jax.ShapeDtypeStruct((m_total, d // 2), i32),
                   jax.ShapeDtypeStruct((8, m_total), i32), jax.ShapeDtypeStruct((8, m_total), f32),
                   jax.ShapeDtypeStruct((8, m_total), i32), jax.ShapeDtypeStruct((N_EXPERTS, 1), i32)],
        grid=(n_tiles + 1,),
        in_specs=in_specs,
        out_specs=[row(d), row_prev(d // 2), col, col, col, pl.BlockSpec((N_EXPERTS, 1), lambda i: (0, 0))],
        scratch_shapes=[pltpu.VMEM((N_EXPERTS, 1), f32), pltpu.VMEM((tile, d), f32)],
        input_output_aliases=aliases,
        compiler_params=_params(("arbitrary",)),
        name="merge_and_route",
    )(*args)


def routing_plan(eid, rank, counts, p_max):
    counts = counts.reshape(N_EXPERTS)
    padded = ((counts + EXPERT_TILE - 1) // EXPERT_TILE) * EXPERT_TILE
    ends = jnp.cumsum(padded)
    starts = ends - padded
    onehot = eid[:, :, None] == jnp.arange(N_EXPERTS, dtype=i32)[None, None, :]
    pos = rank + jnp.sum(jnp.where(onehot, starts[None, None, :], 0), axis=-1)
    n_tiles = p_max // EXPERT_TILE
    tile_start = jnp.arange(n_tiles, dtype=i32) * EXPERT_TILE
    tile_valid = tile_start < ends[-1]
    tile_exp = jnp.sum((ends[None, :] <= tile_start[:, None]).astype(i32), axis=1)
    return pos.astype(i32), jnp.minimum(tile_exp, N_EXPERTS - 1), tile_valid.astype(i32)


def _sc_worker_id():
    return lax.axis_index("subcore") * SC_CORES + lax.axis_index("core")


def sc_scatter_rows(table, pos, p_rows):
    m, w = table.shape
    n_chunks = m // SC_IDX_CHUNK
    steps = -(-n_chunks // SC_WORKERS)
    pos3 = pos.reshape(8, n_chunks, SC_IDX_CHUNK).transpose(1, 0, 2)
    mesh = plsc.VectorSubcoreMesh(core_axis_name="core", subcore_axis_name="subcore")

    @functools.partial(
        pl.kernel,
        out_type=jax.ShapeDtypeStruct((p_rows, w), table.dtype),
        mesh=mesh,
        scratch_types=[
            pltpu.VMEM((8, SC_IDX_CHUNK), i32),
            pltpu.VMEM((SC_IDX_CHUNK, w), table.dtype),
            pltpu.SemaphoreType.DMA,
        ],
    )
    def scatter(x_hbm, p_hbm, o_hbm, idx_v, rows_v, sem):
        wid = _sc_worker_id()

        @pl.loop(0, steps)
        def _(si):
            chunk = si * SC_WORKERS + wid

            @pl.when(chunk < n_chunks)
            def _():
                pltpu.sync_copy(p_hbm.at[chunk], idx_v)
                pltpu.sync_copy(x_hbm.at[pl.ds(chunk * SC_IDX_CHUNK, SC_IDX_CHUNK)], rows_v)
                copies = [pltpu.async_copy(rows_v, o_hbm.at[idx_v.at[k]], sem) for k in range(TOP_K)]
                for cp in copies:
                    cp.wait()

    return scatter(table, pos3)


def sc_gather_rows(table, idx):
    n_idx = idx.shape[0]
    w = table.shape[1]
    n_chunks = n_idx // SC_IDX_CHUNK
    steps = -(-n_chunks // SC_WORKERS)
    half = SC_IDX_CHUNK // 2
    mesh = plsc.VectorSubcoreMesh(core_axis_name="core", subcore_axis_name="subcore")

    @functools.partial(
        pl.kernel,
        out_type=jax.ShapeDtypeStruct((n_idx, w), table.dtype),
        mesh=mesh,
        scratch_types=[
            pltpu.VMEM((SC_IDX_CHUNK,), i32),
            pltpu.VMEM((half, w), table.dtype),
            pltpu.VMEM((half, w), table.dtype),
            pltpu.SemaphoreType.DMA,
            pltpu.SemaphoreType.DMA,
            pltpu.SemaphoreType.DMA,
            pltpu.SemaphoreType.DMA,
        ],
    )
    def gather(x_hbm, i_hbm, o_hbm, idx_v, buf0, buf1, g0_sem, g1_sem, w0_sem, w1_sem):
        wid = _sc_worker_id()

        @pl.loop(0, steps)
        def _(si):
            chunk = si * SC_WORKERS + wid

            @pl.when(chunk < n_chunks)
            def _():
                cbase = chunk * SC_IDX_CHUNK
                pltpu.sync_copy(i_hbm.at[pl.ds(cbase, SC_IDX_CHUNK)], idx_v)
                g0 = pltpu.async_copy(x_hbm.at[idx_v.at[pl.ds(0, half)]], buf0, g0_sem)
                g1 = pltpu.async_copy(x_hbm.at[idx_v.at[pl.ds(half, half)]], buf1, g1_sem)
                g0.wait()
                w0 = pltpu.async_copy(buf0, o_hbm.at[pl.ds(cbase, half)], w0_sem)
                g1.wait()
                w1 = pltpu.async_copy(buf1, o_hbm.at[pl.ds(cbase + half, half)], w1_sem)
                w0.wait()
                w1.wait()

    return gather(table, idx)


def _expert_kernel(te_ref, tv_ref, nx_ref, sl_ref, nu_ref, x_ref, wg_hbm, wu_hbm, wd_hbm, o_ref,
                   wg_f, wu_f, wd_f, wg_b, wu_b, wd_b, sems, *, layer):
    i = pl.program_id(0)

    def weight_copies(expert, slot):
        return [pltpu.make_async_copy(hbm.at[layer, expert], buf.at[slot], sems.at[slot, j])
                for j, (hbm, buf) in enumerate(((wg_hbm, wg_f), (wu_hbm, wu_f), (wd_hbm, wd_f)))]

    @pl.when(i == 0)
    def _():
        for cp in weight_copies(te_ref[0], 0):
            cp.start()

    @pl.when((i == 0) | (te_ref[i] != te_ref[jnp.maximum(i - 1, 0)]))
    def _():
        slot = sl_ref[i]
        for cp in weight_copies(te_ref[i], slot):
            cp.wait()
        wg_b[...] = wg_f[slot].astype(bf16)
        wu_b[...] = wu_f[slot].astype(bf16)
        wd_b[...] = wd_f[slot].astype(bf16)

        @pl.when(nx_ref[i] >= 0)
        def _():
            for cp in weight_copies(nx_ref[i], 1 - slot):
                cp.start()

    @pl.when(tv_ref[i] != 0)
    def _():
        lo, hi = _unpack_bf16_pairs(x_ref[...])
        half = lo.shape[1]
        a = (jnp.dot(lo, wg_b[:half], preferred_element_type=f32)
             + jnp.dot(hi, wg_b[half:], preferred_element_type=f32))
        b = (jnp.dot(lo, wu_b[:half], preferred_element_type=f32)
             + jnp.dot(hi, wu_b[half:], preferred_element_type=f32))
        hid = (_silu(a) * b).astype(bf16)
        o_ref[...] = _pack_bf16_pairs(jnp.dot(hid, wd_b[...], preferred_element_type=f32))


def grouped_experts(xs, tile_exp, tile_valid, wg, wu, wd, layer):
    p, half = xs.shape
    d = 2 * half
    n_tiles = p // EXPERT_TILE
    first = jnp.concatenate([jnp.ones((1,), bool), tile_exp[1:] != tile_exp[:-1]])
    slot = (jnp.cumsum(first.astype(i32)) - 1) % 2
    nxt_at = jnp.sum((tile_exp[None, :] <= tile_exp[:, None]).astype(i32), axis=1)
    nxt = jnp.where(nxt_at < n_tiles, tile_exp[jnp.minimum(nxt_at, n_tiles - 1)], -1)
    n_used = jnp.sum(tile_valid).reshape(1)
    tile = pl.BlockSpec((EXPERT_TILE, half), lambda i, te, tv, nx, sl, nu: (jnp.minimum(i, nu[0] - 1), 0))
    hbm = pl.BlockSpec(memory_space=pl.ANY)
    grid_spec = pltpu.PrefetchScalarGridSpec(
        num_scalar_prefetch=5,
        grid=(n_tiles,),
        in_specs=[tile, hbm, hbm, hbm],
        out_specs=tile,
        scratch_shapes=[pltpu.VMEM((2, d, D_EXPERT), f32), pltpu.VMEM((2, d, D_EXPERT), f32),
                        pltpu.VMEM((2, D_EXPERT, d), f32),
                        pltpu.VMEM((d, D_EXPERT), bf16), pltpu.VMEM((d, D_EXPERT), bf16),
                        pltpu.VMEM((D_EXPERT, d), bf16), pltpu.SemaphoreType.DMA((2, 3))],
    )
    return pl.pallas_call(
        functools.partial(_expert_kernel, layer=layer),
        out_shape=jax.ShapeDtypeStruct((p, half), i32),
        grid_spec=grid_spec,
        compiler_params=_params(("arbitrary",)),
        name="grouped_experts",
    )(tile_exp, tile_valid, nxt.astype(i32), slot.astype(i32), n_used.astype(i32), xs, wg, wu, wd)


def _combine_kernel(x_ref, yg_ref, wt_ref, g2_ref, sh2_ref, sc2_ref, n2_ref, wsg_ref, wsu_ref, wsd_ref, fg_ref, o_ref,
                    *, final):
    x = x_ref[...]
    hb = _rms_mod(x, n2_ref[...], sc2_ref[0], sh2_ref[0]).astype(bf16)
    a = jnp.dot(hb, wsg_ref[...], preferred_element_type=f32)
    b = jnp.dot(hb, wsu_ref[...], preferred_element_type=f32)
    f = jnp.dot((_silu(a) * b).astype(bf16), wsd_ref[...], preferred_element_type=f32)
    wt = wt_ref[...]
    half = x.shape[1] // 2
    f_lo, f_hi = f[:, :half], f[:, half:]
    for k in range(TOP_K):
        packed = yg_ref[k]
        w = wt[:, k:k + 1]
        f_lo = f_lo + w * lax.bitcast_convert_type(lax.shift_left(packed, 16), f32)
        f_hi = f_hi + w * lax.bitcast_convert_type(packed & jnp.int32(-65536), f32)
    xo = x + g2_ref[0] * jnp.concatenate([f_lo, f_hi], axis=1)
    if final:
        ms = jnp.mean(xo * xo, axis=-1, keepdims=True)
        xo = xo * lax.rsqrt(ms + RMS_EPS) * fg_ref[...]
    o_ref[...] = xo


def combine(x2, yg, wt_rows, row_offset, mods, rows_per_mod, tile, lw, final_g, final):
    m, d = x2.shape
    off = row_offset // tile
    row = lambda w: pl.BlockSpec((tile, w), lambda i: (i, 0))
    modspec = pl.BlockSpec((1, 1, d), lambda i: ((i * tile) // rows_per_mod, 0, 0))
    return pl.pallas_call(
        functools.partial(_combine_kernel, final=final),
        out_shape=jax.ShapeDtypeStruct((m, d), f32),
        grid=(m // tile,),
        in_specs=[row(d), pl.BlockSpec((TOP_K, tile, d // 2), lambda i: (0, i + off, 0)),
                  pl.BlockSpec((tile, 8), lambda i: (i + off, 0)), modspec, modspec, modspec,
                  _const_spec((1, d)), _const_spec((d, D_EXPERT)), _const_spec((d, D_EXPERT)),
                  _const_spec((D_EXPERT, d)), _const_spec((1, d))],
        out_specs=row(d),
        compiler_params=_params(("parallel",)),
        name="combine",
    )(x2, yg, wt_rows, mods["g2"], mods["sh2"], mods["sc2"], lw["norm2_g"], lw["w_sh_gate"], lw["w_sh_up"],
      lw["w_sh_down"], final_g)


def routed_experts(h2p, eid, rank, counts, lw, layer):
    m = h2p.shape[0]
    p_max = m * TOP_K + N_EXPERTS * EXPERT_TILE
    pos, tile_exp, tile_valid = routing_plan(eid, rank, counts, p_max)
    xs = sc_scatter_rows(h2p, pos, p_max)
    ys = grouped_experts(xs, tile_exp, tile_valid, lw["w_exp_gate"], lw["w_exp_up"], lw["w_exp_down"], layer)
    return sc_gather_rows(ys, pos[:TOP_K].reshape(TOP_K * m)).reshape(TOP_K, m, D_MODEL // 2)


def rope_tables(seq):
    rows = seq // GRID_W
    row = jnp.repeat(jnp.arange(rows), GRID_W).astype(f32)
    col = jnp.tile(jnp.arange(GRID_W), rows).astype(f32)
    axis_dim = HEAD_DIM // 2
    inv_freq = 1.0 / (ROPE_THETA ** (jnp.arange(0, axis_dim, 2, dtype=f32) / axis_dim))
    ang_r = row[:, None] * inv_freq
    ang_c = col[:, None] * inv_freq
    ang = jnp.concatenate([ang_r, ang_r, ang_c, ang_c], axis=-1)
    cos, sin = jnp.cos(ang), jnp.sin(ang)
    seg = (jnp.arange(HEAD_DIM) // 16) % 2
    sa = jnp.where(seg == 0, -sin, 0.0)
    sb = jnp.where(seg == 1, sin, 0.0)
    rep = lambda t: jnp.tile(t, (1, 4))
    return rep(cos), rep(sa), rep(sb)


def _router_weight_pieces(w):
    hi, mid, lo = _split3(w)
    return jnp.concatenate([hi, mid, lo, jnp.zeros_like(hi)], axis=1)


def identity_rope_tables(rows):
    return jnp.ones((rows, BRANCH_W), f32), jnp.zeros((rows, BRANCH_W), f32), jnp.zeros((rows, BRANCH_W), f32)


def kernel(x, c, ctx, c_ctx, w_mod, b_mod, norm1_g, norm2_g, w_in, q_norm_g, k_norm_g, sink, gm_norm_g, gm_ws, gm_b, w_br, w_o, w_router, router_bias, w_exp_gate, w_exp_up, w_exp_down, w_sh_gate, w_sh_up, w_sh_down, final_norm_g):
    bsz_all, seq, d = x.shape
    n_ctx = ctx.shape[1]
    depth = w_mod.shape[0]

    cc = jnp.concatenate([c, c_ctx[None, :], jnp.zeros((MOD_ROWS - bsz_all - 1, d), f32)], axis=0)
    mod_all = compute_mod(cc, w_mod, b_mod)

    lat_tables = rope_tables(seq)
    ctx_tables = identity_rope_tables(ROW_TILE)
    wc_lat, ws_lat = dft_tables(seq)
    wc_ctx, ws_ctx = dft_tables(n_ctx)
    cs64 = channel_dft_table()
    bd = jnp.asarray(np.kron(np.eye(4), np.full((HEAD_DIM, HEAD_DIM), 1.0 / HEAD_DIM)), dtype=bf16)
    final_g = final_norm_g.reshape(1, d)
    tri = jnp.asarray(np.triu(np.ones((MERGE_TILE, MERGE_TILE)), 1), dtype=bf16)

    lws = []
    for l in range(depth):
        lws.append({
            "norm1_g": norm1_g[l].reshape(1, d),
            "norm2_g": norm2_g[l].reshape(1, d),
            "w_z": w_in[l, :, :OFF_GATE].astype(bf16),
            "w_gate": w_in[l, :, OFF_GATE:].astype(bf16),
            "qn": jnp.tile(q_norm_g[l], BRANCH_W // HEAD_DIM).reshape(1, BRANCH_W),
            "kn": jnp.tile(k_norm_g[l], KV_W // HEAD_DIM).reshape(1, KV_W),
            "bd": bd,
            "gm_norm_g": gm_norm_g[l].reshape(1, BRANCH_W),
            "gm_ws": gm_ws[l].astype(bf16),
            "gm_bias": jnp.repeat(gm_b[l].T, GROUP_DIM, axis=1),
            "cs64": cs64,
            "w_br": w_br[l].astype(bf16),
            "w_o": w_o[l].astype(bf16),
            "w_router3": _router_weight_pieces(w_router[l]),
            "router_bias": router_bias[l].reshape(N_EXPERTS, 1),
            "tri": tri,
            "w_exp_gate": w_exp_gate,
            "w_exp_up": w_exp_up,
            "w_exp_down": w_exp_down,
            "w_sh_gate": w_sh_gate[l].astype(bf16),
            "w_sh_up": w_sh_up[l].astype(bf16),
            "w_sh_down": w_sh_down[l].astype(bf16),
        })

    return _layers(x, ctx, mod_all[:, :bsz_all], mod_all[:, bsz_all], lws, sink, lat_tables, ctx_tables,
                   (wc_lat, ws_lat), (wc_ctx, ws_ctx), final_g)


def _layers(x, ctx, mod_lat, mod_ctx, lws, sink, lat_tables, ctx_tables, dft_lat, dft_ctx, final_g):
    bsz, seq, d = x.shape
    n_ctx = ctx.shape[1]
    depth = len(lws)
    n_lat = bsz * seq
    n_cx = bsz * n_ctx
    wc_lat, ws_lat = dft_lat
    wc_ctx, ws_ctx = dft_ctx
    xl = x.reshape(n_lat, d)
    xc = ctx.reshape(n_cx, d)
    for l in range(depth):
        ctx_out = l < depth - 1
        lw = lws[l]
        names = ("sh1", "sc1", "g1", "sh2", "sc2", "g2")
        mods_lat = {n: mod_lat[l, :, i * d:(i + 1) * d].reshape(bsz, 1, d) for i, n in enumerate(names)}
        mods_ctx = {n: mod_ctx[l, i * d:(i + 1) * d].reshape(1, 1, d) for i, n in enumerate(names)}
        sink_l = sink[l]

        qg, kg, vg, qw, kw, vw, fxc, fxs, o_d = in_projection(
            xl, mods_lat["sh1"], mods_lat["sc1"], seq, lat_tables, seq, ROW_TILE, lw)
        cqg, ckg, cvg, cqw, ckw, cvw, cfxc, cfxs, co_d = in_projection(
            xc, mods_ctx["sh1"], mods_ctx["sc1"], n_cx, ctx_tables, ROW_TILE, ROW_TILE, lw)
        b3 = lambda t, rows: t.reshape(bsz, rows, t.shape[-1])
        ckg3, cvg3, ckw3, cvw3 = b3(ckg, n_ctx), b3(cvg, n_ctx), b3(ckw, n_ctx), b3(cvw, n_ctx)

        o_a = dft_mix(wc_lat, ws_lat, b3(fxc, seq), b3(fxs, seq), ROW_TILE).reshape(n_lat, BRANCH_W)
        o_b = full_attention(b3(qg, seq), [(ckg3, cvg3), (b3(kg, seq), b3(vg, seq))], None, FULL_ATTN_Q_ROWS)
        o_b = o_b.reshape(n_lat, BRANCH_W)
        o_c = window_attention(b3(qw, seq), b3(kw, seq), b3(vw, seq), ckw3, cvw3, sink_l).reshape(n_lat, BRANCH_W)
        m_total = n_lat + (n_cx if ctx_out else 0)
        route = None
        if ctx_out:
            route = [jnp.zeros((m_total, d // 2), i32), jnp.zeros((8, m_total), i32), jnp.zeros((8, m_total), f32),
                     jnp.zeros((8, m_total), i32), jnp.zeros((N_EXPERTS, 1), i32)]
        xl, *route = merge_and_route(xl, mods_lat, seq, (o_a, o_b, o_c, o_d), MERGE_TILE, lw, m_total, 0, route)
        if ctx_out:
            co_a = dft_mix(wc_ctx, ws_ctx, b3(cfxc, n_ctx), b3(cfxs, n_ctx), n_ctx).reshape(n_cx, BRANCH_W)
            co_b = full_attention(b3(cqg, n_ctx), [(ckg3, cvg3)], None, n_ctx).reshape(n_cx, BRANCH_W)
            co_c = full_attention(b3(cqw, n_ctx), [(ckw3, cvw3)], sink_l, n_ctx).reshape(n_cx, BRANCH_W)
            xc, *route = merge_and_route(xc, mods_ctx, n_cx, (co_a, co_b, co_c, co_d), MERGE_TILE, lw, m_total, n_lat,
                                         route)
        h2p, eid, wt, rank, counts = route
        yg = routed_experts(h2p, eid, rank, counts, lw, l)
        wt_rows = wt.T
        xl = combine(xl, yg, wt_rows, 0, mods_lat, seq, ROW_TILE, lw, final_g, not ctx_out)
        if ctx_out:
            xc = combine(xc, yg, wt_rows, n_lat, mods_ctx, n_cx, ROW_TILE, lw, final_g, False)
    return xl.reshape(bsz, seq, d)
```

```python
import functools
import math

import jax
import jax.numpy as jnp
import numpy as np
from jax import lax
from jax.experimental import pallas as pl
from jax.experimental.pallas import tpu as pltpu
from jax.experimental.pallas import tpu_sc as plsc

f32 = jnp.float32
bf16 = jnp.bfloat16
i32 = jnp.int32

D_MODEL = 1024
HEAD_DIM = 64
GRID_W = 64
ROPE_THETA = 10000.0
ATTN_SCALE = HEAD_DIM ** -0.5
RMS_EPS = 1e-6
NEG_INF = -1e30
Q_BLOCK = 128
WINDOW = 128
GM_CHUNK = 128
N_BRANCHES = 4
BRANCH_W = 256
KV_W = 128
Q_EXP_W = 4 * KV_W
ROPE_SHIFT = HEAD_DIM // 4
GROUP_DIM = 64
OFF_KV = 0
OFF_Q = 512
OFF_FN = 1024
OFF_GM = 1280
OFF_GATE = 1792
N_EXPERTS = 64
TOP_K = 6
N_GROUPS = 8
GROUP_SIZE = N_EXPERTS // N_GROUPS
TOPK_GROUPS = 4
D_EXPERT = 256
ROUTED_SCALE = 2.5

SC_CORES = 2
SC_SUBCORES = 16
SC_WORKERS = SC_CORES * SC_SUBCORES
SC_IDX_CHUNK = 128

ROW_TILE = 1024
MERGE_TILE = 512
ROUTER_AFTER_BRANCHES = 1
RANK_AFTER_BRANCHES = 3
EXPERT_TILE = 1024
LATENT_ATTN_Q_ROWS = 512
LATENT_ATTN_GLOBAL_BLOCKS = 2
FULL_ATTN_ROW_BLOCKS = 2
MOD_ROWS = 24
VMEM_LIMIT = 56 * 1024 * 1024


def _params(sem, vmem=VMEM_LIMIT):
    return pltpu.CompilerParams(dimension_semantics=sem, vmem_limit_bytes=vmem)


def _const_spec(shape):
    nd = len(shape)
    return pl.BlockSpec(shape, lambda *_: (0,) * nd, pipeline_mode=pl.Buffered(1))


def _rms_mod(x, g, sc, sh):
    ms = jnp.mean(x * x, axis=-1, keepdims=True)
    return (x * lax.rsqrt(ms + RMS_EPS) * g) * (1.0 + sc) + sh


def _gelu(x):
    return 0.5 * x * (1.0 + jnp.tanh(math.sqrt(2.0 / math.pi) * (x + 0.044715 * (x * x * x))))


def _silu(x):
    return x * jax.nn.sigmoid(x)


def _mod_kernel(a_ref, w_ref, b_ref, o_ref):
    a = _silu(a_ref[...]).astype(bf16)
    o_ref[0] = jnp.dot(a, w_ref[0].astype(bf16), preferred_element_type=f32) + b_ref[0]


def compute_mod(cc, w_mod, b_mod):
    depth, d, n = w_mod.shape
    tn = 1536
    return pl.pallas_call(
        _mod_kernel,
        out_shape=jax.ShapeDtypeStruct((depth, MOD_ROWS, n), f32),
        grid=(depth, n // tn),
        in_specs=[
            pl.BlockSpec((MOD_ROWS, d), lambda l, j: (0, 0)),
            pl.BlockSpec((1, d, tn), lambda l, j: (l, 0, j)),
            pl.BlockSpec((1, 1, tn), lambda l, j: (l, 0, j)),
        ],
        out_specs=pl.BlockSpec((1, MOD_ROWS, tn), lambda l, j: (l, 0, j)),
        compiler_params=_params(("parallel", "parallel")),
        name="mod_proj",
    )(cc, w_mod, b_mod.reshape(depth, 1, n))


def _inproj_kernel(x_ref, sh_ref, sc_ref, g_ref, w_ref, qn_ref, kn_ref, bd_ref, gmg_ref, ws_ref, gb_ref, cs_ref,
                   cos_ref, sa_ref, sb_ref,
                   qg_ref, kg_ref, vg_ref, qw_ref, kw_ref, vw_ref, xc_ref, xs_ref, od_ref):
    tile = x_ref.shape[0]
    hb = _rms_mod(x_ref[...], g_ref[...], sc_ref[0], sh_ref[0]).astype(bf16)

    def proj(a, b):
        return jnp.dot(hb, w_ref[:, a:b], preferred_element_type=f32)

    def square_pieces(t):
        sq = t * t
        hi = sq.astype(bf16)
        return hi, (sq - hi.astype(f32)).astype(bf16)

    def headnorm(t, pieces, gain):
        w = t.shape[1]
        b = bd_ref[:w, :w]
        ms = (jnp.dot(pieces[0], b, preferred_element_type=f32) + jnp.dot(pieces[1], b, preferred_element_type=f32))
        return t * lax.rsqrt(ms + RMS_EPS) * gain

    def rope(t):
        w = t.shape[1]
        return (t * cos_ref[:, :w] + pltpu.roll(t, w - ROPE_SHIFT, 1) * sa_ref[:, :w]
                + pltpu.roll(t, ROPE_SHIFT, 1) * sb_ref[:, :w])

    def expand_heads(q):
        lane = lax.broadcasted_iota(i32, (1, KV_W), 1)
        low = lane < HEAD_DIM
        blocks = []
        for kv in range(2):
            pair = q[:, KV_W * kv:KV_W * (kv + 1)]
            swapped = pltpu.roll(pair, HEAD_DIM, 1)
            keep = low if kv == 0 else jnp.logical_not(low)
            g0, g1 = (pair, swapped) if kv == 0 else (swapped, pair)
            blocks.append(jnp.where(keep, g0, 0.0))
            blocks.append(jnp.where(keep, g1, 0.0))
        return jnp.concatenate(blocks, axis=1)

    kv = proj(OFF_KV, OFF_Q)
    qq = proj(OFF_Q, OFF_FN)
    fn = proj(OFF_FN, OFF_GM).astype(bf16)
    uv = proj(OFF_GM, OFF_GATE)

    k_sq = square_pieces(kv[:, :KV_W])
    q_sq = square_pieces(qq[:, :BRANCH_W])
    u = _gelu(uv[:, :BRANCH_W])
    v = _gelu(uv[:, BRANCH_W:])
    vms = jnp.mean(v * v, axis=-1, keepdims=True)
    vn = (v * lax.rsqrt(vms + RMS_EPS) * gmg_ref[...]).astype(bf16)

    kg_ref[...] = rope(headnorm(kv[:, :KV_W], k_sq, kn_ref[...])).astype(bf16)
    vg_ref[...] = kv[:, KV_W:2 * KV_W].astype(bf16)
    kw_ref[...] = rope(kv[:, 2 * KV_W:3 * KV_W]).astype(bf16)
    vw_ref[...] = kv[:, 3 * KV_W:].astype(bf16)

    qg = rope(headnorm(qq[:, :BRANCH_W], q_sq, qn_ref[...])) * ATTN_SCALE
    qg_ref[...] = expand_heads(qg).astype(bf16)
    qw = rope(qq[:, BRANCH_W:]) * ATTN_SCALE
    qw_ref[...] = expand_heads(qw).astype(bf16)

    xcs = jnp.dot(fn, cs_ref[...], preferred_element_type=f32)
    xc_ref[...] = xcs[:, :BRANCH_W].astype(bf16)
    xs_ref[...] = xcs[:, BRANCH_W:].astype(bf16)

    lane_grp = lax.broadcasted_iota(i32, (1, BRANCH_W), 1) // GROUP_DIM
    for c in range(tile // GM_CHUNK):
        rows = slice(c * GM_CHUNK, (c + 1) * GM_CHUNK)
        vch = vn[rows]
        sv = gb_ref[...]
        for g in range(4):
            r = jnp.dot(ws_ref[g], vch, preferred_element_type=f32)
            sv = sv + jnp.where(lane_grp == g, r, 0.0)
        od_ref[rows, :] = (u[rows] * sv).astype(bf16)


def in_projection(x2, sh, sc, rows_per_mod, tables, rows_per_seq, tile, lw):
    m, d = x2.shape
    cos_t, sa_t, sb_t = tables
    seq_blocks = rows_per_seq // tile
    row = lambda w: pl.BlockSpec((tile, w), lambda i: (i, 0))
    modspec = pl.BlockSpec((1, 1, d), lambda i: ((i * tile) // rows_per_mod, 0, 0))
    tabspec = pl.BlockSpec((tile, BRANCH_W), lambda i: (i % seq_blocks, 0))
    out_w = [Q_EXP_W, KV_W, KV_W, Q_EXP_W, KV_W, KV_W, BRANCH_W, BRANCH_W, BRANCH_W]
    return pl.pallas_call(
        _inproj_kernel,
        out_shape=[jax.ShapeDtypeStruct((m, w), bf16) for w in out_w],
        grid=(m // tile,),
        in_specs=[
            row(d), modspec, modspec, _const_spec((1, d)), _const_spec((d, OFF_GATE)),
            _const_spec((1, BRANCH_W)), _const_spec((1, KV_W)), _const_spec((BRANCH_W, BRANCH_W)),
            _const_spec((1, BRANCH_W)), _const_spec((4, GM_CHUNK, GM_CHUNK)), _const_spec((GM_CHUNK, BRANCH_W)),
            _const_spec((BRANCH_W, 2 * BRANCH_W)),
            tabspec, tabspec, tabspec,
        ],
        out_specs=[row(w) for w in out_w],
        compiler_params=_params(("parallel",)),
        name="in_projection",
    )(x2, sh, sc, lw["norm1_g"], lw["w_z"], lw["qn"], lw["kn"], lw["bd"], lw["gm_norm_g"], lw["gm_ws"], lw["gm_bias"],
      lw["cs64"], cos_t, sa_t, sb_t)


def _dft_kernel(wc_ref, ws_ref, xc_ref, xs_ref, o_ref):
    acc = jnp.dot(wc_ref[...], xc_ref[0], preferred_element_type=f32)
    acc = acc + jnp.dot(ws_ref[...], xs_ref[0], preferred_element_type=f32)
    o_ref[0] = acc.astype(bf16)


def dft_mix(wc, ws, xc, xs, tile):
    nb, length, w = xc.shape
    return pl.pallas_call(
        _dft_kernel,
        out_shape=jax.ShapeDtypeStruct((nb, length, w), bf16),
        grid=(length // tile, nb),
        in_specs=[
            pl.BlockSpec((tile, length), lambda i, b: (i, 0)),
            pl.BlockSpec((tile, length), lambda i, b: (i, 0)),
            pl.BlockSpec((1, length, w), lambda i, b: (b, 0, 0)),
            pl.BlockSpec((1, length, w), lambda i, b: (b, 0, 0)),
        ],
        out_specs=pl.BlockSpec((1, tile, w), lambda i, b: (b, i, 0)),
        compiler_params=_params(("parallel", "parallel")),
        name="dft_mix",
    )(wc, ws, xc, xs)


def dft_tables(length):
    jk = (np.arange(length)[:, None] * np.arange(length)[None, :]) % length
    ang = 2.0 * np.pi * jk / length
    s = 1.0 / math.sqrt(length)
    return jnp.asarray(np.cos(ang) * s, dtype=bf16), jnp.asarray(-np.sin(ang) * s, dtype=bf16)


def channel_dft_table():
    jk = (np.arange(64)[:, None] * np.arange(64)[None, :]) % 64
    ang = 2.0 * np.pi * jk / 64
    eye = np.eye(4)
    c = np.kron(eye, np.cos(ang) / 8.0)
    s = np.kron(eye, np.sin(ang) / 8.0)
    return jnp.asarray(np.concatenate([c, s], axis=1), dtype=bf16)


def _attend_blocks(blocks, sink_ref):
    lane = lax.broadcasted_iota(i32, (1, KV_W), 1)
    low = lane < HEAD_DIM
    units = []
    for q_all, pieces, with_sink in blocks:
        qb = q_all.shape[0]
        for kv in range(2):
            q = jnp.concatenate([q_all[:, KV_W * (2 * kv):KV_W * (2 * kv + 1)],
                                 q_all[:, KV_W * (2 * kv + 1):KV_W * (2 * kv + 2)]], axis=0)
            sink_col = None
            if with_sink:
                sink_col = jnp.concatenate([jnp.full((qb, 1), sink_ref[2 * kv], f32),
                                            jnp.full((qb, 1), sink_ref[2 * kv + 1], f32)], axis=0)
            units.append((q, pieces, sink_col, low if kv == 0 else jnp.logical_not(low)))

    scores = []
    for q, pieces, _, _ in units:
        unit_scores = []
        for k, _, mask in pieces:
            s = lax.dot_general(q, k, (((1,), (1,)), ((), ())), preferred_element_type=f32)
            unit_scores.append(s if mask is None else jnp.where(mask, s, NEG_INF))
        scores.append(unit_scores)

    maxes = []
    for (_, _, sink_col, _), unit_scores in zip(units, scores):
        m = unit_scores[0].max(axis=-1, keepdims=True)
        for s in unit_scores[1:]:
            m = jnp.maximum(m, s.max(axis=-1, keepdims=True))
        maxes.append(m if sink_col is None else jnp.maximum(m, sink_col))

    probs = [[jnp.exp((s - m).astype(bf16)) for s in unit_scores] for unit_scores, m in zip(scores, maxes)]

    results = []
    for (_, pieces, sink_col, own), unit_probs, m in zip(units, probs, maxes):
        acc = None
        for p, (_, v, _) in zip(unit_probs, pieces):
            pv = jnp.dot(p, jnp.where(own, v, jnp.ones_like(v)), preferred_element_type=f32)
            acc = pv if acc is None else acc + pv
        denom = pltpu.roll(acc, HEAD_DIM, 1)
        if sink_col is not None:
            denom = denom + jnp.exp(sink_col - m)
        results.append(acc * (1.0 / denom))

    outs = []
    for i, (q_all, _, _) in enumerate(blocks):
        qb = q_all.shape[0]
        r_kv0, r_kv1 = results[2 * i], results[2 * i + 1]
        lo = jnp.where(low, r_kv0[:qb], pltpu.roll(r_kv0[qb:], HEAD_DIM, 1))
        hi = jnp.where(low, pltpu.roll(r_kv1[:qb], HEAD_DIM, 1), r_kv1[qb:])
        outs.append(jnp.concatenate([lo, hi], axis=1))
    return outs


def _full_attn_kernel(*refs, n_pieces, has_sink, row_blocks):
    pos = 0
    sink_ref = None
    if has_sink:
        sink_ref = refs[0]
        pos = 1
    q_ref = refs[pos]
    kv_refs = refs[pos + 1:pos + 1 + 2 * n_pieces]
    o_ref = refs[pos + 1 + 2 * n_pieces]
    pieces = [(kv_refs[2 * i][0], kv_refs[2 * i + 1][0], None) for i in range(n_pieces)]
    rows = q_ref.shape[1] // row_blocks
    blocks = [(q_ref[0, j * rows:(j + 1) * rows, :], pieces, has_sink) for j in range(row_blocks)]
    for j, out in enumerate(_attend_blocks(blocks, sink_ref)):
        o_ref[0, j * rows:(j + 1) * rows, :] = out.astype(bf16)


def full_attention(q, pieces, sink, qb):
    nb, lq, _ = q.shape
    in_specs = []
    args = []
    if sink is not None:
        in_specs.append(pl.BlockSpec(memory_space=pltpu.SMEM))
        args.append(sink)
    in_specs.append(pl.BlockSpec((1, qb, Q_EXP_W), lambda b, i: (b, i, 0)))
    args.append(q)
    for k, v in pieces:
        spec = pl.BlockSpec((1, k.shape[1], KV_W), lambda b, i: (b, 0, 0))
        in_specs += [spec, spec]
        args += [k, v]
    return pl.pallas_call(
        functools.partial(_full_attn_kernel, n_pieces=len(pieces), has_sink=sink is not None,
                          row_blocks=FULL_ATTN_ROW_BLOCKS),
        out_shape=jax.ShapeDtypeStruct((nb, lq, BRANCH_W), bf16),
        grid=(nb, lq // qb),
        in_specs=in_specs,
        out_specs=pl.BlockSpec((1, qb, BRANCH_W), lambda b, i: (b, i, 0)),
        compiler_params=_params(("parallel", "parallel")),
        name="full_attention",
    )(*args)


def _latent_attn_kernel(sink_ref, qg_ref, qw_ref, kg_ref, vg_ref, kw_ref, vw_ref, kgc_ref, vgc_ref, kwc_ref, vwc_ref,
                        ob_ref, oc_ref, *, seq):
    rows = qg_ref.shape[1]
    global_rows = rows // LATENT_ATTN_GLOBAL_BLOCKS
    window_blocks = rows // Q_BLOCK
    global_pieces = [(kgc_ref[0], vgc_ref[0], None), (kg_ref[0], vg_ref[0], None)]
    work = [(qg_ref[0, j * global_rows:(j + 1) * global_rows, :], global_pieces, False)
            for j in range(LATENT_ATTN_GLOBAL_BLOCKS)]
    span = 3 * Q_BLOCK
    ctx_piece = (kwc_ref[0], vwc_ref[0], None)
    for j in range(window_blocks):
        n = pl.program_id(1) * window_blocks + j
        start = pl.multiple_of(jnp.clip((n - 1) * Q_BLOCK, 0, seq - span), Q_BLOCK)
        kwin = kw_ref[0, pl.ds(start, span), :]
        vwin = vw_ref[0, pl.ds(start, span), :]
        row = lax.broadcasted_iota(i32, (2 * Q_BLOCK, span), 0) % Q_BLOCK + n * Q_BLOCK
        col = lax.broadcasted_iota(i32, (2 * Q_BLOCK, span), 1) + start
        mask = jnp.abs(row - col) <= WINDOW
        work.append((qw_ref[0, j * Q_BLOCK:(j + 1) * Q_BLOCK, :], [ctx_piece, (kwin, vwin, mask)], True))
    outs = _attend_blocks(work, sink_ref)
    for j in range(LATENT_ATTN_GLOBAL_BLOCKS):
        ob_ref[0, j * global_rows:(j + 1) * global_rows, :] = outs[j].astype(bf16)
    for j in range(window_blocks):
        oc_ref[0, j * Q_BLOCK:(j + 1) * Q_BLOCK, :] = outs[LATENT_ATTN_GLOBAL_BLOCKS + j].astype(bf16)


def latent_attention(qg, qw, kg, vg, kw, vw, kgc, vgc, kwc, vwc, sink):
    nb, seq, _ = qg.shape
    n_ctx = kgc.shape[1]
    rows = LATENT_ATTN_Q_ROWS
    assert seq % rows == 0 and rows % Q_BLOCK == 0 and seq >= 3 * Q_BLOCK, seq
    qspec = pl.BlockSpec((1, rows, Q_EXP_W), lambda b, i: (b, i, 0))
    full = lambda l: pl.BlockSpec((1, l, KV_W), lambda b, i: (b, 0, 0))
    ospec = pl.BlockSpec((1, rows, BRANCH_W), lambda b, i: (b, i, 0))
    return pl.pallas_call(
        functools.partial(_latent_attn_kernel, seq=seq),
        out_shape=[jax.ShapeDtypeStruct((nb, seq, BRANCH_W), bf16)] * 2,
        grid=(nb, seq // rows),
        in_specs=[pl.BlockSpec(memory_space=pltpu.SMEM), qspec, qspec, full(seq), full(seq), full(seq), full(seq),
                  full(n_ctx), full(n_ctx), full(n_ctx), full(n_ctx)],
        out_specs=[ospec, ospec],
        compiler_params=_params(("parallel", "parallel")),
        name="latent_attention",
    )(sink, qg, qw, kg, vg, kw, vw, kgc, vgc, kwc, vwc)


def _route(logits_t, bias_col):
    t = logits_t.shape[1]
    scores = jax.nn.sigmoid(logits_t)
    choice = scores + bias_col
    sub = lax.broadcasted_iota(i32, (GROUP_SIZE, t), 0)
    grp_score = []
    for g in range(N_GROUPS):
        cg = choice[g * GROUP_SIZE:(g + 1) * GROUP_SIZE]
        m1 = cg.max(axis=0, keepdims=True)
        first = jnp.min(jnp.where(cg == m1, sub, GROUP_SIZE), axis=0, keepdims=True)
        m2 = jnp.where(sub == first, -jnp.inf, cg).max(axis=0, keepdims=True)
        grp_score.append(m1 + m2)
    keep = []
    for g in range(N_GROUPS):
        beaten = jnp.zeros((1, t), i32)
        for o in range(N_GROUPS):
            if o == g:
                continue
            wins = (grp_score[o] > grp_score[g]) | ((grp_score[o] == grp_score[g]) & (o < g))
            beaten = beaten + wins.astype(i32)
        keep.append(jnp.broadcast_to(beaten < TOPK_GROUPS, (GROUP_SIZE, t)))
    masked = jnp.where(jnp.concatenate(keep, axis=0), choice, NEG_INF)
    eid = lax.broadcasted_iota(i32, (N_EXPERTS, t), 0)
    ids, wts = [], []
    for _ in range(TOP_K):
        m = masked.max(axis=0, keepdims=True)
        pick = jnp.min(jnp.where(masked == m, eid, N_EXPERTS), axis=0, keepdims=True)
        sel = eid == pick
        ids.append(pick)
        wts.append(jnp.sum(jnp.where(sel, scores, 0.0), axis=0, keepdims=True))
        masked = jnp.where(sel, -jnp.inf, masked)
    total = wts[0]
    for w in wts[1:]:
        total = total + w
    norm = ROUTED_SCALE / total
    return ids, [w * norm for w in wts]


def _pack_bf16_pairs(x):
    w = x.shape[1] // 2
    lo = lax.bitcast_convert_type(x[:, :w].astype(bf16).astype(f32), i32)
    hi = lax.bitcast_convert_type(x[:, w:].astype(bf16).astype(f32), i32)
    return lax.shift_right_logical(lo, 16) | (hi & jnp.int32(-65536))


def _unpack_bf16_pairs(p):
    lo = lax.bitcast_convert_type(lax.shift_left(p, 16), f32)
    hi = lax.bitcast_convert_type(p & jnp.int32(-65536), f32)
    return lo.astype(bf16), hi.astype(bf16)


def _merge_kernel(x_ref, sh_ref, sc_ref, g1_ref, sh2_ref, sc2_ref, n1_ref, n2_ref, oa_ref, ob_ref, oc_ref, od_ref,
                  wg_ref, wbr_ref, wo_ref, wr_ref, rb_ref, tri_ref, *rest, extends):
    if extends:
        cnt_in_ref = rest[0]
        rest = rest[5:]
    xo_ref, h2_ref, eid_ref, wt_ref, rank_ref, cnt_ref, run_ref, xn_prev_ref = rest
    step = pl.program_id(0)

    @pl.when(step == 0)
    def _():
        run_ref[...] = cnt_in_ref[...].astype(f32) if extends else jnp.zeros_like(run_ref)
        xn_prev_ref[...] = jnp.zeros_like(xn_prev_ref)

    x = x_ref[...]
    hb = _rms_mod(x, n1_ref[...], sc_ref[0], sh_ref[0]).astype(bf16)
    h2 = _rms_mod(xn_prev_ref[...], n2_ref[...], sc2_ref[0], sh2_ref[0])
    h2_ref[...] = _pack_bf16_pairs(h2)
    y = None
    for i, o_ref in enumerate((oa_ref, ob_ref, oc_ref, od_ref)):
        if i == ROUTER_AFTER_BRANCHES:
            ids, wts = _route(_router_logits(wr_ref[...], h2), rb_ref[...])
        if i == RANK_AFTER_BRANCHES:
            _rank_entries(ids, wts, tri_ref, run_ref, (step > 0).astype(f32), eid_ref, wt_ref, rank_ref, cnt_ref)
        logit = jnp.dot(hb, wg_ref[:, i * D_MODEL:(i + 1) * D_MODEL], preferred_element_type=f32)
        proj = jnp.dot(o_ref[...], wbr_ref[i], preferred_element_type=f32)
        term = jax.nn.sigmoid(logit.astype(bf16)) * proj.astype(bf16)
        y = term if y is None else y + term
    xn = x + g1_ref[0] * jnp.dot(y, wo_ref[...], preferred_element_type=f32)
    xo_ref[...] = xn
    xn_prev_ref[...] = xn


def _split3(x):
    def head(v):
        return lax.bitcast_convert_type(lax.bitcast_convert_type(v, i32) & jnp.int32(-65536), f32)

    hi = head(x)
    r1 = x - hi
    mid = head(r1)
    lo = r1 - mid
    return hi.astype(bf16), mid.astype(bf16), lo.astype(bf16)


def _router_logits(w3, h2):
    prod = None
    for piece in _split3(h2):
        p = jnp.dot(piece, w3, preferred_element_type=f32)
        prod = p if prod is None else prod + p
    lane = lax.broadcasted_iota(i32, (1, 2 * N_EXPERTS), 1)
    low = prod[:, :2 * N_EXPERTS]
    logits = low + pltpu.roll(low, N_EXPERTS, 1) + prod[:, 2 * N_EXPERTS:]
    return jnp.where(lane < N_EXPERTS, logits, 0.0).T[:N_EXPERTS]


def _rank_entries(ids, wts, tri_ref, run_ref, live, eid_ref, wt_ref, rank_ref, cnt_ref):
    t = ids[0].shape[1]
    eid = lax.broadcasted_iota(i32, (N_EXPERTS, t), 0)
    hits = [eid == pick for pick in ids]
    chosen = hits[0]
    for h in hits[1:]:
        chosen = chosen | h
    chosen = jnp.where(chosen, 1.0, 0.0)
    prefix = jnp.dot(chosen.astype(bf16), tri_ref[...], preferred_element_type=f32)
    offset = run_ref[...] + prefix
    ranks = [jnp.sum(jnp.where(h, offset, 0.0), axis=0, keepdims=True).astype(i32) for h in hits]
    run_ref[...] += live * jnp.sum(chosen, axis=1, keepdims=True)
    cnt_ref[...] = run_ref[...].astype(i32)

    pad_i = [jnp.zeros((1, t), i32)] * (8 - TOP_K)
    eid_ref[...] = jnp.concatenate(ids + pad_i, axis=0)
    rank_ref[...] = jnp.concatenate(ranks + pad_i, axis=0)
    wt_ref[...] = jnp.concatenate(wts + [jnp.zeros((1, t), f32)] * (8 - TOP_K), axis=0)


def merge_and_route(x2, mods, rows_per_mod, branches, tile, lw, m_total, row_offset, prior):
    m, d = x2.shape
    off = row_offset // tile
    n_tiles = m // tile
    cur = lambda i: jnp.minimum(i, n_tiles - 1)
    prev = lambda i: jnp.maximum(i - 1, 0)
    row = lambda w: pl.BlockSpec((tile, w), lambda i: (cur(i), 0))
    row_prev = lambda w: pl.BlockSpec((tile, w), lambda i: (prev(i) + off, 0))
    modspec = pl.BlockSpec((1, 1, d), lambda i: ((cur(i) * tile) // rows_per_mod, 0, 0))
    modspec_prev = pl.BlockSpec((1, 1, d), lambda i: ((prev(i) * tile) // rows_per_mod, 0, 0))
    col = pl.BlockSpec((8, tile), lambda i: (0, prev(i) + off))
    in_specs = [row(d)] + [modspec] * 3 + [modspec_prev] * 2 + [_const_spec((1, d)), _const_spec((1, d))] + [
        row(BRANCH_W)] * 4 + [
        _const_spec((d, N_BRANCHES * d)), _const_spec((N_BRANCHES, BRANCH_W, d)), _const_spec((d, d)),
        _const_spec((d, 4 * N_EXPERTS)), _const_spec((N_EXPERTS, 1)), _const_spec((tile, tile))]
    args = [x2, mods["sh1"], mods["sc1"], mods["g1"], mods["sh2"], mods["sc2"], lw["norm1_g"], lw["norm2_g"],
            *branches, lw["w_gate"], lw["w_br"], lw["w_o"], lw["w_router3"], lw["router_bias"], lw["tri"]]
    aliases = {}
    if prior is not None:
        h2p, eid, wt, rank, counts = prior
        n_in = len(args)
        in_specs += [_const_spec((N_EXPERTS, 1))] + [pl.BlockSpec(memory_space=pl.ANY)] * 4
        args += [counts, h2p, eid, wt, rank]
        aliases = {n_in + 1 + j: 1 + j for j in range(4)}
    return pl.pallas_call(
        functools.partial(_merge_kernel, extends=prior is not None),
        out_shape=[jax.ShapeDtypeStruct((m, d), f32), jax.ShapeDtypeStruct((m_total, d // 2), i32),
                   jax.ShapeDtypeStruct((8, m_total), i32), jax.ShapeDtypeStruct((8, m_total), f32),
                   jax.ShapeDtypeStruct((8, m_total), i32), jax.ShapeDtypeStruct((N_EXPERTS, 1), i32)],
        grid=(n_tiles + 1,),
        in_specs=in_specs,
        out_specs=[row(d), row_prev(d // 2), col, col, col, pl.BlockSpec((N_EXPERTS, 1), lambda i: (0, 0))],
        scratch_shapes=[pltpu.VMEM((N_EXPERTS, 1), f32), pltpu.VMEM((tile, d), f32)],
        input_output_aliases=aliases,
        compiler_params=_params(("arbitrary",)),
        name="merge_and_route",
    )(*args)


def routing_plan(eid, rank, counts, p_max):
    counts = counts.reshape(N_EXPERTS)
    padded = ((counts + EXPERT_TILE - 1) // EXPERT_TILE) * EXPERT_TILE
    ends = jnp.cumsum(padded)
    starts = ends - padded
    onehot = eid[:, :, None] == jnp.arange(N_EXPERTS, dtype=i32)[None, None, :]
    pos = rank + jnp.sum(jnp.where(onehot, starts[None, None, :], 0), axis=-1)
    n_tiles = p_max // EXPERT_TILE
    tile_start = jnp.arange(n_tiles, dtype=i32) * EXPERT_TILE
    tile_valid = tile_start < ends[-1]
    tile_exp = jnp.sum((ends[None, :] <= tile_start[:, None]).astype(i32), axis=1)
    return pos.astype(i32), jnp.minimum(tile_exp, N_EXPERTS - 1), tile_valid.astype(i32)


def _sc_worker_id():
    return lax.axis_index("subcore") * SC_CORES + lax.axis_index("core")


def sc_scatter_rows(table, pos, p_rows):
    m, w = table.shape
    n_chunks = m // SC_IDX_CHUNK
    steps = -(-n_chunks // SC_WORKERS)
    pos3 = pos.reshape(8, n_chunks, SC_IDX_CHUNK).transpose(1, 0, 2)
    mesh = plsc.VectorSubcoreMesh(core_axis_name="core", subcore_axis_name="subcore")

    @functools.partial(
        pl.kernel,
        out_type=jax.ShapeDtypeStruct((p_rows, w), table.dtype),
        mesh=mesh,
        scratch_types=[
            pltpu.VMEM((8, SC_IDX_CHUNK), i32),
            pltpu.VMEM((SC_IDX_CHUNK, w), table.dtype),
            pltpu.SemaphoreType.DMA,
        ],
    )
    def scatter(x_hbm, p_hbm, o_hbm, idx_v, rows_v, sem):
        wid = _sc_worker_id()

        @pl.loop(0, steps)
        def _(si):
            chunk = si * SC_WORKERS + wid

            @pl.when(chunk < n_chunks)
            def _():
                pltpu.sync_copy(p_hbm.at[chunk], idx_v)
                pltpu.sync_copy(x_hbm.at[pl.ds(chunk * SC_IDX_CHUNK, SC_IDX_CHUNK)], rows_v)
                copies = [pltpu.async_copy(rows_v, o_hbm.at[idx_v.at[k]], sem) for k in range(TOP_K)]
                for cp in copies:
                    cp.wait()

    return scatter(table, pos3)


def sc_gather_rows(table, idx):
    n_idx = idx.shape[0]
    w = table.shape[1]
    n_chunks = n_idx // SC_IDX_CHUNK
    steps = -(-n_chunks // SC_WORKERS)
    half = SC_IDX_CHUNK // 2
    mesh = plsc.VectorSubcoreMesh(core_axis_name="core", subcore_axis_name="subcore")

    @functools.partial(
        pl.kernel,
        out_type=jax.ShapeDtypeStruct((n_idx, w), table.dtype),
        mesh=mesh,
        scratch_types=[
            pltpu.VMEM((SC_IDX_CHUNK,), i32),
            pltpu.VMEM((half, w), table.dtype),
            pltpu.VMEM((half, w), table.dtype),
            pltpu.SemaphoreType.DMA,
            pltpu.SemaphoreType.DMA,
            pltpu.SemaphoreType.DMA,
            pltpu.SemaphoreType.DMA,
        ],
    )
    def gather(x_hbm, i_hbm, o_hbm, idx_v, buf0, buf1, g0_sem, g1_sem, w0_sem, w1_sem):
        wid = _sc_worker_id()

        @pl.loop(0, steps)
        def _(si):
            chunk = si * SC_WORKERS + wid

            @pl.when(chunk < n_chunks)
            def _():
                cbase = chunk * SC_IDX_CHUNK
                pltpu.sync_copy(i_hbm.at[pl.ds(cbase, SC_IDX_CHUNK)], idx_v)
                g0 = pltpu.async_copy(x_hbm.at[idx_v.at[pl.ds(0, half)]], buf0, g0_sem)
                g1 = pltpu.async_copy(x_hbm.at[idx_v.at[pl.ds(half, half)]], buf1, g1_sem)
                g0.wait()
                w0 = pltpu.async_copy(buf0, o_hbm.at[pl.ds(cbase, half)], w0_sem)
                g1.wait()
                w1 = pltpu.async_copy(buf1, o_hbm.at[pl.ds(cbase + half, half)], w1_sem)
                w0.wait()
                w1.wait()

    return gather(table, idx)


def _expert_kernel(te_ref, tv_ref, nx_ref, sl_ref, nu_ref, x_ref, wg_hbm, wu_hbm, wd_hbm, o_ref,
                   wg_f, wu_f, wd_f, wg_b, wu_b, wd_b, sems, *, layer):
    i = pl.program_id(0)

    def weight_copies(expert, slot):
        return [pltpu.make_async_copy(hbm.at[layer, expert], buf.at[slot], sems.at[slot, j])
                for j, (hbm, buf) in enumerate(((wg_hbm, wg_f), (wu_hbm, wu_f), (wd_hbm, wd_f)))]

    @pl.when(i == 0)
    def _():
        for cp in weight_copies(te_ref[0], 0):
            cp.start()

    @pl.when((i == 0) | (te_ref[i] != te_ref[jnp.maximum(i - 1, 0)]))
    def _():
        slot = sl_ref[i]
        for cp in weight_copies(te_ref[i], slot):
            cp.wait()
        wg_b[...] = wg_f[slot].astype(bf16)
        wu_b[...] = wu_f[slot].astype(bf16)
        wd_b[...] = wd_f[slot].astype(bf16)

        @pl.when(nx_ref[i] >= 0)
        def _():
            for cp in weight_copies(nx_ref[i], 1 - slot):
                cp.start()

    @pl.when(tv_ref[i] != 0)
    def _():
        lo, hi = _unpack_bf16_pairs(x_ref[...])
        half = lo.shape[1]
        a = (jnp.dot(lo, wg_b[:half], preferred_element_type=f32)
             + jnp.dot(hi, wg_b[half:], preferred_element_type=f32))
        b = (jnp.dot(lo, wu_b[:half], preferred_element_type=f32)
             + jnp.dot(hi, wu_b[half:], preferred_element_type=f32))
        hid = (_silu(a) * b).astype(bf16)
        o_ref[...] = _pack_bf16_pairs(jnp.dot(hid, wd_b[...], preferred_element_type=f32))


def grouped_experts(xs, tile_exp, tile_valid, wg, wu, wd, layer):
    p, half = xs.shape
    d = 2 * half
    n_tiles = p // EXPERT_TILE
    first = jnp.concatenate([jnp.ones((1,), bool), tile_exp[1:] != tile_exp[:-1]])
    slot = (jnp.cumsum(first.astype(i32)) - 1) % 2
    nxt_at = jnp.sum((tile_exp[None, :] <= tile_exp[:, None]).astype(i32), axis=1)
    nxt = jnp.where(nxt_at < n_tiles, tile_exp[jnp.minimum(nxt_at, n_tiles - 1)], -1)
    n_used = jnp.sum(tile_valid).reshape(1)
    tile = pl.BlockSpec((EXPERT_TILE, half), lambda i, te, tv, nx, sl, nu: (jnp.minimum(i, nu[0] - 1), 0))
    hbm = pl.BlockSpec(memory_space=pl.ANY)
    grid_spec = pltpu.PrefetchScalarGridSpec(
        num_scalar_prefetch=5,
        grid=(n_tiles,),
        in_specs=[tile, hbm, hbm, hbm],
        out_specs=tile,
        scratch_shapes=[pltpu.VMEM((2, d, D_EXPERT), f32), pltpu.VMEM((2, d, D_EXPERT), f32),
                        pltpu.VMEM((2, D_EXPERT, d), f32),
                        pltpu.VMEM((d, D_EXPERT), bf16), pltpu.VMEM((d, D_EXPERT), bf16),
                        pltpu.VMEM((D_EXPERT, d), bf16), pltpu.SemaphoreType.DMA((2, 3))],
    )
    return pl.pallas_call(
        functools.partial(_expert_kernel, layer=layer),
        out_shape=jax.ShapeDtypeStruct((p, half), i32),
        grid_spec=grid_spec,
        compiler_params=_params(("arbitrary",)),
        name="grouped_experts",
    )(tile_exp, tile_valid, nxt.astype(i32), slot.astype(i32), n_used.astype(i32), xs, wg, wu, wd)


def _combine_kernel(x_ref, yg_ref, wt_ref, g2_ref, sh2_ref, sc2_ref, n2_ref, wsg_ref, wsu_ref, wsd_ref, fg_ref, o_ref,
                    *, final):
    x = x_ref[...]
    hb = _rms_mod(x, n2_ref[...], sc2_ref[0], sh2_ref[0]).astype(bf16)
    a = jnp.dot(hb, wsg_ref[...], preferred_element_type=f32)
    b = jnp.dot(hb, wsu_ref[...], preferred_element_type=f32)
    f = jnp.dot((_silu(a) * b).astype(bf16), wsd_ref[...], preferred_element_type=f32)
    wt = wt_ref[...]
    half = x.shape[1] // 2
    f_lo, f_hi = f[:, :half], f[:, half:]
    for k in range(TOP_K):
        packed = yg_ref[k]
        w = wt[:, k:k + 1]
        f_lo = f_lo + w * lax.bitcast_convert_type(lax.shift_left(packed, 16), f32)
        f_hi = f_hi + w * lax.bitcast_convert_type(packed & jnp.int32(-65536), f32)
    xo = x + g2_ref[0] * jnp.concatenate([f_lo, f_hi], axis=1)
    if final:
        ms = jnp.mean(xo * xo, axis=-1, keepdims=True)
        xo = xo * lax.rsqrt(ms + RMS_EPS) * fg_ref[...]
    o_ref[...] = xo


def combine(x2, yg, wt_rows, row_offset, mods, rows_per_mod, tile, lw, final_g, final):
    m, d = x2.shape
    off = row_offset // tile
    row = lambda w: pl.BlockSpec((tile, w), lambda i: (i, 0))
    modspec = pl.BlockSpec((1, 1, d), lambda i: ((i * tile) // rows_per_mod, 0, 0))
    return pl.pallas_call(
        functools.partial(_combine_kernel, final=final),
        out_shape=jax.ShapeDtypeStruct((m, d), f32),
        grid=(m // tile,),
        in_specs=[row(d), pl.BlockSpec((TOP_K, tile, d // 2), lambda i: (0, i + off, 0)),
                  pl.BlockSpec((tile, 8), lambda i: (i + off, 0)), modspec, modspec, modspec,
                  _const_spec((1, d)), _const_spec((d, D_EXPERT)), _const_spec((d, D_EXPERT)),
                  _const_spec((D_EXPERT, d)), _const_spec((1, d))],
        out_specs=row(d),
        compiler_params=_params(("parallel",)),
        name="combine",
    )(x2, yg, wt_rows, mods["g2"], mods["sh2"], mods["sc2"], lw["norm2_g"], lw["w_sh_gate"], lw["w_sh_up"],
      lw["w_sh_down"], final_g)


def routed_experts(h2p, eid, rank, counts, lw, layer):
    m = h2p.shape[0]
    p_max = m * TOP_K + N_EXPERTS * EXPERT_TILE
    pos, tile_exp, tile_valid = routing_plan(eid, rank, counts, p_max)
    xs = sc_scatter_rows(h2p, pos, p_max)
    ys = grouped_experts(xs, tile_exp, tile_valid, lw["w_exp_gate"], lw["w_exp_up"], lw["w_exp_down"], layer)
    return sc_gather_rows(ys, pos[:TOP_K].reshape(TOP_K * m)).reshape(TOP_K, m, D_MODEL // 2)


def rope_tables(seq):
    rows = seq // GRID_W
    row = jnp.repeat(jnp.arange(rows), GRID_W).astype(f32)
    col = jnp.tile(jnp.arange(GRID_W), rows).astype(f32)
    axis_dim = HEAD_DIM // 2
    inv_freq = 1.0 / (ROPE_THETA ** (jnp.arange(0, axis_dim, 2, dtype=f32) / axis_dim))
    ang_r = row[:, None] * inv_freq
    ang_c = col[:, None] * inv_freq
    ang = jnp.concatenate([ang_r, ang_r, ang_c, ang_c], axis=-1)
    cos, sin = jnp.cos(ang), jnp.sin(ang)
    seg = (jnp.arange(HEAD_DIM) // 16) % 2
    sa = jnp.where(seg == 0, -sin, 0.0)
    sb = jnp.where(seg == 1, sin, 0.0)
    rep = lambda t: jnp.tile(t, (1, 4))
    return rep(cos), rep(sa), rep(sb)


def _router_weight_pieces(w):
    hi, mid, lo = _split3(w)
    return jnp.concatenate([hi, mid, lo, jnp.zeros_like(hi)], axis=1)


def identity_rope_tables(rows):
    return jnp.ones((rows, BRANCH_W), f32), jnp.zeros((rows, BRANCH_W), f32), jnp.zeros((rows, BRANCH_W), f32)


def kernel(x, c, ctx, c_ctx, w_mod, b_mod, norm1_g, norm2_g, w_in, q_norm_g, k_norm_g, sink, gm_norm_g, gm_ws, gm_b, w_br, w_o, w_router, router_bias, w_exp_gate, w_exp_up, w_exp_down, w_sh_gate, w_sh_up, w_sh_down, final_norm_g):
    bsz_all, seq, d = x.shape
    n_ctx = ctx.shape[1]
    depth = w_mod.shape[0]

    cc = jnp.concatenate([c, c_ctx[None, :], jnp.zeros((MOD_ROWS - bsz_all - 1, d), f32)], axis=0)
    mod_all = compute_mod(cc, w_mod, b_mod)

    lat_tables = rope_tables(seq)
    ctx_tables = identity_rope_tables(ROW_TILE)
    wc_lat, ws_lat = dft_tables(seq)
    wc_ctx, ws_ctx = dft_tables(n_ctx)
    cs64 = channel_dft_table()
    bd = jnp.asarray(np.kron(np.eye(4), np.full((HEAD_DIM, HEAD_DIM), 1.0 / HEAD_DIM)), dtype=bf16)
    final_g = final_norm_g.reshape(1, d)
    tri = jnp.asarray(np.triu(np.ones((MERGE_TILE, MERGE_TILE)), 1), dtype=bf16)

    lws = []
    for l in range(depth):
        lws.append({
            "norm1_g": norm1_g[l].reshape(1, d),
            "norm2_g": norm2_g[l].reshape(1, d),
            "w_z": w_in[l, :, :OFF_GATE].astype(bf16),
            "w_gate": w_in[l, :, OFF_GATE:].astype(bf16),
            "qn": jnp.tile(q_norm_g[l], BRANCH_W // HEAD_DIM).reshape(1, BRANCH_W),
            "kn": jnp.tile(k_norm_g[l], KV_W // HEAD_DIM).reshape(1, KV_W),
            "bd": bd,
            "gm_norm_g": gm_norm_g[l].reshape(1, BRANCH_W),
            "gm_ws": gm_ws[l].astype(bf16),
            "gm_bias": jnp.repeat(gm_b[l].T, GROUP_DIM, axis=1),
            "cs64": cs64,
            "w_br": w_br[l].astype(bf16),
            "w_o": w_o[l].astype(bf16),
            "w_router3": _router_weight_pieces(w_router[l]),
            "router_bias": router_bias[l].reshape(N_EXPERTS, 1),
            "tri": tri,
            "w_exp_gate": w_exp_gate,
            "w_exp_up": w_exp_up,
            "w_exp_down": w_exp_down,
            "w_sh_gate": w_sh_gate[l].astype(bf16),
            "w_sh_up": w_sh_up[l].astype(bf16),
            "w_sh_down": w_sh_down[l].astype(bf16),
        })

    return _layers(x, ctx, mod_all[:, :bsz_all], mod_all[:, bsz_all], lws, sink, lat_tables, ctx_tables,
                   (wc_lat, ws_lat), (wc_ctx, ws_ctx), final_g)


def _layers(x, ctx, mod_lat, mod_ctx, lws, sink, lat_tables, ctx_tables, dft_lat, dft_ctx, final_g):
    bsz, seq, d = x.shape
    n_ctx = ctx.shape[1]
    depth = len(lws)
    n_lat = bsz * seq
    n_cx = bsz * n_ctx
    wc_lat, ws_lat = dft_lat
    wc_ctx, ws_ctx = dft_ctx
    xl = x.reshape(n_lat, d)
    xc = ctx.reshape(n_cx, d)
    for l in range(depth):
        ctx_out = l < depth - 1
        lw = lws[l]
        names = ("sh1", "sc1", "g1", "sh2", "sc2", "g2")
        mods_lat = {n: mod_lat[l, :, i * d:(i + 1) * d].reshape(bsz, 1, d) for i, n in enumerate(names)}
        mods_ctx = {n: mod_ctx[l, i * d:(i + 1) * d].reshape(1, 1, d) for i, n in enumerate(names)}
        sink_l = sink[l]

        qg, kg, vg, qw, kw, vw, fxc, fxs, o_d = in_projection(
            xl, mods_lat["sh1"], mods_lat["sc1"], seq, lat_tables, seq, ROW_TILE, lw)
        cqg, ckg, cvg, cqw, ckw, cvw, cfxc, cfxs, co_d = in_projection(
            xc, mods_ctx["sh1"], mods_ctx["sc1"], n_cx, ctx_tables, ROW_TILE, ROW_TILE, lw)
        b3 = lambda t, rows: t.reshape(bsz, rows, t.shape[-1])
        ckg3, cvg3, ckw3, cvw3 = b3(ckg, n_ctx), b3(cvg, n_ctx), b3(ckw, n_ctx), b3(cvw, n_ctx)

        o_a = dft_mix(wc_lat, ws_lat, b3(fxc, seq), b3(fxs, seq), ROW_TILE).reshape(n_lat, BRANCH_W)
        o_b, o_c = latent_attention(b3(qg, seq), b3(qw, seq), b3(kg, seq), b3(vg, seq), b3(kw, seq), b3(vw, seq),
                                    ckg3, cvg3, ckw3, cvw3, sink_l)
        o_b, o_c = o_b.reshape(n_lat, BRANCH_W), o_c.reshape(n_lat, BRANCH_W)
        m_total = n_lat + (n_cx if ctx_out else 0)
        route = None
        if ctx_out:
            route = [jnp.zeros((m_total, d // 2), i32), jnp.zeros((8, m_total), i32), jnp.zeros((8, m_total), f32),
                     jnp.zeros((8, m_total), i32), jnp.zeros((N_EXPERTS, 1), i32)]
        xl, *route = merge_and_route(xl, mods_lat, seq, (o_a, o_b, o_c, o_d), MERGE_TILE, lw, m_total, 0, route)
        if ctx_out:
            co_a = dft_mix(wc_ctx, ws_ctx, b3(cfxc, n_ctx), b3(cfxs, n_ctx), n_ctx).reshape(n_cx, BRANCH_W)
            co_b = full_attention(b3(cqg, n_ctx), [(ckg3, cvg3)], None, n_ctx).reshape(n_cx, BRANCH_W)
            co_c = full_attention(b3(cqw, n_ctx), [(ckw3, cvw3)], sink_l, n_ctx).reshape(n_cx, BRANCH_W)
            xc, *route = merge_and_route(xc, mods_ctx, n_cx, (co_a, co_b, co_c, co_d), MERGE_TILE, lw, m_total, n_lat,
                                         route)
        h2p, eid, wt, rank, counts = route
        yg = routed_experts(h2p, eid, rank, counts, lw, l)
        wt_rows = wt.T
        xl = combine(xl, yg, wt_rows, 0, mods_lat, seq, ROW_TILE, lw, final_g, not ctx_out)
        if ctx_out:
            xc = combine(xc, yg, wt_rows, n_lat, mods_ctx, n_cx, ROW_TILE, lw, final_g, False)
    return xl.reshape(bsz, seq, d)
```

```python
import functools
import math

import jax
import jax.numpy as jnp
import numpy as np
from jax import lax
from jax.experimental import pallas as pl
from jax.experimental.pallas import tpu as pltpu
from jax.experimental.pallas import tpu_sc as plsc

f32 = jnp.float32
bf16 = jnp.bfloat16
i32 = jnp.int32

D_MODEL = 1024
HEAD_DIM = 64
GRID_W = 64
ROPE_THETA = 10000.0
ATTN_SCALE = HEAD_DIM ** -0.5
RMS_EPS = 1e-6
NEG_INF = -1e30
Q_BLOCK = 128
WINDOW = 128
GM_CHUNK = 128
N_BRANCHES = 4
BRANCH_W = 256
KV_W = 128
Q_EXP_W = 4 * KV_W
ROPE_SHIFT = HEAD_DIM // 4
GROUP_DIM = 64
OFF_KV = 0
OFF_Q = 512
OFF_FN = 1024
OFF_GM = 1280
OFF_GATE = 1792
N_EXPERTS = 64
TOP_K = 6
N_GROUPS = 8
GROUP_SIZE = N_EXPERTS // N_GROUPS
TOPK_GROUPS = 4
D_EXPERT = 256
ROUTED_SCALE = 2.5

SC_CORES = 2
SC_SUBCORES = 16
SC_WORKERS = SC_CORES * SC_SUBCORES
SC_IDX_CHUNK = 128

ROW_TILE = 1024
COMBINE_PROJECT_TILE = 512
MERGE_TILE = 512
ROUTER_AFTER_BRANCHES = 1
RANK_AFTER_BRANCHES = 3
EXPERT_TILE = 1024
LATENT_ATTN_Q_ROWS = 512
LATENT_ATTN_GLOBAL_BLOCKS = 2
FULL_ATTN_ROW_BLOCKS = 2
MOD_ROWS = 24
VMEM_LIMIT = 56 * 1024 * 1024


def _params(sem, vmem=VMEM_LIMIT):
    return pltpu.CompilerParams(dimension_semantics=sem, vmem_limit_bytes=vmem)


def _const_spec(shape):
    nd = len(shape)
    return pl.BlockSpec(shape, lambda *_: (0,) * nd, pipeline_mode=pl.Buffered(1))


def _rms_mod(x, g, sc, sh):
    ms = jnp.mean(x * x, axis=-1, keepdims=True)
    return (x * lax.rsqrt(ms + RMS_EPS) * g) * (1.0 + sc) + sh


def _gelu(x):
    return 0.5 * x * (1.0 + jnp.tanh(math.sqrt(2.0 / math.pi) * (x + 0.044715 * (x * x * x))))


def _silu(x):
    return x * jax.nn.sigmoid(x)


def _mod_kernel(a_ref, w_ref, b_ref, o_ref):
    a = _silu(a_ref[...]).astype(bf16)
    o_ref[0] = jnp.dot(a, w_ref[0].astype(bf16), preferred_element_type=f32) + b_ref[0]


def compute_mod(cc, w_mod, b_mod):
    depth, d, n = w_mod.shape
    tn = 1536
    return pl.pallas_call(
        _mod_kernel,
        out_shape=jax.ShapeDtypeStruct((depth, MOD_ROWS, n), f32),
        grid=(depth, n // tn),
        in_specs=[
            pl.BlockSpec((MOD_ROWS, d), lambda l, j: (0, 0)),
            pl.BlockSpec((1, d, tn), lambda l, j: (l, 0, j)),
            pl.BlockSpec((1, 1, tn), lambda l, j: (l, 0, j)),
        ],
        out_specs=pl.BlockSpec((1, MOD_ROWS, tn), lambda l, j: (l, 0, j)),
        compiler_params=_params(("parallel", "parallel")),
        name="mod_proj",
    )(cc, w_mod, b_mod.reshape(depth, 1, n))


def _inproj_kernel(x_ref, sh_ref, sc_ref, g_ref, w_ref, *rest):
    _inproj_finish(_inproj_project(x_ref, sh_ref, sc_ref, g_ref, w_ref), *rest)


def _inproj_project(x_ref, sh_ref, sc_ref, g_ref, w_ref):
    hb = _rms_mod(x_ref[...], g_ref[...], sc_ref[0], sh_ref[0]).astype(bf16)
    return [jnp.dot(hb, w_ref[:, a:b], preferred_element_type=f32)
            for a, b in ((OFF_KV, OFF_Q), (OFF_Q, OFF_FN), (OFF_FN, OFF_GM), (OFF_GM, OFF_GATE))]


def _inproj_finish(raw, qn_ref, kn_ref, bd_ref, gmg_ref, ws_ref, gb_ref, cs_ref, cos_ref, sa_ref, sb_ref,
                   qg_ref, kg_ref, vg_ref, qw_ref, kw_ref, vw_ref, xc_ref, xs_ref, od_ref):
    kv, qq, fn, uv = raw
    fn = fn.astype(bf16)
    tile = kv.shape[0]

    def square_pieces(t):
        sq = t * t
        hi = sq.astype(bf16)
        return hi, (sq - hi.astype(f32)).astype(bf16)

    def headnorm(t, pieces, gain):
        w = t.shape[1]
        b = bd_ref[:w, :w]
        ms = (jnp.dot(pieces[0], b, preferred_element_type=f32) + jnp.dot(pieces[1], b, preferred_element_type=f32))
        return t * lax.rsqrt(ms + RMS_EPS) * gain

    def rope(t):
        w = t.shape[1]
        return (t * cos_ref[:, :w] + pltpu.roll(t, w - ROPE_SHIFT, 1) * sa_ref[:, :w]
                + pltpu.roll(t, ROPE_SHIFT, 1) * sb_ref[:, :w])

    def expand_heads(q):
        lane = lax.broadcasted_iota(i32, (1, KV_W), 1)
        low = lane < HEAD_DIM
        blocks = []
        for kv in range(2):
            pair = q[:, KV_W * kv:KV_W * (kv + 1)]
            swapped = pltpu.roll(pair, HEAD_DIM, 1)
            keep = low if kv == 0 else jnp.logical_not(low)
            g0, g1 = (pair, swapped) if kv == 0 else (swapped, pair)
            blocks.append(jnp.where(keep, g0, 0.0))
            blocks.append(jnp.where(keep, g1, 0.0))
        return jnp.concatenate(blocks, axis=1)

    k_sq = square_pieces(kv[:, :KV_W])
    q_sq = square_pieces(qq[:, :BRANCH_W])
    u = _gelu(uv[:, :BRANCH_W])
    v = _gelu(uv[:, BRANCH_W:])
    vms = jnp.mean(v * v, axis=-1, keepdims=True)
    vn = (v * lax.rsqrt(vms + RMS_EPS) * gmg_ref[...]).astype(bf16)

    kg_ref[...] = rope(headnorm(kv[:, :KV_W], k_sq, kn_ref[...])).astype(bf16)
    vg_ref[...] = kv[:, KV_W:2 * KV_W].astype(bf16)
    kw_ref[...] = rope(kv[:, 2 * KV_W:3 * KV_W]).astype(bf16)
    vw_ref[...] = kv[:, 3 * KV_W:].astype(bf16)

    qg = rope(headnorm(qq[:, :BRANCH_W], q_sq, qn_ref[...])) * ATTN_SCALE
    qg_ref[...] = expand_heads(qg).astype(bf16)
    qw = rope(qq[:, BRANCH_W:]) * ATTN_SCALE
    qw_ref[...] = expand_heads(qw).astype(bf16)

    xcs = jnp.dot(fn, cs_ref[...], preferred_element_type=f32)
    xc_ref[...] = xcs[:, :BRANCH_W].astype(bf16)
    xs_ref[...] = xcs[:, BRANCH_W:].astype(bf16)

    lane_grp = lax.broadcasted_iota(i32, (1, BRANCH_W), 1) // GROUP_DIM
    for c in range(tile // GM_CHUNK):
        rows = slice(c * GM_CHUNK, (c + 1) * GM_CHUNK)
        vch = vn[rows]
        sv = gb_ref[...]
        for g in range(4):
            r = jnp.dot(ws_ref[g], vch, preferred_element_type=f32)
            sv = sv + jnp.where(lane_grp == g, r, 0.0)
        od_ref[rows, :] = (u[rows] * sv).astype(bf16)


def in_projection(x2, sh, sc, rows_per_mod, tables, rows_per_seq, tile, lw):
    m, d = x2.shape
    cos_t, sa_t, sb_t = tables
    seq_blocks = rows_per_seq // tile
    row = lambda w: pl.BlockSpec((tile, w), lambda i: (i, 0))
    modspec = pl.BlockSpec((1, 1, d), lambda i: ((i * tile) // rows_per_mod, 0, 0))
    tabspec = pl.BlockSpec((tile, BRANCH_W), lambda i: (i % seq_blocks, 0))
    out_w = [Q_EXP_W, KV_W, KV_W, Q_EXP_W, KV_W, KV_W, BRANCH_W, BRANCH_W, BRANCH_W]
    return pl.pallas_call(
        _inproj_kernel,
        out_shape=[jax.ShapeDtypeStruct((m, w), bf16) for w in out_w],
        grid=(m // tile,),
        in_specs=[
            row(d), modspec, modspec, _const_spec((1, d)), _const_spec((d, OFF_GATE)),
            _const_spec((1, BRANCH_W)), _const_spec((1, KV_W)), _const_spec((BRANCH_W, BRANCH_W)),
            _const_spec((1, BRANCH_W)), _const_spec((4, GM_CHUNK, GM_CHUNK)), _const_spec((GM_CHUNK, BRANCH_W)),
            _const_spec((BRANCH_W, 2 * BRANCH_W)),
            tabspec, tabspec, tabspec,
        ],
        out_specs=[row(w) for w in out_w],
        compiler_params=_params(("parallel",)),
        name="in_projection",
    )(x2, sh, sc, lw["norm1_g"], lw["w_z"], lw["qn"], lw["kn"], lw["bd"], lw["gm_norm_g"], lw["gm_ws"], lw["gm_bias"],
      lw["cs64"], cos_t, sa_t, sb_t)


def _dft_kernel(wc_ref, ws_ref, xc_ref, xs_ref, o_ref):
    acc = jnp.dot(wc_ref[...], xc_ref[0], preferred_element_type=f32)
    acc = acc + jnp.dot(ws_ref[...], xs_ref[0], preferred_element_type=f32)
    o_ref[0] = acc.astype(bf16)


def dft_mix(wc, ws, xc, xs, tile):
    nb, length, w = xc.shape
    return pl.pallas_call(
        _dft_kernel,
        out_shape=jax.ShapeDtypeStruct((nb, length, w), bf16),
        grid=(length // tile, nb),
        in_specs=[
            pl.BlockSpec((tile, length), lambda i, b: (i, 0)),
            pl.BlockSpec((tile, length), lambda i, b: (i, 0)),
            pl.BlockSpec((1, length, w), lambda i, b: (b, 0, 0)),
            pl.BlockSpec((1, length, w), lambda i, b: (b, 0, 0)),
        ],
        out_specs=pl.BlockSpec((1, tile, w), lambda i, b: (b, i, 0)),
        compiler_params=_params(("parallel", "parallel")),
        name="dft_mix",
    )(wc, ws, xc, xs)


def dft_tables(length):
    jk = (np.arange(length)[:, None] * np.arange(length)[None, :]) % length
    ang = 2.0 * np.pi * jk / length
    s = 1.0 / math.sqrt(length)
    return jnp.asarray(np.cos(ang) * s, dtype=bf16), jnp.asarray(-np.sin(ang) * s, dtype=bf16)


def channel_dft_table():
    jk = (np.arange(64)[:, None] * np.arange(64)[None, :]) % 64
    ang = 2.0 * np.pi * jk / 64
    eye = np.eye(4)
    c = np.kron(eye, np.cos(ang) / 8.0)
    s = np.kron(eye, np.sin(ang) / 8.0)
    return jnp.asarray(np.concatenate([c, s], axis=1), dtype=bf16)


def _attend_blocks(blocks, sink_ref):
    lane = lax.broadcasted_iota(i32, (1, KV_W), 1)
    low = lane < HEAD_DIM
    units = []
    for q_all, pieces, with_sink in blocks:
        qb = q_all.shape[0]
        for kv in range(2):
            q = jnp.concatenate([q_all[:, KV_W * (2 * kv):KV_W * (2 * kv + 1)],
                                 q_all[:, KV_W * (2 * kv + 1):KV_W * (2 * kv + 2)]], axis=0)
            sink_col = None
            if with_sink:
                sink_col = jnp.concatenate([jnp.full((qb, 1), sink_ref[2 * kv], f32),
                                            jnp.full((qb, 1), sink_ref[2 * kv + 1], f32)], axis=0)
            units.append((q, pieces, sink_col, low if kv == 0 else jnp.logical_not(low)))

    scores = []
    for q, pieces, _, _ in units:
        unit_scores = []
        for k, _, mask in pieces:
            s = lax.dot_general(q, k, (((1,), (1,)), ((), ())), preferred_element_type=f32)
            unit_scores.append(s if mask is None else jnp.where(mask, s, NEG_INF))
        scores.append(unit_scores)

    maxes = []
    for (_, _, sink_col, _), unit_scores in zip(units, scores):
        m = unit_scores[0].max(axis=-1, keepdims=True)
        for s in unit_scores[1:]:
            m = jnp.maximum(m, s.max(axis=-1, keepdims=True))
        maxes.append(m if sink_col is None else jnp.maximum(m, sink_col))

    probs = [[jnp.exp((s - m).astype(bf16)) for s in unit_scores] for unit_scores, m in zip(scores, maxes)]

    results = []
    for (_, pieces, sink_col, own), unit_probs, m in zip(units, probs, maxes):
        acc = None
        for p, (_, v, _) in zip(unit_probs, pieces):
            pv = jnp.dot(p, jnp.where(own, v, jnp.ones_like(v)), preferred_element_type=f32)
            acc = pv if acc is None else acc + pv
        denom = pltpu.roll(acc, HEAD_DIM, 1)
        if sink_col is not None:
            denom = denom + jnp.exp(sink_col - m)
        results.append(acc * (1.0 / denom))

    outs = []
    for i, (q_all, _, _) in enumerate(blocks):
        qb = q_all.shape[0]
        r_kv0, r_kv1 = results[2 * i], results[2 * i + 1]
        lo = jnp.where(low, r_kv0[:qb], pltpu.roll(r_kv0[qb:], HEAD_DIM, 1))
        hi = jnp.where(low, pltpu.roll(r_kv1[:qb], HEAD_DIM, 1), r_kv1[qb:])
        outs.append(jnp.concatenate([lo, hi], axis=1))
    return outs


def _full_attn_kernel(*refs, n_pieces, has_sink, row_blocks):
    pos = 0
    sink_ref = None
    if has_sink:
        sink_ref = refs[0]
        pos = 1
    q_ref = refs[pos]
    kv_refs = refs[pos + 1:pos + 1 + 2 * n_pieces]
    o_ref = refs[pos + 1 + 2 * n_pieces]
    pieces = [(kv_refs[2 * i][0], kv_refs[2 * i + 1][0], None) for i in range(n_pieces)]
    rows = q_ref.shape[1] // row_blocks
    blocks = [(q_ref[0, j * rows:(j + 1) * rows, :], pieces, has_sink) for j in range(row_blocks)]
    for j, out in enumerate(_attend_blocks(blocks, sink_ref)):
        o_ref[0, j * rows:(j + 1) * rows, :] = out.astype(bf16)


def full_attention(q, pieces, sink, qb):
    nb, lq, _ = q.shape
    in_specs = []
    args = []
    if sink is not None:
        in_specs.append(pl.BlockSpec(memory_space=pltpu.SMEM))
        args.append(sink)
    in_specs.append(pl.BlockSpec((1, qb, Q_EXP_W), lambda b, i: (b, i, 0)))
    args.append(q)
    for k, v in pieces:
        spec = pl.BlockSpec((1, k.shape[1], KV_W), lambda b, i: (b, 0, 0))
        in_specs += [spec, spec]
        args += [k, v]
    return pl.pallas_call(
        functools.partial(_full_attn_kernel, n_pieces=len(pieces), has_sink=sink is not None,
                          row_blocks=FULL_ATTN_ROW_BLOCKS),
        out_shape=jax.ShapeDtypeStruct((nb, lq, BRANCH_W), bf16),
        grid=(nb, lq // qb),
        in_specs=in_specs,
        out_specs=pl.BlockSpec((1, qb, BRANCH_W), lambda b, i: (b, i, 0)),
        compiler_params=_params(("parallel", "parallel")),
        name="full_attention",
    )(*args)


def _latent_attn_kernel(sink_ref, qg_ref, qw_ref, kg_ref, vg_ref, kw_ref, vw_ref, kgc_ref, vgc_ref, kwc_ref, vwc_ref,
                        ob_ref, oc_ref, *, seq):
    rows = qg_ref.shape[1]
    global_rows = rows // LATENT_ATTN_GLOBAL_BLOCKS
    window_blocks = rows // Q_BLOCK
    global_pieces = [(kgc_ref[0], vgc_ref[0], None), (kg_ref[0], vg_ref[0], None)]
    global_work = [(qg_ref[0, j * global_rows:(j + 1) * global_rows, :], global_pieces, False)
                   for j in range(LATENT_ATTN_GLOBAL_BLOCKS)]
    span = 3 * Q_BLOCK
    ctx_piece = (kwc_ref[0], vwc_ref[0], None)
    work = []
    for j in range(window_blocks):
        n = pl.program_id(1) * window_blocks + j
        start = pl.multiple_of(jnp.clip((n - 1) * Q_BLOCK, 0, seq - span), Q_BLOCK)
        kwin = kw_ref[0, pl.ds(start, span), :]
        vwin = vw_ref[0, pl.ds(start, span), :]
        row = lax.broadcasted_iota(i32, (2 * Q_BLOCK, span), 0) % Q_BLOCK + n * Q_BLOCK
        col = lax.broadcasted_iota(i32, (2 * Q_BLOCK, span), 1) + start
        mask = jnp.abs(row - col) <= WINDOW
        work.append((qw_ref[0, j * Q_BLOCK:(j + 1) * Q_BLOCK, :], [ctx_piece, (kwin, vwin, mask)], True))
    outs = _attend_blocks(work + global_work, sink_ref)
    for j in range(window_blocks):
        oc_ref[0, j * Q_BLOCK:(j + 1) * Q_BLOCK, :] = outs[j].astype(bf16)
    for j in range(LATENT_ATTN_GLOBAL_BLOCKS):
        ob_ref[0, j * global_rows:(j + 1) * global_rows, :] = outs[window_blocks + j].astype(bf16)


def latent_attention(qg, qw, kg, vg, kw, vw, kgc, vgc, kwc, vwc, sink):
    nb, seq, _ = qg.shape
    n_ctx = kgc.shape[1]
    rows = LATENT_ATTN_Q_ROWS
    assert seq % rows == 0 and rows % Q_BLOCK == 0 and seq >= 3 * Q_BLOCK, seq
    qspec = pl.BlockSpec((1, rows, Q_EXP_W), lambda b, i: (b, i, 0))
    full = lambda l: pl.BlockSpec((1, l, KV_W), lambda b, i: (b, 0, 0))
    ospec = pl.BlockSpec((1, rows, BRANCH_W), lambda b, i: (b, i, 0))
    return pl.pallas_call(
        functools.partial(_latent_attn_kernel, seq=seq),
        out_shape=[jax.ShapeDtypeStruct((nb, seq, BRANCH_W), bf16)] * 2,
        grid=(nb, seq // rows),
        in_specs=[pl.BlockSpec(memory_space=pltpu.SMEM), qspec, qspec, full(seq), full(seq), full(seq), full(seq),
                  full(n_ctx), full(n_ctx), full(n_ctx), full(n_ctx)],
        out_specs=[ospec, ospec],
        compiler_params=_params(("parallel", "parallel")),
        name="latent_attention",
    )(sink, qg, qw, kg, vg, kw, vw, kgc, vgc, kwc, vwc)


def _route(logits_t, bias_col):
    t = logits_t.shape[1]
    scores = jax.nn.sigmoid(logits_t)
    choice = scores + bias_col
    sub = lax.broadcasted_iota(i32, (GROUP_SIZE, t), 0)
    grp_score = []
    for g in range(N_GROUPS):
        cg = choice[g * GROUP_SIZE:(g + 1) * GROUP_SIZE]
        m1 = cg.max(axis=0, keepdims=True)
        first = jnp.min(jnp.where(cg == m1, sub, GROUP_SIZE), axis=0, keepdims=True)
        m2 = jnp.where(sub == first, -jnp.inf, cg).max(axis=0, keepdims=True)
        grp_score.append(m1 + m2)
    keep = []
    for g in range(N_GROUPS):
        beaten = jnp.zeros((1, t), i32)
        for o in range(N_GROUPS):
            if o == g:
                continue
            wins = (grp_score[o] > grp_score[g]) | ((grp_score[o] == grp_score[g]) & (o < g))
            beaten = beaten + wins.astype(i32)
        keep.append(jnp.broadcast_to(beaten < TOPK_GROUPS, (GROUP_SIZE, t)))
    masked = jnp.where(jnp.concatenate(keep, axis=0), choice, NEG_INF)
    eid = lax.broadcasted_iota(i32, (N_EXPERTS, t), 0)
    ids, wts = [], []
    for _ in range(TOP_K):
        m = masked.max(axis=0, keepdims=True)
        pick = jnp.min(jnp.where(masked == m, eid, N_EXPERTS), axis=0, keepdims=True)
        sel = eid == pick
        ids.append(pick)
        wts.append(jnp.sum(jnp.where(sel, scores, 0.0), axis=0, keepdims=True))
        masked = jnp.where(sel, -jnp.inf, masked)
    total = wts[0]
    for w in wts[1:]:
        total = total + w
    norm = ROUTED_SCALE / total
    return ids, [w * norm for w in wts]


def _pack_bf16_pairs(x):
    w = x.shape[1] // 2
    lo = lax.bitcast_convert_type(x[:, :w].astype(bf16).astype(f32), i32)
    hi = lax.bitcast_convert_type(x[:, w:].astype(bf16).astype(f32), i32)
    return lax.shift_right_logical(lo, 16) | (hi & jnp.int32(-65536))


def _unpack_bf16_pairs(p):
    lo = lax.bitcast_convert_type(lax.shift_left(p, 16), f32)
    hi = lax.bitcast_convert_type(p & jnp.int32(-65536), f32)
    return lo.astype(bf16), hi.astype(bf16)


def _merge_kernel(x_ref, sh_ref, sc_ref, g1_ref, sh2_ref, sc2_ref, n1_ref, n2_ref, oa_ref, ob_ref, oc_ref, od_ref,
                  wg_ref, wbr_ref, wo_ref, wr_ref, rb_ref, tri_ref, *rest, extends):
    if extends:
        cnt_in_ref = rest[0]
        rest = rest[5:]
    xo_ref, h2_ref, eid_ref, wt_ref, rank_ref, cnt_ref, run_ref, xn_prev_ref = rest
    step = pl.program_id(0)

    @pl.when(step == 0)
    def _():
        run_ref[...] = cnt_in_ref[...].astype(f32) if extends else jnp.zeros_like(run_ref)
        xn_prev_ref[...] = jnp.zeros_like(xn_prev_ref)

    x = x_ref[...]
    hb = _rms_mod(x, n1_ref[...], sc_ref[0], sh_ref[0]).astype(bf16)
    h2 = _rms_mod(xn_prev_ref[...], n2_ref[...], sc2_ref[0], sh2_ref[0])
    h2_ref[...] = _pack_bf16_pairs(h2)
    y = None
    for i, o_ref in enumerate((oa_ref, ob_ref, oc_ref, od_ref)):
        if i == ROUTER_AFTER_BRANCHES:
            ids, wts = _route(_router_logits(wr_ref[...], h2), rb_ref[...])
        if i == RANK_AFTER_BRANCHES:
            _rank_entries(ids, wts, tri_ref, run_ref, (step > 0).astype(f32), eid_ref, wt_ref, rank_ref, cnt_ref)
        logit = jnp.dot(hb, wg_ref[:, i * D_MODEL:(i + 1) * D_MODEL], preferred_element_type=f32)
        proj = jnp.dot(o_ref[...], wbr_ref[i], preferred_element_type=f32)
        term = jax.nn.sigmoid(logit.astype(bf16)) * proj.astype(bf16)
        y = term if y is None else y + term
    xn = x + g1_ref[0] * jnp.dot(y, wo_ref[...], preferred_element_type=f32)
    xo_ref[...] = xn
    xn_prev_ref[...] = xn


def _split3(x):
    def head(v):
        return lax.bitcast_convert_type(lax.bitcast_convert_type(v, i32) & jnp.int32(-65536), f32)

    hi = head(x)
    r1 = x - hi
    mid = head(r1)
    lo = r1 - mid
    return hi.astype(bf16), mid.astype(bf16), lo.astype(bf16)


def _router_logits(w3, h2):
    prod = None
    for piece in _split3(h2):
        p = jnp.dot(piece, w3, preferred_element_type=f32)
        prod = p if prod is None else prod + p
    lane = lax.broadcasted_iota(i32, (1, 2 * N_EXPERTS), 1)
    low = prod[:, :2 * N_EXPERTS]
    logits = low + pltpu.roll(low, N_EXPERTS, 1) + prod[:, 2 * N_EXPERTS:]
    return jnp.where(lane < N_EXPERTS, logits, 0.0).T[:N_EXPERTS]


def _rank_entries(ids, wts, tri_ref, run_ref, live, eid_ref, wt_ref, rank_ref, cnt_ref):
    t = ids[0].shape[1]
    eid = lax.broadcasted_iota(i32, (N_EXPERTS, t), 0)
    hits = [eid == pick for pick in ids]
    chosen = hits[0]
    for h in hits[1:]:
        chosen = chosen | h
    chosen = jnp.where(chosen, 1.0, 0.0)
    prefix = jnp.dot(chosen.astype(bf16), tri_ref[...], preferred_element_type=f32)
    offset = run_ref[...] + prefix
    ranks = [jnp.sum(jnp.where(h, offset, 0.0), axis=0, keepdims=True).astype(i32) for h in hits]
    run_ref[...] += live * jnp.sum(chosen, axis=1, keepdims=True)
    cnt_ref[...] = run_ref[...].astype(i32)

    pad_i = [jnp.zeros((1, t), i32)] * (8 - TOP_K)
    eid_ref[...] = jnp.concatenate(ids + pad_i, axis=0)
    rank_ref[...] = jnp.concatenate(ranks + pad_i, axis=0)
    wt_ref[...] = jnp.concatenate(wts + [jnp.zeros((1, t), f32)] * (8 - TOP_K), axis=0)


def merge_and_route(x2, mods, rows_per_mod, branches, tile, lw, m_total, row_offset, prior):
    m, d = x2.shape
    off = row_offset // tile
    n_tiles = m // tile
    cur = lambda i: jnp.minimum(i, n_tiles - 1)
    prev = lambda i: jnp.maximum(i - 1, 0)
    row = lambda w: pl.BlockSpec((tile, w), lambda i: (cur(i), 0))
    row_prev = lambda w: pl.BlockSpec((tile, w), lambda i: (prev(i) + off, 0))
    modspec = pl.BlockSpec((1, 1, d), lambda i: ((cur(i) * tile) // rows_per_mod, 0, 0))
    modspec_prev = pl.BlockSpec((1, 1, d), lambda i: ((prev(i) * tile) // rows_per_mod, 0, 0))
    col = pl.BlockSpec((8, tile), lambda i: (0, prev(i) + off))
    in_specs = [row(d)] + [modspec] * 3 + [modspec_prev] * 2 + [_const_spec((1, d)), _const_spec((1, d))] + [
        row(BRANCH_W)] * 4 + [
        _const_spec((d, N_BRANCHES * d)), _const_spec((N_BRANCHES, BRANCH_W, d)), _const_spec((d, d)),
        _const_spec((d, 4 * N_EXPERTS)), _const_spec((N_EXPERTS, 1)), _const_spec((tile, tile))]
    args = [x2, mods["sh1"], mods["sc1"], mods["g1"], mods["sh2"], mods["sc2"], lw["norm1_g"], lw["norm2_g"],
            *branches, lw["w_gate"], lw["w_br"], lw["w_o"], lw["w_router3"], lw["router_bias"], lw["tri"]]
    aliases = {}
    if prior is not None:
        h2p, eid, wt, rank, counts = prior
        n_in = len(args)
        in_specs += [_const_spec((N_EXPERTS, 1))] + [pl.BlockSpec(memory_space=pl.ANY)] * 4
        args += [counts, h2p, eid, wt, rank]
        aliases = {n_in + 1 + j: 1 + j for j in range(4)}
    return pl.pallas_call(
        functools.partial(_merge_kernel, extends=prior is not None),
        out_shape=[jax.ShapeDtypeStruct((m, d), f32), jax.ShapeDtypeStruct((m_total, d // 2), i32),
                   jax.ShapeDtypeStruct((8, m_total), i32), jax.ShapeDtypeStruct((8, m_total), f32),
                   jax.ShapeDtypeStruct((8, m_total), i32), jax.ShapeDtypeStruct((N_EXPERTS, 1), i32)],
        grid=(n_tiles + 1,),
        in_specs=in_specs,
        out_specs=[row(d), row_prev(d // 2), col, col, col, pl.BlockSpec((N_EXPERTS, 1), lambda i: (0, 0))],
        scratch_shapes=[pltpu.VMEM((N_EXPERTS, 1), f32), pltpu.VMEM((tile, d), f32)],
        input_output_aliases=aliases,
        compiler_params=_params(("arbitrary",)),
        name="merge_and_route",
    )(*args)


def routing_plan(eid, rank, counts, p_max):
    counts = counts.reshape(N_EXPERTS)
    padded = ((counts + EXPERT_TILE - 1) // EXPERT_TILE) * EXPERT_TILE
    ends = jnp.cumsum(padded)
    starts = ends - padded
    onehot = eid[:, :, None] == jnp.arange(N_EXPERTS, dtype=i32)[None, None, :]
    pos = rank + jnp.sum(jnp.where(onehot, starts[None, None, :], 0), axis=-1)
    n_tiles = p_max // EXPERT_TILE
    tile_start = jnp.arange(n_tiles, dtype=i32) * EXPERT_TILE
    tile_valid = tile_start < ends[-1]
    tile_exp = jnp.sum((ends[None, :] <= tile_start[:, None]).astype(i32), axis=1)
    return pos.astype(i32), jnp.minimum(tile_exp, N_EXPERTS - 1), tile_valid.astype(i32)


def _sc_worker_id():
    return lax.axis_index("subcore") * SC_CORES + lax.axis_index("core")


def sc_scatter_rows(table, pos, p_rows):
    m, w = table.shape
    n_chunks = m // SC_IDX_CHUNK
    steps = -(-n_chunks // SC_WORKERS)
    pos3 = pos.reshape(8, n_chunks, SC_IDX_CHUNK).transpose(1, 0, 2)
    mesh = plsc.VectorSubcoreMesh(core_axis_name="core", subcore_axis_name="subcore")

    @functools.partial(
        pl.kernel,
        out_type=jax.ShapeDtypeStruct((p_rows, w), table.dtype),
        mesh=mesh,
        scratch_types=[
            pltpu.VMEM((8, SC_IDX_CHUNK), i32),
            pltpu.VMEM((SC_IDX_CHUNK, w), table.dtype),
            pltpu.SemaphoreType.DMA,
        ],
    )
    def scatter(x_hbm, p_hbm, o_hbm, idx_v, rows_v, sem):
        wid = _sc_worker_id()

        @pl.loop(0, steps)
        def _(si):
            chunk = si * SC_WORKERS + wid

            @pl.when(chunk < n_chunks)
            def _():
                pltpu.sync_copy(p_hbm.at[chunk], idx_v)
                pltpu.sync_copy(x_hbm.at[pl.ds(chunk * SC_IDX_CHUNK, SC_IDX_CHUNK)], rows_v)
                copies = [pltpu.async_copy(rows_v, o_hbm.at[idx_v.at[k]], sem) for k in range(TOP_K)]
                for cp in copies:
                    cp.wait()

    return scatter(table, pos3)


def sc_gather_rows(table, idx):
    n_idx = idx.shape[0]
    w = table.shape[1]
    n_chunks = n_idx // SC_IDX_CHUNK
    steps = -(-n_chunks // SC_WORKERS)
    half = SC_IDX_CHUNK // 2
    mesh = plsc.VectorSubcoreMesh(core_axis_name="core", subcore_axis_name="subcore")

    @functools.partial(
        pl.kernel,
        out_type=jax.ShapeDtypeStruct((n_idx, w), table.dtype),
        mesh=mesh,
        scratch_types=[
            pltpu.VMEM((SC_IDX_CHUNK,), i32),
            pltpu.VMEM((half, w), table.dtype),
            pltpu.VMEM((half, w), table.dtype),
            pltpu.SemaphoreType.DMA,
            pltpu.SemaphoreType.DMA,
            pltpu.SemaphoreType.DMA,
            pltpu.SemaphoreType.DMA,
        ],
    )
    def gather(x_hbm, i_hbm, o_hbm, idx_v, buf0, buf1, g0_sem, g1_sem, w0_sem, w1_sem):
        wid = _sc_worker_id()

        @pl.loop(0, steps)
        def _(si):
            chunk = si * SC_WORKERS + wid

            @pl.when(chunk < n_chunks)
            def _():
                cbase = chunk * SC_IDX_CHUNK
                pltpu.sync_copy(i_hbm.at[pl.ds(cbase, SC_IDX_CHUNK)], idx_v)
                g0 = pltpu.async_copy(x_hbm.at[idx_v.at[pl.ds(0, half)]], buf0, g0_sem)
                g1 = pltpu.async_copy(x_hbm.at[idx_v.at[pl.ds(half, half)]], buf1, g1_sem)
                g0.wait()
                w0 = pltpu.async_copy(buf0, o_hbm.at[pl.ds(cbase, half)], w0_sem)
                g1.wait()
                w1 = pltpu.async_copy(buf1, o_hbm.at[pl.ds(cbase + half, half)], w1_sem)
                w0.wait()
                w1.wait()

    return gather(table, idx)


def _expert_kernel(te_ref, tv_ref, nx_ref, sl_ref, nu_ref, x_ref, wg_hbm, wu_hbm, wd_hbm, o_ref,
                   wg_f, wu_f, wd_f, wg_b, wu_b, wd_b, sems, *, layer):
    i = pl.program_id(0)

    def weight_copies(expert, slot):
        return [pltpu.make_async_copy(hbm.at[layer, expert], buf.at[slot], sems.at[slot, j])
                for j, (hbm, buf) in enumerate(((wg_hbm, wg_f), (wu_hbm, wu_f), (wd_hbm, wd_f)))]

    @pl.when(i == 0)
    def _():
        for cp in weight_copies(te_ref[0], 0):
            cp.start()

    @pl.when((i == 0) | (te_ref[i] != te_ref[jnp.maximum(i - 1, 0)]))
    def _():
        slot = sl_ref[i]
        for cp in weight_copies(te_ref[i], slot):
            cp.wait()
        wg_b[...] = wg_f[slot].astype(bf16)
        wu_b[...] = wu_f[slot].astype(bf16)
        wd_b[...] = wd_f[slot].astype(bf16)

        @pl.when(nx_ref[i] >= 0)
        def _():
            for cp in weight_copies(nx_ref[i], 1 - slot):
                cp.start()

    @pl.when(tv_ref[i] != 0)
    def _():
        lo, hi = _unpack_bf16_pairs(x_ref[...])
        half = lo.shape[1]
        a = (jnp.dot(lo, wg_b[:half], preferred_element_type=f32)
             + jnp.dot(hi, wg_b[half:], preferred_element_type=f32))
        b = (jnp.dot(lo, wu_b[:half], preferred_element_type=f32)
             + jnp.dot(hi, wu_b[half:], preferred_element_type=f32))
        hid = (_silu(a) * b).astype(bf16)
        o_ref[...] = _pack_bf16_pairs(jnp.dot(hid, wd_b[...], preferred_element_type=f32))


def grouped_experts(xs, tile_exp, tile_valid, wg, wu, wd, layer):
    p, half = xs.shape
    d = 2 * half
    n_tiles = p // EXPERT_TILE
    first = jnp.concatenate([jnp.ones((1,), bool), tile_exp[1:] != tile_exp[:-1]])
    slot = (jnp.cumsum(first.astype(i32)) - 1) % 2
    nxt_at = jnp.sum((tile_exp[None, :] <= tile_exp[:, None]).astype(i32), axis=1)
    nxt = jnp.where(nxt_at < n_tiles, tile_exp[jnp.minimum(nxt_at, n_tiles - 1)], -1)
    n_used = jnp.sum(tile_valid).reshape(1)
    tile = pl.BlockSpec((EXPERT_TILE, half), lambda i, te, tv, nx, sl, nu: (jnp.minimum(i, nu[0] - 1), 0))
    hbm = pl.BlockSpec(memory_space=pl.ANY)
    grid_spec = pltpu.PrefetchScalarGridSpec(
        num_scalar_prefetch=5,
        grid=(n_tiles,),
        in_specs=[tile, hbm, hbm, hbm],
        out_specs=tile,
        scratch_shapes=[pltpu.VMEM((2, d, D_EXPERT), f32), pltpu.VMEM((2, d, D_EXPERT), f32),
                        pltpu.VMEM((2, D_EXPERT, d), f32),
                        pltpu.VMEM((d, D_EXPERT), bf16), pltpu.VMEM((d, D_EXPERT), bf16),
                        pltpu.VMEM((D_EXPERT, d), bf16), pltpu.SemaphoreType.DMA((2, 3))],
    )
    return pl.pallas_call(
        functools.partial(_expert_kernel, layer=layer),
        out_shape=jax.ShapeDtypeStruct((p, half), i32),
        grid_spec=grid_spec,
        compiler_params=_params(("arbitrary",)),
        name="grouped_experts",
    )(tile_exp, tile_valid, nxt.astype(i32), slot.astype(i32), n_used.astype(i32), xs, wg, wu, wd)


def _combine_kernel(x_ref, yg_ref, wt_ref, g2_ref, sh2_ref, sc2_ref, n2_ref, wsg_ref, wsu_ref, wsd_ref, fg_ref, o_ref,
                    *, final):
    x = x_ref[...]
    hb = _rms_mod(x, n2_ref[...], sc2_ref[0], sh2_ref[0]).astype(bf16)
    a = jnp.dot(hb, wsg_ref[...], preferred_element_type=f32)
    b = jnp.dot(hb, wsu_ref[...], preferred_element_type=f32)
    f = jnp.dot((_silu(a) * b).astype(bf16), wsd_ref[...], preferred_element_type=f32)
    wt = wt_ref[...]
    half = x.shape[1] // 2
    f_lo, f_hi = f[:, :half], f[:, half:]
    for k in range(TOP_K):
        packed = yg_ref[k]
        w = wt[:, k:k + 1]
        f_lo = f_lo + w * lax.bitcast_convert_type(lax.shift_left(packed, 16), f32)
        f_hi = f_hi + w * lax.bitcast_convert_type(packed & jnp.int32(-65536), f32)
    xo = x + g2_ref[0] * jnp.concatenate([f_lo, f_hi], axis=1)
    if final:
        ms = jnp.mean(xo * xo, axis=-1, keepdims=True)
        xo = xo * lax.rsqrt(ms + RMS_EPS) * fg_ref[...]
    o_ref[...] = xo


def combine(x2, yg, wt_rows, row_offset, mods, rows_per_mod, tile, lw, final_g, final):
    m, d = x2.shape
    off = row_offset // tile
    row = lambda w: pl.BlockSpec((tile, w), lambda i: (i, 0))
    modspec = pl.BlockSpec((1, 1, d), lambda i: ((i * tile) // rows_per_mod, 0, 0))
    return pl.pallas_call(
        functools.partial(_combine_kernel, final=final),
        out_shape=jax.ShapeDtypeStruct((m, d), f32),
        grid=(m // tile,),
        in_specs=[row(d), pl.BlockSpec((TOP_K, tile, d // 2), lambda i: (0, i + off, 0)),
                  pl.BlockSpec((tile, 8), lambda i: (i + off, 0)), modspec, modspec, modspec,
                  _const_spec((1, d)), _const_spec((d, D_EXPERT)), _const_spec((d, D_EXPERT)),
                  _const_spec((D_EXPERT, d)), _const_spec((1, d))],
        out_specs=row(d),
        compiler_params=_params(("parallel",)),
        name="combine",
    )(x2, yg, wt_rows, mods["g2"], mods["sh2"], mods["sc2"], lw["norm2_g"], lw["w_sh_gate"], lw["w_sh_up"],
      lw["w_sh_down"], final_g)


N_COMBINE_INPUTS = 11
N_INPROJ_INPUTS = 15


def _combine_project_kernel(*refs):
    combine_in = refs[:N_COMBINE_INPUTS]
    inproj_in = refs[N_COMBINE_INPUTS:N_COMBINE_INPUTS + N_INPROJ_INPUTS - 1]
    xo_ref, *inproj_out, prev_ref = refs[N_COMBINE_INPUTS + N_INPROJ_INPUTS - 1:]

    @pl.when(pl.program_id(0) == 0)
    def _():
        prev_ref[...] = jnp.zeros_like(prev_ref)

    raw = _inproj_project(prev_ref, *inproj_in[:4])
    _combine_kernel(*combine_in, xo_ref, final=False)
    _inproj_finish(raw, *inproj_in[4:], *inproj_out)
    prev_ref[...] = xo_ref[...]


def combine_and_project(x2, yg, wt_rows, mods, rows_per_mod, lw, final_g, next_sh, next_sc, tables, rows_per_seq,
                        next_lw, tile):
    m, d = x2.shape
    n_tiles = m // tile
    seq_blocks = rows_per_seq // tile
    cos_t, sa_t, sb_t = tables
    cur = lambda i: jnp.minimum(i, n_tiles - 1)
    prev = lambda i: jnp.maximum(i - 1, 0)
    row_cur = lambda w: pl.BlockSpec((tile, w), lambda i: (cur(i), 0))
    row_prev = lambda w: pl.BlockSpec((tile, w), lambda i: (prev(i), 0))
    mod_cur = pl.BlockSpec((1, 1, d), lambda i: ((cur(i) * tile) // rows_per_mod, 0, 0))
    mod_prev = pl.BlockSpec((1, 1, d), lambda i: ((prev(i) * tile) // rows_per_mod, 0, 0))
    tabspec = pl.BlockSpec((tile, BRANCH_W), lambda i: (prev(i) % seq_blocks, 0))
    out_w = [Q_EXP_W, KV_W, KV_W, Q_EXP_W, KV_W, KV_W, BRANCH_W, BRANCH_W, BRANCH_W]
    combine_specs = [row_cur(d), pl.BlockSpec((TOP_K, tile, d // 2), lambda i: (0, cur(i), 0)), row_cur(8),
                     mod_cur, mod_cur, mod_cur, _const_spec((1, d)), _const_spec((d, D_EXPERT)),
                     _const_spec((d, D_EXPERT)), _const_spec((D_EXPERT, d)), _const_spec((1, d))]
    inproj_specs = [mod_prev, mod_prev, _const_spec((1, d)), _const_spec((d, OFF_GATE)),
                    _const_spec((1, BRANCH_W)), _const_spec((1, KV_W)), _const_spec((BRANCH_W, BRANCH_W)),
                    _const_spec((1, BRANCH_W)), _const_spec((4, GM_CHUNK, GM_CHUNK)), _const_spec((GM_CHUNK, BRANCH_W)),
                    _const_spec((BRANCH_W, 2 * BRANCH_W)), tabspec, tabspec, tabspec]
    assert len(combine_specs) == N_COMBINE_INPUTS and len(inproj_specs) == N_INPROJ_INPUTS - 1
    return pl.pallas_call(
        _combine_project_kernel,
        out_shape=[jax.ShapeDtypeStruct((m, d), f32)] + [jax.ShapeDtypeStruct((m, w), bf16) for w in out_w],
        grid=(n_tiles + 1,),
        in_specs=combine_specs + inproj_specs,
        out_specs=[row_cur(d)] + [row_prev(w) for w in out_w],
        scratch_shapes=[pltpu.VMEM((tile, d), f32)],
        compiler_params=_params(("arbitrary",)),
        name="combine_and_project",
    )(x2, yg, wt_rows, mods["g2"], mods["sh2"], mods["sc2"], lw["norm2_g"], lw["w_sh_gate"], lw["w_sh_up"],
      lw["w_sh_down"], final_g,
      next_sh, next_sc, next_lw["norm1_g"], next_lw["w_z"], next_lw["qn"], next_lw["kn"], next_lw["bd"],
      next_lw["gm_norm_g"], next_lw["gm_ws"], next_lw["gm_bias"], next_lw["cs64"], cos_t, sa_t, sb_t)


def routed_experts(h2p, eid, rank, counts, lw, layer):
    m = h2p.shape[0]
    p_max = m * TOP_K + N_EXPERTS * EXPERT_TILE
    pos, tile_exp, tile_valid = routing_plan(eid, rank, counts, p_max)
    xs = sc_scatter_rows(h2p, pos, p_max)
    ys = grouped_experts(xs, tile_exp, tile_valid, lw["w_exp_gate"], lw["w_exp_up"], lw["w_exp_down"], layer)
    return sc_gather_rows(ys, pos[:TOP_K].reshape(TOP_K * m)).reshape(TOP_K, m, D_MODEL // 2)


def rope_tables(seq):
    rows = seq // GRID_W
    row = jnp.repeat(jnp.arange(rows), GRID_W).astype(f32)
    col = jnp.tile(jnp.arange(GRID_W), rows).astype(f32)
    axis_dim = HEAD_DIM // 2
    inv_freq = 1.0 / (ROPE_THETA ** (jnp.arange(0, axis_dim, 2, dtype=f32) / axis_dim))
    ang_r = row[:, None] * inv_freq
    ang_c = col[:, None] * inv_freq
    ang = jnp.concatenate([ang_r, ang_r, ang_c, ang_c], axis=-1)
    cos, sin = jnp.cos(ang), jnp.sin(ang)
    seg = (jnp.arange(HEAD_DIM) // 16) % 2
    sa = jnp.where(seg == 0, -sin, 0.0)
    sb = jnp.where(seg == 1, sin, 0.0)
    rep = lambda t: jnp.tile(t, (1, 4))
    return rep(cos), rep(sa), rep(sb)


def _router_weight_pieces(w):
    hi, mid, lo = _split3(w)
    return jnp.concatenate([hi, mid, lo, jnp.zeros_like(hi)], axis=1)


def identity_rope_tables(rows):
    return jnp.ones((rows, BRANCH_W), f32), jnp.zeros((rows, BRANCH_W), f32), jnp.zeros((rows, BRANCH_W), f32)


def kernel(x, c, ctx, c_ctx, w_mod, b_mod, norm1_g, norm2_g, w_in, q_norm_g, k_norm_g, sink, gm_norm_g, gm_ws, gm_b, w_br, w_o, w_router, router_bias, w_exp_gate, w_exp_up, w_exp_down, w_sh_gate, w_sh_up, w_sh_down, final_norm_g):
    bsz_all, seq, d = x.shape
    n_ctx = ctx.shape[1]
    depth = w_mod.shape[0]

    cc = jnp.concatenate([c, c_ctx[None, :], jnp.zeros((MOD_ROWS - bsz_all - 1, d), f32)], axis=0)
    mod_all = compute_mod(cc, w_mod, b_mod)

    lat_tables = rope_tables(seq)
    ctx_tables = identity_rope_tables(ROW_TILE)
    wc_lat, ws_lat = dft_tables(seq)
    wc_ctx, ws_ctx = dft_tables(n_ctx)
    cs64 = channel_dft_table()
    bd = jnp.asarray(np.kron(np.eye(4), np.full((HEAD_DIM, HEAD_DIM), 1.0 / HEAD_DIM)), dtype=bf16)
    final_g = final_norm_g.reshape(1, d)
    tri = jnp.asarray(np.triu(np.ones((MERGE_TILE, MERGE_TILE)), 1), dtype=bf16)

    lws = []
    for l in range(depth):
        lws.append({
            "norm1_g": norm1_g[l].reshape(1, d),
            "norm2_g": norm2_g[l].reshape(1, d),
            "w_z": w_in[l, :, :OFF_GATE].astype(bf16),
            "w_gate": w_in[l, :, OFF_GATE:].astype(bf16),
            "qn": jnp.tile(q_norm_g[l], BRANCH_W // HEAD_DIM).reshape(1, BRANCH_W),
            "kn": jnp.tile(k_norm_g[l], KV_W // HEAD_DIM).reshape(1, KV_W),
            "bd": bd,
            "gm_norm_g": gm_norm_g[l].reshape(1, BRANCH_W),
            "gm_ws": gm_ws[l].astype(bf16),
            "gm_bias": jnp.repeat(gm_b[l].T, GROUP_DIM, axis=1),
            "cs64": cs64,
            "w_br": w_br[l].astype(bf16),
            "w_o": w_o[l].astype(bf16),
            "w_router3": _router_weight_pieces(w_router[l]),
            "router_bias": router_bias[l].reshape(N_EXPERTS, 1),
            "tri": tri,
            "w_exp_gate": w_exp_gate,
            "w_exp_up": w_exp_up,
            "w_exp_down": w_exp_down,
            "w_sh_gate": w_sh_gate[l].astype(bf16),
            "w_sh_up": w_sh_up[l].astype(bf16),
            "w_sh_down": w_sh_down[l].astype(bf16),
        })

    return _layers(x, ctx, mod_all[:, :bsz_all], mod_all[:, bsz_all], lws, sink, lat_tables, ctx_tables,
                   (wc_lat, ws_lat), (wc_ctx, ws_ctx), final_g)


def _layers(x, ctx, mod_lat, mod_ctx, lws, sink, lat_tables, ctx_tables, dft_lat, dft_ctx, final_g):
    bsz, seq, d = x.shape
    n_ctx = ctx.shape[1]
    depth = len(lws)
    n_lat = bsz * seq
    n_cx = bsz * n_ctx
    wc_lat, ws_lat = dft_lat
    wc_ctx, ws_ctx = dft_ctx
    xl = x.reshape(n_lat, d)
    xc = ctx.reshape(n_cx, d)
    lat_proj = None
    for l in range(depth):
        ctx_out = l < depth - 1
        lw = lws[l]
        names = ("sh1", "sc1", "g1", "sh2", "sc2", "g2")
        mods_lat = {n: mod_lat[l, :, i * d:(i + 1) * d].reshape(bsz, 1, d) for i, n in enumerate(names)}
        mods_ctx = {n: mod_ctx[l, i * d:(i + 1) * d].reshape(1, 1, d) for i, n in enumerate(names)}
        sink_l = sink[l]

        if lat_proj is None:
            lat_proj = in_projection(xl, mods_lat["sh1"], mods_lat["sc1"], seq, lat_tables, seq, ROW_TILE, lw)
        qg, kg, vg, qw, kw, vw, fxc, fxs, o_d = lat_proj
        cqg, ckg, cvg, cqw, ckw, cvw, cfxc, cfxs, co_d = in_projection(
            xc, mods_ctx["sh1"], mods_ctx["sc1"], n_cx, ctx_tables, ROW_TILE, ROW_TILE, lw)
        b3 = lambda t, rows: t.reshape(bsz, rows, t.shape[-1])
        ckg3, cvg3, ckw3, cvw3 = b3(ckg, n_ctx), b3(cvg, n_ctx), b3(ckw, n_ctx), b3(cvw, n_ctx)

        o_a = dft_mix(wc_lat, ws_lat, b3(fxc, seq), b3(fxs, seq), ROW_TILE).reshape(n_lat, BRANCH_W)
        o_b, o_c = latent_attention(b3(qg, seq), b3(qw, seq), b3(kg, seq), b3(vg, seq), b3(kw, seq), b3(vw, seq),
                                    ckg3, cvg3, ckw3, cvw3, sink_l)
        o_b, o_c = o_b.reshape(n_lat, BRANCH_W), o_c.reshape(n_lat, BRANCH_W)
        m_total = n_lat + (n_cx if ctx_out else 0)
        route = None
        if ctx_out:
            route = [jnp.zeros((m_total, d // 2), i32), jnp.zeros((8, m_total), i32), jnp.zeros((8, m_total), f32),
                     jnp.zeros((8, m_total), i32), jnp.zeros((N_EXPERTS, 1), i32)]
        xl, *route = merge_and_route(xl, mods_lat, seq, (o_a, o_b, o_c, o_d), MERGE_TILE, lw, m_total, 0, route)
        if ctx_out:
            co_a = dft_mix(wc_ctx, ws_ctx, b3(cfxc, n_ctx), b3(cfxs, n_ctx), n_ctx).reshape(n_cx, BRANCH_W)
            co_b = full_attention(b3(cqg, n_ctx), [(ckg3, cvg3)], None, n_ctx).reshape(n_cx, BRANCH_W)
            co_c = full_attention(b3(cqw, n_ctx), [(ckw3, cvw3)], sink_l, n_ctx).reshape(n_cx, BRANCH_W)
            xc, *route = merge_and_route(xc, mods_ctx, n_cx, (co_a, co_b, co_c, co_d), MERGE_TILE, lw, m_total, n_lat,
                                         route)
        h2p, eid, wt, rank, counts = route
        yg = routed_experts(h2p, eid, rank, counts, lw, l)
        wt_rows = wt.T
        if ctx_out:
            nxt = {n: mod_lat[l + 1, :, i * d:(i + 1) * d].reshape(bsz, 1, d) for i, n in enumerate(names[:2])}
            xl, *lat_proj = combine_and_project(xl, yg, wt_rows, mods_lat, seq, lw, final_g, nxt["sh1"], nxt["sc1"],
                                                lat_tables, seq, lws[l + 1], COMBINE_PROJECT_TILE)
            xc = combine(xc, yg, wt_rows, n_lat, mods_ctx, n_cx, ROW_TILE, lw, final_g, False)
        else:
            xl = combine(xl, yg, wt_rows, 0, mods_lat, seq, ROW_TILE, lw, final_g, True)
    return xl.reshape(bsz, seq, d)
```

```python
import functools
import math

import jax
import jax.numpy as jnp
import numpy as np
from jax import lax
from jax.experimental import pallas as pl
from jax.experimental.pallas import tpu as pltpu
from jax.experimental.pallas import tpu_sc as plsc

f32 = jnp.float32
bf16 = jnp.bfloat16
i32 = jnp.int32

D_MODEL = 1024
HEAD_DIM = 64
GRID_W = 64
ROPE_THETA = 10000.0
ATTN_SCALE = HEAD_DIM ** -0.5
RMS_EPS = 1e-6
NEG_INF = -1e30
Q_BLOCK = 128
WINDOW = 128
GM_CHUNK = 128
N_BRANCHES = 4
BRANCH_W = 256
KV_W = 128
Q_EXP_W = 4 * KV_W
ROPE_SHIFT = HEAD_DIM // 4
GROUP_DIM = 64
OFF_KV = 0
OFF_Q = 512
OFF_FN = 1024
OFF_GM = 1280
OFF_GATE = 1792
N_EXPERTS = 64
TOP_K = 6
N_GROUPS = 8
GROUP_SIZE = N_EXPERTS // N_GROUPS
TOPK_GROUPS = 4
D_EXPERT = 256
ROUTED_SCALE = 2.5

SC_CORES = 2
SC_SUBCORES = 16
SC_WORKERS = SC_CORES * SC_SUBCORES
SC_IDX_CHUNK = 128

ROW_TILE = 1024
COMBINE_PROJECT_TILE = 512
MERGE_TILE = 512
ROUTER_AFTER_BRANCHES = 1
RANK_AFTER_BRANCHES = 3
EXPERT_TILE = 1024
LATENT_ATTN_Q_ROWS = 512
LATENT_ATTN_GLOBAL_BLOCKS = 2
FULL_ATTN_ROW_BLOCKS = 2
MOD_ROWS = 24
VMEM_LIMIT = 56 * 1024 * 1024


def _params(sem, vmem=VMEM_LIMIT):
    return pltpu.CompilerParams(dimension_semantics=sem, vmem_limit_bytes=vmem)


def _const_spec(shape):
    nd = len(shape)
    return pl.BlockSpec(shape, lambda *_: (0,) * nd, pipeline_mode=pl.Buffered(1))


def _rms_mod(x, g, sc, sh):
    ms = jnp.mean(x * x, axis=-1, keepdims=True)
    return (x * lax.rsqrt(ms + RMS_EPS) * g) * (1.0 + sc) + sh


def _gelu(x):
    return 0.5 * x * (1.0 + jnp.tanh(math.sqrt(2.0 / math.pi) * (x + 0.044715 * (x * x * x))))


def _silu(x):
    return x * jax.nn.sigmoid(x)


def _mod_kernel(a_ref, w_ref, b_ref, o_ref):
    a = _silu(a_ref[...]).astype(bf16)
    o_ref[0] = jnp.dot(a, w_ref[0].astype(bf16), preferred_element_type=f32) + b_ref[0]


def compute_mod(cc, w_mod, b_mod):
    depth, d, n = w_mod.shape
    tn = 1536
    return pl.pallas_call(
        _mod_kernel,
        out_shape=jax.ShapeDtypeStruct((depth, MOD_ROWS, n), f32),
        grid=(depth, n // tn),
        in_specs=[
            pl.BlockSpec((MOD_ROWS, d), lambda l, j: (0, 0)),
            pl.BlockSpec((1, d, tn), lambda l, j: (l, 0, j)),
            pl.BlockSpec((1, 1, tn), lambda l, j: (l, 0, j)),
        ],
        out_specs=pl.BlockSpec((1, MOD_ROWS, tn), lambda l, j: (l, 0, j)),
        compiler_params=_params(("parallel", "parallel")),
        name="mod_proj",
    )(cc, w_mod, b_mod.reshape(depth, 1, n))


def _inproj_kernel(x_ref, sh_ref, sc_ref, g_ref, w_ref, *rest):
    _inproj_finish(_inproj_project(x_ref, sh_ref, sc_ref, g_ref, w_ref), *rest)


def _inproj_project(x_ref, sh_ref, sc_ref, g_ref, w_ref):
    hb = _rms_mod(x_ref[...], g_ref[...], sc_ref[0], sh_ref[0]).astype(bf16)
    return [jnp.dot(hb, w_ref[:, a:b], preferred_element_type=f32)
            for a, b in ((OFF_KV, OFF_Q), (OFF_Q, OFF_FN), (OFF_FN, OFF_GM), (OFF_GM, OFF_GATE))]


def _inproj_finish(raw, qn_ref, kn_ref, bd_ref, gmg_ref, ws_ref, gb_ref, cs_ref, cos_ref, sa_ref, sb_ref,
                   qg_ref, kg_ref, vg_ref, qw_ref, kw_ref, vw_ref, xc_ref, xs_ref, od_ref):
    kv, qq, fn, uv = raw
    fn = fn.astype(bf16)
    tile = kv.shape[0]

    def square_pieces(t):
        sq = t * t
        hi = sq.astype(bf16)
        return hi, (sq - hi.astype(f32)).astype(bf16)

    def headnorm(t, pieces, gain):
        w = t.shape[1]
        b = bd_ref[:w, :w]
        ms = (jnp.dot(pieces[0], b, preferred_element_type=f32) + jnp.dot(pieces[1], b, preferred_element_type=f32))
        return t * lax.rsqrt(ms + RMS_EPS) * gain

    def rope(t):
        w = t.shape[1]
        return (t * cos_ref[:, :w] + pltpu.roll(t, w - ROPE_SHIFT, 1) * sa_ref[:, :w]
                + pltpu.roll(t, ROPE_SHIFT, 1) * sb_ref[:, :w])

    def expand_heads(q):
        lane = lax.broadcasted_iota(i32, (1, KV_W), 1)
        low = lane < HEAD_DIM
        blocks = []
        for kv in range(2):
            pair = q[:, KV_W * kv:KV_W * (kv + 1)]
            swapped = pltpu.roll(pair, HEAD_DIM, 1)
            keep = low if kv == 0 else jnp.logical_not(low)
            g0, g1 = (pair, swapped) if kv == 0 else (swapped, pair)
            blocks.append(jnp.where(keep, g0, 0.0))
            blocks.append(jnp.where(keep, g1, 0.0))
        return jnp.concatenate(blocks, axis=1)

    k_sq = square_pieces(kv[:, :KV_W])
    q_sq = square_pieces(qq[:, :BRANCH_W])
    u = _gelu(uv[:, :BRANCH_W])
    v = _gelu(uv[:, BRANCH_W:])
    vms = jnp.mean(v * v, axis=-1, keepdims=True)
    vn = (v * lax.rsqrt(vms + RMS_EPS) * gmg_ref[...]).astype(bf16)

    kg_ref[...] = rope(headnorm(kv[:, :KV_W], k_sq, kn_ref[...])).astype(bf16)
    vg_ref[...] = kv[:, KV_W:2 * KV_W].astype(bf16)
    kw_ref[...] = rope(kv[:, 2 * KV_W:3 * KV_W]).astype(bf16)
    vw_ref[...] = kv[:, 3 * KV_W:].astype(bf16)

    qg = rope(headnorm(qq[:, :BRANCH_W], q_sq, qn_ref[...])) * ATTN_SCALE
    qg_ref[...] = expand_heads(qg).astype(bf16)
    qw = rope(qq[:, BRANCH_W:]) * ATTN_SCALE
    qw_ref[...] = expand_heads(qw).astype(bf16)

    xcs = jnp.dot(fn, cs_ref[...], preferred_element_type=f32)
    xc_ref[...] = xcs[:, :BRANCH_W].astype(bf16)
    xs_ref[...] = xcs[:, BRANCH_W:].astype(bf16)

    lane_grp = lax.broadcasted_iota(i32, (1, BRANCH_W), 1) // GROUP_DIM
    for c in range(tile // GM_CHUNK):
        rows = slice(c * GM_CHUNK, (c + 1) * GM_CHUNK)
        vch = vn[rows]
        sv = gb_ref[...]
        for g in range(4):
            r = jnp.dot(ws_ref[g], vch, preferred_element_type=f32)
            sv = sv + jnp.where(lane_grp == g, r, 0.0)
        od_ref[rows, :] = (u[rows] * sv).astype(bf16)


def in_projection(x2, sh, sc, rows_per_mod, tables, rows_per_seq, tile, lw):
    m, d = x2.shape
    cos_t, sa_t, sb_t = tables
    seq_blocks = rows_per_seq // tile
    row = lambda w: pl.BlockSpec((tile, w), lambda i: (i, 0))
    modspec = pl.BlockSpec((1, 1, d), lambda i: ((i * tile) // rows_per_mod, 0, 0))
    tabspec = pl.BlockSpec((tile, BRANCH_W), lambda i: (i % seq_blocks, 0))
    out_w = [Q_EXP_W, KV_W, KV_W, Q_EXP_W, KV_W, KV_W, BRANCH_W, BRANCH_W, BRANCH_W]
    return pl.pallas_call(
        _inproj_kernel,
        out_shape=[jax.ShapeDtypeStruct((m, w), bf16) for w in out_w],
        grid=(m // tile,),
        in_specs=[
            row(d), modspec, modspec, _const_spec((1, d)), _const_spec((d, OFF_GATE)),
            _const_spec((1, BRANCH_W)), _const_spec((1, KV_W)), _const_spec((BRANCH_W, BRANCH_W)),
            _const_spec((1, BRANCH_W)), _const_spec((4, GM_CHUNK, GM_CHUNK)), _const_spec((GM_CHUNK, BRANCH_W)),
            _const_spec((BRANCH_W, 2 * BRANCH_W)),
            tabspec, tabspec, tabspec,
        ],
        out_specs=[row(w) for w in out_w],
        compiler_params=_params(("parallel",)),
        name="in_projection",
    )(x2, sh, sc, lw["norm1_g"], lw["w_z"], lw["qn"], lw["kn"], lw["bd"], lw["gm_norm_g"], lw["gm_ws"], lw["gm_bias"],
      lw["cs64"], cos_t, sa_t, sb_t)


def _dft_kernel(wc_ref, ws_ref, xc_ref, xs_ref, o_ref):
    acc = jnp.dot(wc_ref[...], xc_ref[0], preferred_element_type=f32)
    acc = acc + jnp.dot(ws_ref[...], xs_ref[0], preferred_element_type=f32)
    o_ref[0] = acc.astype(bf16)


def dft_mix(wc, ws, xc, xs, tile):
    nb, length, w = xc.shape
    return pl.pallas_call(
        _dft_kernel,
        out_shape=jax.ShapeDtypeStruct((nb, length, w), bf16),
        grid=(length // tile, nb),
        in_specs=[
            pl.BlockSpec((tile, length), lambda i, b: (i, 0)),
            pl.BlockSpec((tile, length), lambda i, b: (i, 0)),
            pl.BlockSpec((1, length, w), lambda i, b: (b, 0, 0)),
            pl.BlockSpec((1, length, w), lambda i, b: (b, 0, 0)),
        ],
        out_specs=pl.BlockSpec((1, tile, w), lambda i, b: (b, i, 0)),
        compiler_params=_params(("parallel", "parallel")),
        name="dft_mix",
    )(wc, ws, xc, xs)


def dft_tables(length):
    jk = (np.arange(length)[:, None] * np.arange(length)[None, :]) % length
    ang = 2.0 * np.pi * jk / length
    s = 1.0 / math.sqrt(length)
    return jnp.asarray(np.cos(ang) * s, dtype=bf16), jnp.asarray(-np.sin(ang) * s, dtype=bf16)


def channel_dft_table():
    jk = (np.arange(64)[:, None] * np.arange(64)[None, :]) % 64
    ang = 2.0 * np.pi * jk / 64
    eye = np.eye(4)
    c = np.kron(eye, np.cos(ang) / 8.0)
    s = np.kron(eye, np.sin(ang) / 8.0)
    return jnp.asarray(np.concatenate([c, s], axis=1), dtype=bf16)


def _attend_blocks(blocks, sink_ref):
    lane = lax.broadcasted_iota(i32, (1, KV_W), 1)
    low = lane < HEAD_DIM
    units = []
    for q_all, pieces, with_sink in blocks:
        qb = q_all.shape[0]
        for kv in range(2):
            q = jnp.concatenate([q_all[:, KV_W * (2 * kv):KV_W * (2 * kv + 1)],
                                 q_all[:, KV_W * (2 * kv + 1):KV_W * (2 * kv + 2)]], axis=0)
            sink_col = None
            if with_sink:
                sink_col = jnp.concatenate([jnp.full((qb, 1), sink_ref[2 * kv], f32),
                                            jnp.full((qb, 1), sink_ref[2 * kv + 1], f32)], axis=0)
            units.append((q, pieces, sink_col, low if kv == 0 else jnp.logical_not(low)))

    scores = []
    for q, pieces, _, _ in units:
        unit_scores = []
        for k, _, mask in pieces:
            s = lax.dot_general(q, k, (((1,), (1,)), ((), ())), preferred_element_type=f32)
            unit_scores.append(s if mask is None else jnp.where(mask, s, NEG_INF))
        scores.append(unit_scores)

    maxes = []
    for (_, _, sink_col, _), unit_scores in zip(units, scores):
        m = unit_scores[0].max(axis=-1, keepdims=True)
        for s in unit_scores[1:]:
            m = jnp.maximum(m, s.max(axis=-1, keepdims=True))
        maxes.append(m if sink_col is None else jnp.maximum(m, sink_col))

    probs = [[jnp.exp((s - m).astype(bf16)) for s in unit_scores] for unit_scores, m in zip(scores, maxes)]

    results = []
    for (_, pieces, sink_col, own), unit_probs, m in zip(units, probs, maxes):
        acc = None
        for p, (_, v, _) in zip(unit_probs, pieces):
            pv = jnp.dot(p, jnp.where(own, v, jnp.ones_like(v)), preferred_element_type=f32)
            acc = pv if acc is None else acc + pv
        denom = pltpu.roll(acc, HEAD_DIM, 1)
        if sink_col is not None:
            denom = denom + jnp.exp(sink_col - m)
        results.append(acc * (1.0 / denom))

    outs = []
    for i, (q_all, _, _) in enumerate(blocks):
        qb = q_all.shape[0]
        r_kv0, r_kv1 = results[2 * i], results[2 * i + 1]
        lo = jnp.where(low, r_kv0[:qb], pltpu.roll(r_kv0[qb:], HEAD_DIM, 1))
        hi = jnp.where(low, pltpu.roll(r_kv1[:qb], HEAD_DIM, 1), r_kv1[qb:])
        outs.append(jnp.concatenate([lo, hi], axis=1))
    return outs


def _full_attn_kernel(*refs, n_pieces, has_sink, row_blocks):
    pos = 0
    sink_ref = None
    if has_sink:
        sink_ref = refs[0]
        pos = 1
    q_ref = refs[pos]
    kv_refs = refs[pos + 1:pos + 1 + 2 * n_pieces]
    o_ref = refs[pos + 1 + 2 * n_pieces]
    pieces = [(kv_refs[2 * i][0], kv_refs[2 * i + 1][0], None) for i in range(n_pieces)]
    rows = q_ref.shape[1] // row_blocks
    blocks = [(q_ref[0, j * rows:(j + 1) * rows, :], pieces, has_sink) for j in range(row_blocks)]
    for j, out in enumerate(_attend_blocks(blocks, sink_ref)):
        o_ref[0, j * rows:(j + 1) * rows, :] = out.astype(bf16)


def full_attention(q, pieces, sink, qb):
    nb, lq, _ = q.shape
    in_specs = []
    args = []
    if sink is not None:
        in_specs.append(pl.BlockSpec(memory_space=pltpu.SMEM))
        args.append(sink)
    in_specs.append(pl.BlockSpec((1, qb, Q_EXP_W), lambda b, i: (b, i, 0)))
    args.append(q)
    for k, v in pieces:
        spec = pl.BlockSpec((1, k.shape[1], KV_W), lambda b, i: (b, 0, 0))
        in_specs += [spec, spec]
        args += [k, v]
    return pl.pallas_call(
        functools.partial(_full_attn_kernel, n_pieces=len(pieces), has_sink=sink is not None,
                          row_blocks=FULL_ATTN_ROW_BLOCKS),
        out_shape=jax.ShapeDtypeStruct((nb, lq, BRANCH_W), bf16),
        grid=(nb, lq // qb),
        in_specs=in_specs,
        out_specs=pl.BlockSpec((1, qb, BRANCH_W), lambda b, i: (b, i, 0)),
        compiler_params=_params(("parallel", "parallel")),
        name="full_attention",
    )(*args)


def _latent_attn_kernel(sink_ref, qg_ref, qw_ref, kg_ref, vg_ref, kw_ref, vw_ref, kgc_ref, vgc_ref, kwc_ref, vwc_ref,
                        ob_ref, oc_ref, *, seq):
    rows = qg_ref.shape[1]
    global_rows = rows // LATENT_ATTN_GLOBAL_BLOCKS
    window_blocks = rows // Q_BLOCK
    global_pieces = [(kgc_ref[0], vgc_ref[0], None), (kg_ref[0], vg_ref[0], None)]
    global_work = [(qg_ref[0, j * global_rows:(j + 1) * global_rows, :], global_pieces, False)
                   for j in range(LATENT_ATTN_GLOBAL_BLOCKS)]
    span = 3 * Q_BLOCK
    ctx_piece = (kwc_ref[0], vwc_ref[0], None)
    work = []
    for j in range(window_blocks):
        n = pl.program_id(1) * window_blocks + j
        start = pl.multiple_of(jnp.clip((n - 1) * Q_BLOCK, 0, seq - span), Q_BLOCK)
        kwin = kw_ref[0, pl.ds(start, span), :]
        vwin = vw_ref[0, pl.ds(start, span), :]
        row = lax.broadcasted_iota(i32, (2 * Q_BLOCK, span), 0) % Q_BLOCK + n * Q_BLOCK
        col = lax.broadcasted_iota(i32, (2 * Q_BLOCK, span), 1) + start
        mask = jnp.abs(row - col) <= WINDOW
        work.append((qw_ref[0, j * Q_BLOCK:(j + 1) * Q_BLOCK, :], [ctx_piece, (kwin, vwin, mask)], True))
    outs = _attend_blocks(work + global_work, sink_ref)
    for j in range(window_blocks):
        oc_ref[0, j * Q_BLOCK:(j + 1) * Q_BLOCK, :] = outs[j].astype(bf16)
    for j in range(LATENT_ATTN_GLOBAL_BLOCKS):
        ob_ref[0, j * global_rows:(j + 1) * global_rows, :] = outs[window_blocks + j].astype(bf16)


def latent_attention(qg, qw, kg, vg, kw, vw, kgc, vgc, kwc, vwc, sink):
    nb, seq, _ = qg.shape
    n_ctx = kgc.shape[1]
    rows = LATENT_ATTN_Q_ROWS
    assert seq % rows == 0 and rows % Q_BLOCK == 0 and seq >= 3 * Q_BLOCK, seq
    qspec = pl.BlockSpec((1, rows, Q_EXP_W), lambda b, i: (b, i, 0))
    full = lambda l: pl.BlockSpec((1, l, KV_W), lambda b, i: (b, 0, 0))
    ospec = pl.BlockSpec((1, rows, BRANCH_W), lambda b, i: (b, i, 0))
    return pl.pallas_call(
        functools.partial(_latent_attn_kernel, seq=seq),
        out_shape=[jax.ShapeDtypeStruct((nb, seq, BRANCH_W), bf16)] * 2,
        grid=(nb, seq // rows),
        in_specs=[pl.BlockSpec(memory_space=pltpu.SMEM), qspec, qspec, full(seq), full(seq), full(seq), full(seq),
                  full(n_ctx), full(n_ctx), full(n_ctx), full(n_ctx)],
        out_specs=[ospec, ospec],
        compiler_params=_params(("parallel", "parallel")),
        name="latent_attention",
    )(sink, qg, qw, kg, vg, kw, vw, kgc, vgc, kwc, vwc)


def _route(logits_t, bias_col):
    t = logits_t.shape[1]
    scores = jax.nn.sigmoid(logits_t)
    choice = scores + bias_col
    sub = lax.broadcasted_iota(i32, (GROUP_SIZE, t), 0)
    grp_score = []
    for g in range(N_GROUPS):
        cg = choice[g * GROUP_SIZE:(g + 1) * GROUP_SIZE]
        m1 = cg.max(axis=0, keepdims=True)
        first = jnp.min(jnp.where(cg == m1, sub, GROUP_SIZE), axis=0, keepdims=True)
        m2 = jnp.where(sub == first, -jnp.inf, cg).max(axis=0, keepdims=True)
        grp_score.append(m1 + m2)
    keep = []
    for g in range(N_GROUPS):
        beaten = jnp.zeros((1, t), i32)
        for o in range(N_GROUPS):
            if o == g:
                continue
            wins = (grp_score[o] > grp_score[g]) | ((grp_score[o] == grp_score[g]) & (o < g))
            beaten = beaten + wins.astype(i32)
        keep.append(jnp.broadcast_to(beaten < TOPK_GROUPS, (GROUP_SIZE, t)))
    masked = jnp.where(jnp.concatenate(keep, axis=0), choice, NEG_INF)
    eid = lax.broadcasted_iota(i32, (N_EXPERTS, t), 0)
    ids, wts = [], []
    for _ in range(TOP_K):
        m = masked.max(axis=0, keepdims=True)
        pick = jnp.min(jnp.where(masked == m, eid, N_EXPERTS), axis=0, keepdims=True)
        sel = eid == pick
        ids.append(pick)
        wts.append(jnp.sum(jnp.where(sel, scores, 0.0), axis=0, keepdims=True))
        masked = jnp.where(sel, -jnp.inf, masked)
    total = wts[0]
    for w in wts[1:]:
        total = total + w
    norm = ROUTED_SCALE / total
    return ids, [w * norm for w in wts]


def _pack_bf16_pairs(x):
    w = x.shape[1] // 2
    lo = lax.bitcast_convert_type(x[:, :w].astype(bf16).astype(f32), i32)
    hi = lax.bitcast_convert_type(x[:, w:].astype(bf16).astype(f32), i32)
    return lax.shift_right_logical(lo, 16) | (hi & jnp.int32(-65536))


def _unpack_bf16_pairs(p):
    lo = lax.bitcast_convert_type(lax.shift_left(p, 16), f32)
    hi = lax.bitcast_convert_type(p & jnp.int32(-65536), f32)
    return lo.astype(bf16), hi.astype(bf16)


def _merge_kernel(x_ref, sh_ref, sc_ref, g1_ref, sh2_ref, sc2_ref, n1_ref, n2_ref, oa_ref, ob_ref, oc_ref, od_ref,
                  wg_ref, wbr_ref, wo_ref, wr_ref, rb_ref, tri_ref, *rest, extends):
    if extends:
        cnt_in_ref = rest[0]
        rest = rest[5:]
    xo_ref, h2_ref, eid_ref, wt_ref, rank_ref, cnt_ref, run_ref, xn_prev_ref = rest
    step = pl.program_id(0)

    @pl.when(step == 0)
    def _():
        run_ref[...] = cnt_in_ref[...].astype(f32) if extends else jnp.zeros_like(run_ref)
        xn_prev_ref[...] = jnp.zeros_like(xn_prev_ref)

    x = x_ref[...]
    hb = _rms_mod(x, n1_ref[...], sc_ref[0], sh_ref[0]).astype(bf16)
    h2 = _rms_mod(xn_prev_ref[...], n2_ref[...], sc2_ref[0], sh2_ref[0])
    h2_ref[...] = _pack_bf16_pairs(h2)
    y = None
    for i, o_ref in enumerate((oa_ref, ob_ref, oc_ref, od_ref)):
        if i == ROUTER_AFTER_BRANCHES:
            ids, wts = _route(_router_logits(wr_ref[...], h2), rb_ref[...])
        if i == RANK_AFTER_BRANCHES:
            _rank_entries(ids, wts, tri_ref, run_ref, (step > 0).astype(f32), eid_ref, wt_ref, rank_ref, cnt_ref)
        logit = jnp.dot(hb, wg_ref[:, i * D_MODEL:(i + 1) * D_MODEL], preferred_element_type=f32)
        proj = jnp.dot(o_ref[...], wbr_ref[i], preferred_element_type=f32)
        term = jax.nn.sigmoid(logit.astype(bf16)) * proj.astype(bf16)
        y = term if y is None else y + term
    xn = x + g1_ref[0] * jnp.dot(y, wo_ref[...], preferred_element_type=f32)
    xo_ref[...] = xn
    xn_prev_ref[...] = xn


def _split3(x):
    def head(v):
        return lax.bitcast_convert_type(lax.bitcast_convert_type(v, i32) & jnp.int32(-65536), f32)

    hi = head(x)
    r1 = x - hi
    mid = head(r1)
    lo = r1 - mid
    return hi.astype(bf16), mid.astype(bf16), lo.astype(bf16)


def _router_logits(w3, h2):
    prod = None
    for piece in _split3(h2):
        p = jnp.dot(piece, w3, preferred_element_type=f32)
        prod = p if prod is None else prod + p
    lane = lax.broadcasted_iota(i32, (1, 2 * N_EXPERTS), 1)
    low = prod[:, :2 * N_EXPERTS]
    logits = low + pltpu.roll(low, N_EXPERTS, 1) + prod[:, 2 * N_EXPERTS:]
    return jnp.where(lane < N_EXPERTS, logits, 0.0).T[:N_EXPERTS]


def _rank_entries(ids, wts, tri_ref, run_ref, live, eid_ref, wt_ref, rank_ref, cnt_ref):
    t = ids[0].shape[1]
    eid = lax.broadcasted_iota(i32, (N_EXPERTS, t), 0)
    hits = [eid == pick for pick in ids]
    chosen = hits[0]
    for h in hits[1:]:
        chosen = chosen | h
    chosen = jnp.where(chosen, 1.0, 0.0)
    prefix = jnp.dot(chosen.astype(bf16), tri_ref[...], preferred_element_type=f32)
    offset = run_ref[...] + prefix
    ranks = [jnp.sum(jnp.where(h, offset, 0.0), axis=0, keepdims=True).astype(i32) for h in hits]
    run_ref[...] += live * jnp.sum(chosen, axis=1, keepdims=True)
    cnt_ref[...] = run_ref[...].astype(i32)

    pad_i = [jnp.zeros((1, t), i32)] * (8 - TOP_K)
    eid_ref[...] = jnp.concatenate(ids + pad_i, axis=0)
    rank_ref[...] = jnp.concatenate(ranks + pad_i, axis=0)
    wt_ref[...] = jnp.concatenate(wts + [jnp.zeros((1, t), f32)] * (8 - TOP_K), axis=0)


def merge_and_route(x2, mods, rows_per_mod, branches, tile, lw, m_total, row_offset, prior):
    m, d = x2.shape
    off = row_offset // tile
    n_tiles = m // tile
    cur = lambda i: jnp.minimum(i, n_tiles - 1)
    prev = lambda i: jnp.maximum(i - 1, 0)
    row = lambda w: pl.BlockSpec((tile, w), lambda i: (cur(i), 0))
    row_prev = lambda w: pl.BlockSpec((tile, w), lambda i: (prev(i) + off, 0))
    modspec = pl.BlockSpec((1, 1, d), lambda i: ((cur(i) * tile) // rows_per_mod, 0, 0))
    modspec_prev = pl.BlockSpec((1, 1, d), lambda i: ((prev(i) * tile) // rows_per_mod, 0, 0))
    col = pl.BlockSpec((8, tile), lambda i: (0, prev(i) + off))
    in_specs = [row(d)] + [modspec] * 3 + [modspec_prev] * 2 + [_const_spec((1, d)), _const_spec((1, d))] + [
        row(BRANCH_W)] * 4 + [
        _const_spec((d, N_BRANCHES * d)), _const_spec((N_BRANCHES, BRANCH_W, d)), _const_spec((d, d)),
        _const_spec((d, 4 * N_EXPERTS)), _const_spec((N_EXPERTS, 1)), _const_spec((tile, tile))]
    args = [x2, mods["sh1"], mods["sc1"], mods["g1"], mods["sh2"], mods["sc2"], lw["norm1_g"], lw["norm2_g"],
            *branches, lw["w_gate"], lw["w_br"], lw["w_o"], lw["w_router3"], lw["router_bias"], lw["tri"]]
    aliases = {}
    if prior is not None:
        h2p, eid, wt, rank, counts = prior
        n_in = len(args)
        in_specs += [_const_spec((N_EXPERTS, 1))] + [pl.BlockSpec(memory_space=pl.ANY)] * 4
        args += [counts, h2p, eid, wt, rank]
        aliases = {n_in + 1 + j: 1 + j for j in range(4)}
    return pl.pallas_call(
        functools.partial(_merge_kernel, extends=prior is not None),
        out_shape=[jax.ShapeDtypeStruct((m, d), f32), jax.ShapeDtypeStruct((m_total, d // 2), i32),
                   jax.ShapeDtypeStruct((8, m_total), i32), jax.ShapeDtypeStruct((8, m_total), f32),
                   jax.ShapeDtypeStruct((8, m_total), i32), jax.ShapeDtypeStruct((N_EXPERTS, 1), i32)],
        grid=(n_tiles + 1,),
        in_specs=in_specs,
        out_specs=[row(d), row_prev(d // 2), col, col, col, pl.BlockSpec((N_EXPERTS, 1), lambda i: (0, 0))],
        scratch_shapes=[pltpu.VMEM((N_EXPERTS, 1), f32), pltpu.VMEM((tile, d), f32)],
        input_output_aliases=aliases,
        compiler_params=_params(("arbitrary",)),
        name="merge_and_route",
    )(*args)


def routing_plan(eid, rank, counts, p_max):
    counts = counts.reshape(N_EXPERTS)
    padded = ((counts + EXPERT_TILE - 1) // EXPERT_TILE) * EXPERT_TILE
    ends = jnp.cumsum(padded)
    starts = ends - padded
    onehot = eid[:, :, None] == jnp.arange(N_EXPERTS, dtype=i32)[None, None, :]
    pos = rank + jnp.sum(jnp.where(onehot, starts[None, None, :], 0), axis=-1)
    n_tiles = p_max // EXPERT_TILE
    tile_start = jnp.arange(n_tiles, dtype=i32) * EXPERT_TILE
    tile_valid = tile_start < ends[-1]
    tile_exp = jnp.sum((ends[None, :] <= tile_start[:, None]).astype(i32), axis=1)
    return pos.astype(i32), jnp.minimum(tile_exp, N_EXPERTS - 1), tile_valid.astype(i32)


def _sc_worker_id():
    return lax.axis_index("subcore") * SC_CORES + lax.axis_index("core")


def sc_scatter_rows(table, pos, p_rows):
    m, w = table.shape
    n_chunks = m // SC_IDX_CHUNK
    steps = -(-n_chunks // SC_WORKERS)
    pos3 = pos.reshape(8, n_chunks, SC_IDX_CHUNK).transpose(1, 0, 2)
    mesh = plsc.VectorSubcoreMesh(core_axis_name="core", subcore_axis_name="subcore")

    @functools.partial(
        pl.kernel,
        out_type=jax.ShapeDtypeStruct((p_rows, w), table.dtype),
        mesh=mesh,
        scratch_types=[
            pltpu.VMEM((8, SC_IDX_CHUNK), i32),
            pltpu.VMEM((SC_IDX_CHUNK, w), table.dtype),
            pltpu.SemaphoreType.DMA,
        ],
    )
    def scatter(x_hbm, p_hbm, o_hbm, idx_v, rows_v, sem):
        wid = _sc_worker_id()

        @pl.loop(0, steps)
        def _(si):
            chunk = si * SC_WORKERS + wid

            @pl.when(chunk < n_chunks)
            def _():
                pltpu.sync_copy(p_hbm.at[chunk], idx_v)
                pltpu.sync_copy(x_hbm.at[pl.ds(chunk * SC_IDX_CHUNK, SC_IDX_CHUNK)], rows_v)
                copies = [pltpu.async_copy(rows_v, o_hbm.at[idx_v.at[k]], sem) for k in range(TOP_K)]
                for cp in copies:
                    cp.wait()

    return scatter(table, pos3)


def sc_gather_rows(table, idx):
    n_idx = idx.shape[0]
    w = table.shape[1]
    n_chunks = n_idx // SC_IDX_CHUNK
    steps = -(-n_chunks // SC_WORKERS)
    half = SC_IDX_CHUNK // 2
    mesh = plsc.VectorSubcoreMesh(core_axis_name="core", subcore_axis_name="subcore")

    @functools.partial(
        pl.kernel,
        out_type=jax.ShapeDtypeStruct((n_idx, w), table.dtype),
        mesh=mesh,
        scratch_types=[
            pltpu.VMEM((SC_IDX_CHUNK,), i32),
            pltpu.VMEM((half, w), table.dtype),
            pltpu.VMEM((half, w), table.dtype),
            pltpu.SemaphoreType.DMA,
            pltpu.SemaphoreType.DMA,
            pltpu.SemaphoreType.DMA,
            pltpu.SemaphoreType.DMA,
        ],
    )
    def gather(x_hbm, i_hbm, o_hbm, idx_v, buf0, buf1, g0_sem, g1_sem, w0_sem, w1_sem):
        wid = _sc_worker_id()

        @pl.loop(0, steps)
        def _(si):
            chunk = si * SC_WORKERS + wid

            @pl.when(chunk < n_chunks)
            def _():
                cbase = chunk * SC_IDX_CHUNK
                pltpu.sync_copy(i_hbm.at[pl.ds(cbase, SC_IDX_CHUNK)], idx_v)
                g0 = pltpu.async_copy(x_hbm.at[idx_v.at[pl.ds(0, half)]], buf0, g0_sem)
                g1 = pltpu.async_copy(x_hbm.at[idx_v.at[pl.ds(half, half)]], buf1, g1_sem)
                g0.wait()
                w0 = pltpu.async_copy(buf0, o_hbm.at[pl.ds(cbase, half)], w0_sem)
                g1.wait()
                w1 = pltpu.async_copy(buf1, o_hbm.at[pl.ds(cbase + half, half)], w1_sem)
                w0.wait()
                w1.wait()

    return gather(table, idx)


def _expert_kernel(te_ref, tv_ref, nx_ref, sl_ref, nu_ref, x_ref, wg_hbm, wu_hbm, wd_hbm, o_ref,
                   wg_f, wu_f, wd_f, wg_b, wu_b, wd_b, sems, *, layer):
    i = pl.program_id(0)

    def weight_copies(expert, slot):
        return [pltpu.make_async_copy(hbm.at[layer, expert], buf.at[slot], sems.at[slot, j])
                for j, (hbm, buf) in enumerate(((wg_hbm, wg_f), (wu_hbm, wu_f), (wd_hbm, wd_f)))]

    @pl.when(i == 0)
    def _():
        for cp in weight_copies(te_ref[0], 0):
            cp.start()

    @pl.when((i == 0) | (te_ref[i] != te_ref[jnp.maximum(i - 1, 0)]))
    def _():
        slot = sl_ref[i]
        for cp in weight_copies(te_ref[i], slot):
            cp.wait()
        wg_b[...] = wg_f[slot].astype(bf16)
        wu_b[...] = wu_f[slot].astype(bf16)
        wd_b[...] = wd_f[slot].astype(bf16)

        @pl.when(nx_ref[i] >= 0)
        def _():
            for cp in weight_copies(nx_ref[i], 1 - slot):
                cp.start()

    @pl.when(tv_ref[i] != 0)
    def _():
        lo, hi = _unpack_bf16_pairs(x_ref[...])
        half = lo.shape[1]
        a = (jnp.dot(lo, wg_b[:half], preferred_element_type=f32)
             + jnp.dot(hi, wg_b[half:], preferred_element_type=f32))
        b = (jnp.dot(lo, wu_b[:half], preferred_element_type=f32)
             + jnp.dot(hi, wu_b[half:], preferred_element_type=f32))
        hid = (_silu(a) * b).astype(bf16)
        o_ref[...] = _pack_bf16_pairs(jnp.dot(hid, wd_b[...], preferred_element_type=f32))


def grouped_experts(xs, tile_exp, tile_valid, wg, wu, wd, layer):
    p, half = xs.shape
    d = 2 * half
    n_tiles = p // EXPERT_TILE
    first = jnp.concatenate([jnp.ones((1,), bool), tile_exp[1:] != tile_exp[:-1]])
    slot = (jnp.cumsum(first.astype(i32)) - 1) % 2
    nxt_at = jnp.sum((tile_exp[None, :] <= tile_exp[:, None]).astype(i32), axis=1)
    nxt = jnp.where(nxt_at < n_tiles, tile_exp[jnp.minimum(nxt_at, n_tiles - 1)], -1)
    n_used = jnp.sum(tile_valid).reshape(1)
    tile = pl.BlockSpec((EXPERT_TILE, half), lambda i, te, tv, nx, sl, nu: (jnp.minimum(i, nu[0] - 1), 0))
    hbm = pl.BlockSpec(memory_space=pl.ANY)
    grid_spec = pltpu.PrefetchScalarGridSpec(
        num_scalar_prefetch=5,
        grid=(n_tiles,),
        in_specs=[tile, hbm, hbm, hbm],
        out_specs=tile,
        scratch_shapes=[pltpu.VMEM((2, d, D_EXPERT), f32), pltpu.VMEM((2, d, D_EXPERT), f32),
                        pltpu.VMEM((2, D_EXPERT, d), f32),
                        pltpu.VMEM((d, D_EXPERT), bf16), pltpu.VMEM((d, D_EXPERT), bf16),
                        pltpu.VMEM((D_EXPERT, d), bf16), pltpu.SemaphoreType.DMA((2, 3))],
    )
    return pl.pallas_call(
        functools.partial(_expert_kernel, layer=layer),
        out_shape=jax.ShapeDtypeStruct((p, half), i32),
        grid_spec=grid_spec,
        compiler_params=_params(("arbitrary",)),
        name="grouped_experts",
    )(tile_exp, tile_valid, nxt.astype(i32), slot.astype(i32), n_used.astype(i32), xs, wg, wu, wd)


def _combine_kernel(x_ref, yg_ref, wt_ref, g2_ref, sh2_ref, sc2_ref, n2_ref, wsg_ref, wsu_ref, wsd_ref, fg_ref, o_ref,
                    *, final):
    x = x_ref[...]
    hb = _rms_mod(x, n2_ref[...], sc2_ref[0], sh2_ref[0]).astype(bf16)
    a = jnp.dot(hb, wsg_ref[...], preferred_element_type=f32)
    b = jnp.dot(hb, wsu_ref[...], preferred_element_type=f32)
    f = jnp.dot((_silu(a) * b).astype(bf16), wsd_ref[...], preferred_element_type=f32)
    wt = wt_ref[...].T
    half = x.shape[1] // 2
    f_lo, f_hi = f[:, :half], f[:, half:]
    for k in range(TOP_K):
        packed = yg_ref[k]
        w = wt[:, k:k + 1]
        f_lo = f_lo + w * lax.bitcast_convert_type(lax.shift_left(packed, 16), f32)
        f_hi = f_hi + w * lax.bitcast_convert_type(packed & jnp.int32(-65536), f32)
    xo = x + g2_ref[0] * jnp.concatenate([f_lo, f_hi], axis=1)
    if final:
        ms = jnp.mean(xo * xo, axis=-1, keepdims=True)
        xo = xo * lax.rsqrt(ms + RMS_EPS) * fg_ref[...]
    o_ref[...] = xo


def combine(x2, yg, wt_rows, row_offset, mods, rows_per_mod, tile, lw, final_g, final):
    m, d = x2.shape
    off = row_offset // tile
    row = lambda w: pl.BlockSpec((tile, w), lambda i: (i, 0))
    modspec = pl.BlockSpec((1, 1, d), lambda i: ((i * tile) // rows_per_mod, 0, 0))
    return pl.pallas_call(
        functools.partial(_combine_kernel, final=final),
        out_shape=jax.ShapeDtypeStruct((m, d), f32),
        grid=(m // tile,),
        in_specs=[row(d), pl.BlockSpec((TOP_K, tile, d // 2), lambda i: (0, i + off, 0)),
                  pl.BlockSpec((8, tile), lambda i: (0, i + off)), modspec, modspec, modspec,
                  _const_spec((1, d)), _const_spec((d, D_EXPERT)), _const_spec((d, D_EXPERT)),
                  _const_spec((D_EXPERT, d)), _const_spec((1, d))],
        out_specs=row(d),
        compiler_params=_params(("parallel",)),
        name="combine",
    )(x2, yg, wt_rows, mods["g2"], mods["sh2"], mods["sc2"], lw["norm2_g"], lw["w_sh_gate"], lw["w_sh_up"],
      lw["w_sh_down"], final_g)


N_COMBINE_INPUTS = 11
N_INPROJ_INPUTS = 15


def _combine_project_kernel(*refs):
    combine_in = refs[:N_COMBINE_INPUTS]
    inproj_in = refs[N_COMBINE_INPUTS:N_COMBINE_INPUTS + N_INPROJ_INPUTS - 1]
    xo_ref, *inproj_out, prev_ref = refs[N_COMBINE_INPUTS + N_INPROJ_INPUTS - 1:]

    @pl.when(pl.program_id(0) == 0)
    def _():
        prev_ref[...] = jnp.zeros_like(prev_ref)

    raw = _inproj_project(prev_ref, *inproj_in[:4])
    _combine_kernel(*combine_in, xo_ref, final=False)
    _inproj_finish(raw, *inproj_in[4:], *inproj_out)
    prev_ref[...] = xo_ref[...]


def combine_and_project(x2, yg, wt_rows, mods, rows_per_mod, lw, final_g, next_sh, next_sc, tables, rows_per_seq,
                        next_lw, tile):
    m, d = x2.shape
    n_tiles = m // tile
    seq_blocks = rows_per_seq // tile
    cos_t, sa_t, sb_t = tables
    cur = lambda i: jnp.minimum(i, n_tiles - 1)
    prev = lambda i: jnp.maximum(i - 1, 0)
    row_cur = lambda w: pl.BlockSpec((tile, w), lambda i: (cur(i), 0))
    row_prev = lambda w: pl.BlockSpec((tile, w), lambda i: (prev(i), 0))
    mod_cur = pl.BlockSpec((1, 1, d), lambda i: ((cur(i) * tile) // rows_per_mod, 0, 0))
    mod_prev = pl.BlockSpec((1, 1, d), lambda i: ((prev(i) * tile) // rows_per_mod, 0, 0))
    tabspec = pl.BlockSpec((tile, BRANCH_W), lambda i: (prev(i) % seq_blocks, 0))
    out_w = [Q_EXP_W, KV_W, KV_W, Q_EXP_W, KV_W, KV_W, BRANCH_W, BRANCH_W, BRANCH_W]
    combine_specs = [row_cur(d), pl.BlockSpec((TOP_K, tile, d // 2), lambda i: (0, cur(i), 0)),
                     pl.BlockSpec((8, tile), lambda i: (0, cur(i))),
                     mod_cur, mod_cur, mod_cur, _const_spec((1, d)), _const_spec((d, D_EXPERT)),
                     _const_spec((d, D_EXPERT)), _const_spec((D_EXPERT, d)), _const_spec((1, d))]
    inproj_specs = [mod_prev, mod_prev, _const_spec((1, d)), _const_spec((d, OFF_GATE)),
                    _const_spec((1, BRANCH_W)), _const_spec((1, KV_W)), _const_spec((BRANCH_W, BRANCH_W)),
                    _const_spec((1, BRANCH_W)), _const_spec((4, GM_CHUNK, GM_CHUNK)), _const_spec((GM_CHUNK, BRANCH_W)),
                    _const_spec((BRANCH_W, 2 * BRANCH_W)), tabspec, tabspec, tabspec]
    assert len(combine_specs) == N_COMBINE_INPUTS and len(inproj_specs) == N_INPROJ_INPUTS - 1
    return pl.pallas_call(
        _combine_project_kernel,
        out_shape=[jax.ShapeDtypeStruct((m, d), f32)] + [jax.ShapeDtypeStruct((m, w), bf16) for w in out_w],
        grid=(n_tiles + 1,),
        in_specs=combine_specs + inproj_specs,
        out_specs=[row_cur(d)] + [row_prev(w) for w in out_w],
        scratch_shapes=[pltpu.VMEM((tile, d), f32)],
        compiler_params=_params(("arbitrary",)),
        name="combine_and_project",
    )(x2, yg, wt_rows, mods["g2"], mods["sh2"], mods["sc2"], lw["norm2_g"], lw["w_sh_gate"], lw["w_sh_up"],
      lw["w_sh_down"], final_g,
      next_sh, next_sc, next_lw["norm1_g"], next_lw["w_z"], next_lw["qn"], next_lw["kn"], next_lw["bd"],
      next_lw["gm_norm_g"], next_lw["gm_ws"], next_lw["gm_bias"], next_lw["cs64"], cos_t, sa_t, sb_t)


def routed_experts(h2p, eid, rank, counts, lw, layer):
    m = h2p.shape[0]
    p_max = m * TOP_K + N_EXPERTS * EXPERT_TILE
    pos, tile_exp, tile_valid = routing_plan(eid, rank, counts, p_max)
    xs = sc_scatter_rows(h2p, pos, p_max)
    ys = grouped_experts(xs, tile_exp, tile_valid, lw["w_exp_gate"], lw["w_exp_up"], lw["w_exp_down"], layer)
    return sc_gather_rows(ys, pos[:TOP_K].reshape(TOP_K * m)).reshape(TOP_K, m, D_MODEL // 2)


def rope_tables(seq):
    rows = seq // GRID_W
    row = jnp.repeat(jnp.arange(rows), GRID_W).astype(f32)
    col = jnp.tile(jnp.arange(GRID_W), rows).astype(f32)
    axis_dim = HEAD_DIM // 2
    inv_freq = 1.0 / (ROPE_THETA ** (jnp.arange(0, axis_dim, 2, dtype=f32) / axis_dim))
    ang_r = row[:, None] * inv_freq
    ang_c = col[:, None] * inv_freq
    ang = jnp.concatenate([ang_r, ang_r, ang_c, ang_c], axis=-1)
    cos, sin = jnp.cos(ang), jnp.sin(ang)
    seg = (jnp.arange(HEAD_DIM) // 16) % 2
    sa = jnp.where(seg == 0, -sin, 0.0)
    sb = jnp.where(seg == 1, sin, 0.0)
    rep = lambda t: jnp.tile(t, (1, 4))
    return rep(cos), rep(sa), rep(sb)


def _router_weight_pieces(w):
    hi, mid, lo = _split3(w)
    return jnp.concatenate([hi, mid, lo, jnp.zeros_like(hi)], axis=1)


def identity_rope_tables(rows):
    return jnp.ones((rows, BRANCH_W), f32), jnp.zeros((rows, BRANCH_W), f32), jnp.zeros((rows, BRANCH_W), f32)


def kernel(x, c, ctx, c_ctx, w_mod, b_mod, norm1_g, norm2_g, w_in, q_norm_g, k_norm_g, sink, gm_norm_g, gm_ws, gm_b, w_br, w_o, w_router, router_bias, w_exp_gate, w_exp_up, w_exp_down, w_sh_gate, w_sh_up, w_sh_down, final_norm_g):
    bsz_all, seq, d = x.shape
    n_ctx = ctx.shape[1]
    depth = w_mod.shape[0]

    cc = jnp.concatenate([c, c_ctx[None, :], jnp.zeros((MOD_ROWS - bsz_all - 1, d), f32)], axis=0)
    mod_all = compute_mod(cc, w_mod, b_mod)

    lat_tables = rope_tables(seq)
    ctx_tables = identity_rope_tables(ROW_TILE)
    wc_lat, ws_lat = dft_tables(seq)
    wc_ctx, ws_ctx = dft_tables(n_ctx)
    cs64 = channel_dft_table()
    bd = jnp.asarray(np.kron(np.eye(4), np.full((HEAD_DIM, HEAD_DIM), 1.0 / HEAD_DIM)), dtype=bf16)
    final_g = final_norm_g.reshape(1, d)
    tri = jnp.asarray(np.triu(np.ones((MERGE_TILE, MERGE_TILE)), 1), dtype=bf16)

    lws = []
    for l in range(depth):
        lws.append({
            "norm1_g": norm1_g[l].reshape(1, d),
            "norm2_g": norm2_g[l].reshape(1, d),
            "w_z": w_in[l, :, :OFF_GATE].astype(bf16),
            "w_gate": w_in[l, :, OFF_GATE:].astype(bf16),
            "qn": jnp.tile(q_norm_g[l], BRANCH_W // HEAD_DIM).reshape(1, BRANCH_W),
            "kn": jnp.tile(k_norm_g[l], KV_W // HEAD_DIM).reshape(1, KV_W),
            "bd": bd,
            "gm_norm_g": gm_norm_g[l].reshape(1, BRANCH_W),
            "gm_ws": gm_ws[l].astype(bf16),
            "gm_bias": jnp.repeat(gm_b[l].T, GROUP_DIM, axis=1),
            "cs64": cs64,
            "w_br": w_br[l].astype(bf16),
            "w_o": w_o[l].astype(bf16),
            "w_router3": _router_weight_pieces(w_router[l]),
            "router_bias": router_bias[l].reshape(N_EXPERTS, 1),
            "tri": tri,
            "w_exp_gate": w_exp_gate,
            "w_exp_up": w_exp_up,
            "w_exp_down": w_exp_down,
            "w_sh_gate": w_sh_gate[l].astype(bf16),
            "w_sh_up": w_sh_up[l].astype(bf16),
            "w_sh_down": w_sh_down[l].astype(bf16),
        })

    return _layers(x, ctx, mod_all[:, :bsz_all], mod_all[:, bsz_all], lws, sink, lat_tables, ctx_tables,
                   (wc_lat, ws_lat), (wc_ctx, ws_ctx), final_g)


def _layers(x, ctx, mod_lat, mod_ctx, lws, sink, lat_tables, ctx_tables, dft_lat, dft_ctx, final_g):
    bsz, seq, d = x.shape
    n_ctx = ctx.shape[1]
    depth = len(lws)
    n_lat = bsz * seq
    n_cx = bsz * n_ctx
    wc_lat, ws_lat = dft_lat
    wc_ctx, ws_ctx = dft_ctx
    xl = x.reshape(n_lat, d)
    xc = ctx.reshape(n_cx, d)
    lat_proj = None
    for l in range(depth):
        ctx_out = l < depth - 1
        lw = lws[l]
        names = ("sh1", "sc1", "g1", "sh2", "sc2", "g2")
        mods_lat = {n: mod_lat[l, :, i * d:(i + 1) * d].reshape(bsz, 1, d) for i, n in enumerate(names)}
        mods_ctx = {n: mod_ctx[l, i * d:(i + 1) * d].reshape(1, 1, d) for i, n in enumerate(names)}
        sink_l = sink[l]

        if lat_proj is None:
            lat_proj = in_projection(xl, mods_lat["sh1"], mods_lat["sc1"], seq, lat_tables, seq, ROW_TILE, lw)
        qg, kg, vg, qw, kw, vw, fxc, fxs, o_d = lat_proj
        cqg, ckg, cvg, cqw, ckw, cvw, cfxc, cfxs, co_d = in_projection(
            xc, mods_ctx["sh1"], mods_ctx["sc1"], n_cx, ctx_tables, ROW_TILE, ROW_TILE, lw)
        b3 = lambda t, rows: t.reshape(bsz, rows, t.shape[-1])
        ckg3, cvg3, ckw3, cvw3 = b3(ckg, n_ctx), b3(cvg, n_ctx), b3(ckw, n_ctx), b3(cvw, n_ctx)

        o_a = dft_mix(wc_lat, ws_lat, b3(fxc, seq), b3(fxs, seq), ROW_TILE).reshape(n_lat, BRANCH_W)
        o_b, o_c = latent_attention(b3(qg, seq), b3(qw, seq), b3(kg, seq), b3(vg, seq), b3(kw, seq), b3(vw, seq),
                                    ckg3, cvg3, ckw3, cvw3, sink_l)
        o_b, o_c = o_b.reshape(n_lat, BRANCH_W), o_c.reshape(n_lat, BRANCH_W)
        m_total = n_lat + (n_cx if ctx_out else 0)
        route = None
        if ctx_out:
            route = [jnp.zeros((m_total, d // 2), i32), jnp.zeros((8, m_total), i32), jnp.zeros((8, m_total), f32),
                     jnp.zeros((8, m_total), i32), jnp.zeros((N_EXPERTS, 1), i32)]
        xl, *route = merge_and_route(xl, mods_lat, seq, (o_a, o_b, o_c, o_d), MERGE_TILE, lw, m_total, 0, route)
        if ctx_out:
            co_a = dft_mix(wc_ctx, ws_ctx, b3(cfxc, n_ctx), b3(cfxs, n_ctx), n_ctx).reshape(n_cx, BRANCH_W)
            co_b = full_attention(b3(cqg, n_ctx), [(ckg3, cvg3)], None, n_ctx).reshape(n_cx, BRANCH_W)
            co_c = full_attention(b3(cqw, n_ctx), [(ckw3, cvw3)], sink_l, n_ctx).reshape(n_cx, BRANCH_W)
            xc, *route = merge_and_route(xc, mods_ctx, n_cx, (co_a, co_b, co_c, co_d), MERGE_TILE, lw, m_total, n_lat,
                                         route)
        h2p, eid, wt, rank, counts = route
        yg = routed_experts(h2p, eid, rank, counts, lw, l)
        wt_rows = wt
        if ctx_out:
            nxt = {n: mod_lat[l + 1, :, i * d:(i + 1) * d].reshape(bsz, 1, d) for i, n in enumerate(names[:2])}
            xl, *lat_proj = combine_and_project(xl, yg, wt_rows, mods_lat, seq, lw, final_g, nxt["sh1"], nxt["sc1"],
                                                lat_tables, seq, lws[l + 1], COMBINE_PROJECT_TILE)
            xc = combine(xc, yg, wt_rows, n_lat, mods_ctx, n_cx, ROW_TILE, lw, final_g, False)
        else:
            xl = combine(xl, yg, wt_rows, 0, mods_lat, seq, ROW_TILE, lw, final_g, True)
    return xl.reshape(bsz, seq, d)
```

```python
import functools
import math

import jax
import jax.numpy as jnp
import numpy as np
from jax import lax
from jax.experimental import pallas as pl
from jax.experimental.pallas import tpu as pltpu
from jax.experimental.pallas import tpu_sc as plsc

f32 = jnp.float32
bf16 = jnp.bfloat16
i32 = jnp.int32

D_MODEL = 1024
HEAD_DIM = 64
GRID_W = 64
ROPE_THETA = 10000.0
ATTN_SCALE = HEAD_DIM ** -0.5
RMS_EPS = 1e-6
NEG_INF = -1e30
Q_BLOCK = 128
WINDOW = 128
GM_CHUNK = 128
N_BRANCHES = 4
BRANCH_W = 256
KV_W = 128
Q_EXP_W = 4 * KV_W
ROPE_SHIFT = HEAD_DIM // 4
GROUP_DIM = 64
OFF_KV = 0
OFF_Q = 512
OFF_FN = 1024
OFF_GM = 1280
OFF_GATE = 1792
N_EXPERTS = 64
TOP_K = 6
N_GROUPS = 8
GROUP_SIZE = N_EXPERTS // N_GROUPS
TOPK_GROUPS = 4
D_EXPERT = 256
ROUTED_SCALE = 2.5

SC_CORES = 2
SC_SUBCORES = 16
SC_WORKERS = SC_CORES * SC_SUBCORES
SC_IDX_CHUNK = 128

ROW_TILE = 1024
COMBINE_PROJECT_TILE = 512
MERGE_TILE = 512
ROUTER_AFTER_BRANCHES = 1
RANK_AFTER_BRANCHES = 3
EXPERT_TILE = 1024
LATENT_ATTN_Q_ROWS = 512
LATENT_ATTN_GLOBAL_BLOCKS = 2
FULL_ATTN_ROW_BLOCKS = 2
MOD_ROWS = 24
VMEM_LIMIT = 56 * 1024 * 1024


def _params(sem, vmem=VMEM_LIMIT):
    return pltpu.CompilerParams(dimension_semantics=sem, vmem_limit_bytes=vmem)


def _const_spec(shape):
    nd = len(shape)
    return pl.BlockSpec(shape, lambda *_: (0,) * nd, pipeline_mode=pl.Buffered(1))


def _rms_mod(x, g, sc, sh):
    ms = jnp.mean(x * x, axis=-1, keepdims=True)
    return (x * lax.rsqrt(ms + RMS_EPS) * g) * (1.0 + sc) + sh


def _gelu(x):
    return 0.5 * x * (1.0 + jnp.tanh(math.sqrt(2.0 / math.pi) * (x + 0.044715 * (x * x * x))))


def _silu(x):
    return x * jax.nn.sigmoid(x)


def _mod_kernel(a_ref, w_ref, b_ref, o_ref):
    a = _silu(a_ref[...]).astype(bf16)
    o_ref[0] = jnp.dot(a, w_ref[0].astype(bf16), preferred_element_type=f32) + b_ref[0]


def compute_mod(cc, w_mod, b_mod):
    depth, d, n = w_mod.shape
    tn = 1536
    return pl.pallas_call(
        _mod_kernel,
        out_shape=jax.ShapeDtypeStruct((depth, MOD_ROWS, n), f32),
        grid=(depth, n // tn),
        in_specs=[
            pl.BlockSpec((MOD_ROWS, d), lambda l, j: (0, 0)),
            pl.BlockSpec((1, d, tn), lambda l, j: (l, 0, j)),
            pl.BlockSpec((1, 1, tn), lambda l, j: (l, 0, j)),
        ],
        out_specs=pl.BlockSpec((1, MOD_ROWS, tn), lambda l, j: (l, 0, j)),
        compiler_params=_params(("parallel", "parallel")),
        name="mod_proj",
    )(cc, w_mod, b_mod.reshape(depth, 1, n))


def _inproj_kernel(x_ref, sh_ref, sc_ref, g_ref, w_ref, *rest):
    _inproj_finish(_inproj_project(x_ref, sh_ref, sc_ref, g_ref, w_ref), *rest)


def _inproj_project(x_ref, sh_ref, sc_ref, g_ref, w_ref):
    hb = _rms_mod(x_ref[...], g_ref[...], sc_ref[0], sh_ref[0]).astype(bf16)
    return [jnp.dot(hb, w_ref[:, a:b], preferred_element_type=f32)
            for a, b in ((OFF_KV, OFF_Q), (OFF_Q, OFF_FN), (OFF_FN, OFF_GM), (OFF_GM, OFF_GATE))]


def _inproj_finish(raw, qn_ref, kn_ref, bd_ref, gmg_ref, ws_ref, gb_ref, cs_ref, cos_ref, sa_ref, sb_ref,
                   qg_ref, kg_ref, vg_ref, qw_ref, kw_ref, vw_ref, xc_ref, xs_ref, od_ref):
    kv, qq, fn, uv = raw
    fn = fn.astype(bf16)
    tile = kv.shape[0]

    def square_pieces(t):
        sq = t * t
        hi = sq.astype(bf16)
        return hi, (sq - hi.astype(f32)).astype(bf16)

    def headnorm(t, pieces, gain):
        w = t.shape[1]
        b = bd_ref[:w, :w]
        ms = (jnp.dot(pieces[0], b, preferred_element_type=f32) + jnp.dot(pieces[1], b, preferred_element_type=f32))
        return t * lax.rsqrt(ms + RMS_EPS) * gain

    def rope(t):
        w = t.shape[1]
        return (t * cos_ref[:, :w] + pltpu.roll(t, w - ROPE_SHIFT, 1) * sa_ref[:, :w]
                + pltpu.roll(t, ROPE_SHIFT, 1) * sb_ref[:, :w])

    def expand_heads(q):
        lane = lax.broadcasted_iota(i32, (1, KV_W), 1)
        low = lane < HEAD_DIM
        blocks = []
        for kv in range(2):
            pair = q[:, KV_W * kv:KV_W * (kv + 1)]
            swapped = pltpu.roll(pair, HEAD_DIM, 1)
            keep = low if kv == 0 else jnp.logical_not(low)
            g0, g1 = (pair, swapped) if kv == 0 else (swapped, pair)
            blocks.append(jnp.where(keep, g0, 0.0))
            blocks.append(jnp.where(keep, g1, 0.0))
        return jnp.concatenate(blocks, axis=1)

    k_sq = square_pieces(kv[:, :KV_W])
    q_sq = square_pieces(qq[:, :BRANCH_W])
    u = _gelu(uv[:, :BRANCH_W])
    v = _gelu(uv[:, BRANCH_W:])
    vms = jnp.mean(v * v, axis=-1, keepdims=True)
    vn = (v * lax.rsqrt(vms + RMS_EPS) * gmg_ref[...]).astype(bf16)

    kg_ref[...] = rope(headnorm(kv[:, :KV_W], k_sq, kn_ref[...])).astype(bf16)
    vg_ref[...] = kv[:, KV_W:2 * KV_W].astype(bf16)
    kw_ref[...] = rope(kv[:, 2 * KV_W:3 * KV_W]).astype(bf16)
    vw_ref[...] = kv[:, 3 * KV_W:].astype(bf16)

    qg = rope(headnorm(qq[:, :BRANCH_W], q_sq, qn_ref[...])) * ATTN_SCALE
    qg_ref[...] = expand_heads(qg).astype(bf16)
    qw = rope(qq[:, BRANCH_W:]) * ATTN_SCALE
    qw_ref[...] = expand_heads(qw).astype(bf16)

    xcs = jnp.dot(fn, cs_ref[...], preferred_element_type=f32)
    xc_ref[...] = xcs[:, :BRANCH_W].astype(bf16)
    xs_ref[...] = xcs[:, BRANCH_W:].astype(bf16)

    lane_grp = lax.broadcasted_iota(i32, (1, BRANCH_W), 1) // GROUP_DIM
    for c in range(tile // GM_CHUNK):
        rows = slice(c * GM_CHUNK, (c + 1) * GM_CHUNK)
        vch = vn[rows]
        sv = gb_ref[...]
        for g in range(4):
            r = jnp.dot(ws_ref[g], vch, preferred_element_type=f32)
            sv = sv + jnp.where(lane_grp == g, r, 0.0)
        od_ref[rows, :] = (u[rows] * sv).astype(bf16)


def in_projection(x2, sh, sc, rows_per_mod, tables, rows_per_seq, tile, lw):
    m, d = x2.shape
    cos_t, sa_t, sb_t = tables
    seq_blocks = rows_per_seq // tile
    row = lambda w: pl.BlockSpec((tile, w), lambda i: (i, 0))
    modspec = pl.BlockSpec((1, 1, d), lambda i: ((i * tile) // rows_per_mod, 0, 0))
    tabspec = pl.BlockSpec((tile, BRANCH_W), lambda i: (i % seq_blocks, 0))
    out_w = [Q_EXP_W, KV_W, KV_W, Q_EXP_W, KV_W, KV_W, BRANCH_W, BRANCH_W, BRANCH_W]
    return pl.pallas_call(
        _inproj_kernel,
        out_shape=[jax.ShapeDtypeStruct((m, w), bf16) for w in out_w],
        grid=(m // tile,),
        in_specs=[
            row(d), modspec, modspec, _const_spec((1, d)), _const_spec((d, OFF_GATE)),
            _const_spec((1, BRANCH_W)), _const_spec((1, KV_W)), _const_spec((BRANCH_W, BRANCH_W)),
            _const_spec((1, BRANCH_W)), _const_spec((4, GM_CHUNK, GM_CHUNK)), _const_spec((GM_CHUNK, BRANCH_W)),
            _const_spec((BRANCH_W, 2 * BRANCH_W)),
            tabspec, tabspec, tabspec,
        ],
        out_specs=[row(w) for w in out_w],
        compiler_params=_params(("parallel",)),
        name="in_projection",
    )(x2, sh, sc, lw["norm1_g"], lw["w_z"], lw["qn"], lw["kn"], lw["bd"], lw["gm_norm_g"], lw["gm_ws"], lw["gm_bias"],
      lw["cs64"], cos_t, sa_t, sb_t)


def _dft_kernel(wc_ref, ws_ref, xc_ref, xs_ref, o_ref):
    acc = jnp.dot(wc_ref[...], xc_ref[0], preferred_element_type=f32)
    acc = acc + jnp.dot(ws_ref[...], xs_ref[0], preferred_element_type=f32)
    o_ref[0] = acc.astype(bf16)


def dft_mix(wc, ws, xc, xs, tile):
    nb, length, w = xc.shape
    return pl.pallas_call(
        _dft_kernel,
        out_shape=jax.ShapeDtypeStruct((nb, length, w), bf16),
        grid=(length // tile, nb),
        in_specs=[
            pl.BlockSpec((tile, length), lambda i, b: (i, 0)),
            pl.BlockSpec((tile, length), lambda i, b: (i, 0)),
            pl.BlockSpec((1, length, w), lambda i, b: (b, 0, 0)),
            pl.BlockSpec((1, length, w), lambda i, b: (b, 0, 0)),
        ],
        out_specs=pl.BlockSpec((1, tile, w), lambda i, b: (b, i, 0)),
        compiler_params=_params(("parallel", "parallel")),
        name="dft_mix",
    )(wc, ws, xc, xs)


def dft_tables(length):
    jk = (np.arange(length)[:, None] * np.arange(length)[None, :]) % length
    ang = 2.0 * np.pi * jk / length
    s = 1.0 / math.sqrt(length)
    return jnp.asarray(np.cos(ang) * s, dtype=bf16), jnp.asarray(-np.sin(ang) * s, dtype=bf16)


def channel_dft_table():
    jk = (np.arange(64)[:, None] * np.arange(64)[None, :]) % 64
    ang = 2.0 * np.pi * jk / 64
    eye = np.eye(4)
    c = np.kron(eye, np.cos(ang) / 8.0)
    s = np.kron(eye, np.sin(ang) / 8.0)
    return jnp.asarray(np.concatenate([c, s], axis=1), dtype=bf16)


def _attend_blocks(blocks, sink_ref):
    lane = lax.broadcasted_iota(i32, (1, KV_W), 1)
    low = lane < HEAD_DIM
    units = []
    for q_all, pieces, with_sink in blocks:
        qb = q_all.shape[0]
        for kv in range(2):
            q = jnp.concatenate([q_all[:, KV_W * (2 * kv):KV_W * (2 * kv + 1)],
                                 q_all[:, KV_W * (2 * kv + 1):KV_W * (2 * kv + 2)]], axis=0)
            sink_col = None
            if with_sink:
                sink_col = jnp.concatenate([jnp.full((qb, 1), sink_ref[2 * kv], f32),
                                            jnp.full((qb, 1), sink_ref[2 * kv + 1], f32)], axis=0)
            units.append((q, pieces, sink_col, low if kv == 0 else jnp.logical_not(low)))

    scores = []
    for q, pieces, _, _ in units:
        unit_scores = []
        for k, _, mask in pieces:
            s = lax.dot_general(q, k, (((1,), (1,)), ((), ())), preferred_element_type=f32)
            unit_scores.append(s if mask is None else jnp.where(mask, s, NEG_INF))
        scores.append(unit_scores)

    maxes = []
    for (_, _, sink_col, _), unit_scores in zip(units, scores):
        m = unit_scores[0].max(axis=-1, keepdims=True)
        for s in unit_scores[1:]:
            m = jnp.maximum(m, s.max(axis=-1, keepdims=True))
        maxes.append(m if sink_col is None else jnp.maximum(m, sink_col))

    probs = [[jnp.exp((s - m).astype(bf16)) for s in unit_scores] for unit_scores, m in zip(scores, maxes)]

    results = []
    for (_, pieces, sink_col, own), unit_probs, m in zip(units, probs, maxes):
        acc = None
        for p, (_, v, _) in zip(unit_probs, pieces):
            pv = jnp.dot(p, jnp.where(own, v, jnp.ones_like(v)), preferred_element_type=f32)
            acc = pv if acc is None else acc + pv
        denom = pltpu.roll(acc, HEAD_DIM, 1)
        if sink_col is not None:
            denom = denom + jnp.exp(sink_col - m)
        results.append(acc * (1.0 / denom))

    outs = []
    for i, (q_all, _, _) in enumerate(blocks):
        qb = q_all.shape[0]
        r_kv0, r_kv1 = results[2 * i], results[2 * i + 1]
        lo = jnp.where(low, r_kv0[:qb], pltpu.roll(r_kv0[qb:], HEAD_DIM, 1))
        hi = jnp.where(low, pltpu.roll(r_kv1[:qb], HEAD_DIM, 1), r_kv1[qb:])
        outs.append(jnp.concatenate([lo, hi], axis=1))
    return outs


def _full_attn_kernel(*refs, n_pieces, has_sink, row_blocks):
    pos = 0
    sink_ref = None
    if has_sink:
        sink_ref = refs[0]
        pos = 1
    q_ref = refs[pos]
    kv_refs = refs[pos + 1:pos + 1 + 2 * n_pieces]
    o_ref = refs[pos + 1 + 2 * n_pieces]
    pieces = [(kv_refs[2 * i][0], kv_refs[2 * i + 1][0], None) for i in range(n_pieces)]
    rows = q_ref.shape[1] // row_blocks
    blocks = [(q_ref[0, j * rows:(j + 1) * rows, :], pieces, has_sink) for j in range(row_blocks)]
    for j, out in enumerate(_attend_blocks(blocks, sink_ref)):
        o_ref[0, j * rows:(j + 1) * rows, :] = out.astype(bf16)


def full_attention(q, pieces, sink, qb):
    nb, lq, _ = q.shape
    in_specs = []
    args = []
    if sink is not None:
        in_specs.append(pl.BlockSpec(memory_space=pltpu.SMEM))
        args.append(sink)
    in_specs.append(pl.BlockSpec((1, qb, Q_EXP_W), lambda b, i: (b, i, 0)))
    args.append(q)
    for k, v in pieces:
        spec = pl.BlockSpec((1, k.shape[1], KV_W), lambda b, i: (b, 0, 0))
        in_specs += [spec, spec]
        args += [k, v]
    return pl.pallas_call(
        functools.partial(_full_attn_kernel, n_pieces=len(pieces), has_sink=sink is not None,
                          row_blocks=FULL_ATTN_ROW_BLOCKS),
        out_shape=jax.ShapeDtypeStruct((nb, lq, BRANCH_W), bf16),
        grid=(nb, lq // qb),
        in_specs=in_specs,
        out_specs=pl.BlockSpec((1, qb, BRANCH_W), lambda b, i: (b, i, 0)),
        compiler_params=_params(("parallel", "parallel")),
        name="full_attention",
    )(*args)


def _latent_attn_kernel(sink_ref, qg_ref, qw_ref, kg_ref, vg_ref, kw_ref, vw_ref, kgc_ref, vgc_ref, kwc_ref, vwc_ref,
                        ob_ref, oc_ref, *, seq):
    rows = qg_ref.shape[1]
    global_rows = rows // LATENT_ATTN_GLOBAL_BLOCKS
    window_blocks = rows // Q_BLOCK
    global_pieces = [(kgc_ref[0], vgc_ref[0], None), (kg_ref[0], vg_ref[0], None)]
    global_work = [(qg_ref[0, j * global_rows:(j + 1) * global_rows, :], global_pieces, False)
                   for j in range(LATENT_ATTN_GLOBAL_BLOCKS)]
    span = 3 * Q_BLOCK
    ctx_piece = (kwc_ref[0], vwc_ref[0], None)
    work = []
    for j in range(window_blocks):
        n = pl.program_id(1) * window_blocks + j
        start = pl.multiple_of(jnp.clip((n - 1) * Q_BLOCK, 0, seq - span), Q_BLOCK)
        kwin = kw_ref[0, pl.ds(start, span), :]
        vwin = vw_ref[0, pl.ds(start, span), :]
        row = lax.broadcasted_iota(i32, (2 * Q_BLOCK, span), 0) % Q_BLOCK + n * Q_BLOCK
        col = lax.broadcasted_iota(i32, (2 * Q_BLOCK, span), 1) + start
        mask = jnp.abs(row - col) <= WINDOW
        work.append((qw_ref[0, j * Q_BLOCK:(j + 1) * Q_BLOCK, :], [ctx_piece, (kwin, vwin, mask)], True))
    outs = _attend_blocks(work + global_work, sink_ref)
    for j in range(window_blocks):
        oc_ref[0, j * Q_BLOCK:(j + 1) * Q_BLOCK, :] = outs[j].astype(bf16)
    for j in range(LATENT_ATTN_GLOBAL_BLOCKS):
        ob_ref[0, j * global_rows:(j + 1) * global_rows, :] = outs[window_blocks + j].astype(bf16)


def latent_attention(qg, qw, kg, vg, kw, vw, kgc, vgc, kwc, vwc, sink):
    nb, seq, _ = qg.shape
    n_ctx = kgc.shape[1]
    rows = LATENT_ATTN_Q_ROWS
    assert seq % rows == 0 and rows % Q_BLOCK == 0 and seq >= 3 * Q_BLOCK, seq
    qspec = pl.BlockSpec((1, rows, Q_EXP_W), lambda b, i: (b, i, 0))
    full = lambda l: pl.BlockSpec((1, l, KV_W), lambda b, i: (b, 0, 0))
    ospec = pl.BlockSpec((1, rows, BRANCH_W), lambda b, i: (b, i, 0))
    return pl.pallas_call(
        functools.partial(_latent_attn_kernel, seq=seq),
        out_shape=[jax.ShapeDtypeStruct((nb, seq, BRANCH_W), bf16)] * 2,
        grid=(nb, seq // rows),
        in_specs=[pl.BlockSpec(memory_space=pltpu.SMEM), qspec, qspec, full(seq), full(seq), full(seq), full(seq),
                  full(n_ctx), full(n_ctx), full(n_ctx), full(n_ctx)],
        out_specs=[ospec, ospec],
        compiler_params=_params(("parallel", "parallel")),
        name="latent_attention",
    )(sink, qg, qw, kg, vg, kw, vw, kgc, vgc, kwc, vwc)


def _route(logits_t, bias_col):
    t = logits_t.shape[1]
    scores = jax.nn.sigmoid(logits_t)
    choice = scores + bias_col
    sub = lax.broadcasted_iota(i32, (GROUP_SIZE, t), 0)
    grp_score = []
    for g in range(N_GROUPS):
        cg = choice[g * GROUP_SIZE:(g + 1) * GROUP_SIZE]
        m1 = cg.max(axis=0, keepdims=True)
        first = jnp.min(jnp.where(cg == m1, sub, GROUP_SIZE), axis=0, keepdims=True)
        m2 = jnp.where(sub == first, -jnp.inf, cg).max(axis=0, keepdims=True)
        grp_score.append(m1 + m2)
    keep = []
    for g in range(N_GROUPS):
        beaten = jnp.zeros((1, t), i32)
        for o in range(N_GROUPS):
            if o == g:
                continue
            wins = (grp_score[o] > grp_score[g]) | ((grp_score[o] == grp_score[g]) & (o < g))
            beaten = beaten + wins.astype(i32)
        keep.append(jnp.broadcast_to(beaten < TOPK_GROUPS, (GROUP_SIZE, t)))
    masked = jnp.where(jnp.concatenate(keep, axis=0), choice, NEG_INF)
    eid = lax.broadcasted_iota(i32, (N_EXPERTS, t), 0)
    ids, wts = [], []
    for _ in range(TOP_K):
        m = masked.max(axis=0, keepdims=True)
        pick = jnp.min(jnp.where(masked == m, eid, N_EXPERTS), axis=0, keepdims=True)
        sel = eid == pick
        ids.append(pick)
        wts.append(jnp.sum(jnp.where(sel, scores, 0.0), axis=0, keepdims=True))
        masked = jnp.where(sel, -jnp.inf, masked)
    total = wts[0]
    for w in wts[1:]:
        total = total + w
    norm = ROUTED_SCALE / total
    return ids, [w * norm for w in wts]


def _pack_bf16_pairs(x):
    w = x.shape[1] // 2
    lo = lax.bitcast_convert_type(x[:, :w].astype(bf16).astype(f32), i32)
    hi = lax.bitcast_convert_type(x[:, w:].astype(bf16).astype(f32), i32)
    return lax.shift_right_logical(lo, 16) | (hi & jnp.int32(-65536))


def _unpack_bf16_pairs(p):
    lo = lax.bitcast_convert_type(lax.shift_left(p, 16), f32)
    hi = lax.bitcast_convert_type(p & jnp.int32(-65536), f32)
    return lo.astype(bf16), hi.astype(bf16)


def _merge_kernel(x_ref, sh_ref, sc_ref, g1_ref, sh2_ref, sc2_ref, n1_ref, n2_ref, oa_ref, ob_ref, oc_ref, od_ref,
                  wg_ref, wbr_ref, wo_ref, wr_ref, rb_ref, tri_ref, *rest, extends):
    if extends:
        cnt_in_ref = rest[0]
        rest = rest[5:]
    xo_ref, h2_ref, eid_ref, wt_ref, rank_ref, cnt_ref, run_ref, xn_prev_ref = rest
    step = pl.program_id(0)

    @pl.when(step == 0)
    def _():
        run_ref[...] = cnt_in_ref[...].astype(f32) if extends else jnp.zeros_like(run_ref)
        xn_prev_ref[...] = jnp.zeros_like(xn_prev_ref)

    x = x_ref[...]
    hb = _rms_mod(x, n1_ref[...], sc_ref[0], sh_ref[0]).astype(bf16)
    h2 = _rms_mod(xn_prev_ref[...], n2_ref[...], sc2_ref[0], sh2_ref[0])
    h2_ref[...] = _pack_bf16_pairs(h2)
    y = None
    for i, o_ref in enumerate((oa_ref, ob_ref, oc_ref, od_ref)):
        if i == ROUTER_AFTER_BRANCHES:
            ids, wts = _route(_router_logits(wr_ref[...], h2), rb_ref[...])
        if i == RANK_AFTER_BRANCHES:
            _rank_entries(ids, wts, tri_ref, run_ref, (step > 0).astype(f32), eid_ref, wt_ref, rank_ref, cnt_ref)
        logit = jnp.dot(hb, wg_ref[:, i * D_MODEL:(i + 1) * D_MODEL], preferred_element_type=f32)
        proj = jnp.dot(o_ref[...], wbr_ref[i], preferred_element_type=f32)
        term = jax.nn.sigmoid(logit.astype(bf16)) * proj.astype(bf16)
        y = term if y is None else y + term
    xn = x + g1_ref[0] * jnp.dot(y, wo_ref[...], preferred_element_type=f32)
    xo_ref[...] = xn
    xn_prev_ref[...] = xn


def _split3(x):
    def head(v):
        return lax.bitcast_convert_type(lax.bitcast_convert_type(v, i32) & jnp.int32(-65536), f32)

    hi = head(x)
    r1 = x - hi
    mid = head(r1)
    lo = r1 - mid
    return hi.astype(bf16), mid.astype(bf16), lo.astype(bf16)


def _router_logits(w3, h2):
    prod = None
    for piece in _split3(h2):
        p = jnp.dot(piece, w3, preferred_element_type=f32)
        prod = p if prod is None else prod + p
    lane = lax.broadcasted_iota(i32, (1, 2 * N_EXPERTS), 1)
    low = prod[:, :2 * N_EXPERTS]
    logits = low + pltpu.roll(low, N_EXPERTS, 1) + prod[:, 2 * N_EXPERTS:]
    return jnp.where(lane < N_EXPERTS, logits, 0.0).T[:N_EXPERTS]


def _rank_entries(ids, wts, tri_ref, run_ref, live, eid_ref, wt_ref, rank_ref, cnt_ref):
    t = ids[0].shape[1]
    eid = lax.broadcasted_iota(i32, (N_EXPERTS, t), 0)
    hits = [eid == pick for pick in ids]
    chosen = hits[0]
    for h in hits[1:]:
        chosen = chosen | h
    chosen = jnp.where(chosen, 1.0, 0.0)
    prefix = jnp.dot(chosen.astype(bf16), tri_ref[...], preferred_element_type=f32)
    offset = run_ref[...] + prefix
    ranks = [jnp.sum(jnp.where(h, offset, 0.0), axis=0, keepdims=True).astype(i32) for h in hits]
    run_ref[...] += live * jnp.sum(chosen, axis=1, keepdims=True)
    cnt_ref[...] = run_ref[...].astype(i32)

    pad_i = [jnp.zeros((1, t), i32)] * (8 - TOP_K)
    eid_ref[...] = jnp.concatenate(ids + pad_i, axis=0)
    rank_ref[...] = jnp.concatenate(ranks + pad_i, axis=0)
    wt_ref[...] = jnp.concatenate(wts + [jnp.zeros((1, t), f32)] * (8 - TOP_K), axis=0)


def merge_and_route(x2, mods, rows_per_mod, branches, tile, lw, m_total, row_offset, prior):
    m, d = x2.shape
    off = row_offset // tile
    n_tiles = m // tile
    cur = lambda i: jnp.minimum(i, n_tiles - 1)
    prev = lambda i: jnp.maximum(i - 1, 0)
    row = lambda w: pl.BlockSpec((tile, w), lambda i: (cur(i), 0))
    row_prev = lambda w: pl.BlockSpec((tile, w), lambda i: (prev(i) + off, 0))
    modspec = pl.BlockSpec((1, 1, d), lambda i: ((cur(i) * tile) // rows_per_mod, 0, 0))
    modspec_prev = pl.BlockSpec((1, 1, d), lambda i: ((prev(i) * tile) // rows_per_mod, 0, 0))
    col = pl.BlockSpec((8, tile), lambda i: (0, prev(i) + off))
    in_specs = [row(d)] + [modspec] * 3 + [modspec_prev] * 2 + [_const_spec((1, d)), _const_spec((1, d))] + [
        row(BRANCH_W)] * 4 + [
        _const_spec((d, N_BRANCHES * d)), _const_spec((N_BRANCHES, BRANCH_W, d)), _const_spec((d, d)),
        _const_spec((d, 4 * N_EXPERTS)), _const_spec((N_EXPERTS, 1)), _const_spec((tile, tile))]
    args = [x2, mods["sh1"], mods["sc1"], mods["g1"], mods["sh2"], mods["sc2"], lw["norm1_g"], lw["norm2_g"],
            *branches, lw["w_gate"], lw["w_br"], lw["w_o"], lw["w_router3"], lw["router_bias"], lw["tri"]]
    aliases = {}
    if prior is not None:
        h2p, eid, wt, rank, counts = prior
        n_in = len(args)
        in_specs += [_const_spec((N_EXPERTS, 1))] + [pl.BlockSpec(memory_space=pl.ANY)] * 4
        args += [counts, h2p, eid, wt, rank]
        aliases = {n_in + 1 + j: 1 + j for j in range(4)}
    return pl.pallas_call(
        functools.partial(_merge_kernel, extends=prior is not None),
        out_shape=[jax.ShapeDtypeStruct((m, d), f32), jax.ShapeDtypeStruct((m_total, d // 2), i32),
                   jax.ShapeDtypeStruct((8, m_total), i32), jax.ShapeDtypeStruct((8, m_total), f32),
                   jax.ShapeDtypeStruct((8, m_total), i32), jax.ShapeDtypeStruct((N_EXPERTS, 1), i32)],
        grid=(n_tiles + 1,),
        in_specs=in_specs,
        out_specs=[row(d), row_prev(d // 2), col, col, col, pl.BlockSpec((N_EXPERTS, 1), lambda i: (0, 0))],
        scratch_shapes=[pltpu.VMEM((N_EXPERTS, 1), f32), pltpu.VMEM((tile, d), f32)],
        input_output_aliases=aliases,
        compiler_params=_params(("arbitrary",)),
        name="merge_and_route",
    )(*args)


def routing_plan(eid, rank, counts, p_max):
    counts = counts.reshape(N_EXPERTS)
    padded = ((counts + EXPERT_TILE - 1) // EXPERT_TILE) * EXPERT_TILE
    ends = jnp.cumsum(padded)
    starts = ends - padded
    onehot = eid[:, :, None] == jnp.arange(N_EXPERTS, dtype=i32)[None, None, :]
    pos = rank + jnp.sum(jnp.where(onehot, starts[None, None, :], 0), axis=-1)
    n_tiles = p_max // EXPERT_TILE
    tile_start = jnp.arange(n_tiles, dtype=i32) * EXPERT_TILE
    tile_valid = tile_start < ends[-1]
    tile_exp = jnp.sum((ends[None, :] <= tile_start[:, None]).astype(i32), axis=1)
    return pos.astype(i32), jnp.minimum(tile_exp, N_EXPERTS - 1), tile_valid.astype(i32)


def _sc_worker_id():
    return lax.axis_index("subcore") * SC_CORES + lax.axis_index("core")


def sc_scatter_rows(table, pos, p_rows):
    m, w = table.shape
    n_chunks = m // SC_IDX_CHUNK
    steps = -(-n_chunks // SC_WORKERS)
    pos3 = pos.reshape(8, n_chunks, SC_IDX_CHUNK).transpose(1, 0, 2)
    mesh = plsc.VectorSubcoreMesh(core_axis_name="core", subcore_axis_name="subcore")

    @functools.partial(
        pl.kernel,
        out_type=jax.ShapeDtypeStruct((p_rows, w), table.dtype),
        mesh=mesh,
        scratch_types=[
            pltpu.VMEM((8, SC_IDX_CHUNK), i32),
            pltpu.VMEM((SC_IDX_CHUNK, w), table.dtype),
            pltpu.SemaphoreType.DMA,
        ],
    )
    def scatter(x_hbm, p_hbm, o_hbm, idx_v, rows_v, sem):
        wid = _sc_worker_id()

        @pl.loop(0, steps)
        def _(si):
            chunk = si * SC_WORKERS + wid

            @pl.when(chunk < n_chunks)
            def _():
                pltpu.sync_copy(p_hbm.at[chunk], idx_v)
                pltpu.sync_copy(x_hbm.at[pl.ds(chunk * SC_IDX_CHUNK, SC_IDX_CHUNK)], rows_v)
                copies = [pltpu.async_copy(rows_v, o_hbm.at[idx_v.at[k]], sem) for k in range(TOP_K)]
                for cp in copies:
                    cp.wait()

    return scatter(table, pos3)


def sc_gather_rows(table, idx):
    n_idx = idx.shape[0]
    w = table.shape[1]
    n_chunks = n_idx // SC_IDX_CHUNK
    steps = -(-n_chunks // SC_WORKERS)
    half = SC_IDX_CHUNK // 2
    mesh = plsc.VectorSubcoreMesh(core_axis_name="core", subcore_axis_name="subcore")

    @functools.partial(
        pl.kernel,
        out_type=jax.ShapeDtypeStruct((n_idx, w), table.dtype),
        mesh=mesh,
        scratch_types=[
            pltpu.VMEM((SC_IDX_CHUNK,), i32),
            pltpu.VMEM((half, w), table.dtype),
            pltpu.VMEM((half, w), table.dtype),
            pltpu.SemaphoreType.DMA,
            pltpu.SemaphoreType.DMA,
            pltpu.SemaphoreType.DMA,
            pltpu.SemaphoreType.DMA,
        ],
    )
    def gather(x_hbm, i_hbm, o_hbm, idx_v, buf0, buf1, g0_sem, g1_sem, w0_sem, w1_sem):
        wid = _sc_worker_id()

        @pl.loop(0, steps)
        def _(si):
            chunk = si * SC_WORKERS + wid

            @pl.when(chunk < n_chunks)
            def _():
                cbase = chunk * SC_IDX_CHUNK
                pltpu.sync_copy(i_hbm.at[pl.ds(cbase, SC_IDX_CHUNK)], idx_v)
                g0 = pltpu.async_copy(x_hbm.at[idx_v.at[pl.ds(0, half)]], buf0, g0_sem)
                g1 = pltpu.async_copy(x_hbm.at[idx_v.at[pl.ds(half, half)]], buf1, g1_sem)
                g0.wait()
                w0 = pltpu.async_copy(buf0, o_hbm.at[pl.ds(cbase, half)], w0_sem)
                g1.wait()
                w1 = pltpu.async_copy(buf1, o_hbm.at[pl.ds(cbase + half, half)], w1_sem)
                w0.wait()
                w1.wait()

    return gather(table, idx)


def _expert_kernel(te_ref, tv_ref, nx_ref, sl_ref, nu_ref, x_ref, wg_hbm, wu_hbm, wd_hbm, o_ref,
                   wg_f, wu_f, wd_f, wg_b, wu_b, wd_b, sems, *, layer):
    i = pl.program_id(0)

    def weight_copies(expert, slot):
        return [pltpu.make_async_copy(hbm.at[layer, expert], buf.at[slot], sems.at[slot, j])
                for j, (hbm, buf) in enumerate(((wg_hbm, wg_f), (wu_hbm, wu_f), (wd_hbm, wd_f)))]

    @pl.when(i == 0)
    def _():
        for cp in weight_copies(te_ref[0], 0):
            cp.start()

    @pl.when((i == 0) | (te_ref[i] != te_ref[jnp.maximum(i - 1, 0)]))
    def _():
        slot = sl_ref[i]
        for cp in weight_copies(te_ref[i], slot):
            cp.wait()
        wg_b[...] = wg_f[slot].astype(bf16)
        wu_b[...] = wu_f[slot].astype(bf16)
        wd_b[...] = wd_f[slot].astype(bf16)

        @pl.when(nx_ref[i] >= 0)
        def _():
            for cp in weight_copies(nx_ref[i], 1 - slot):
                cp.start()

    @pl.when(tv_ref[i] != 0)
    def _():
        lo, hi = _unpack_bf16_pairs(x_ref[...])
        half = lo.shape[1]
        a = (jnp.dot(lo, wg_b[:half], preferred_element_type=f32)
             + jnp.dot(hi, wg_b[half:], preferred_element_type=f32))
        b = (jnp.dot(lo, wu_b[:half], preferred_element_type=f32)
             + jnp.dot(hi, wu_b[half:], preferred_element_type=f32))
        hid = (_silu(a) * b).astype(bf16)
        o_ref[...] = _pack_bf16_pairs(jnp.dot(hid, wd_b[...], preferred_element_type=f32))


def grouped_experts(xs, tile_exp, tile_valid, wg, wu, wd, layer):
    p, half = xs.shape
    d = 2 * half
    n_tiles = p // EXPERT_TILE
    first = jnp.concatenate([jnp.ones((1,), bool), tile_exp[1:] != tile_exp[:-1]])
    slot = (jnp.cumsum(first.astype(i32)) - 1) % 2
    nxt_at = jnp.sum((tile_exp[None, :] <= tile_exp[:, None]).astype(i32), axis=1)
    nxt = jnp.where(nxt_at < n_tiles, tile_exp[jnp.minimum(nxt_at, n_tiles - 1)], -1)
    n_used = jnp.sum(tile_valid).reshape(1)
    tile = pl.BlockSpec((EXPERT_TILE, half), lambda i, te, tv, nx, sl, nu: (jnp.minimum(i, nu[0] - 1), 0))
    hbm = pl.BlockSpec(memory_space=pl.ANY)
    grid_spec = pltpu.PrefetchScalarGridSpec(
        num_scalar_prefetch=5,
        grid=(n_tiles,),
        in_specs=[tile, hbm, hbm, hbm],
        out_specs=tile,
        scratch_shapes=[pltpu.VMEM((2, d, D_EXPERT), f32), pltpu.VMEM((2, d, D_EXPERT), f32),
                        pltpu.VMEM((2, D_EXPERT, d), f32),
                        pltpu.VMEM((d, D_EXPERT), bf16), pltpu.VMEM((d, D_EXPERT), bf16),
                        pltpu.VMEM((D_EXPERT, d), bf16), pltpu.SemaphoreType.DMA((2, 3))],
    )
    return pl.pallas_call(
        functools.partial(_expert_kernel, layer=layer),
        out_shape=jax.ShapeDtypeStruct((p, half), i32),
        grid_spec=grid_spec,
        compiler_params=_params(("arbitrary",)),
        name="grouped_experts",
    )(tile_exp, tile_valid, nxt.astype(i32), slot.astype(i32), n_used.astype(i32), xs, wg, wu, wd)


def _combine_kernel(x_ref, yg_ref, wt_ref, g2_ref, sh2_ref, sc2_ref, n2_ref, wsg_ref, wsu_ref, wsd_ref, fg_ref, o_ref,
                    *, final):
    x = x_ref[...]
    hb = _rms_mod(x, n2_ref[...], sc2_ref[0], sh2_ref[0]).astype(bf16)
    a = jnp.dot(hb, wsg_ref[...], preferred_element_type=f32)
    b = jnp.dot(hb, wsu_ref[...], preferred_element_type=f32)
    f = jnp.dot((_silu(a) * b).astype(bf16), wsd_ref[...], preferred_element_type=f32)
    wt = wt_ref[...].T
    half = x.shape[1] // 2
    f_lo, f_hi = f[:, :half], f[:, half:]
    for k in range(TOP_K):
        packed = yg_ref[k]
        w = wt[:, k:k + 1]
        f_lo = f_lo + w * lax.bitcast_convert_type(lax.shift_left(packed, 16), f32)
        f_hi = f_hi + w * lax.bitcast_convert_type(packed & jnp.int32(-65536), f32)
    xo = x + g2_ref[0] * jnp.concatenate([f_lo, f_hi], axis=1)
    if final:
        ms = jnp.mean(xo * xo, axis=-1, keepdims=True)
        xo = xo * lax.rsqrt(ms + RMS_EPS) * fg_ref[...]
    o_ref[...] = xo


def combine_streamed(x2, yg, wt_rows, mods, rows_per_mod, tile, lw, final_g, final):
    m, d = x2.shape
    row = lambda w: pl.BlockSpec((tile, w), lambda i: (i, 0))
    whole = lambda shape: pl.BlockSpec(shape, lambda i: (0,) * len(shape))
    modspec = pl.BlockSpec((1, 1, d), lambda i: ((i * tile) // rows_per_mod, 0, 0))
    in_specs = [row(d), pl.BlockSpec((TOP_K, tile, d // 2), lambda i: (0, i, 0), pipeline_mode=pl.Buffered(3)),
                pl.BlockSpec((8, tile), lambda i: (0, i)), modspec, modspec, modspec, whole((1, d)),
                whole((d, D_EXPERT)), whole((d, D_EXPERT)), whole((D_EXPERT, d)), whole((1, d))]

    def stream(*refs):
        pltpu.emit_pipeline(functools.partial(_combine_kernel, final=final), grid=(m // tile,), in_specs=in_specs,
                            out_specs=[row(d)])(*refs)

    hbm = pl.BlockSpec(memory_space=pl.ANY)
    return pl.pallas_call(
        stream,
        out_shape=jax.ShapeDtypeStruct((m, d), f32),
        in_specs=[hbm] * N_COMBINE_INPUTS,
        out_specs=hbm,
        compiler_params=pltpu.CompilerParams(vmem_limit_bytes=VMEM_LIMIT),
        name="combine_streamed",
    )(x2, yg, wt_rows, mods["g2"], mods["sh2"], mods["sc2"], lw["norm2_g"], lw["w_sh_gate"], lw["w_sh_up"],
      lw["w_sh_down"], final_g)


def combine(x2, yg, wt_rows, row_offset, mods, rows_per_mod, tile, lw, final_g, final):
    m, d = x2.shape
    off = row_offset // tile
    row = lambda w: pl.BlockSpec((tile, w), lambda i: (i, 0))
    modspec = pl.BlockSpec((1, 1, d), lambda i: ((i * tile) // rows_per_mod, 0, 0))
    return pl.pallas_call(
        functools.partial(_combine_kernel, final=final),
        out_shape=jax.ShapeDtypeStruct((m, d), f32),
        grid=(m // tile,),
        in_specs=[row(d), pl.BlockSpec((TOP_K, tile, d // 2), lambda i: (0, i + off, 0)),
                  pl.BlockSpec((8, tile), lambda i: (0, i + off)), modspec, modspec, modspec,
                  _const_spec((1, d)), _const_spec((d, D_EXPERT)), _const_spec((d, D_EXPERT)),
                  _const_spec((D_EXPERT, d)), _const_spec((1, d))],
        out_specs=row(d),
        compiler_params=_params(("parallel",)),
        name="combine",
    )(x2, yg, wt_rows, mods["g2"], mods["sh2"], mods["sc2"], lw["norm2_g"], lw["w_sh_gate"], lw["w_sh_up"],
      lw["w_sh_down"], final_g)


N_COMBINE_INPUTS = 11
N_INPROJ_INPUTS = 15


def _combine_project_kernel(*refs):
    combine_in = refs[:N_COMBINE_INPUTS]
    inproj_in = refs[N_COMBINE_INPUTS:N_COMBINE_INPUTS + N_INPROJ_INPUTS - 1]
    xo_ref, *inproj_out, prev_ref = refs[N_COMBINE_INPUTS + N_INPROJ_INPUTS - 1:]

    @pl.when(pl.program_id(0) == 0)
    def _():
        prev_ref[...] = jnp.zeros_like(prev_ref)

    raw = _inproj_project(prev_ref, *inproj_in[:4])
    _combine_kernel(*combine_in, xo_ref, final=False)
    _inproj_finish(raw, *inproj_in[4:], *inproj_out)
    prev_ref[...] = xo_ref[...]


def combine_and_project(x2, yg, wt_rows, mods, rows_per_mod, lw, final_g, next_sh, next_sc, tables, rows_per_seq,
                        next_lw, tile):
    m, d = x2.shape
    n_tiles = m // tile
    seq_blocks = rows_per_seq // tile
    cos_t, sa_t, sb_t = tables
    cur = lambda i: jnp.minimum(i, n_tiles - 1)
    prev = lambda i: jnp.maximum(i - 1, 0)
    row_cur = lambda w: pl.BlockSpec((tile, w), lambda i: (cur(i), 0))
    row_prev = lambda w: pl.BlockSpec((tile, w), lambda i: (prev(i), 0))
    mod_cur = pl.BlockSpec((1, 1, d), lambda i: ((cur(i) * tile) // rows_per_mod, 0, 0))
    mod_prev = pl.BlockSpec((1, 1, d), lambda i: ((prev(i) * tile) // rows_per_mod, 0, 0))
    tabspec = pl.BlockSpec((tile, BRANCH_W), lambda i: (prev(i) % seq_blocks, 0))
    out_w = [Q_EXP_W, KV_W, KV_W, Q_EXP_W, KV_W, KV_W, BRANCH_W, BRANCH_W, BRANCH_W]
    combine_specs = [row_cur(d), pl.BlockSpec((TOP_K, tile, d // 2), lambda i: (0, cur(i), 0)),
                     pl.BlockSpec((8, tile), lambda i: (0, cur(i))),
                     mod_cur, mod_cur, mod_cur, _const_spec((1, d)), _const_spec((d, D_EXPERT)),
                     _const_spec((d, D_EXPERT)), _const_spec((D_EXPERT, d)), _const_spec((1, d))]
    inproj_specs = [mod_prev, mod_prev, _const_spec((1, d)), _const_spec((d, OFF_GATE)),
                    _const_spec((1, BRANCH_W)), _const_spec((1, KV_W)), _const_spec((BRANCH_W, BRANCH_W)),
                    _const_spec((1, BRANCH_W)), _const_spec((4, GM_CHUNK, GM_CHUNK)), _const_spec((GM_CHUNK, BRANCH_W)),
                    _const_spec((BRANCH_W, 2 * BRANCH_W)), tabspec, tabspec, tabspec]
    assert len(combine_specs) == N_COMBINE_INPUTS and len(inproj_specs) == N_INPROJ_INPUTS - 1
    return pl.pallas_call(
        _combine_project_kernel,
        out_shape=[jax.ShapeDtypeStruct((m, d), f32)] + [jax.ShapeDtypeStruct((m, w), bf16) for w in out_w],
        grid=(n_tiles + 1,),
        in_specs=combine_specs + inproj_specs,
        out_specs=[row_cur(d)] + [row_prev(w) for w in out_w],
        scratch_shapes=[pltpu.VMEM((tile, d), f32)],
        compiler_params=_params(("arbitrary",)),
        name="combine_and_project",
    )(x2, yg, wt_rows, mods["g2"], mods["sh2"], mods["sc2"], lw["norm2_g"], lw["w_sh_gate"], lw["w_sh_up"],
      lw["w_sh_down"], final_g,
      next_sh, next_sc, next_lw["norm1_g"], next_lw["w_z"], next_lw["qn"], next_lw["kn"], next_lw["bd"],
      next_lw["gm_norm_g"], next_lw["gm_ws"], next_lw["gm_bias"], next_lw["cs64"], cos_t, sa_t, sb_t)


def routed_experts(h2p, eid, rank, counts, lw, layer):
    m = h2p.shape[0]
    p_max = m * TOP_K + N_EXPERTS * EXPERT_TILE
    pos, tile_exp, tile_valid = routing_plan(eid, rank, counts, p_max)
    xs = sc_scatter_rows(h2p, pos, p_max)
    ys = grouped_experts(xs, tile_exp, tile_valid, lw["w_exp_gate"], lw["w_exp_up"], lw["w_exp_down"], layer)
    return sc_gather_rows(ys, pos[:TOP_K].reshape(TOP_K * m)).reshape(TOP_K, m, D_MODEL // 2)


def rope_tables(seq):
    rows = seq // GRID_W
    row = jnp.repeat(jnp.arange(rows), GRID_W).astype(f32)
    col = jnp.tile(jnp.arange(GRID_W), rows).astype(f32)
    axis_dim = HEAD_DIM // 2
    inv_freq = 1.0 / (ROPE_THETA ** (jnp.arange(0, axis_dim, 2, dtype=f32) / axis_dim))
    ang_r = row[:, None] * inv_freq
    ang_c = col[:, None] * inv_freq
    ang = jnp.concatenate([ang_r, ang_r, ang_c, ang_c], axis=-1)
    cos, sin = jnp.cos(ang), jnp.sin(ang)
    seg = (jnp.arange(HEAD_DIM) // 16) % 2
    sa = jnp.where(seg == 0, -sin, 0.0)
    sb = jnp.where(seg == 1, sin, 0.0)
    rep = lambda t: jnp.tile(t, (1, 4))
    return rep(cos), rep(sa), rep(sb)


def _router_weight_pieces(w):
    hi, mid, lo = _split3(w)
    return jnp.concatenate([hi, mid, lo, jnp.zeros_like(hi)], axis=1)


def identity_rope_tables(rows):
    return jnp.ones((rows, BRANCH_W), f32), jnp.zeros((rows, BRANCH_W), f32), jnp.zeros((rows, BRANCH_W), f32)


def kernel(x, c, ctx, c_ctx, w_mod, b_mod, norm1_g, norm2_g, w_in, q_norm_g, k_norm_g, sink, gm_norm_g, gm_ws, gm_b, w_br, w_o, w_router, router_bias, w_exp_gate, w_exp_up, w_exp_down, w_sh_gate, w_sh_up, w_sh_down, final_norm_g):
    bsz_all, seq, d = x.shape
    n_ctx = ctx.shape[1]
    depth = w_mod.shape[0]

    cc = jnp.concatenate([c, c_ctx[None, :], jnp.zeros((MOD_ROWS - bsz_all - 1, d), f32)], axis=0)
    mod_all = compute_mod(cc, w_mod, b_mod)

    lat_tables = rope_tables(seq)
    ctx_tables = identity_rope_tables(ROW_TILE)
    wc_lat, ws_lat = dft_tables(seq)
    wc_ctx, ws_ctx = dft_tables(n_ctx)
    cs64 = channel_dft_table()
    bd = jnp.asarray(np.kron(np.eye(4), np.full((HEAD_DIM, HEAD_DIM), 1.0 / HEAD_DIM)), dtype=bf16)
    final_g = final_norm_g.reshape(1, d)
    tri = jnp.asarray(np.triu(np.ones((MERGE_TILE, MERGE_TILE)), 1), dtype=bf16)

    lws = []
    for l in range(depth):
        lws.append({
            "norm1_g": norm1_g[l].reshape(1, d),
            "norm2_g": norm2_g[l].reshape(1, d),
            "w_z": w_in[l, :, :OFF_GATE].astype(bf16),
            "w_gate": w_in[l, :, OFF_GATE:].astype(bf16),
            "qn": jnp.tile(q_norm_g[l], BRANCH_W // HEAD_DIM).reshape(1, BRANCH_W),
            "kn": jnp.tile(k_norm_g[l], KV_W // HEAD_DIM).reshape(1, KV_W),
            "bd": bd,
            "gm_norm_g": gm_norm_g[l].reshape(1, BRANCH_W),
            "gm_ws": gm_ws[l].astype(bf16),
            "gm_bias": jnp.repeat(gm_b[l].T, GROUP_DIM, axis=1),
            "cs64": cs64,
            "w_br": w_br[l].astype(bf16),
            "w_o": w_o[l].astype(bf16),
            "w_router3": _router_weight_pieces(w_router[l]),
            "router_bias": router_bias[l].reshape(N_EXPERTS, 1),
            "tri": tri,
            "w_exp_gate": w_exp_gate,
            "w_exp_up": w_exp_up,
            "w_exp_down": w_exp_down,
            "w_sh_gate": w_sh_gate[l].astype(bf16),
            "w_sh_up": w_sh_up[l].astype(bf16),
            "w_sh_down": w_sh_down[l].astype(bf16),
        })

    return _layers(x, ctx, mod_all[:, :bsz_all], mod_all[:, bsz_all], lws, sink, lat_tables, ctx_tables,
                   (wc_lat, ws_lat), (wc_ctx, ws_ctx), final_g)


def _layers(x, ctx, mod_lat, mod_ctx, lws, sink, lat_tables, ctx_tables, dft_lat, dft_ctx, final_g):
    bsz, seq, d = x.shape
    n_ctx = ctx.shape[1]
    depth = len(lws)
    n_lat = bsz * seq
    n_cx = bsz * n_ctx
    wc_lat, ws_lat = dft_lat
    wc_ctx, ws_ctx = dft_ctx
    xl = x.reshape(n_lat, d)
    xc = ctx.reshape(n_cx, d)
    lat_proj = None
    for l in range(depth):
        ctx_out = l < depth - 1
        lw = lws[l]
        names = ("sh1", "sc1", "g1", "sh2", "sc2", "g2")
        mods_lat = {n: mod_lat[l, :, i * d:(i + 1) * d].reshape(bsz, 1, d) for i, n in enumerate(names)}
        mods_ctx = {n: mod_ctx[l, i * d:(i + 1) * d].reshape(1, 1, d) for i, n in enumerate(names)}
        sink_l = sink[l]

        if lat_proj is None:
            lat_proj = in_projection(xl, mods_lat["sh1"], mods_lat["sc1"], seq, lat_tables, seq, ROW_TILE, lw)
        qg, kg, vg, qw, kw, vw, fxc, fxs, o_d = lat_proj
        cqg, ckg, cvg, cqw, ckw, cvw, cfxc, cfxs, co_d = in_projection(
            xc, mods_ctx["sh1"], mods_ctx["sc1"], n_cx, ctx_tables, ROW_TILE, ROW_TILE, lw)
        b3 = lambda t, rows: t.reshape(bsz, rows, t.shape[-1])
        ckg3, cvg3, ckw3, cvw3 = b3(ckg, n_ctx), b3(cvg, n_ctx), b3(ckw, n_ctx), b3(cvw, n_ctx)

        o_a = dft_mix(wc_lat, ws_lat, b3(fxc, seq), b3(fxs, seq), ROW_TILE).reshape(n_lat, BRANCH_W)
        o_b, o_c = latent_attention(b3(qg, seq), b3(qw, seq), b3(kg, seq), b3(vg, seq), b3(kw, seq), b3(vw, seq),
                                    ckg3, cvg3, ckw3, cvw3, sink_l)
        o_b, o_c = o_b.reshape(n_lat, BRANCH_W), o_c.reshape(n_lat, BRANCH_W)
        m_total = n_lat + (n_cx if ctx_out else 0)
        route = None
        if ctx_out:
            route = [jnp.zeros((m_total, d // 2), i32), jnp.zeros((8, m_total), i32), jnp.zeros((8, m_total), f32),
                     jnp.zeros((8, m_total), i32), jnp.zeros((N_EXPERTS, 1), i32)]
        xl, *route = merge_and_route(xl, mods_lat, seq, (o_a, o_b, o_c, o_d), MERGE_TILE, lw, m_total, 0, route)
        if ctx_out:
            co_a = dft_mix(wc_ctx, ws_ctx, b3(cfxc, n_ctx), b3(cfxs, n_ctx), n_ctx).reshape(n_cx, BRANCH_W)
            co_b = full_attention(b3(cqg, n_ctx), [(ckg3, cvg3)], None, n_ctx).reshape(n_cx, BRANCH_W)
            co_c = full_attention(b3(cqw, n_ctx), [(ckw3, cvw3)], sink_l, n_ctx).reshape(n_cx, BRANCH_W)
            xc, *route = merge_and_route(xc, mods_ctx, n_cx, (co_a, co_b, co_c, co_d), MERGE_TILE, lw, m_total, n_lat,
                                         route)
        h2p, eid, wt, rank, counts = route
        yg = routed_experts(h2p, eid, rank, counts, lw, l)
        wt_rows = wt
        if ctx_out:
            nxt = {n: mod_lat[l + 1, :, i * d:(i + 1) * d].reshape(bsz, 1, d) for i, n in enumerate(names[:2])}
            xl, *lat_proj = combine_and_project(xl, yg, wt_rows, mods_lat, seq, lw, final_g, nxt["sh1"], nxt["sc1"],
                                                lat_tables, seq, lws[l + 1], COMBINE_PROJECT_TILE)
            xc = combine(xc, yg, wt_rows, n_lat, mods_ctx, n_cx, ROW_TILE, lw, final_g, False)
        else:
            xl = combine_streamed(xl, yg, wt_rows, mods_lat, seq, COMBINE_PROJECT_TILE, lw, final_g, True)
    return xl.reshape(bsz, seq, d)
```
